```python
import math
import jax, jax.numpy as jnp
from jax import lax
import numpy as np


D_MODEL = 2048
BATCH = 2
SEQ = 4096
DEPTH = 2
DEC_BATCH = 8
DEC_SEQ = 8
PAST_LEN = 16384
PAGE_SIZE = 128

D_HALF = D_MODEL // 2
N_EVEN = (DEPTH + 1) // 2
N_ODD = DEPTH // 2
POOL_WINDOWS = (2, 4, 8, 16)
POOL_GROUPS = 4
POOL_DG = D_HALF // POOL_GROUPS
POOL_PAD = 15
MLSTM_HEADS = 4
MLSTM_DV = D_HALF // MLSTM_HEADS
MLSTM_DK = MLSTM_DV // 2
MLSTM_CHUNK = 64
GATE_CAP = 15.0
GMLP_CHUNK = 128
GMLP_GROUPS = 4
GMLP_DC = D_HALF
GMLP_DG = GMLP_DC // GMLP_GROUPS
NSA_HEADS = 8
NSA_DH = D_HALF // NSA_HEADS
NSA_KV = 2
NSA_G = NSA_HEADS // NSA_KV
CMP_STRIDE = 16
CMP_LEN = 32
SEL_BLOCK = 64
N_SEL = 16
WINDOW = 512
QBLK = 128
FORCE_SCORE = 1e9
NEG = -1e30
FFN_HIDDEN = ((8 * D_MODEL + 2) // 3 + 255) // 256 * 256
EVEN_SIZES = (D_HALF, MLSTM_HEADS * MLSTM_DK, MLSTM_HEADS * MLSTM_DK, MLSTM_HEADS * MLSTM_DV, MLSTM_HEADS * MLSTM_DV, MLSTM_HEADS, MLSTM_HEADS)
ODD_SIZES = (GMLP_DC, GMLP_DC, NSA_HEADS * NSA_DH, 2 * NSA_KV * NSA_DH, 2 * NSA_KV * NSA_DH, 2 * NSA_KV * NSA_DH, 3 * NSA_HEADS)

kernel_name = "hybrid_pool_mlstm_gmlp_nsa_step"


def _split(z, sizes):
    return jnp.split(z, np.cumsum(sizes)[:-1].tolist(), axis=-1)


def _rmsnorm(x, g, eps=1e-6):
    xf = x.astype(jnp.float32)
    y = xf * lax.rsqrt(jnp.mean(xf * xf, axis=-1, keepdims=True) + eps)
    return (y * g.astype(jnp.float32)).astype(x.dtype)


def _layernorm(x, g, eps=1e-5):
    xf = x.astype(jnp.float32)
    xc = xf - jnp.mean(xf, axis=-1, keepdims=True)
    y = xc * lax.rsqrt(jnp.mean(xc * xc, axis=-1, keepdims=True) + eps)
    return (y * g.astype(jnp.float32)).astype(x.dtype)


def _swiglu(x, w1, w3, w2):
    return (jax.nn.silu(x @ w1) * (x @ w3)) @ w2


def _masked_softmax(s, mask):
    s = jnp.where(mask, s.astype(jnp.float32), NEG)
    return jax.nn.softmax(s, axis=-1) * mask


def _pool_mix(u_ext, t0, pool_w, pool_scale):
    T = u_ext.shape[1] - POOL_PAD
    S = jnp.pad(jnp.cumsum(u_ext.astype(jnp.float32), axis=1), ((0, 0), (1, 0), (0, 0)))
    pos = t0 + jnp.arange(T)
    x_new = u_ext[:, POOL_PAD:].astype(jnp.float32)
    outs = []
    for g, w in enumerate(POOL_WINDOWS):
        c = slice(g * POOL_DG, (g + 1) * POOL_DG)
        tot = S[:, POOL_PAD + 1:POOL_PAD + 1 + T, c] - S[:, POOL_PAD + 1 - w:POOL_PAD + 1 - w + T, c]
        cnt = jnp.minimum(w, pos + 1).astype(jnp.float32)
        y = tot / cnt[None, :, None] - x_new[..., c]
        outs.append(jnp.einsum('ntc,cd->ntd', y.astype(u_ext.dtype), pool_w[g]))
    return jnp.concatenate(outs, axis=-1) * pool_scale


def _mlstm(q, k, v, i_pre, f_pre, C0, n0, m0):
    N, T = q.shape[:2]
    L = math.gcd(T, MLSTM_CHUNK)
    nc = T // L

    def chunks(a):
        a = a.astype(jnp.float32).reshape((N, nc, L) + a.shape[2:])
        return jnp.moveaxis(jnp.moveaxis(a, 1, 0), 3, 2)

    li = GATE_CAP * jnp.tanh(i_pre.astype(jnp.float32) / GATE_CAP)
    lf = jax.nn.log_sigmoid(GATE_CAP * jnp.tanh(f_pre.astype(jnp.float32) / GATE_CAP))
    causal = jnp.tril(jnp.ones((L, L), dtype=bool))

    def step(carry, inp):
        C, n, m = carry
        qc, kc, vc, lic, lfc = inp
        b = jnp.cumsum(lfc, axis=-1)
        d = jnp.where(causal, b[..., :, None] - b[..., None, :] + lic[..., None, :], -jnp.inf)
        inter = b + m[..., None]
        m_t = jnp.maximum(inter, d.max(-1))
        w_in = jnp.exp(inter - m_t)
        s = jnp.einsum('nhld,nhsd->nhls', qc, kc) * jnp.exp(d - m_t[..., None])
        num = w_in[..., None] * jnp.einsum('nhld,nhdv->nhlv', qc, C) + jnp.einsum('nhls,nhsv->nhlv', s, vc)
        den = w_in * jnp.einsum('nhld,nhd->nhl', qc, n) + s.sum(-1)
        h = num / jnp.maximum(jnp.abs(den), jnp.exp(-m_t))[..., None]
        b_end = b[..., -1]
        g = b_end[..., None] - b + lic
        m_new = jnp.maximum(b_end + m, g.max(-1))
        w_c = jnp.exp(b_end + m - m_new)
        w_k = jnp.exp(g - m_new[..., None])
        C_new = w_c[..., None, None] * C + jnp.einsum('nhs,nhsd,nhsv->nhdv', w_k, kc, vc)
        n_new = w_c[..., None] * n + jnp.einsum('nhs,nhsd->nhd', w_k, kc)
        return (C_new, n_new, m_new), h

    carry0 = (C0.astype(jnp.float32), n0.astype(jnp.float32), m0.astype(jnp.float32))
    (C, n, m), h = lax.scan(step, carry0, (chunks(q), chunks(k), chunks(v), chunks(li), chunks(lf)))
    h = h.transpose(1, 0, 3, 2, 4).reshape(N, T, MLSTM_HEADS, MLSTM_DV)
    return h.astype(q.dtype), C, n, m


def _even_layer(x, t0, pool_buf, C0, n0, m0, ng, w_in, w_out, pool_w, pool_scale, gate_b, mnorm_g, w1, w3, w2):
    N, T, _ = x.shape
    z = _rmsnorm(x, ng[0]) @ w_in
    u, q, k, v, o, ig, fg = _split(z, EVEN_SIZES)
    u_ext = jnp.concatenate([pool_buf.astype(x.dtype), u], axis=1)
    y_a = _pool_mix(u_ext, t0, pool_w, pool_scale)
    q = q.reshape(N, T, MLSTM_HEADS, MLSTM_DK) * (MLSTM_DK ** -0.5)
    k = k.reshape(N, T, MLSTM_HEADS, MLSTM_DK)
    v = v.reshape(N, T, MLSTM_HEADS, MLSTM_DV)
    h, C, n, m = _mlstm(q, k, v, ig + gate_b[0], fg + gate_b[1], C0, n0, m0)
    h = _rmsnorm(h, mnorm_g.reshape(MLSTM_HEADS, MLSTM_DV)).reshape(N, T, MLSTM_HEADS * MLSTM_DV)
    y_b = jax.nn.sigmoid(o) * h
    y = jnp.concatenate([y_a, y_b], axis=-1) @ w_out
    x = x + _rmsnorm(y, ng[1])
    x = x + _rmsnorm(_swiglu(_rmsnorm(x, ng[2]), w1, w3, w2), ng[3])
    return x, u_ext[:, -POOL_PAD:], C, n, m


def _gmlp(u, v, g, ws, bs):
    N, T, _ = u.shape
    vn = _layernorm(v, g)
    Tp = -(-T // GMLP_CHUNK) * GMLP_CHUNK
    nc = Tp // GMLP_CHUNK
    vp = jnp.pad(vn, ((0, 0), (0, Tp - T), (0, 0))).reshape(N, nc, GMLP_CHUNK, GMLP_GROUPS, GMLP_DG)
    wm = ws * jnp.tril(jnp.ones((GMLP_CHUNK, GMLP_CHUNK), dtype=ws.dtype))
    mix = jnp.einsum('gts,ncsgd->nctgd', wm, vp) + bs.T[None, None, :, :, None]
    mix = mix.reshape(N, Tp, GMLP_DC)[:, :T]
    return u * mix, vn


def _compress(kv, w_pos, w_cmp):
    N, T = kv.shape[:2]
    ns = T // CMP_STRIDE
    sub = kv[:, :ns * CMP_STRIDE].reshape(N, ns, CMP_STRIDE, 2, NSA_KV, NSA_DH)
    a = jnp.einsum('nsjckd,jc->nsckd', sub, w_pos[:CMP_STRIDE])
    b = jnp.einsum('nsjckd,jc->nsckd', sub, w_pos[CMP_STRIDE:])
    blk = a[:, :-1] + b[:, 1:]
    out = jnp.einsum('nickd,cde->nicke', blk, w_cmp)
    cmp_end = (jnp.arange(ns - 1) + 2) * CMP_STRIDE - 1
    return out[:, :, 0], out[:, :, 1], cmp_end


def _cmp_to_sel(imp, n_sel_blocks):
    R = SEL_BLOCK // CMP_STRIDE
    padded = jnp.pad(imp, ((0, 0),) * (imp.ndim - 1) + ((0, n_sel_blocks * R - imp.shape[-1]),))
    grp = padded.reshape(imp.shape[:-1] + (n_sel_blocks, R)).sum(-1)
    prev = padded[..., R - 1::R]
    prev = jnp.pad(prev[..., :-1], ((0, 0),) * (imp.ndim - 1) + ((1, 0),))
    return grp + prev


def _nsa_attend(q, pos, gates, k_cmp, v_cmp, cmp_end, gather_sel, n_sel_blocks, kv_win, pos_win):
    N, Q = q.shape[:2]
    qg = q.reshape(N, Q, NSA_KV, NSA_G, NSA_DH) * (NSA_DH ** -0.5)
    m_c = cmp_end[None, :] <= pos[:, None]
    p_c = _masked_softmax(jnp.einsum('nqkgd,nckd->nkgqc', qg, k_cmp), m_c)
    o_cmp = jnp.einsum('nkgqc,nckd->nqkgd', p_c.astype(q.dtype), v_cmp)
    imp = _cmp_to_sel(p_c.sum(axis=2), n_sel_blocks)
    blk = jnp.arange(n_sel_blocks)[None, :]
    qblk = (pos // SEL_BLOCK)[:, None]
    forced = (blk == 0) | (blk == qblk) | (blk == qblk - 1)
    score = jnp.where(forced, FORCE_SCORE, jnp.where(blk > qblk, -1.0, imp))
    _, idx = lax.top_k(score, min(N_SEL, n_sel_blocks))
    g = gather_sel(idx)
    nk = idx.shape[-1] * SEL_BLOCK
    k_s = g[..., 0, :].reshape(N, NSA_KV, Q, nk, NSA_DH)
    v_s = g[..., 1, :].reshape(N, NSA_KV, Q, nk, NSA_DH)
    tok = (idx[..., None] * SEL_BLOCK + jnp.arange(SEL_BLOCK)).reshape(N, NSA_KV, Q, nk)
    m_s = (tok <= pos[:, None])[:, :, None]
    p_s = _masked_softmax(jnp.einsum('nqkgd,nkqsd->nkgqs', qg, k_s), m_s)
    o_sel = jnp.einsum('nkgqs,nkqsd->nqkgd', p_s.astype(q.dtype), v_s)
    m_w = (pos_win[None] >= 0) & (pos_win[None] <= pos[:, None]) & (pos_win[None] > pos[:, None] - WINDOW)
    p_w = _masked_softmax(jnp.einsum('nqkgd,nwkd->nkgqw', qg, kv_win[:, :, 0]), m_w)
    o_win = jnp.einsum('nkgqw,nwkd->nqkgd', p_w.astype(q.dtype), kv_win[:, :, 1])
    g3 = gates.reshape(N, Q, 3, NSA_KV, NSA_G, 1)
    o = g3[:, :, 0] * o_cmp + g3[:, :, 1] * o_sel + g3[:, :, 2] * o_win
    return o.reshape(N, Q, NSA_HEADS * NSA_DH)


def _nsa_prompt(q, gates, kvc, kvs, kvw, cmp_pos, cmp_w):
    N, T = q.shape[:2]
    k_cmp, v_cmp, cmp_end = _compress(kvc, cmp_pos, cmp_w)
    n_sel_blocks = T // SEL_BLOCK
    sel_blocks = kvs.reshape(N, n_sel_blocks, SEL_BLOCK, 2, NSA_KV, NSA_DH).transpose(0, 4, 1, 2, 3, 5)
    bi = jnp.arange(N)[:, None, None, None]
    kvi = jnp.arange(NSA_KV)[None, :, None, None]

    def gather(idx):
        return sel_blocks[bi, kvi, idx]

    kvw_pad = jnp.pad(kvw, ((0, 0), (WINDOW, 0), (0, 0), (0, 0), (0, 0)))
    nqb = T // QBLK
    q_b = q.reshape(N, nqb, QBLK, NSA_HEADS, NSA_DH).swapaxes(0, 1)
    g_b = gates.reshape(N, nqb, QBLK, 3, NSA_HEADS).swapaxes(0, 1)

    def block(args):
        qb, gb, n = args
        start = n * QBLK
        pos = start + jnp.arange(QBLK)
        kw = lax.dynamic_slice_in_dim(kvw_pad, start, WINDOW + QBLK, axis=1)
        pos_w = start - WINDOW + jnp.arange(WINDOW + QBLK)
        return _nsa_attend(qb, pos, gb, k_cmp, v_cmp, cmp_end, gather, n_sel_blocks, kw, pos_w)

    o = lax.map(block, (q_b, g_b, jnp.arange(nqb)))
    o = o.swapaxes(0, 1).reshape(N, T, NSA_HEADS * NSA_DH)
    return o, kvw[:, -min(WINDOW, T):]


def _nsa_sample(q, gates, kvc, kvs, kvw, cmp_pos, cmp_w, win_buf, pool_cmp, pool_sel, page_table):
    N, T = q.shape[:2]
    past_len = page_table.shape[1] * PAGE_SIZE
    kv_past = pool_cmp[page_table].reshape(N, past_len, 2, NSA_KV, NSA_DH)
    k_cmp, v_cmp, cmp_end = _compress(jnp.concatenate([kv_past, kvc], axis=1), cmp_pos, cmp_w)
    t_total = past_len + T
    n_sel_blocks = -(-t_total // SEL_BLOCK)
    n_past_blocks = past_len // SEL_BLOCK
    n_new_blocks = n_sel_blocks - n_past_blocks
    bpp = PAGE_SIZE // SEL_BLOCK
    pool_r = pool_sel.reshape(pool_sel.shape[0], bpp, SEL_BLOCK, 2, NSA_KV, NSA_DH)
    new_b = jnp.pad(kvs, ((0, 0), (0, n_new_blocks * SEL_BLOCK - T), (0, 0), (0, 0), (0, 0)))
    new_b = new_b.reshape(N, n_new_blocks, SEL_BLOCK, 2, NSA_KV, NSA_DH)
    bi = jnp.arange(N)[:, None, None, None]
    kvi = jnp.arange(NSA_KV)[None, :, None, None]

    def gather(idx):
        pi = jnp.minimum(idx, n_past_blocks - 1)
        phys = page_table[bi, pi // bpp]
        g_past = pool_r[phys, pi % bpp, :, :, kvi, :]
        ni = jnp.clip(idx - n_past_blocks, 0, n_new_blocks - 1)
        g_new = new_b[bi, ni, :, :, kvi, :]
        return jnp.where((idx < n_past_blocks)[..., None, None, None], g_past, g_new)

    wb = win_buf.shape[1]
    kw = jnp.concatenate([win_buf.astype(kvw.dtype), kvw], axis=1)
    pos = past_len + jnp.arange(T)
    pos_w = past_len - wb + jnp.arange(wb + T)
    o = _nsa_attend(q, pos, gates, k_cmp, v_cmp, cmp_end, gather, n_sel_blocks, kw, pos_w)
    return o, kw[:, -wb:]


def _odd_layer(x, ng, w_in, w_out, gnorm_g, ws, bs, cmp_pos, cmp_w, gate_b, w1, w3, w2, past=None):
    N, T, _ = x.shape
    z = _rmsnorm(x, ng[0]) @ w_in
    u, v, q, kvc, kvs, kvw, gt = _split(z, ODD_SIZES)
    y_c, vn = _gmlp(u, v, gnorm_g, ws, bs)
    q = q.reshape(N, T, NSA_HEADS, NSA_DH)
    kvc = kvc.reshape(N, T, 2, NSA_KV, NSA_DH)
    kvs = kvs.reshape(N, T, 2, NSA_KV, NSA_DH)
    kvw = kvw.reshape(N, T, 2, NSA_KV, NSA_DH)
    gates = jax.nn.sigmoid((gt + gate_b).astype(jnp.float32)).astype(x.dtype).reshape(N, T, 3, NSA_HEADS)
    if past is None:
        o, win_state = _nsa_prompt(q, gates, kvc, kvs, kvw, cmp_pos, cmp_w)
    else:
        o, win_state = _nsa_sample(q, gates, kvc, kvs, kvw, cmp_pos, cmp_w, *past)
    y = jnp.concatenate([y_c, o], axis=-1) @ w_out
    x = x + _rmsnorm(y, ng[1])
    x = x + _rmsnorm(_swiglu(_rmsnorm(x, ng[2]), w1, w3, w2), ng[3])
    return x, vn, kvc, kvs, win_state


def setup_inputs(seed: int = 0) -> dict:
    key = jax.random.key(seed)
    ks = jax.random.split(key, 32)
    f32 = jnp.float32

    def nrm(k, shape, scale=1.0):
        return jax.random.normal(k, shape, f32) * scale

    n_pages = PAST_LEN // PAGE_SIZE
    n_pool = (DEC_BATCH * n_pages * 5) // 4
    wb = min(WINDOW, PAST_LEN)
    kv_row = (2, NSA_KV, NSA_DH)
    even_in = sum(EVEN_SIZES)
    odd_in = sum(ODD_SIZES)
    page_table = jax.random.permutation(ks[9], n_pool)[:DEC_BATCH * n_pages].reshape(DEC_BATCH, n_pages).astype(jnp.int32)
    mlstm_gate_b = jnp.stack([-2.0 + nrm(ks[14], (N_EVEN, MLSTM_HEADS), 0.5),
                              3.0 + nrm(ks[15], (N_EVEN, MLSTM_HEADS), 0.5)], axis=1)
    return {
        'x_prompt': nrm(ks[0], (BATCH, SEQ, D_MODEL)),
        'x_sample': nrm(ks[1], (DEC_BATCH, DEC_SEQ, D_MODEL)),
        'state_pool': nrm(ks[2], (N_EVEN, DEC_BATCH, POOL_PAD, D_HALF)),
        'state_mlstm_c': nrm(ks[3], (N_EVEN, DEC_BATCH, MLSTM_HEADS, MLSTM_DK, MLSTM_DV), MLSTM_DK ** -0.5),
        'state_mlstm_n': nrm(ks[4], (N_EVEN, DEC_BATCH, MLSTM_HEADS, MLSTM_DK), MLSTM_DK ** -0.5),
        'state_mlstm_m': nrm(ks[5], (N_EVEN, DEC_BATCH, MLSTM_HEADS), 0.5),
        'state_win_kv': nrm(ks[6], (N_ODD, DEC_BATCH, wb) + kv_row),
        'cache_cmp_kv': nrm(ks[7], (N_ODD, n_pool, PAGE_SIZE) + kv_row),
        'cache_sel_kv': nrm(ks[8], (N_ODD, n_pool, PAGE_SIZE) + kv_row),
        'page_table': page_table,
        'norm_g': 1.0 + nrm(ks[10], (DEPTH, 4, D_MODEL), 0.02),
        'w_in_even': nrm(ks[11], (N_EVEN, D_MODEL, even_in), D_MODEL ** -0.5),
        'w_out_even': nrm(ks[12], (N_EVEN, 2 * D_HALF, D_MODEL), (2 * D_HALF) ** -0.5),
        'pool_w': nrm(ks[13], (N_EVEN, POOL_GROUPS, POOL_DG, POOL_DG), POOL_DG ** -0.5),
        'pool_scale': 0.5 + nrm(ks[16], (N_EVEN, D_HALF), 0.1),
        'mlstm_gate_b': mlstm_gate_b,
        'mlstm_norm_g': 1.0 + nrm(ks[17], (N_EVEN, MLSTM_HEADS * MLSTM_DV), 0.02),
        'w_in_odd': nrm(ks[18], (N_ODD, D_MODEL, odd_in), D_MODEL ** -0.5),
        'w_out_odd': nrm(ks[19], (N_ODD, 2 * D_HALF, D_MODEL), (2 * D_HALF) ** -0.5),
        'gmlp_norm_g': 1.0 + nrm(ks[20], (N_ODD, GMLP_DC), 0.02),
        'gmlp_ws': nrm(ks[21], (N_ODD, GMLP_GROUPS, GMLP_CHUNK, GMLP_CHUNK), GMLP_CHUNK ** -0.5),
        'gmlp_bs': 1.0 + nrm(ks[22], (N_ODD, GMLP_GROUPS, GMLP_CHUNK), 0.1),
        'nsa_cmp_pos': CMP_LEN ** -0.5 * (1.0 + nrm(ks[23], (N_ODD, CMP_LEN, 2), 0.1)),
        'nsa_cmp_w': nrm(ks[24], (N_ODD, 2, NSA_DH, NSA_DH), NSA_DH ** -0.5),
        'nsa_gate_b': nrm(ks[25], (N_ODD, 3 * NSA_HEADS), 0.1),
        'ffn_w1': nrm(ks[26], (DEPTH, D_MODEL, FFN_HIDDEN), D_MODEL ** -0.5),
        'ffn_w3': nrm(ks[27], (DEPTH, D_MODEL, FFN_HIDDEN), D_MODEL ** -0.5),
        'ffn_w2': nrm(ks[28], (DEPTH, FFN_HIDDEN, D_MODEL), FFN_HIDDEN ** -0.5),
    }


def reference(x_prompt, x_sample, state_pool, state_mlstm_c, state_mlstm_n, state_mlstm_m, state_win_kv,
              cache_cmp_kv, cache_sel_kv, page_table, norm_g, w_in_even, w_out_even, pool_w, pool_scale,
              mlstm_gate_b, mlstm_norm_g, w_in_odd, w_out_odd, gmlp_norm_g, gmlp_ws, gmlp_bs, nsa_cmp_pos,
              nsa_cmp_w, nsa_gate_b, ffn_w1, ffn_w3, ffn_w2):
    B = x_prompt.shape[0]
    past_len = page_table.shape[1] * PAGE_SIZE
    xp, xs = x_prompt, x_sample
    pool_p, pool_s, c_p, c_s, n_p, n_s, m_p, m_s = [], [], [], [], [], [], [], []
    gv_s, cmp_p, cmp_s, sel_p, sel_s, win_p, win_s = [], [], [], [], [], [], []
    for l in range(DEPTH):
        j = l // 2
        ffn = (ffn_w1[l], ffn_w3[l], ffn_w2[l])
        if l % 2 == 0:
            ew = (norm_g[l], w_in_even[j], w_out_even[j], pool_w[j], pool_scale[j], mlstm_gate_b[j], mlstm_norm_g[j]) + ffn
            zp = jnp.zeros((B, POOL_PAD, D_HALF), xp.dtype)
            zc = jnp.zeros((B, MLSTM_HEADS, MLSTM_DK, MLSTM_DV), jnp.float32)
            zn = jnp.zeros((B, MLSTM_HEADS, MLSTM_DK), jnp.float32)
            zm = jnp.zeros((B, MLSTM_HEADS), jnp.float32)
            xp, pb, c, n, m = _even_layer(xp, 0, zp, zc, zn, zm, *ew)
            pool_p.append(pb); c_p.append(c); n_p.append(n); m_p.append(m)
            xs, pb, c, n, m = _even_layer(xs, past_len, state_pool[j], state_mlstm_c[j], state_mlstm_n[j], state_mlstm_m[j], *ew)
            pool_s.append(pb); c_s.append(c); n_s.append(n); m_s.append(m)
        else:
            ow = (norm_g[l], w_in_odd[j], w_out_odd[j], gmlp_norm_g[j], gmlp_ws[j], gmlp_bs[j], nsa_cmp_pos[j], nsa_cmp_w[j], nsa_gate_b[j]) + ffn
            xp, _, kc, ksl, wv = _odd_layer(xp, *ow)
            cmp_p.append(kc); sel_p.append(ksl); win_p.append(wv)
            xs, vn, kc, ksl, wv = _odd_layer(xs, *ow, past=(state_win_kv[j], cache_cmp_kv[j], cache_sel_kv[j], page_table))
            gv_s.append(vn); cmp_s.append(kc); sel_s.append(ksl); win_s.append(wv)
    st = jnp.stack
    return (xp, xs, st(pool_p), st(pool_s), st(c_p), st(c_s), st(n_p), st(n_s), st(m_p), st(m_s),
            st(gv_s), st(cmp_p), st(cmp_s), st(sel_p), st(sel_s), st(win_p), st(win_s))
```

```python
import functools
import math

import numpy as np
import jax
import jax.numpy as jnp
from jax import lax
from jax.experimental import pallas as pl
from jax.experimental.pallas import tpu as pltpu

F32 = jnp.float32
BF16 = jnp.bfloat16
NEG = -1e30

D_HALF = 1024
POOL_WINDOWS = (2, 4, 8, 16)
POOL_DG = 256
POOL_PAD = 15
HALO = 16
MLSTM_HEADS = 4
MLSTM_DK = 128
MLSTM_DV = 256
GATE_CAP = 15.0
GMLP_CHUNK = 128
GMLP_GROUPS = 4
GMLP_DG = 256
NSA_HEADS = 8
NSA_DH = 128
NSA_KV = 2
NSA_G = 4
CMP_STRIDE = 16
SEL_BLOCK = 64
N_SEL = 16
WINDOW = 512
QBLK = 128
PAGE = 128
FORCE_SCORE = 1e9
LANE = 128
VMEM_LIMIT = 56 * 1024 * 1024

NT_DIMS = (((1,), (1,)), ((), ()))
TN_DIMS = (((0,), (0,)), ((), ()))


def _cparams(*sem):
    return pltpu.CompilerParams(dimension_semantics=sem, vmem_limit_bytes=VMEM_LIMIT)


def _tile(m, pref):
    if m <= pref:
        return m
    for t in range(pref, 7, -1):
        if m % t == 0 and t % 8 == 0:
            return t
    return m


def _dot(a, b):
    return jnp.dot(a, b, preferred_element_type=F32)


def _dot_nt(a, b):
    return lax.dot_general(a, b, NT_DIMS, preferred_element_type=F32)


def _rms(x, g, eps=1e-6):
    return x * lax.rsqrt(jnp.mean(x * x, axis=-1, keepdims=True) + eps) * g


def _norm_mm_body(x_ref, g_ref, w_ref, wg_ref, o_ref, og_ref, xn_ref):
    @pl.when(pl.program_id(1) == 0)
    def _():
        xn = _rms(x_ref[...], g_ref[...]).astype(BF16)
        xn_ref[...] = xn
        og_ref[...] = _dot(xn, wg_ref[...])

    o_ref[...] = _dot(xn_ref[...], w_ref[...])


def _norm_matmul(x, g, w, wg, tm_pref=1024, tn_pref=512):
    M, K = x.shape
    N = w.shape[1]
    tm = _tile(M, tm_pref)
    tn = _tile(N, tn_pref)
    return pl.pallas_call(
        _norm_mm_body,
        grid=(M // tm, N // tn),
        in_specs=[
            pl.BlockSpec((tm, K), lambda i, j: (i, 0)),
            pl.BlockSpec((1, K), lambda i, j: (0, 0)),
            pl.BlockSpec((K, tn), lambda i, j: (0, j)),
            pl.BlockSpec((K, LANE), lambda i, j: (0, 0)),
        ],
        out_specs=[
            pl.BlockSpec((tm, tn), lambda i, j: (i, j)),
            pl.BlockSpec((tm, LANE), lambda i, j: (i, 0)),
        ],
        out_shape=[jax.ShapeDtypeStruct((M, N), F32), jax.ShapeDtypeStruct((M, LANE), F32)],
        scratch_shapes=[pltpu.VMEM((tm, K), BF16)],
        compiler_params=_cparams("parallel", "arbitrary"),
        name="norm_matmul",
    )(x, g, w, wg)


def _out_proj_body(a1_ref, a2_ref, w1_ref, w2_ref, res_ref, g_ref, o_ref):
    y = _dot(a1_ref[...].astype(BF16), w1_ref[...]) + _dot(a2_ref[...].astype(BF16), w2_ref[...])
    o_ref[...] = res_ref[...] + _rms(y, g_ref[...])


def _out_proj(a1, a2, w1, w2, res, g, tm_pref=512):
    M, K1 = a1.shape
    K2 = a2.shape[1]
    D = w1.shape[1]
    tm = _tile(M, tm_pref)
    return pl.pallas_call(
        _out_proj_body,
        grid=(M // tm,),
        in_specs=[
            pl.BlockSpec((tm, K1), lambda i: (i, 0)),
            pl.BlockSpec((tm, K2), lambda i: (i, 0)),
            pl.BlockSpec((K1, D), lambda i: (0, 0)),
            pl.BlockSpec((K2, D), lambda i: (0, 0)),
            pl.BlockSpec((tm, D), lambda i: (i, 0)),
            pl.BlockSpec((1, D), lambda i: (0, 0)),
        ],
        out_specs=pl.BlockSpec((tm, D), lambda i: (i, 0)),
        out_shape=jax.ShapeDtypeStruct((M, D), F32),
        compiler_params=_cparams("parallel"),
        name="out_proj",
    )(a1, a2, w1, w2, res, g)


def _ffn_body(x_ref, g2_ref, g3_ref, w1_ref, w3_ref, w2_ref, o_ref, xn_ref, acc_ref):
    j = pl.program_id(1)

    @pl.when(j == 0)
    def _():
        xn_ref[...] = _rms(x_ref[...], g2_ref[...]).astype(BF16)
        acc_ref[...] = jnp.zeros_like(acc_ref)

    xn = xn_ref[...]
    h1 = _dot(xn, w1_ref[...])
    h3 = _dot(xn, w3_ref[...])
    h = (h1 * jax.nn.sigmoid(h1) * h3).astype(BF16)
    acc_ref[...] += _dot(h, w2_ref[...])

    @pl.when(j == pl.num_programs(1) - 1)
    def _():
        o_ref[...] = x_ref[...] + _rms(acc_ref[...], g3_ref[...])


def _ffn(x, g2, g3, w1, w3, w2, tm_pref=512, th_pref=512):
    M, D = x.shape
    H = w1.shape[1]
    tm = _tile(M, tm_pref)
    th = _tile(H, th_pref)
    return pl.pallas_call(
        _ffn_body,
        grid=(M // tm, H // th),
        in_specs=[
            pl.BlockSpec((tm, D), lambda i, j: (i, 0)),
            pl.BlockSpec((1, D), lambda i, j: (0, 0)),
            pl.BlockSpec((1, D), lambda i, j: (0, 0)),
            pl.BlockSpec((D, th), lambda i, j: (0, j)),
            pl.BlockSpec((D, th), lambda i, j: (0, j)),
            pl.BlockSpec((th, D), lambda i, j: (j, 0)),
        ],
        out_specs=pl.BlockSpec((tm, D), lambda i, j: (i, 0)),
        out_shape=jax.ShapeDtypeStruct((M, D), F32),
        scratch_shapes=[pltpu.VMEM((tm, D), BF16), pltpu.VMEM((tm, D), F32)],
        compiler_params=_cparams("parallel", "arbitrary"),
        name="ffn",
    )(x, g2, g3, w1, w3, w2)


def _pool_body(u_ref, prev_ref, st_ref, pw_ref, ps_ref, o_ref, ext_ref, *, tT, t0):
    t = pl.program_id(1)
    ext_ref[0:HALO, :] = jnp.where(t == 0, st_ref[...], prev_ref[...])
    ext_ref[HALO:HALO + tT, :] = u_ref[...]
    pos = t0 + t * tT + lax.broadcasted_iota(jnp.int32, (tT, 1), 0)
    for g, w in enumerate(POOL_WINDOWS):
        cs = slice(g * POOL_DG, (g + 1) * POOL_DG)
        x_new = ext_ref[HALO:HALO + tT, cs]
        tot = x_new
        for i in range(1, w):
            tot = tot + ext_ref[HALO - i:HALO - i + tT, cs]
        cnt = jnp.minimum(w, pos + 1).astype(F32)
        y = tot / cnt - x_new
        o_ref[:, cs] = _dot(y.astype(BF16), pw_ref[g]) * ps_ref[:, cs]


def _pool_mix(z3, st, pool_w, pool_scale, t0, tT_pref=512):
    N, T = z3.shape[:2]
    tT = _tile(T, tT_pref)
    nT = T // tT
    if nT > 1:
        assert tT % HALO == 0
        prev, prev_spec = z3, pl.BlockSpec((None, HALO, D_HALF), lambda n, t: (n, jnp.maximum(t * (tT // HALO) - 1, 0), 0))
    else:
        prev, prev_spec = st, pl.BlockSpec((None, HALO, D_HALF), lambda n, t: (n, 0, 0))
    return pl.pallas_call(
        functools.partial(_pool_body, tT=tT, t0=t0),
        grid=(N, nT),
        in_specs=[
            pl.BlockSpec((None, tT, D_HALF), lambda n, t: (n, t, 0)),
            prev_spec,
            pl.BlockSpec((None, HALO, D_HALF), lambda n, t: (n, 0, 0)),
            pl.BlockSpec((len(POOL_WINDOWS), POOL_DG, POOL_DG), lambda n, t: (0, 0, 0)),
            pl.BlockSpec((1, D_HALF), lambda n, t: (0, 0)),
        ],
        out_specs=pl.BlockSpec((None, tT, D_HALF), lambda n, t: (n, t, 0)),
        out_shape=jax.ShapeDtypeStruct((N, T, D_HALF), F32),
        scratch_shapes=[pltpu.VMEM((HALO + tT, D_HALF), F32)],
        compiler_params=_cparams("parallel", "arbitrary"),
        name="pool_mix",
    )(z3, prev, st, pool_w, pool_scale)


def _mlstm_body(q_ref, k_ref, v_ref, o_ref, zg_ref, gb_ref, mg_ref, c0_ref, n0_ref, m0_ref,
                y_ref, cN_ref, nN_ref, mN_ref, C_s, n_s, m_s, *, L, t_valid):
    c = pl.program_id(1)

    @pl.when(c == 0)
    def _():
        C_s[...] = c0_ref[...]
        n_s[...] = n0_ref[...]
        m_s[...] = m0_ref[...]

    a = GATE_CAP * jnp.tanh((zg_ref[...] + gb_ref[...]) / GATE_CAP)
    lane = lax.broadcasted_iota(jnp.int32, (L, LANE), 1)
    logsig = jnp.minimum(a, 0.0) - jnp.log1p(jnp.exp(-jnp.abs(a)))
    A = jnp.where(lane < MLSTM_HEADS, a, logsig)
    if t_valid < L:
        row = lax.broadcasted_iota(jnp.int32, (L, LANE), 0)
        A = jnp.where(row < t_valid, A, jnp.where(lane < MLSTM_HEADS, NEG, 0.0))
    r_i = lax.broadcasted_iota(jnp.int32, (L, L), 0)
    c_i = lax.broadcasted_iota(jnp.int32, (L, L), 1)
    causal = r_i >= c_i
    Bc = jnp.dot(causal.astype(F32), A, preferred_element_type=F32, precision=lax.Precision.HIGHEST)
    At = A.T
    Bt = Bc.T

    for h in range(MLSTM_HEADS):
        ks = slice(h * MLSTM_DK, (h + 1) * MLSTM_DK)
        vs = slice(h * MLSTM_DV, (h + 1) * MLSTM_DV)
        qh = q_ref[:, ks] * (MLSTM_DK ** -0.5)
        kh = k_ref[:, ks]
        vh = v_ref[:, vs].astype(BF16)
        b_c = Bc[:, MLSTM_HEADS + h:MLSTM_HEADS + h + 1]
        b_r = Bt[MLSTM_HEADS + h:MLSTM_HEADS + h + 1, :]
        li_c = A[:, h:h + 1]
        li_r = At[h:h + 1, :]
        m = m_s[h]
        Ch = C_s[h]
        nh = n_s[h]

        d = jnp.where(causal, b_c - b_r + li_r, NEG)
        inter = b_c + m
        m_t = jnp.maximum(inter, jnp.max(d, axis=-1, keepdims=True))
        w_in = jnp.exp(inter - m_t)
        qb = qh.astype(BF16)
        s = _dot_nt(qb, kh.astype(BF16)) * jnp.exp(d - m_t)
        num = w_in * _dot(qb, Ch.astype(BF16)) + _dot(s.astype(BF16), vh)
        den = w_in * jnp.sum(qh * nh, axis=-1, keepdims=True) + jnp.sum(s, axis=-1, keepdims=True)
        hh = num / jnp.maximum(jnp.abs(den), jnp.exp(-m_t))

        b_end = b_c[L - 1:L, :]
        g_c = b_end - b_c + li_c
        m_new = jnp.maximum(b_end + m, jnp.max(g_c, axis=0, keepdims=True))
        w_c = jnp.exp(b_end + m - m_new)
        kw = kh * jnp.exp(g_c - m_new)
        C_s[h] = w_c * Ch + lax.dot_general(kw.astype(BF16), vh, TN_DIMS, preferred_element_type=F32)
        n_s[h] = w_c * nh + jnp.sum(kw, axis=0, keepdims=True)
        m_s[h] = m_new

        hn = _rms(hh, mg_ref[:, vs])
        y_ref[:, vs] = jax.nn.sigmoid(o_ref[:, vs]) * hn

    @pl.when(c == pl.num_programs(1) - 1)
    def _():
        cN_ref[...] = C_s[...]
        nN_ref[...] = n_s[...]
        mN_ref[...] = m_s[...]


def _mlstm(z3, zg3, gate_b, mnorm_g, C0, n0, m0, L, t_valid):
    N, Tp = z3.shape[:2]
    nc = Tp // L
    H = MLSTM_HEADS
    qk_w = H * MLSTM_DK
    v_w = H * MLSTM_DV
    st = lambda n, c: (n, 0, 0, 0)
    outs = pl.pallas_call(
        functools.partial(_mlstm_body, L=L, t_valid=t_valid),
        grid=(N, nc),
        in_specs=[
            pl.BlockSpec((None, L, qk_w), lambda n, c: (n, c, D_HALF // qk_w)),
            pl.BlockSpec((None, L, qk_w), lambda n, c: (n, c, D_HALF // qk_w + 1)),
            pl.BlockSpec((None, L, v_w), lambda n, c: (n, c, 2)),
            pl.BlockSpec((None, L, v_w), lambda n, c: (n, c, 3)),
            pl.BlockSpec((None, L, LANE), lambda n, c: (n, c, 0)),
            pl.BlockSpec((1, LANE), lambda n, c: (0, 0)),
            pl.BlockSpec((1, v_w), lambda n, c: (0, 0)),
            pl.BlockSpec((None, H, MLSTM_DK, MLSTM_DV), st),
            pl.BlockSpec((None, H, 1, MLSTM_DK), st),
            pl.BlockSpec((None, H, 1, 1), st),
        ],
        out_specs=[
            pl.BlockSpec((None, L, v_w), lambda n, c: (n, c, 0)),
            pl.BlockSpec((None, H, MLSTM_DK, MLSTM_DV), st),
            pl.BlockSpec((None, H, 1, MLSTM_DK), st),
            pl.BlockSpec((None, H, 1, 1), st),
        ],
        out_shape=[
            jax.ShapeDtypeStruct((N, Tp, v_w), F32),
            jax.ShapeDtypeStruct((N, H, MLSTM_DK, MLSTM_DV), F32),
            jax.ShapeDtypeStruct((N, H, 1, MLSTM_DK), F32),
            jax.ShapeDtypeStruct((N, H, 1, 1), F32),
        ],
        scratch_shapes=[
            pltpu.VMEM((H, MLSTM_DK, MLSTM_DV), F32),
            pltpu.VMEM((H, 1, MLSTM_DK), F32),
            pltpu.VMEM((H, 1, 1), F32),
        ],
        compiler_params=_cparams("parallel", "arbitrary"),
        name="mlstm",
    )(z3, z3, z3, z3, zg3, gate_b, mnorm_g, C0, n0.reshape(N, H, 1, MLSTM_DK), m0.reshape(N, H, 1, 1))
    y, C, n, m = outs
    return y, C, n.reshape(N, H, MLSTM_DK), m.reshape(N, H)


def _gmlp_body(u_ref, v_ref, g_ref, ws_ref, bs_ref, y_ref, vn_ref, *, tT):
    v = v_ref[...]
    vc = v - jnp.mean(v, axis=-1, keepdims=True)
    vn = vc * lax.rsqrt(jnp.mean(vc * vc, axis=-1, keepdims=True) + 1e-5) * g_ref[...]
    vn_ref[...] = vn
    r_i = lax.broadcasted_iota(jnp.int32, (GMLP_CHUNK, GMLP_CHUNK), 0)
    c_i = lax.broadcasted_iota(jnp.int32, (GMLP_CHUNK, GMLP_CHUNK), 1)
    for g in range(GMLP_GROUPS):
        cs = slice(g * GMLP_DG, (g + 1) * GMLP_DG)
        wm = jnp.where(r_i >= c_i, ws_ref[g], 0.0).astype(BF16)
        bias = bs_ref[:, g:g + 1]
        for ch in range(tT // GMLP_CHUNK):
            rs = slice(ch * GMLP_CHUNK, (ch + 1) * GMLP_CHUNK)
            mix = _dot(wm, vn[rs, cs].astype(BF16)) + bias
            y_ref[rs, cs] = u_ref[rs, cs] * mix


def _gmlp(z3, gnorm_g, ws, bs_t, tT_pref=512):
    N, Tp = z3.shape[:2]
    tT = _tile(Tp, tT_pref)
    assert tT % GMLP_CHUNK == 0
    return pl.pallas_call(
        functools.partial(_gmlp_body, tT=tT),
        grid=(N, Tp // tT),
        in_specs=[
            pl.BlockSpec((None, tT, D_HALF), lambda n, t: (n, t, 0)),
            pl.BlockSpec((None, tT, D_HALF), lambda n, t: (n, t, 1)),
            pl.BlockSpec((1, D_HALF), lambda n, t: (0, 0)),
            pl.BlockSpec((GMLP_GROUPS, GMLP_CHUNK, GMLP_CHUNK), lambda n, t: (0, 0, 0)),
            pl.BlockSpec((GMLP_CHUNK, LANE), lambda n, t: (0, 0)),
        ],
        out_specs=[
            pl.BlockSpec((None, tT, D_HALF), lambda n, t: (n, t, 0)),
            pl.BlockSpec((None, tT, D_HALF), lambda n, t: (n, t, 0)),
        ],
        out_shape=[jax.ShapeDtypeStruct((N, Tp, D_HALF), F32), jax.ShapeDtypeStruct((N, Tp, D_HALF), F32)],
        compiler_params=_cparams("parallel", "parallel"),
        name="gmlp",
    )(z3, z3, gnorm_g, ws, bs_t)


PAGES_PER_STEP = 8
BLK_PER_PAGE = PAGE // CMP_STRIDE


def _compress_body(pt_ref, *refs):
    page_refs = refs[:PAGES_PER_STEP]
    wab_ref, a_ref, b_ref = refs[PAGES_PER_STEP:]
    wa = wab_ref[0]
    wb = wab_ref[1]
    for p, pr in enumerate(page_refs):
        x = pr[...].reshape(BLK_PER_PAGE, CMP_STRIDE, 4 * NSA_DH)
        rs = slice(p * BLK_PER_PAGE, (p + 1) * BLK_PER_PAGE)
        a_ref[rs, :] = jnp.sum(x * wa[None], axis=1)
        b_ref[rs, :] = jnp.sum(x * wb[None], axis=1)


def _compress_ab(pages, page_table, wab, col_block):
    N, n_pages = page_table.shape
    assert n_pages % PAGES_PER_STEP == 0
    W = 4 * NSA_DH
    rows = PAGES_PER_STEP * BLK_PER_PAGE
    page_specs = [
        pl.BlockSpec((None, PAGE, W), functools.partial(
            lambda n, s, pt, r: (pt[n, s * PAGES_PER_STEP + r], 0, col_block), r=r))
        for r in range(PAGES_PER_STEP)
    ]
    grid_spec = pltpu.PrefetchScalarGridSpec(
        num_scalar_prefetch=1,
        grid=(N, n_pages // PAGES_PER_STEP),
        in_specs=page_specs + [pl.BlockSpec((2, CMP_STRIDE, W), lambda n, s, pt: (0, 0, 0))],
        out_specs=[pl.BlockSpec((None, rows, W), lambda n, s, pt: (n, s, 0))] * 2,
    )
    ns = n_pages * BLK_PER_PAGE
    return pl.pallas_call(
        _compress_body,
        grid_spec=grid_spec,
        out_shape=[jax.ShapeDtypeStruct((N, ns, W), F32)] * 2,
        compiler_params=_cparams("parallel", "parallel"),
        name="nsa_compress",
    )(page_table, *([pages] * PAGES_PER_STEP), wab)


def _cmp_proj_body(a_ref, b_ref, w_ref, kc_ref, vc_ref, *, ns):
    b_next = pltpu.roll(b_ref[...], ns - 1, 0)
    row = lax.broadcasted_iota(jnp.int32, (ns, 1), 0)
    blk = jnp.where(row < ns - 1, a_ref[...] + b_next, 0.0).astype(BF16)
    w0 = w_ref[0].astype(BF16)
    w1 = w_ref[1].astype(BF16)
    for kv in range(NSA_KV):
        cs = slice(kv * NSA_DH, (kv + 1) * NSA_DH)
        kc_ref[:, cs] = _dot(blk[:, kv * NSA_DH:(kv + 1) * NSA_DH], w0)
        vc_ref[:, cs] = _dot(blk[:, (NSA_KV + kv) * NSA_DH:(NSA_KV + kv + 1) * NSA_DH], w1)


def _cmp_proj(a, b, w_cmp):
    N, ns, W = a.shape
    return pl.pallas_call(
        functools.partial(_cmp_proj_body, ns=ns),
        grid=(N,),
        in_specs=[
            pl.BlockSpec((None, ns, W), lambda n: (n, 0, 0)),
            pl.BlockSpec((None, ns, W), lambda n: (n, 0, 0)),
            pl.BlockSpec((2, NSA_DH, NSA_DH), lambda n: (0, 0, 0)),
        ],
        out_specs=[pl.BlockSpec((None, ns, NSA_KV * NSA_DH), lambda n: (n, 0, 0))] * 2,
        out_shape=[jax.ShapeDtypeStruct((N, ns, NSA_KV * NSA_DH), F32)] * 2,
        compiler_params=_cparams("parallel"),
        name="nsa_cmp_proj",
    )(a, b, w_cmp)


def _split3(x):
    hi = x.astype(BF16)
    r1 = x - hi.astype(F32)
    mid = r1.astype(BF16)
    lo = (r1 - mid.astype(F32)).astype(BF16)
    return hi, mid, lo


def _cmp_to_sel(p_sum, a_mat):
    hi, mid, lo = _split3(p_sum)
    return _dot(hi, a_mat) + _dot(mid, a_mat) + _dot(lo, a_mat)


def _topk_mask(score, n_valid, k):
    lane = lax.broadcasted_iota(jnp.int32, score.shape, 1)
    rank = jnp.zeros(score.shape, F32)
    for jp in range(n_valid):
        col = score[:, jp:jp + 1]
        beats = (col > score) | ((col == score) & (lane > jp))
        rank = rank + beats.astype(F32)
    return ((rank < k) & (lane < n_valid)).astype(F32)


def _sel_scores(imp, qblk, n_valid):
    lane = lax.broadcasted_iota(jnp.int32, imp.shape, 1)
    forced = (lane == 0) | (lane == qblk) | (lane == qblk - 1)
    score = jnp.where(forced, FORCE_SCORE, jnp.where(lane > qblk, -1.0, imp))
    return jnp.where(lane < n_valid, score, -2.0)


def _masked_softmax_rows(s, mask):
    s = jnp.where(mask, s, NEG)
    e = jnp.exp(s - jnp.max(s, axis=-1, keepdims=True))
    return jnp.where(mask, e / jnp.sum(e, axis=-1, keepdims=True), 0.0)


def _stack_heads(q, scale):
    return (jnp.concatenate([q[:, g * NSA_DH:(g + 1) * NSA_DH] for g in range(NSA_G)], axis=0) * scale).astype(BF16)


def _gate_col(gates, idx):
    lane = lax.broadcasted_iota(jnp.int32, gates.shape, 1)
    return jnp.sum(jnp.where(lane == idx, gates, 0.0), axis=-1, keepdims=True)


SEL_TK = 256
WIN_KEYS = WINDOW + QBLK


def _nsa_prompt_body(q_ref, zg_ref, gb_ref, kc_ref, vc_ref, ks_ref, vs_ref, kw_ref, vw_ref, amat_ref,
                     o_ref, ks_bf, vs_bf, kw_bf, vw_bf, kc_bf, vc_bf, *, T, n_sel, nsb):
    kv = pl.program_id(1)
    qb = pl.program_id(2)

    @pl.when(qb == 0)
    def _():
        ks_bf[...] = ks_ref[...].astype(BF16)
        vs_bf[...] = vs_ref[...].astype(BF16)
        kw_bf[...] = kw_ref[...].astype(BF16)
        vw_bf[...] = vw_ref[...].astype(BF16)
        kc_bf[...] = kc_ref[...].astype(BF16)
        vc_bf[...] = vc_ref[...].astype(BF16)

    R = QBLK
    start = qb * R
    qs = _stack_heads(q_ref[...], NSA_DH ** -0.5)
    pos = start + lax.broadcasted_iota(jnp.int32, (R, 1), 0)
    rep = lambda x: jnp.concatenate([x] * NSA_G, axis=0)
    pos4 = rep(pos)

    ns = kc_bf.shape[0]
    cmp_end = (lax.broadcasted_iota(jnp.int32, (NSA_G * R, ns), 1) + 2) * CMP_STRIDE - 1
    p_c = _masked_softmax_rows(_dot_nt(qs, kc_bf[...]), cmp_end <= pos4)
    o_cmp = _dot(p_c.astype(BF16), vc_bf[...])
    p_sum = p_c[0:R] + p_c[R:2 * R] + p_c[2 * R:3 * R] + p_c[3 * R:4 * R]

    imp = _cmp_to_sel(p_sum, amat_ref[...])
    sel = _topk_mask(_sel_scores(imp, pos // SEL_BLOCK, n_sel), n_sel, N_SEL).astype(BF16)

    blk_row = lax.broadcasted_iota(jnp.int32, (nsb, SEL_TK), 0)
    tok_col = lax.broadcasted_iota(jnp.int32, (nsb, SEL_TK), 1)
    lane_tk = lax.broadcasted_iota(jnp.int32, (NSA_G * R, SEL_TK), 1)

    def sel_step(kt, carry):
        m_prev, l_prev, acc = carry
        k0 = pl.multiple_of(kt * SEL_TK, SEL_TK)
        s = _dot_nt(qs, ks_bf[pl.ds(k0, SEL_TK), :])
        expand = (blk_row == (k0 + tok_col) // SEL_BLOCK).astype(BF16)
        mask = (rep(_dot(sel, expand)) > 0.5) & (k0 + lane_tk <= pos4)
        s = jnp.where(mask, s, NEG)
        m_new = jnp.maximum(m_prev, jnp.max(s, axis=-1, keepdims=True))
        alpha = jnp.exp(m_prev - m_new)
        p = jnp.where(mask, jnp.exp(s - m_new), 0.0)
        l_new = alpha * l_prev + jnp.sum(p, axis=-1, keepdims=True)
        acc = alpha * acc + _dot(p.astype(BF16), vs_bf[pl.ds(k0, SEL_TK), :])
        return m_new, l_new, acc

    n_steps = (start + R - 1) // SEL_TK + 1
    init = (jnp.full((NSA_G * R, 1), NEG, F32), jnp.zeros((NSA_G * R, 1), F32), jnp.zeros((NSA_G * R, NSA_DH), F32))
    _, l_s, acc_s = lax.fori_loop(0, n_steps, sel_step, init)
    o_sel = acc_s / l_s

    w0 = pl.multiple_of(jnp.maximum(start - WINDOW, 0), QBLK)
    tok_w = w0 + lax.broadcasted_iota(jnp.int32, (NSA_G * R, WIN_KEYS), 1)
    m_w = (tok_w <= pos4) & (tok_w > pos4 - WINDOW)
    p_w = _masked_softmax_rows(_dot_nt(qs, kw_bf[pl.ds(w0, WIN_KEYS), :]), m_w)
    o_win = _dot(p_w.astype(BF16), vw_bf[pl.ds(w0, WIN_KEYS), :])

    gates = jax.nn.sigmoid(zg_ref[...] + gb_ref[...])
    for g in range(NSA_G):
        rs = slice(g * R, (g + 1) * R)
        head = kv * NSA_G + g
        o_ref[:, g * NSA_DH:(g + 1) * NSA_DH] = (
            _gate_col(gates, head) * o_cmp[rs]
            + _gate_col(gates, NSA_HEADS + head) * o_sel[rs]
            + _gate_col(gates, 2 * NSA_HEADS + head) * o_win[rs])


def _sel_map(ns, n_sel, nsb):
    i = np.arange(ns)[:, None]
    j = np.arange(nsb)[None, :]
    r = SEL_BLOCK // CMP_STRIDE
    return jnp.asarray(((i >= r * j - 1) & (i <= r * j + r - 1) & (j < n_sel)).astype(np.float32), BF16)


def _nsa_prompt(z3, zg3, gate_b, kc, vc):
    N, T = z3.shape[:2]
    assert T % SEL_TK == 0 and T >= WIN_KEYS
    ns = kc.shape[1]
    n_sel = T // SEL_BLOCK
    nsb = -(-n_sel // LANE) * LANE
    amat = _sel_map(ns, n_sel, nsb)
    qcol = 2 * D_HALF // (NSA_G * NSA_DH)
    kvs_col = (3 * D_HALF + 4 * NSA_DH) // NSA_DH
    kvw_col = kvs_col + 4
    full = lambda off: pl.BlockSpec((None, T, NSA_DH), lambda n, kv, qb: (n, 0, off + kv))
    return pl.pallas_call(
        functools.partial(_nsa_prompt_body, T=T, n_sel=n_sel, nsb=nsb),
        grid=(N, NSA_KV, T // QBLK),
        in_specs=[
            pl.BlockSpec((None, QBLK, NSA_G * NSA_DH), lambda n, kv, qb: (n, qb, qcol + kv)),
            pl.BlockSpec((None, QBLK, LANE), lambda n, kv, qb: (n, qb, 0)),
            pl.BlockSpec((1, LANE), lambda n, kv, qb: (0, 0)),
            pl.BlockSpec((None, ns, NSA_DH), lambda n, kv, qb: (n, 0, kv)),
            pl.BlockSpec((None, ns, NSA_DH), lambda n, kv, qb: (n, 0, kv)),
            full(kvs_col), full(kvs_col + NSA_KV), full(kvw_col), full(kvw_col + NSA_KV),
            pl.BlockSpec((ns, nsb), lambda n, kv, qb: (0, 0)),
        ],
        out_specs=pl.BlockSpec((None, QBLK, NSA_G * NSA_DH), lambda n, kv, qb: (n, qb, kv)),
        out_shape=jax.ShapeDtypeStruct((N, T, NSA_HEADS * NSA_DH), F32),
        scratch_shapes=[pltpu.VMEM((T, NSA_DH), BF16)] * 4 + [pltpu.VMEM((ns, NSA_DH), BF16)] * 2,
        compiler_params=_cparams("parallel", "parallel", "arbitrary"),
        name="nsa_prompt",
    )(z3, zg3, gate_b, kc, vc, z3, z3, z3, z3, amat)


def _nsa_sample_a_body(q_ref, kc_ref, vc_ref, kw_ref, vw_ref, amat_ref, ocmp_ref, owin_ref, sel_ref,
                       *, Tq, past_len, n_sel, wb):
    qs = _stack_heads(q_ref[...], NSA_DH ** -0.5)
    pos = past_len + lax.broadcasted_iota(jnp.int32, (Tq, 1), 0)
    pos4 = jnp.concatenate([pos] * NSA_G, axis=0)

    ns = kc_ref.shape[0]
    cmp_i = lax.broadcasted_iota(jnp.int32, (NSA_G * Tq, ns), 1)
    m_c = ((cmp_i + 2) * CMP_STRIDE - 1 <= pos4) & (cmp_i < ns - 1)
    p_c = _masked_softmax_rows(_dot_nt(qs, kc_ref[...].astype(BF16)), m_c)
    ocmp_ref[...] = _dot(p_c.astype(BF16), vc_ref[...].astype(BF16))
    p_sum = p_c[0:Tq] + p_c[Tq:2 * Tq] + p_c[2 * Tq:3 * Tq] + p_c[3 * Tq:4 * Tq]
    imp = _cmp_to_sel(p_sum, amat_ref[...])
    sel_ref[...] = _topk_mask(_sel_scores(imp, pos // SEL_BLOCK, n_sel), n_sel, N_SEL)

    nw = kw_ref.shape[0]
    tok_w = past_len - wb + lax.broadcasted_iota(jnp.int32, (NSA_G * Tq, nw), 1)
    m_w = (tok_w >= 0) & (tok_w <= pos4) & (tok_w > pos4 - WINDOW)
    p_w = _masked_softmax_rows(_dot_nt(qs, kw_ref[...].astype(BF16)), m_w)
    owin_ref[...] = _dot(p_w.astype(BF16), vw_ref[...].astype(BF16))


def _nsa_sample_a(z3, kc, vc, kw_full, past_len, wb):
    N, Tq = z3.shape[:2]
    ns = kc.shape[1]
    nw = kw_full.shape[1]
    n_sel = -(-(past_len + Tq) // SEL_BLOCK)
    nsb = -(-n_sel // LANE) * LANE
    amat = _sel_map(ns, n_sel, nsb)
    qcol = 2 * D_HALF // (NSA_G * NSA_DH)
    R4 = NSA_G * Tq
    return pl.pallas_call(
        functools.partial(_nsa_sample_a_body, Tq=Tq, past_len=past_len, n_sel=n_sel, wb=wb),
        grid=(N, NSA_KV),
        in_specs=[
            pl.BlockSpec((None, Tq, NSA_G * NSA_DH), lambda n, kv: (n, 0, qcol + kv)),
            pl.BlockSpec((None, ns, NSA_DH), lambda n, kv: (n, 0, kv)),
            pl.BlockSpec((None, ns, NSA_DH), lambda n, kv: (n, 0, kv)),
            pl.BlockSpec((None, nw, NSA_DH), lambda n, kv: (n, 0, kv)),
            pl.BlockSpec((None, nw, NSA_DH), lambda n, kv: (n, 0, NSA_KV + kv)),
            pl.BlockSpec((ns, nsb), lambda n, kv: (0, 0)),
        ],
        out_specs=[
            pl.BlockSpec((None, None, R4, NSA_DH), lambda n, kv: (n, kv, 0, 0)),
            pl.BlockSpec((None, None, R4, NSA_DH), lambda n, kv: (n, kv, 0, 0)),
            pl.BlockSpec((None, None, Tq, nsb), lambda n, kv: (n, kv, 0, 0)),
        ],
        out_shape=[
            jax.ShapeDtypeStruct((N, NSA_KV, R4, NSA_DH), F32),
            jax.ShapeDtypeStruct((N, NSA_KV, R4, NSA_DH), F32),
            jax.ShapeDtypeStruct((N, NSA_KV, Tq, nsb), F32),
        ],
        compiler_params=_cparams("parallel", "parallel"),
        name="nsa_sample_cmp_win",
    )(z3, kc, vc, kw_full, kw_full, amat)


def _nsa_sample_b_body(pt_ref, *refs, Tq, past_len, nsb):
    page_refs = refs[:PAGES_PER_STEP]
    (q_ref, new_ref, sel_ref, ocmp_ref, owin_ref, zg_ref, gb_ref, o_ref, m_s, l_s, acc_s) = refs[PAGES_PER_STEP:]
    s_id = pl.program_id(1)
    R4 = NSA_G * Tq
    rep = lambda x: jnp.concatenate([x] * NSA_G, axis=0)
    pos = past_len + lax.broadcasted_iota(jnp.int32, (Tq, 1), 0)

    @pl.when(s_id == 0)
    def _():
        m_s[...] = jnp.full(m_s.shape, NEG, F32)
        l_s[...] = jnp.zeros(l_s.shape, F32)
        acc_s[...] = jnp.zeros(acc_s.shape, F32)

    def update(kv, qs, k, v, mask):
        s = jnp.where(mask, _dot_nt(qs, k), NEG)
        m_prev = m_s[kv]
        m_new = jnp.maximum(m_prev, jnp.max(s, axis=-1, keepdims=True))
        alpha = jnp.exp(m_prev - m_new)
        p = jnp.where(mask, jnp.exp(s - m_new), 0.0)
        l_s[kv] = alpha * l_s[kv] + jnp.sum(p, axis=-1, keepdims=True)
        acc_s[kv] = alpha * acc_s[kv] + _dot(p.astype(BF16), v)
        m_s[kv] = m_new

    qs = [_stack_heads(q_ref[:, kv * NSA_G * NSA_DH:(kv + 1) * NSA_G * NSA_DH], NSA_DH ** -0.5) for kv in range(NSA_KV)]
    sel = [sel_ref[kv].astype(BF16) for kv in range(NSA_KV)]
    blk_row = lax.broadcasted_iota(jnp.int32, (nsb, PAGE), 0)
    tok_col = lax.broadcasted_iota(jnp.int32, (nsb, PAGE), 1)
    for p, pr in enumerate(page_refs):
        tok0 = (s_id * PAGES_PER_STEP + p) * PAGE
        expand = (blk_row == (tok0 + tok_col) // SEL_BLOCK).astype(BF16)
        page = pr[...].astype(BF16)
        for kv in range(NSA_KV):
            mask = rep(_dot(sel[kv], expand)) > 0.5
            update(kv, qs[kv], page[:, kv * NSA_DH:(kv + 1) * NSA_DH],
                   page[:, (NSA_KV + kv) * NSA_DH:(NSA_KV + kv + 1) * NSA_DH], mask)

    @pl.when(s_id == pl.num_programs(1) - 1)
    def _():
        new = new_ref[...].astype(BF16)
        tok_n = past_len + lax.broadcasted_iota(jnp.int32, (R4, Tq), 1)
        gates = jax.nn.sigmoid(zg_ref[...] + gb_ref[...])
        for kv in range(NSA_KV):
            update(kv, qs[kv], new[:, kv * NSA_DH:(kv + 1) * NSA_DH],
                   new[:, (NSA_KV + kv) * NSA_DH:(NSA_KV + kv + 1) * NSA_DH], tok_n <= rep(pos))
            o_sel = acc_s[kv] / l_s[kv]
            o_cmp = ocmp_ref[kv]
            o_win = owin_ref[kv]
            for g in range(NSA_G):
                rs = slice(g * Tq, (g + 1) * Tq)
                head = kv * NSA_G + g
                o_ref[:, head * NSA_DH:(head + 1) * NSA_DH] = (
                    gates[:, head:head + 1] * o_cmp[rs]
                    + gates[:, NSA_HEADS + head:NSA_HEADS + head + 1] * o_sel[rs]
                    + gates[:, 2 * NSA_HEADS + head:2 * NSA_HEADS + head + 1] * o_win[rs])


def _nsa_sample_b(z3, zg3, gate_b, pool_sel, page_table, sel, o_cmp, o_win, past_len):
    N, Tq = z3.shape[:2]
    n_pages = page_table.shape[1]
    assert n_pages % PAGES_PER_STEP == 0 and past_len % SEL_BLOCK == 0
    nsb = sel.shape[-1]
    W = 4 * NSA_DH
    R4 = NSA_G * Tq
    qcol = 2 * D_HALF // (NSA_HEADS * NSA_DH)
    kvs_col = (3 * D_HALF + W) // W
    page_specs = [
        pl.BlockSpec((None, PAGE, W), functools.partial(
            lambda n, s, pt, r: (pt[n, s * PAGES_PER_STEP + r], 0, 0), r=r))
        for r in range(PAGES_PER_STEP)
    ]
    per_n4 = lambda shape: pl.BlockSpec((None,) + shape, lambda n, s, pt: (n, 0, 0, 0))
    grid_spec = pltpu.PrefetchScalarGridSpec(
        num_scalar_prefetch=1,
        grid=(N, n_pages // PAGES_PER_STEP),
        in_specs=page_specs + [
            pl.BlockSpec((None, Tq, NSA_HEADS * NSA_DH), lambda n, s, pt: (n, 0, qcol)),
            pl.BlockSpec((None, Tq, W), lambda n, s, pt: (n, 0, kvs_col)),
            per_n4((NSA_KV, Tq, nsb)),
            per_n4((NSA_KV, R4, NSA_DH)),
            per_n4((NSA_KV, R4, NSA_DH)),
            pl.BlockSpec((None, Tq, LANE), lambda n, s, pt: (n, 0, 0)),
            pl.BlockSpec((1, LANE), lambda n, s, pt: (0, 0)),
        ],
        out_specs=pl.BlockSpec((None, Tq, NSA_HEADS * NSA_DH), lambda n, s, pt: (n, 0, 0)),
        scratch_shapes=[
            pltpu.VMEM((NSA_KV, R4, 1), F32),
            pltpu.VMEM((NSA_KV, R4, 1), F32),
            pltpu.VMEM((NSA_KV, R4, NSA_DH), F32),
        ],
    )
    return pl.pallas_call(
        functools.partial(_nsa_sample_b_body, Tq=Tq, past_len=past_len, nsb=nsb),
        grid_spec=grid_spec,
        out_shape=jax.ShapeDtypeStruct((N, Tq, NSA_HEADS * NSA_DH), F32),
        compiler_params=_cparams("parallel", "arbitrary"),
        name="nsa_sample_sel",
    )(page_table, *([pool_sel] * PAGES_PER_STEP), z3, z3, sel, o_cmp, o_win, zg3, gate_b)


MLSTM_CHUNK_PROMPT = 256
SAMPLE_PAD = 128


def _pad_rows(x, rows):
    return jnp.pad(x, ((0, 0), (0, rows - x.shape[1]), (0, 0)))


def _lane_pad(v):
    return jnp.pad(v.astype(F32), (0, LANE - v.shape[0])).reshape(1, LANE)


def _split_w_in(w_in, n_main):
    wg = jnp.pad(w_in[:, n_main:], ((0, 0), (0, LANE - (w_in.shape[1] - n_main))))
    return w_in[:, :n_main].astype(BF16), wg.astype(BF16)


def _even_layer(x, t0, pool_buf, C0, n0, m0, w, ffn):
    N, T, D = x.shape
    n_main = 4 * D_HALF
    z, zg = _norm_matmul(x.reshape(N * T, D), w["ng"][0:1], w["w_main"], w["w_gate"])
    z3 = z.reshape(N, T, n_main)
    zg3 = zg.reshape(N, T, LANE)

    st = jnp.pad(pool_buf, ((0, 0), (HALO - POOL_PAD, 0), (0, 0)))
    y_a = _pool_mix(z3, st, w["pool_w"], w["pool_scale"], t0)
    u_ext_tail = jnp.concatenate([pool_buf, z3[:, :, :D_HALF]], axis=1)[:, -POOL_PAD:] if T < POOL_PAD else z3[:, -POOL_PAD:, :D_HALF]

    if T % MLSTM_CHUNK_PROMPT == 0:
        L, zm, zgm = MLSTM_CHUNK_PROMPT, z3, zg3
    else:
        L, zm, zgm = SAMPLE_PAD, _pad_rows(z3, SAMPLE_PAD), _pad_rows(zg3, SAMPLE_PAD)
    y_b, C, n, m = _mlstm(zm, zgm, w["gate_b"], w["mnorm_g"], C0, n0, m0, L, min(T, L))
    y_b = y_b[:, :T]

    x2 = _out_proj(y_a.reshape(N * T, D_HALF), y_b.reshape(N * T, D_HALF), w["w_out_a"], w["w_out_b"],
                   x.reshape(N * T, D), w["ng"][1:2])
    x3 = _ffn(x2, w["ng"][2:3], w["ng"][3:4], *ffn)
    return x3.reshape(N, T, D), u_ext_tail, C, n, m


def _odd_layer(x, w, ffn, past=None):
    N, T, D = x.shape
    n_main = 4 * D_HALF + D_HALF // 2
    z, zg = _norm_matmul(x.reshape(N * T, D), w["ng"][0:1], w["w_main"], w["w_gate"])
    z3 = z.reshape(N, T, n_main)
    zg3 = zg.reshape(N, T, LANE)
    W = 4 * NSA_DH
    kvc = z3[:, :, 3 * D_HALF:3 * D_HALF + W]
    kvs = z3[:, :, 3 * D_HALF + W:3 * D_HALF + 2 * W]
    kvw = z3[:, :, 3 * D_HALF + 2 * W:3 * D_HALF + 3 * W]

    if past is None:
        y_c, _ = _gmlp(z3, w["gnorm_g"], w["ws"], w["bs_t"])
        vn = None
        pt = jnp.arange(N * (T // PAGE), dtype=jnp.int32).reshape(N, T // PAGE)
        a, b = _compress_ab(z3.reshape(N * (T // PAGE), PAGE, n_main), pt, w["wab"], 3 * D_HALF // W)
        kc, vc = _cmp_proj(a, b, w["cmp_w"])
        o = _nsa_prompt(z3, zg3, w["nsa_gate_b"], kc, vc)
        win_state = kvw[:, -min(WINDOW, T):]
    else:
        win_buf, pool_cmp, pool_sel, page_table = past
        past_len = page_table.shape[1] * PAGE
        wb = win_buf.shape[1]
        y_c, vn = _gmlp(_pad_rows(z3[:, :, :2 * D_HALF], SAMPLE_PAD), w["gnorm_g"], w["ws"], w["bs_t"])
        y_c, vn = y_c[:, :T], vn[:, :T]
        assert (past_len + T) // CMP_STRIDE == past_len // CMP_STRIDE
        a, b = _compress_ab(pool_cmp.reshape(pool_cmp.shape[0], PAGE, W), page_table, w["wab"], 0)
        kc, vc = _cmp_proj(a, b, w["cmp_w"])
        kw_all = jnp.concatenate([win_buf.reshape(N, wb, W), kvw], axis=1)
        nw = -(-(wb + T) // LANE) * LANE
        o_cmp, o_win, sel = _nsa_sample_a(z3, kc, vc, _pad_rows(kw_all, nw), past_len, wb)
        o = _nsa_sample_b(z3, zg3, w["nsa_gate_b"], pool_sel.reshape(pool_sel.shape[0], PAGE, W), page_table,
                          sel, o_cmp, o_win, past_len)
        win_state = kw_all[:, -wb:]

    x2 = _out_proj(y_c.reshape(N * T, D_HALF), o.reshape(N * T, D_HALF), w["w_out_a"], w["w_out_b"],
                   x.reshape(N * T, D), w["ng"][1:2])
    x3 = _ffn(x2, w["ng"][2:3], w["ng"][3:4], *ffn)
    kv5 = lambda t: t.reshape(N, t.shape[1], 2, NSA_KV, NSA_DH)
    return x3.reshape(N, T, D), vn, kv5(kvc), kv5(kvs), kv5(win_state)


def kernel(x_prompt, x_sample, state_pool, state_mlstm_c, state_mlstm_n, state_mlstm_m, state_win_kv, cache_cmp_kv, cache_sel_kv, page_table, norm_g, w_in_even, w_out_even, pool_w, pool_scale, mlstm_gate_b, mlstm_norm_g, w_in_odd, w_out_odd, gmlp_norm_g, gmlp_ws, gmlp_bs, nsa_cmp_pos, nsa_cmp_w, nsa_gate_b, ffn_w1, ffn_w3, ffn_w2):
    B = x_prompt.shape[0]
    depth = norm_g.shape[0]
    past_len = page_table.shape[1] * PAGE
    xp, xs = x_prompt, x_sample
    pool_p, pool_s, c_p, c_s, n_p, n_s, m_p, m_s = [], [], [], [], [], [], [], []
    gv_s, cmp_p, cmp_s, sel_p, sel_s, win_p, win_s = [], [], [], [], [], [], []
    for l in range(depth):
        j = l // 2
        ffn = (ffn_w1[l].astype(BF16), ffn_w3[l].astype(BF16), ffn_w2[l].astype(BF16))
        if l % 2 == 0:
            w_main, w_gate = _split_w_in(w_in_even[j], 4 * D_HALF)
            w_out = w_out_even[j].astype(BF16)
            w = dict(ng=norm_g[l], w_main=w_main, w_gate=w_gate, w_out_a=w_out[:D_HALF], w_out_b=w_out[D_HALF:],
                     pool_w=pool_w[j].astype(BF16), pool_scale=pool_scale[j].reshape(1, D_HALF),
                     gate_b=_lane_pad(mlstm_gate_b[j].reshape(-1)), mnorm_g=mlstm_norm_g[j].reshape(1, D_HALF))
            zp = jnp.zeros((B, POOL_PAD, D_HALF), F32)
            zc = jnp.zeros((B, MLSTM_HEADS, MLSTM_DK, MLSTM_DV), F32)
            zn = jnp.zeros((B, MLSTM_HEADS, MLSTM_DK), F32)
            zm = jnp.zeros((B, MLSTM_HEADS), F32)
            xp, pb, c, n, m = _even_layer(xp, 0, zp, zc, zn, zm, w, ffn)
            pool_p.append(pb); c_p.append(c); n_p.append(n); m_p.append(m)
            xs, pb, c, n, m = _even_layer(xs, past_len, state_pool[j], state_mlstm_c[j], state_mlstm_n[j],
                                          state_mlstm_m[j], w, ffn)
            pool_s.append(pb); c_s.append(c); n_s.append(n); m_s.append(m)
        else:
            w_main, w_gate = _split_w_in(w_in_odd[j], 4 * D_HALF + D_HALF // 2)
            w_out = w_out_odd[j].astype(BF16)
            cp = nsa_cmp_pos[j]
            wcol = jnp.repeat(cp, NSA_KV * NSA_DH, axis=1)
            wab = jnp.stack([wcol[:CMP_STRIDE], wcol[CMP_STRIDE:]])
            w = dict(ng=norm_g[l], w_main=w_main, w_gate=w_gate, w_out_a=w_out[:D_HALF], w_out_b=w_out[D_HALF:],
                     gnorm_g=gmlp_norm_g[j].reshape(1, D_HALF), ws=gmlp_ws[j],
                     bs_t=jnp.pad(gmlp_bs[j].T, ((0, 0), (0, LANE - GMLP_GROUPS))),
                     wab=wab, cmp_w=nsa_cmp_w[j], nsa_gate_b=_lane_pad(nsa_gate_b[j]))
            xp, _, kc, ksl, wv = _odd_layer(xp, w, ffn)
            cmp_p.append(kc); sel_p.append(ksl); win_p.append(wv)
            xs, vn, kc, ksl, wv = _odd_layer(xs, w, ffn, past=(state_win_kv[j], cache_cmp_kv[j], cache_sel_kv[j], page_table))
            gv_s.append(vn); cmp_s.append(kc); sel_s.append(ksl); win_s.append(wv)
    st = jnp.stack
    return (xp, xs, st(pool_p), st(pool_s), st(c_p), st(c_s), st(n_p), st(n_s), st(m_p), st(m_s),
            st(gv_s), st(cmp_p), st(cmp_s), st(sel_p), st(sel_s), st(win_p), st(win_s))
```

```python
import functools
import math

import numpy as np
import jax
import jax.numpy as jnp
from jax import lax
from jax.experimental import pallas as pl
from jax.experimental.pallas import tpu as pltpu

F32 = jnp.float32
BF16 = jnp.bfloat16
NEG = -1e30

D_HALF = 1024
POOL_WINDOWS = (2, 4, 8, 16)
POOL_DG = 256
POOL_PAD = 15
HALO = 16
MLSTM_HEADS = 4
MLSTM_DK = 128
MLSTM_DV = 256
GATE_CAP = 15.0
GMLP_CHUNK = 128
GMLP_GROUPS = 4
GMLP_DG = 256
NSA_HEADS = 8
NSA_DH = 128
NSA_KV = 2
NSA_G = 4
CMP_STRIDE = 16
SEL_BLOCK = 64
N_SEL = 16
WINDOW = 512
QBLK = 128
PAGE = 128
FORCE_SCORE = 1e9
LANE = 128
VMEM_LIMIT = 56 * 1024 * 1024

NT_DIMS = (((1,), (1,)), ((), ()))
TN_DIMS = (((0,), (0,)), ((), ()))


def _cparams(*sem):
    return pltpu.CompilerParams(dimension_semantics=sem, vmem_limit_bytes=VMEM_LIMIT)


def _tile(m, pref):
    if m <= pref:
        return m
    for t in range(pref, 7, -1):
        if m % t == 0 and t % 8 == 0:
            return t
    return m


def _dot(a, b):
    return jnp.dot(a, b, preferred_element_type=F32)


def _dot_nt(a, b):
    return lax.dot_general(a, b, NT_DIMS, preferred_element_type=F32)


def _rms(x, g, eps=1e-6):
    return x * lax.rsqrt(jnp.mean(x * x, axis=-1, keepdims=True) + eps) * g


def _norm_mm_body(x_ref, g_ref, w_ref, wg_ref, o_ref, og_ref, xn_ref):
    @pl.when(pl.program_id(1) == 0)
    def _():
        xn = _rms(x_ref[...], g_ref[...]).astype(BF16)
        xn_ref[...] = xn
        og_ref[...] = _dot(xn, wg_ref[...])

    o_ref[...] = _dot(xn_ref[...], w_ref[...])


def _norm_matmul(x, g, w, wg, tm_pref=1024, tn_pref=512):
    M, K = x.shape
    N = w.shape[1]
    tm = _tile(M, tm_pref)
    tn = _tile(N, tn_pref)
    return pl.pallas_call(
        _norm_mm_body,
        grid=(M // tm, N // tn),
        in_specs=[
            pl.BlockSpec((tm, K), lambda i, j: (i, 0)),
            pl.BlockSpec((1, K), lambda i, j: (0, 0)),
            pl.BlockSpec((K, tn), lambda i, j: (0, j)),
            pl.BlockSpec((K, LANE), lambda i, j: (0, 0)),
        ],
        out_specs=[
            pl.BlockSpec((tm, tn), lambda i, j: (i, j)),
            pl.BlockSpec((tm, LANE), lambda i, j: (i, 0)),
        ],
        out_shape=[jax.ShapeDtypeStruct((M, N), F32), jax.ShapeDtypeStruct((M, LANE), F32)],
        scratch_shapes=[pltpu.VMEM((tm, K), BF16)],
        compiler_params=_cparams("parallel", "arbitrary"),
        name="norm_matmul",
    )(x, g, w, wg)


def _out_proj_body(a1_ref, a2_ref, w1_ref, w2_ref, res_ref, g_ref, o_ref):
    y = _dot(a1_ref[...].astype(BF16), w1_ref[...]) + _dot(a2_ref[...].astype(BF16), w2_ref[...])
    o_ref[...] = res_ref[...] + _rms(y, g_ref[...])


def _out_proj(a1, a2, w1, w2, res, g, tm_pref=512):
    M, K1 = a1.shape
    K2 = a2.shape[1]
    D = w1.shape[1]
    tm = _tile(M, tm_pref)
    return pl.pallas_call(
        _out_proj_body,
        grid=(M // tm,),
        in_specs=[
            pl.BlockSpec((tm, K1), lambda i: (i, 0)),
            pl.BlockSpec((tm, K2), lambda i: (i, 0)),
            pl.BlockSpec((K1, D), lambda i: (0, 0)),
            pl.BlockSpec((K2, D), lambda i: (0, 0)),
            pl.BlockSpec((tm, D), lambda i: (i, 0)),
            pl.BlockSpec((1, D), lambda i: (0, 0)),
        ],
        out_specs=pl.BlockSpec((tm, D), lambda i: (i, 0)),
        out_shape=jax.ShapeDtypeStruct((M, D), F32),
        compiler_params=_cparams("parallel"),
        name="out_proj",
    )(a1, a2, w1, w2, res, g)


def _ffn_body(x_ref, g2_ref, g3_ref, w1_ref, w3_ref, w2_ref, o_ref, xn_ref, acc_ref):
    j = pl.program_id(1)

    @pl.when(j == 0)
    def _():
        xn_ref[...] = _rms(x_ref[...], g2_ref[...]).astype(BF16)
        acc_ref[...] = jnp.zeros_like(acc_ref)

    xn = xn_ref[...]
    h1 = _dot(xn, w1_ref[...])
    h3 = _dot(xn, w3_ref[...])
    h = (h1 * jax.nn.sigmoid(h1) * h3).astype(BF16)
    acc_ref[...] += _dot(h, w2_ref[...])

    @pl.when(j == pl.num_programs(1) - 1)
    def _():
        o_ref[...] = x_ref[...] + _rms(acc_ref[...], g3_ref[...])


def _ffn(x, g2, g3, w1, w3, w2, tm_pref=512, th_pref=512):
    M, D = x.shape
    H = w1.shape[1]
    tm = _tile(M, tm_pref)
    th = _tile(H, th_pref)
    return pl.pallas_call(
        _ffn_body,
        grid=(M // tm, H // th),
        in_specs=[
            pl.BlockSpec((tm, D), lambda i, j: (i, 0)),
            pl.BlockSpec((1, D), lambda i, j: (0, 0)),
            pl.BlockSpec((1, D), lambda i, j: (0, 0)),
            pl.BlockSpec((D, th), lambda i, j: (0, j)),
            pl.BlockSpec((D, th), lambda i, j: (0, j)),
            pl.BlockSpec((th, D), lambda i, j: (j, 0)),
        ],
        out_specs=pl.BlockSpec((tm, D), lambda i, j: (i, 0)),
        out_shape=jax.ShapeDtypeStruct((M, D), F32),
        scratch_shapes=[pltpu.VMEM((tm, D), BF16), pltpu.VMEM((tm, D), F32)],
        compiler_params=_cparams("parallel", "arbitrary"),
        name="ffn",
    )(x, g2, g3, w1, w3, w2)


def _pool_body(u_ref, prev_ref, st_ref, pw_ref, ps_ref, o_ref, ext_ref, *, tT, t0):
    t = pl.program_id(1)
    ext_ref[0:HALO, :] = jnp.where(t == 0, st_ref[...], prev_ref[...])
    ext_ref[HALO:HALO + tT, :] = u_ref[...]
    pos = t0 + t * tT + lax.broadcasted_iota(jnp.int32, (tT, 1), 0)
    for g, w in enumerate(POOL_WINDOWS):
        cs = slice(g * POOL_DG, (g + 1) * POOL_DG)
        x_new = ext_ref[HALO:HALO + tT, cs]
        tot = x_new
        for i in range(1, w):
            tot = tot + ext_ref[HALO - i:HALO - i + tT, cs]
        cnt = jnp.minimum(w, pos + 1).astype(F32)
        y = tot / cnt - x_new
        o_ref[:, cs] = _dot(y.astype(BF16), pw_ref[g]) * ps_ref[:, cs]


def _pool_mix(z3, st, pool_w, pool_scale, t0, tT_pref=512):
    N, T = z3.shape[:2]
    tT = _tile(T, tT_pref)
    nT = T // tT
    if nT > 1:
        assert tT % HALO == 0
        prev, prev_spec = z3, pl.BlockSpec((None, HALO, D_HALF), lambda n, t: (n, jnp.maximum(t * (tT // HALO) - 1, 0), 0))
    else:
        prev, prev_spec = st, pl.BlockSpec((None, HALO, D_HALF), lambda n, t: (n, 0, 0))
    return pl.pallas_call(
        functools.partial(_pool_body, tT=tT, t0=t0),
        grid=(N, nT),
        in_specs=[
            pl.BlockSpec((None, tT, D_HALF), lambda n, t: (n, t, 0)),
            prev_spec,
            pl.BlockSpec((None, HALO, D_HALF), lambda n, t: (n, 0, 0)),
            pl.BlockSpec((len(POOL_WINDOWS), POOL_DG, POOL_DG), lambda n, t: (0, 0, 0)),
            pl.BlockSpec((1, D_HALF), lambda n, t: (0, 0)),
        ],
        out_specs=pl.BlockSpec((None, tT, D_HALF), lambda n, t: (n, t, 0)),
        out_shape=jax.ShapeDtypeStruct((N, T, D_HALF), F32),
        scratch_shapes=[pltpu.VMEM((HALO + tT, D_HALF), F32)],
        compiler_params=_cparams("parallel", "arbitrary"),
        name="pool_mix",
    )(z3, prev, st, pool_w, pool_scale)


def _mlstm_body(q_ref, k_ref, v_ref, o_ref, zg_ref, gb_ref, mg_ref, c0_ref, n0_ref, m0_ref,
                y_ref, cN_ref, nN_ref, mN_ref, C_s, n_s, m_s, *, L, t_valid):
    c = pl.program_id(1)

    @pl.when(c == 0)
    def _():
        C_s[...] = c0_ref[...]
        n_s[...] = n0_ref[...]
        m_s[...] = m0_ref[...]

    a = GATE_CAP * jnp.tanh((zg_ref[...] + gb_ref[...]) / GATE_CAP)
    lane = lax.broadcasted_iota(jnp.int32, (L, LANE), 1)
    logsig = jnp.minimum(a, 0.0) - jnp.log1p(jnp.exp(-jnp.abs(a)))
    A = jnp.where(lane < MLSTM_HEADS, a, logsig)
    if t_valid < L:
        row = lax.broadcasted_iota(jnp.int32, (L, LANE), 0)
        A = jnp.where(row < t_valid, A, jnp.where(lane < MLSTM_HEADS, NEG, 0.0))
    r_i = lax.broadcasted_iota(jnp.int32, (L, L), 0)
    c_i = lax.broadcasted_iota(jnp.int32, (L, L), 1)
    causal = r_i >= c_i
    Bc = jnp.dot(causal.astype(F32), A, preferred_element_type=F32, precision=lax.Precision.HIGHEST)
    At = A.T
    Bt = Bc.T

    for h in range(MLSTM_HEADS):
        ks = slice(h * MLSTM_DK, (h + 1) * MLSTM_DK)
        vs = slice(h * MLSTM_DV, (h + 1) * MLSTM_DV)
        qh = q_ref[:, ks] * (MLSTM_DK ** -0.5)
        kh = k_ref[:, ks]
        vh = v_ref[:, vs].astype(BF16)
        b_c = Bc[:, MLSTM_HEADS + h:MLSTM_HEADS + h + 1]
        b_r = Bt[MLSTM_HEADS + h:MLSTM_HEADS + h + 1, :]
        li_c = A[:, h:h + 1]
        li_r = At[h:h + 1, :]
        m = m_s[h]
        Ch = C_s[h]
        nh = n_s[h]

        d = jnp.where(causal, b_c - b_r + li_r, NEG)
        inter = b_c + m
        m_t = jnp.maximum(inter, jnp.max(d, axis=-1, keepdims=True))
        w_in = jnp.exp(inter - m_t)
        qb = qh.astype(BF16)
        s = _dot_nt(qb, kh.astype(BF16)) * jnp.exp(d - m_t)
        num = w_in * _dot(qb, Ch.astype(BF16)) + _dot(s.astype(BF16), vh)
        den = w_in * jnp.sum(qh * nh, axis=-1, keepdims=True) + jnp.sum(s, axis=-1, keepdims=True)
        hh = num / jnp.maximum(jnp.abs(den), jnp.exp(-m_t))

        b_end = b_c[L - 1:L, :]
        g_c = b_end - b_c + li_c
        m_new = jnp.maximum(b_end + m, jnp.max(g_c, axis=0, keepdims=True))
        w_c = jnp.exp(b_end + m - m_new)
        kw = kh * jnp.exp(g_c - m_new)
        C_s[h] = w_c * Ch + lax.dot_general(kw.astype(BF16), vh, TN_DIMS, preferred_element_type=F32)
        n_s[h] = w_c * nh + jnp.sum(kw, axis=0, keepdims=True)
        m_s[h] = m_new

        hn = _rms(hh, mg_ref[:, vs])
        y_ref[:, vs] = jax.nn.sigmoid(o_ref[:, vs]) * hn

    @pl.when(c == pl.num_programs(1) - 1)
    def _():
        cN_ref[...] = C_s[...]
        nN_ref[...] = n_s[...]
        mN_ref[...] = m_s[...]


def _mlstm(z3, zg3, gate_b, mnorm_g, C0, n0, m0, L, t_valid):
    N, Tp = z3.shape[:2]
    nc = Tp // L
    H = MLSTM_HEADS
    qk_w = H * MLSTM_DK
    v_w = H * MLSTM_DV
    st = lambda n, c: (n, 0, 0, 0)
    outs = pl.pallas_call(
        functools.partial(_mlstm_body, L=L, t_valid=t_valid),
        grid=(N, nc),
        in_specs=[
            pl.BlockSpec((None, L, qk_w), lambda n, c: (n, c, D_HALF // qk_w)),
            pl.BlockSpec((None, L, qk_w), lambda n, c: (n, c, D_HALF // qk_w + 1)),
            pl.BlockSpec((None, L, v_w), lambda n, c: (n, c, 2)),
            pl.BlockSpec((None, L, v_w), lambda n, c: (n, c, 3)),
            pl.BlockSpec((None, L, LANE), lambda n, c: (n, c, 0)),
            pl.BlockSpec((1, LANE), lambda n, c: (0, 0)),
            pl.BlockSpec((1, v_w), lambda n, c: (0, 0)),
            pl.BlockSpec((None, H, MLSTM_DK, MLSTM_DV), st),
            pl.BlockSpec((None, H, 1, MLSTM_DK), st),
            pl.BlockSpec((None, H, 1, 1), st),
        ],
        out_specs=[
            pl.BlockSpec((None, L, v_w), lambda n, c: (n, c, 0)),
            pl.BlockSpec((None, H, MLSTM_DK, MLSTM_DV), st),
            pl.BlockSpec((None, H, 1, MLSTM_DK), st),
            pl.BlockSpec((None, H, 1, 1), st),
        ],
        out_shape=[
            jax.ShapeDtypeStruct((N, Tp, v_w), F32),
            jax.ShapeDtypeStruct((N, H, MLSTM_DK, MLSTM_DV), F32),
            jax.ShapeDtypeStruct((N, H, 1, MLSTM_DK), F32),
            jax.ShapeDtypeStruct((N, H, 1, 1), F32),
        ],
        scratch_shapes=[
            pltpu.VMEM((H, MLSTM_DK, MLSTM_DV), F32),
            pltpu.VMEM((H, 1, MLSTM_DK), F32),
            pltpu.VMEM((H, 1, 1), F32),
        ],
        compiler_params=_cparams("parallel", "arbitrary"),
        name="mlstm",
    )(z3, z3, z3, z3, zg3, gate_b, mnorm_g, C0, n0.reshape(N, H, 1, MLSTM_DK), m0.reshape(N, H, 1, 1))
    y, C, n, m = outs
    return y, C, n.reshape(N, H, MLSTM_DK), m.reshape(N, H)


def _gmlp_body(u_ref, v_ref, g_ref, ws_ref, bs_ref, y_ref, vn_ref, *, tT):
    v = v_ref[...]
    vc = v - jnp.mean(v, axis=-1, keepdims=True)
    vn = vc * lax.rsqrt(jnp.mean(vc * vc, axis=-1, keepdims=True) + 1e-5) * g_ref[...]
    vn_ref[...] = vn
    r_i = lax.broadcasted_iota(jnp.int32, (GMLP_CHUNK, GMLP_CHUNK), 0)
    c_i = lax.broadcasted_iota(jnp.int32, (GMLP_CHUNK, GMLP_CHUNK), 1)
    for g in range(GMLP_GROUPS):
        cs = slice(g * GMLP_DG, (g + 1) * GMLP_DG)
        wm = jnp.where(r_i >= c_i, ws_ref[g], 0.0).astype(BF16)
        bias = bs_ref[:, g:g + 1]
        for ch in range(tT // GMLP_CHUNK):
            rs = slice(ch * GMLP_CHUNK, (ch + 1) * GMLP_CHUNK)
            mix = _dot(wm, vn[rs, cs].astype(BF16)) + bias
            y_ref[rs, cs] = u_ref[rs, cs] * mix


def _gmlp(z3, gnorm_g, ws, bs_t, tT_pref=512):
    N, Tp = z3.shape[:2]
    tT = _tile(Tp, tT_pref)
    assert tT % GMLP_CHUNK == 0
    return pl.pallas_call(
        functools.partial(_gmlp_body, tT=tT),
        grid=(N, Tp // tT),
        in_specs=[
            pl.BlockSpec((None, tT, D_HALF), lambda n, t: (n, t, 0)),
            pl.BlockSpec((None, tT, D_HALF), lambda n, t: (n, t, 1)),
            pl.BlockSpec((1, D_HALF), lambda n, t: (0, 0)),
            pl.BlockSpec((GMLP_GROUPS, GMLP_CHUNK, GMLP_CHUNK), lambda n, t: (0, 0, 0)),
            pl.BlockSpec((GMLP_CHUNK, LANE), lambda n, t: (0, 0)),
        ],
        out_specs=[
            pl.BlockSpec((None, tT, D_HALF), lambda n, t: (n, t, 0)),
            pl.BlockSpec((None, tT, D_HALF), lambda n, t: (n, t, 0)),
        ],
        out_shape=[jax.ShapeDtypeStruct((N, Tp, D_HALF), F32), jax.ShapeDtypeStruct((N, Tp, D_HALF), F32)],
        compiler_params=_cparams("parallel", "parallel"),
        name="gmlp",
    )(z3, z3, gnorm_g, ws, bs_t)


PAGES_PER_STEP = 8
BLK_PER_PAGE = PAGE // CMP_STRIDE


def _compress_body(pt_ref, *refs):
    page_refs = refs[:PAGES_PER_STEP]
    wab_ref, a_ref, b_ref = refs[PAGES_PER_STEP:]
    wa = wab_ref[0]
    wb = wab_ref[1]
    for p, pr in enumerate(page_refs):
        x = pr[...].reshape(BLK_PER_PAGE, CMP_STRIDE, 4 * NSA_DH)
        rs = slice(p * BLK_PER_PAGE, (p + 1) * BLK_PER_PAGE)
        a_ref[rs, :] = jnp.sum(x * wa[None], axis=1)
        b_ref[rs, :] = jnp.sum(x * wb[None], axis=1)


def _compress_ab(pages, page_table, wab, col_block):
    N, n_pages = page_table.shape
    assert n_pages % PAGES_PER_STEP == 0
    W = 4 * NSA_DH
    rows = PAGES_PER_STEP * BLK_PER_PAGE
    page_specs = [
        pl.BlockSpec((None, PAGE, W), functools.partial(
            lambda n, s, pt, r: (pt[n, s * PAGES_PER_STEP + r], 0, col_block), r=r))
        for r in range(PAGES_PER_STEP)
    ]
    grid_spec = pltpu.PrefetchScalarGridSpec(
        num_scalar_prefetch=1,
        grid=(N, n_pages // PAGES_PER_STEP),
        in_specs=page_specs + [pl.BlockSpec((2, CMP_STRIDE, W), lambda n, s, pt: (0, 0, 0))],
        out_specs=[pl.BlockSpec((None, rows, W), lambda n, s, pt: (n, s, 0))] * 2,
    )
    ns = n_pages * BLK_PER_PAGE
    return pl.pallas_call(
        _compress_body,
        grid_spec=grid_spec,
        out_shape=[jax.ShapeDtypeStruct((N, ns, W), F32)] * 2,
        compiler_params=_cparams("parallel", "parallel"),
        name="nsa_compress",
    )(page_table, *([pages] * PAGES_PER_STEP), wab)


def _cmp_proj_body(a_ref, b_ref, w_ref, kc_ref, vc_ref, *, ns):
    b_next = pltpu.roll(b_ref[...], ns - 1, 0)
    row = lax.broadcasted_iota(jnp.int32, (ns, 1), 0)
    blk = jnp.where(row < ns - 1, a_ref[...] + b_next, 0.0).astype(BF16)
    w0 = w_ref[0].astype(BF16)
    w1 = w_ref[1].astype(BF16)
    for kv in range(NSA_KV):
        cs = slice(kv * NSA_DH, (kv + 1) * NSA_DH)
        kc_ref[:, cs] = _dot(blk[:, kv * NSA_DH:(kv + 1) * NSA_DH], w0)
        vc_ref[:, cs] = _dot(blk[:, (NSA_KV + kv) * NSA_DH:(NSA_KV + kv + 1) * NSA_DH], w1)


def _cmp_proj(a, b, w_cmp):
    N, ns, W = a.shape
    return pl.pallas_call(
        functools.partial(_cmp_proj_body, ns=ns),
        grid=(N,),
        in_specs=[
            pl.BlockSpec((None, ns, W), lambda n: (n, 0, 0)),
            pl.BlockSpec((None, ns, W), lambda n: (n, 0, 0)),
            pl.BlockSpec((2, NSA_DH, NSA_DH), lambda n: (0, 0, 0)),
        ],
        out_specs=[pl.BlockSpec((None, ns, NSA_KV * NSA_DH), lambda n: (n, 0, 0))] * 2,
        out_shape=[jax.ShapeDtypeStruct((N, ns, NSA_KV * NSA_DH), F32)] * 2,
        compiler_params=_cparams("parallel"),
        name="nsa_cmp_proj",
    )(a, b, w_cmp)


def _split3(x):
    hi = x.astype(BF16)
    r1 = x - hi.astype(F32)
    mid = r1.astype(BF16)
    lo = (r1 - mid.astype(F32)).astype(BF16)
    return hi, mid, lo


def _cmp_to_sel(p_sum, a_mat):
    hi, mid, lo = _split3(p_sum)
    return _dot(hi, a_mat) + _dot(mid, a_mat) + _dot(lo, a_mat)


def _topk_mask(score, n_valid, k):
    lane = lax.broadcasted_iota(jnp.int32, score.shape, 1)
    rank = jnp.zeros(score.shape, F32)
    for jp in range(n_valid):
        col = score[:, jp:jp + 1]
        beats = (col > score) | ((col == score) & (lane > jp))
        rank = rank + beats.astype(F32)
    return ((rank < k) & (lane < n_valid)).astype(F32)


def _sel_scores(imp, qblk, n_valid):
    lane = lax.broadcasted_iota(jnp.int32, imp.shape, 1)
    forced = (lane == 0) | (lane == qblk) | (lane == qblk - 1)
    score = jnp.where(forced, FORCE_SCORE, jnp.where(lane > qblk, -1.0, imp))
    return jnp.where(lane < n_valid, score, -2.0)


def _masked_softmax_rows(s, mask):
    s = jnp.where(mask, s, NEG)
    e = jnp.exp(s - jnp.max(s, axis=-1, keepdims=True))
    return jnp.where(mask, e / jnp.sum(e, axis=-1, keepdims=True), 0.0)


def _stack_heads(q, scale):
    return (jnp.concatenate([q[:, g * NSA_DH:(g + 1) * NSA_DH] for g in range(NSA_G)], axis=0) * scale).astype(BF16)


def _gate_col(gates, idx):
    lane = lax.broadcasted_iota(jnp.int32, gates.shape, 1)
    return jnp.sum(jnp.where(lane == idx, gates, 0.0), axis=-1, keepdims=True)


SEL_TK = 256
WIN_KEYS = WINDOW + QBLK


def _nsa_prompt_body(q_ref, zg_ref, gb_ref, kc_ref, vc_ref, ks_ref, vs_ref, kw_ref, vw_ref, amat_t_ref, exp_ref,
                     o_ref, ks_bf, vs_bf, kw_bf, vw_bf, kc_bf, vc_bf, sc_ref, *, n_sel, nsr):
    qb = pl.program_id(1)

    @pl.when(qb == 0)
    def _():
        ks_bf[...] = ks_ref[...].astype(BF16)
        vs_bf[...] = vs_ref[...].astype(BF16)
        kw_bf[...] = kw_ref[...].astype(BF16)
        vw_bf[...] = vw_ref[...].astype(BF16)
        kc_bf[...] = kc_ref[...].astype(BF16)
        vc_bf[...] = vc_ref[...].astype(BF16)

    R = QBLK
    start = qb * R
    heads = range(NSA_HEADS)
    groups = range(NSA_KV)
    kvc = lambda kv: slice(kv * NSA_DH, (kv + 1) * NSA_DH)
    qh = [(q_ref[:, h * NSA_DH:(h + 1) * NSA_DH] * (NSA_DH ** -0.5)).astype(BF16) for h in heads]
    pos = start + lax.broadcasted_iota(jnp.int32, (R, 1), 0)

    ns = kc_bf.shape[0]
    cmp_end = (lax.broadcasted_iota(jnp.int32, (R, ns), 1) + 2) * CMP_STRIDE - 1
    bias_c = jnp.where(cmp_end <= pos, 0.0, NEG)
    any_c = jnp.where(pos >= 2 * CMP_STRIDE - 1, 1.0, 0.0)
    s_c = [_dot_nt(qh[h], kc_bf[:, kvc(h // NSA_G)]) for h in heads]
    o_cmp, p_sum = [], [None] * NSA_KV
    for h in heads:
        kv = h // NSA_G
        e_c = jnp.exp(s_c[h] + bias_c - jnp.max(s_c[h] + bias_c, axis=-1, keepdims=True))
        p_c = e_c * (any_c / jnp.sum(e_c, axis=-1, keepdims=True))
        o_cmp.append(_dot(p_c.astype(BF16), vc_bf[:, kvc(kv)]))
        p_sum[kv] = p_c if p_sum[kv] is None else p_sum[kv] + p_c

    a_t = amat_t_ref[...]
    blk = lax.broadcasted_iota(jnp.int32, (nsr, R), 0)
    qblk = (start + lax.broadcasted_iota(jnp.int32, (nsr, R), 1)) // SEL_BLOCK
    forced = (blk == 0) | (blk == qblk) | (blk == qblk - 1)
    score = []
    for kv in groups:
        hi, mid, lo = _split3(p_sum[kv])
        imp_t = _dot_nt(a_t, hi) + _dot_nt(a_t, mid) + _dot_nt(a_t, lo)
        sc = jnp.where(forced, FORCE_SCORE, jnp.where(blk > qblk, -1.0, imp_t))
        score.append(jnp.where(blk < n_sel, sc, -2.0))
        sc_ref[kv] = score[kv]

    def rank_step(i, ranks):
        ranks = list(ranks)
        for jp in (2 * i, 2 * i + 1):
            for kv in groups:
                row = sc_ref[kv, pl.ds(jp, 1), :]
                beats = (row > score[kv]) | ((row == score[kv]) & (blk > jp))
                ranks[kv] = ranks[kv] + beats.astype(F32)
        return tuple(ranks)

    n_rank = jnp.minimum(start // SEL_BLOCK + QBLK // SEL_BLOCK, n_sel)
    ranks = lax.fori_loop(0, (n_rank + 1) // 2, rank_step, tuple(jnp.zeros((nsr, R), F32) for _ in groups))
    pad = jnp.zeros((LANE - nsr, R), F32)
    sel = [jnp.concatenate([jnp.where(ranks[kv] < N_SEL, 1.0, 0.0), pad], axis=0).T.astype(BF16) for kv in groups]

    lane_tk = lax.broadcasted_iota(jnp.int32, (R, SEL_TK), 1)

    def sel_tile(kt, carry, causal):
        k0 = pl.multiple_of(kt * SEL_TK, SEL_TK)
        k_t = ks_bf[pl.ds(k0, SEL_TK), :]
        v_t = vs_bf[pl.ds(k0, SEL_TK), :]
        bias = [(_dot(sel[kv], exp_ref[kt]) - 1.0) * (-NEG) for kv in groups]
        if causal:
            bias = [jnp.where(k0 + lane_tk <= pos, b, NEG) for b in bias]
        scores = [_dot_nt(qh[h], k_t[:, kvc(h // NSA_G)]) for h in heads]
        stats = []
        for h in heads:
            m_prev, l_prev, _ = carry[h]
            s = scores[h] + bias[h // NSA_G]
            m_new = jnp.maximum(m_prev, jnp.max(s, axis=-1, keepdims=True))
            alpha = jnp.exp(m_prev - m_new)
            p = jnp.exp(s - m_new)
            stats.append((m_new, alpha, alpha * l_prev + jnp.sum(p, axis=-1, keepdims=True), p.astype(BF16)))
        return tuple((m_new, l_new, alpha * carry[h][2] + _dot(p, v_t[:, kvc(h // NSA_G)]))
                     for h, (m_new, alpha, l_new, p) in enumerate(stats))

    n_tiles = (start + R - 1) // SEL_TK + 1
    init = tuple((jnp.full((R, 1), NEG, F32), jnp.zeros((R, 1), F32), jnp.zeros((R, NSA_DH), F32)) for _ in heads)
    carry = lax.fori_loop(0, n_tiles - 1, lambda kt, c: sel_tile(kt, c, False), init)
    o_sel = [acc / l for _, l, acc in sel_tile(n_tiles - 1, carry, True)]

    w0 = pl.multiple_of(jnp.maximum(start - WINDOW, 0), QBLK)
    tok_w = w0 + lax.broadcasted_iota(jnp.int32, (R, WIN_KEYS), 1)
    bias_w = jnp.where((tok_w <= pos) & (tok_w > pos - WINDOW), 0.0, NEG)
    k_w = kw_bf[pl.ds(w0, WIN_KEYS), :]
    v_w = vw_bf[pl.ds(w0, WIN_KEYS), :]
    s_w = [_dot_nt(qh[h], k_w[:, kvc(h // NSA_G)]) for h in heads]

    gates = jax.nn.sigmoid(zg_ref[...] + gb_ref[...])
    gate = lambda branch, h: gates[:, branch * NSA_HEADS + h:branch * NSA_HEADS + h + 1]
    for h in heads:
        e_w = jnp.exp(s_w[h] + bias_w - jnp.max(s_w[h] + bias_w, axis=-1, keepdims=True))
        o_win = _dot(e_w.astype(BF16), v_w[:, kvc(h // NSA_G)]) / jnp.sum(e_w, axis=-1, keepdims=True)
        o_ref[:, h * NSA_DH:(h + 1) * NSA_DH] = gate(0, h) * o_cmp[h] + gate(1, h) * o_sel[h] + gate(2, h) * o_win


def _sel_map(ns, n_sel, nsb):
    i = np.arange(ns)[:, None]
    j = np.arange(nsb)[None, :]
    r = SEL_BLOCK // CMP_STRIDE
    return jnp.asarray(((i >= r * j - 1) & (i <= r * j + r - 1) & (j < n_sel)).astype(np.float32), BF16)


def _nsa_prompt(z3, zg3, gate_b, kc, vc):
    N, T = z3.shape[:2]
    assert T % SEL_TK == 0 and T >= WIN_KEYS
    ns = kc.shape[1]
    n_sel = T // SEL_BLOCK
    nsr = -(-n_sel // 8) * 8
    assert nsr <= LANE
    amat_t = _sel_map(ns, n_sel, nsr).T
    tok = np.arange(T).reshape(T // SEL_TK, 1, SEL_TK)
    expand = jnp.asarray((tok // SEL_BLOCK == np.arange(LANE)[None, :, None]).astype(np.float32), BF16)
    KVW = NSA_KV * NSA_DH
    qcol = 2 * D_HALF // (NSA_HEADS * NSA_DH)
    kvs_col = (3 * D_HALF + 2 * KVW) // KVW
    kvw_col = kvs_col + 2
    full = lambda off: pl.BlockSpec((None, T, KVW), lambda n, qb: (n, 0, off))
    return pl.pallas_call(
        functools.partial(_nsa_prompt_body, n_sel=n_sel, nsr=nsr),
        grid=(N, T // QBLK),
        in_specs=[
            pl.BlockSpec((None, QBLK, NSA_HEADS * NSA_DH), lambda n, qb: (n, qb, qcol)),
            pl.BlockSpec((None, QBLK, LANE), lambda n, qb: (n, qb, 0)),
            pl.BlockSpec((1, LANE), lambda n, qb: (0, 0)),
            pl.BlockSpec((None, ns, KVW), lambda n, qb: (n, 0, 0)),
            pl.BlockSpec((None, ns, KVW), lambda n, qb: (n, 0, 0)),
            full(kvs_col), full(kvs_col + 1), full(kvw_col), full(kvw_col + 1),
            pl.BlockSpec((nsr, ns), lambda n, qb: (0, 0)),
            pl.BlockSpec((T // SEL_TK, LANE, SEL_TK), lambda n, qb: (0, 0, 0)),
        ],
        out_specs=pl.BlockSpec((None, QBLK, NSA_HEADS * NSA_DH), lambda n, qb: (n, qb, 0)),
        out_shape=jax.ShapeDtypeStruct((N, T, NSA_HEADS * NSA_DH), F32),
        scratch_shapes=([pltpu.VMEM((T, KVW), BF16)] * 4 + [pltpu.VMEM((ns, KVW), BF16)] * 2
                        + [pltpu.VMEM((NSA_KV, nsr, QBLK), F32)]),
        compiler_params=_cparams("parallel", "arbitrary"),
        name="nsa_prompt",
    )(z3, zg3, gate_b, kc, vc, z3, z3, z3, z3, amat_t, expand)


def _nsa_sample_a_body(q_ref, kc_ref, vc_ref, kw_ref, vw_ref, amat_ref, ocmp_ref, owin_ref, sel_ref,
                       *, Tq, past_len, n_sel, wb):
    qs = _stack_heads(q_ref[...], NSA_DH ** -0.5)
    pos = past_len + lax.broadcasted_iota(jnp.int32, (Tq, 1), 0)
    pos4 = jnp.concatenate([pos] * NSA_G, axis=0)

    ns = kc_ref.shape[0]
    cmp_i = lax.broadcasted_iota(jnp.int32, (NSA_G * Tq, ns), 1)
    m_c = ((cmp_i + 2) * CMP_STRIDE - 1 <= pos4) & (cmp_i < ns - 1)
    p_c = _masked_softmax_rows(_dot_nt(qs, kc_ref[...].astype(BF16)), m_c)
    ocmp_ref[...] = _dot(p_c.astype(BF16), vc_ref[...].astype(BF16))
    p_sum = p_c[0:Tq] + p_c[Tq:2 * Tq] + p_c[2 * Tq:3 * Tq] + p_c[3 * Tq:4 * Tq]
    imp = _cmp_to_sel(p_sum, amat_ref[...])
    sel_ref[...] = _topk_mask(_sel_scores(imp, pos // SEL_BLOCK, n_sel), n_sel, N_SEL)

    nw = kw_ref.shape[0]
    tok_w = past_len - wb + lax.broadcasted_iota(jnp.int32, (NSA_G * Tq, nw), 1)
    m_w = (tok_w >= 0) & (tok_w <= pos4) & (tok_w > pos4 - WINDOW)
    p_w = _masked_softmax_rows(_dot_nt(qs, kw_ref[...].astype(BF16)), m_w)
    owin_ref[...] = _dot(p_w.astype(BF16), vw_ref[...].astype(BF16))


def _nsa_sample_a(z3, kc, vc, kw_full, past_len, wb):
    N, Tq = z3.shape[:2]
    ns = kc.shape[1]
    nw = kw_full.shape[1]
    n_sel = -(-(past_len + Tq) // SEL_BLOCK)
    nsb = -(-n_sel // LANE) * LANE
    amat = _sel_map(ns, n_sel, nsb)
    qcol = 2 * D_HALF // (NSA_G * NSA_DH)
    R4 = NSA_G * Tq
    return pl.pallas_call(
        functools.partial(_nsa_sample_a_body, Tq=Tq, past_len=past_len, n_sel=n_sel, wb=wb),
        grid=(N, NSA_KV),
        in_specs=[
            pl.BlockSpec((None, Tq, NSA_G * NSA_DH), lambda n, kv: (n, 0, qcol + kv)),
            pl.BlockSpec((None, ns, NSA_DH), lambda n, kv: (n, 0, kv)),
            pl.BlockSpec((None, ns, NSA_DH), lambda n, kv: (n, 0, kv)),
            pl.BlockSpec((None, nw, NSA_DH), lambda n, kv: (n, 0, kv)),
            pl.BlockSpec((None, nw, NSA_DH), lambda n, kv: (n, 0, NSA_KV + kv)),
            pl.BlockSpec((ns, nsb), lambda n, kv: (0, 0)),
        ],
        out_specs=[
            pl.BlockSpec((None, None, R4, NSA_DH), lambda n, kv: (n, kv, 0, 0)),
            pl.BlockSpec((None, None, R4, NSA_DH), lambda n, kv: (n, kv, 0, 0)),
            pl.BlockSpec((None, None, Tq, nsb), lambda n, kv: (n, kv, 0, 0)),
        ],
        out_shape=[
            jax.ShapeDtypeStruct((N, NSA_KV, R4, NSA_DH), F32),
            jax.ShapeDtypeStruct((N, NSA_KV, R4, NSA_DH), F32),
            jax.ShapeDtypeStruct((N, NSA_KV, Tq, nsb), F32),
        ],
        compiler_params=_cparams("parallel", "parallel"),
        name="nsa_sample_cmp_win",
    )(z3, kc, vc, kw_full, kw_full, amat)


BLK_PER_STEP = PAGES_PER_STEP * PAGE // SEL_BLOCK


def _nsa_sample_b_body(pt_ref, *refs, Tq, past_len):
    page_refs = refs[:PAGES_PER_STEP]
    (q_ref, new_ref, sel_ref, exp_ref, ocmp_ref, owin_ref, zg_ref, gb_ref, o_ref, m_s, l_s, acc_s) = refs[PAGES_PER_STEP:]
    s_id = pl.program_id(1)
    R4 = NSA_G * Tq
    rep = lambda x: jnp.concatenate([x] * NSA_G, axis=0)
    pos = past_len + lax.broadcasted_iota(jnp.int32, (Tq, 1), 0)

    @pl.when(s_id == 0)
    def _():
        m_s[...] = jnp.full(m_s.shape, NEG, F32)
        l_s[...] = jnp.zeros(l_s.shape, F32)
        acc_s[...] = jnp.zeros(acc_s.shape, F32)

    def update(kv, s, mask, pv):
        s = jnp.where(mask, s, NEG)
        m_prev = m_s[kv]
        m_new = jnp.maximum(m_prev, jnp.max(s, axis=-1, keepdims=True))
        alpha = jnp.exp(m_prev - m_new)
        p = jnp.where(mask, jnp.exp(s - m_new), 0.0)
        l_s[kv] = alpha * l_s[kv] + jnp.sum(p, axis=-1, keepdims=True)
        acc_s[kv] = alpha * acc_s[kv] + pv(p.astype(BF16))
        m_s[kv] = m_new

    qs = [_stack_heads(q_ref[:, kv * NSA_G * NSA_DH:(kv + 1) * NSA_G * NSA_DH], NSA_DH ** -0.5) for kv in range(NSA_KV)]
    pages = [pr[...].astype(BF16) for pr in page_refs]
    for kv in range(NSA_KV):
        kc = slice(kv * NSA_DH, (kv + 1) * NSA_DH)
        vc = slice((NSA_KV + kv) * NSA_DH, (NSA_KV + kv + 1) * NSA_DH)
        s = jnp.concatenate([_dot_nt(qs[kv], pg[:, kc]) for pg in pages], axis=1)
        mask = rep(_dot(sel_ref[kv].astype(BF16), exp_ref[...])) > 0.5

        def pv(p, vc=vc):
            out = _dot(p[:, 0:PAGE], pages[0][:, vc])
            for i in range(1, PAGES_PER_STEP):
                out = out + _dot(p[:, i * PAGE:(i + 1) * PAGE], pages[i][:, vc])
            return out

        update(kv, s, mask, pv)

    @pl.when(s_id == pl.num_programs(1) - 1)
    def _():
        new = new_ref[...].astype(BF16)
        tok_n = past_len + lax.broadcasted_iota(jnp.int32, (R4, Tq), 1)
        gates = jax.nn.sigmoid(zg_ref[...] + gb_ref[...])
        for kv in range(NSA_KV):
            k_new = new[:, kv * NSA_DH:(kv + 1) * NSA_DH]
            v_new = new[:, (NSA_KV + kv) * NSA_DH:(NSA_KV + kv + 1) * NSA_DH]
            update(kv, _dot_nt(qs[kv], k_new), tok_n <= rep(pos), lambda p, v_new=v_new: _dot(p, v_new))
            o_sel = acc_s[kv] / l_s[kv]
            o_cmp = ocmp_ref[kv]
            o_win = owin_ref[kv]
            for g in range(NSA_G):
                rs = slice(g * Tq, (g + 1) * Tq)
                head = kv * NSA_G + g
                o_ref[:, head * NSA_DH:(head + 1) * NSA_DH] = (
                    gates[:, head:head + 1] * o_cmp[rs]
                    + gates[:, NSA_HEADS + head:NSA_HEADS + head + 1] * o_sel[rs]
                    + gates[:, 2 * NSA_HEADS + head:2 * NSA_HEADS + head + 1] * o_win[rs])


def _nsa_sample_b(z3, zg3, gate_b, pool_sel, page_table, sel, o_cmp, o_win, past_len):
    N, Tq = z3.shape[:2]
    n_pages = page_table.shape[1]
    assert n_pages % PAGES_PER_STEP == 0 and past_len % SEL_BLOCK == 0
    n_steps = n_pages // PAGES_PER_STEP
    sel_steps = sel[..., :past_len // SEL_BLOCK].reshape(N, NSA_KV, Tq, n_steps, BLK_PER_STEP).transpose(0, 3, 1, 2, 4)
    tok = np.arange(PAGES_PER_STEP * PAGE)[None, :]
    expand = jnp.asarray((tok // SEL_BLOCK == np.arange(BLK_PER_STEP)[:, None]).astype(np.float32), BF16)
    W = 4 * NSA_DH
    R4 = NSA_G * Tq
    qcol = 2 * D_HALF // (NSA_HEADS * NSA_DH)
    kvs_col = (3 * D_HALF + W) // W
    page_specs = [
        pl.BlockSpec((None, PAGE, W), functools.partial(
            lambda n, s, pt, r: (pt[n, s * PAGES_PER_STEP + r], 0, 0), r=r))
        for r in range(PAGES_PER_STEP)
    ]
    per_n4 = lambda shape: pl.BlockSpec((None,) + shape, lambda n, s, pt: (n, 0, 0, 0))
    grid_spec = pltpu.PrefetchScalarGridSpec(
        num_scalar_prefetch=1,
        grid=(N, n_pages // PAGES_PER_STEP),
        in_specs=page_specs + [
            pl.BlockSpec((None, Tq, NSA_HEADS * NSA_DH), lambda n, s, pt: (n, 0, qcol)),
            pl.BlockSpec((None, Tq, W), lambda n, s, pt: (n, 0, kvs_col)),
            pl.BlockSpec((None, None, NSA_KV, Tq, BLK_PER_STEP), lambda n, s, pt: (n, s, 0, 0, 0)),
            pl.BlockSpec((BLK_PER_STEP, PAGES_PER_STEP * PAGE), lambda n, s, pt: (0, 0)),
            per_n4((NSA_KV, R4, NSA_DH)),
            per_n4((NSA_KV, R4, NSA_DH)),
            pl.BlockSpec((None, Tq, LANE), lambda n, s, pt: (n, 0, 0)),
            pl.BlockSpec((1, LANE), lambda n, s, pt: (0, 0)),
        ],
        out_specs=pl.BlockSpec((None, Tq, NSA_HEADS * NSA_DH), lambda n, s, pt: (n, 0, 0)),
        scratch_shapes=[
            pltpu.VMEM((NSA_KV, R4, 1), F32),
            pltpu.VMEM((NSA_KV, R4, 1), F32),
            pltpu.VMEM((NSA_KV, R4, NSA_DH), F32),
        ],
    )
    return pl.pallas_call(
        functools.partial(_nsa_sample_b_body, Tq=Tq, past_len=past_len),
        grid_spec=grid_spec,
        out_shape=jax.ShapeDtypeStruct((N, Tq, NSA_HEADS * NSA_DH), F32),
        compiler_params=_cparams("parallel", "arbitrary"),
        name="nsa_sample_sel",
    )(page_table, *([pool_sel] * PAGES_PER_STEP), z3, z3, sel_steps, expand, o_cmp, o_win, zg3, gate_b)


MLSTM_CHUNK_PROMPT = 256
SAMPLE_PAD = 128


def _pad_rows(x, rows):
    return jnp.pad(x, ((0, 0), (0, rows - x.shape[1]), (0, 0)))


def _lane_pad(v):
    return jnp.pad(v.astype(F32), (0, LANE - v.shape[0])).reshape(1, LANE)


def _split_w_in(w_in, n_main):
    wg = jnp.pad(w_in[:, n_main:], ((0, 0), (0, LANE - (w_in.shape[1] - n_main))))
    return w_in[:, :n_main].astype(BF16), wg.astype(BF16)


def _even_layer(x, t0, pool_buf, C0, n0, m0, w, ffn):
    N, T, D = x.shape
    n_main = 4 * D_HALF
    z, zg = _norm_matmul(x.reshape(N * T, D), w["ng"][0:1], w["w_main"], w["w_gate"])
    z3 = z.reshape(N, T, n_main)
    zg3 = zg.reshape(N, T, LANE)

    st = jnp.pad(pool_buf, ((0, 0), (HALO - POOL_PAD, 0), (0, 0)))
    y_a = _pool_mix(z3, st, w["pool_w"], w["pool_scale"], t0)
    u_ext_tail = jnp.concatenate([pool_buf, z3[:, :, :D_HALF]], axis=1)[:, -POOL_PAD:] if T < POOL_PAD else z3[:, -POOL_PAD:, :D_HALF]

    if T % MLSTM_CHUNK_PROMPT == 0:
        L, zm, zgm = MLSTM_CHUNK_PROMPT, z3, zg3
    else:
        L, zm, zgm = SAMPLE_PAD, _pad_rows(z3, SAMPLE_PAD), _pad_rows(zg3, SAMPLE_PAD)
    y_b, C, n, m = _mlstm(zm, zgm, w["gate_b"], w["mnorm_g"], C0, n0, m0, L, min(T, L))
    y_b = y_b[:, :T]

    x2 = _out_proj(y_a.reshape(N * T, D_HALF), y_b.reshape(N * T, D_HALF), w["w_out_a"], w["w_out_b"],
                   x.reshape(N * T, D), w["ng"][1:2])
    x3 = _ffn(x2, w["ng"][2:3], w["ng"][3:4], *ffn)
    return x3.reshape(N, T, D), u_ext_tail, C, n, m


def _odd_layer(x, w, ffn, past=None):
    N, T, D = x.shape
    n_main = 4 * D_HALF + D_HALF // 2
    z, zg = _norm_matmul(x.reshape(N * T, D), w["ng"][0:1], w["w_main"], w["w_gate"])
    z3 = z.reshape(N, T, n_main)
    zg3 = zg.reshape(N, T, LANE)
    W = 4 * NSA_DH
    kvc = z3[:, :, 3 * D_HALF:3 * D_HALF + W]
    kvs = z3[:, :, 3 * D_HALF + W:3 * D_HALF + 2 * W]
    kvw = z3[:, :, 3 * D_HALF + 2 * W:3 * D_HALF + 3 * W]

    if past is None:
        y_c, _ = _gmlp(z3, w["gnorm_g"], w["ws"], w["bs_t"])
        vn = None
        pt = jnp.arange(N * (T // PAGE), dtype=jnp.int32).reshape(N, T // PAGE)
        a, b = _compress_ab(z3.reshape(N * (T // PAGE), PAGE, n_main), pt, w["wab"], 3 * D_HALF // W)
        kc, vc = _cmp_proj(a, b, w["cmp_w"])
        o = _nsa_prompt(z3, zg3, w["nsa_gate_b"], kc, vc)
        win_state = kvw[:, -min(WINDOW, T):]
    else:
        win_buf, pool_cmp, pool_sel, page_table = past
        past_len = page_table.shape[1] * PAGE
        wb = win_buf.shape[1]
        y_c, vn = _gmlp(_pad_rows(z3[:, :, :2 * D_HALF], SAMPLE_PAD), w["gnorm_g"], w["ws"], w["bs_t"])
        y_c, vn = y_c[:, :T], vn[:, :T]
        assert (past_len + T) // CMP_STRIDE == past_len // CMP_STRIDE
        a, b = _compress_ab(pool_cmp, page_table, w["wab"], 0)
        kc, vc = _cmp_proj(a, b, w["cmp_w"])
        kw_all = jnp.concatenate([win_buf.reshape(N, wb, W), kvw], axis=1)
        nw = -(-(wb + T) // LANE) * LANE
        o_cmp, o_win, sel = _nsa_sample_a(z3, kc, vc, _pad_rows(kw_all, nw), past_len, wb)
        o = _nsa_sample_b(z3, zg3, w["nsa_gate_b"], pool_sel, page_table, sel, o_cmp, o_win, past_len)
        win_state = kw_all[:, -wb:]

    x2 = _out_proj(y_c.reshape(N * T, D_HALF), o.reshape(N * T, D_HALF), w["w_out_a"], w["w_out_b"],
                   x.reshape(N * T, D), w["ng"][1:2])
    x3 = _ffn(x2, w["ng"][2:3], w["ng"][3:4], *ffn)
    kv5 = lambda t: t.reshape(N, t.shape[1], 2, NSA_KV, NSA_DH)
    return x3.reshape(N, T, D), vn, kv5(kvc), kv5(kvs), kv5(win_state)


def kernel(x_prompt, x_sample, state_pool, state_mlstm_c, state_mlstm_n, state_mlstm_m, state_win_kv, cache_cmp_kv, cache_sel_kv, page_table, norm_g, w_in_even, w_out_even, pool_w, pool_scale, mlstm_gate_b, mlstm_norm_g, w_in_odd, w_out_odd, gmlp_norm_g, gmlp_ws, gmlp_bs, nsa_cmp_pos, nsa_cmp_w, nsa_gate_b, ffn_w1, ffn_w3, ffn_w2):
    B = x_prompt.shape[0]
    depth = norm_g.shape[0]
    past_len = page_table.shape[1] * PAGE
    xp, xs = x_prompt, x_sample
    pool_p, pool_s, c_p, c_s, n_p, n_s, m_p, m_s = [], [], [], [], [], [], [], []
    gv_s, cmp_p, cmp_s, sel_p, sel_s, win_p, win_s = [], [], [], [], [], [], []
    for l in range(depth):
        j = l // 2
        ffn = (ffn_w1[l].astype(BF16), ffn_w3[l].astype(BF16), ffn_w2[l].astype(BF16))
        if l % 2 == 0:
            w_main, w_gate = _split_w_in(w_in_even[j], 4 * D_HALF)
            w_out = w_out_even[j].astype(BF16)
            w = dict(ng=norm_g[l], w_main=w_main, w_gate=w_gate, w_out_a=w_out[:D_HALF], w_out_b=w_out[D_HALF:],
                     pool_w=pool_w[j].astype(BF16), pool_scale=pool_scale[j].reshape(1, D_HALF),
                     gate_b=_lane_pad(mlstm_gate_b[j].reshape(-1)), mnorm_g=mlstm_norm_g[j].reshape(1, D_HALF))
            zp = jnp.zeros((B, POOL_PAD, D_HALF), F32)
            zc = jnp.zeros((B, MLSTM_HEADS, MLSTM_DK, MLSTM_DV), F32)
            zn = jnp.zeros((B, MLSTM_HEADS, MLSTM_DK), F32)
            zm = jnp.zeros((B, MLSTM_HEADS), F32)
            xp, pb, c, n, m = _even_layer(xp, 0, zp, zc, zn, zm, w, ffn)
            pool_p.append(pb); c_p.append(c); n_p.append(n); m_p.append(m)
            xs, pb, c, n, m = _even_layer(xs, past_len, state_pool[j], state_mlstm_c[j], state_mlstm_n[j],
                                          state_mlstm_m[j], w, ffn)
            pool_s.append(pb); c_s.append(c); n_s.append(n); m_s.append(m)
        else:
            w_main, w_gate = _split_w_in(w_in_odd[j], 4 * D_HALF + D_HALF // 2)
            w_out = w_out_odd[j].astype(BF16)
            cp = nsa_cmp_pos[j]
            wcol = jnp.repeat(cp, NSA_KV * NSA_DH, axis=1)
            wab = jnp.stack([wcol[:CMP_STRIDE], wcol[CMP_STRIDE:]])
            w = dict(ng=norm_g[l], w_main=w_main, w_gate=w_gate, w_out_a=w_out[:D_HALF], w_out_b=w_out[D_HALF:],
                     gnorm_g=gmlp_norm_g[j].reshape(1, D_HALF), ws=gmlp_ws[j],
                     bs_t=jnp.pad(gmlp_bs[j].T, ((0, 0), (0, LANE - GMLP_GROUPS))),
                     wab=wab, cmp_w=nsa_cmp_w[j], nsa_gate_b=_lane_pad(nsa_gate_b[j]))
            xp, _, kc, ksl, wv = _odd_layer(xp, w, ffn)
            cmp_p.append(kc); sel_p.append(ksl); win_p.append(wv)
            n_pool = cache_cmp_kv.shape[1]
            flat = lambda c: c.reshape(c.shape[0] * n_pool, PAGE, 4 * NSA_DH)
            xs, vn, kc, ksl, wv = _odd_layer(xs, w, ffn, past=(state_win_kv[j], flat(cache_cmp_kv), flat(cache_sel_kv),
                                                               page_table + j * n_pool))
            gv_s.append(vn); cmp_s.append(kc); sel_s.append(ksl); win_s.append(wv)
    st = jnp.stack
    return (xp, xs, st(pool_p), st(pool_s), st(c_p), st(c_s), st(n_p), st(n_s), st(m_p), st(m_s),
            st(gv_s), st(cmp_p), st(cmp_s), st(sel_p), st(sel_s), st(win_p), st(win_s))
```

```python
import functools
import math

import numpy as np
import jax
import jax.numpy as jnp
from jax import lax
from jax.experimental import pallas as pl
from jax.experimental.pallas import tpu as pltpu

F32 = jnp.float32
BF16 = jnp.bfloat16
NEG = -1e30

D_HALF = 1024
POOL_WINDOWS = (2, 4, 8, 16)
POOL_DG = 256
POOL_PAD = 15
HALO = 16
MLSTM_HEADS = 4
MLSTM_DK = 128
MLSTM_DV = 256
GATE_CAP = 15.0
GMLP_CHUNK = 128
GMLP_GROUPS = 4
GMLP_DG = 256
NSA_HEADS = 8
NSA_DH = 128
NSA_KV = 2
NSA_G = 4
CMP_STRIDE = 16
SEL_BLOCK = 64
N_SEL = 16
WINDOW = 512
QBLK = 128
PAGE = 128
FORCE_SCORE = 1e9
LANE = 128
VMEM_LIMIT = 56 * 1024 * 1024

NT_DIMS = (((1,), (1,)), ((), ()))
TN_DIMS = (((0,), (0,)), ((), ()))


def _cparams(*sem):
    return pltpu.CompilerParams(dimension_semantics=sem, vmem_limit_bytes=VMEM_LIMIT)


def _tile(m, pref):
    if m <= pref:
        return m
    for t in range(pref, 7, -1):
        if m % t == 0 and t % 8 == 0:
            return t
    return m


def _dot(a, b):
    return jnp.dot(a, b, preferred_element_type=F32)


def _dot_nt(a, b):
    return lax.dot_general(a, b, NT_DIMS, preferred_element_type=F32)


def _rms(x, g, eps=1e-6):
    return x * lax.rsqrt(jnp.mean(x * x, axis=-1, keepdims=True) + eps) * g


def _norm_mm_body(x_ref, xs_ref, g_ref, w_ref, wg_ref, o_ref, og_ref, os_ref, ogs_ref, xn_ref, xsn_ref):
    i = pl.program_id(0)
    j = pl.program_id(1)
    w = w_ref[...].astype(BF16)

    @pl.when(j == 0)
    def _():
        xn = _rms(x_ref[...], g_ref[...]).astype(BF16)
        xn_ref[...] = xn
        og_ref[...] = _dot(xn, wg_ref[...].astype(BF16))

    o_ref[...] = _dot(xn_ref[...], w)

    @pl.when(i == 0)
    def _():
        @pl.when(j == 0)
        def _():
            xsn = _rms(xs_ref[...], g_ref[...]).astype(BF16)
            xsn_ref[...] = xsn
            ogs_ref[...] = _dot(xsn, wg_ref[...].astype(BF16))

        os_ref[...] = _dot(xsn_ref[...], w)


def _norm_matmul(x, xs, g, w_stack, layer, n_main, wg, tm_pref=1024, tn_pref=512):
    M, K = x.shape
    Ms = xs.shape[0]
    tm = _tile(M, tm_pref)
    tn = _tile(n_main, tn_pref)
    nj = n_main // tn
    return pl.pallas_call(
        _norm_mm_body,
        grid=(M // tm, nj),
        in_specs=[
            pl.BlockSpec((tm, K), lambda i, j: (i, 0)),
            pl.BlockSpec((Ms, K), lambda i, j: (0, 0)),
            pl.BlockSpec((1, K), lambda i, j: (0, 0)),
            pl.BlockSpec((None, K, tn), lambda i, j: (layer, 0, j)),
            pl.BlockSpec((K, LANE), lambda i, j: (0, 0)),
        ],
        out_specs=[
            pl.BlockSpec((tm, tn), lambda i, j: (i, j)),
            pl.BlockSpec((tm, LANE), lambda i, j: (i, 0)),
            pl.BlockSpec((Ms, tn), lambda i, j: (0, jnp.where(i == 0, j, nj - 1))),
            pl.BlockSpec((Ms, LANE), lambda i, j: (0, 0)),
        ],
        out_shape=[jax.ShapeDtypeStruct((M, n_main), F32), jax.ShapeDtypeStruct((M, LANE), F32),
                   jax.ShapeDtypeStruct((Ms, n_main), F32), jax.ShapeDtypeStruct((Ms, LANE), F32)],
        scratch_shapes=[pltpu.VMEM((tm, K), BF16), pltpu.VMEM((Ms, K), BF16)],
        compiler_params=_cparams("arbitrary", "arbitrary"),
        name="norm_matmul",
    )(x, xs, g, w_stack, wg)


def _out_proj_body(a1_ref, a2_ref, res_ref, a1s_ref, a2s_ref, ress_ref, w1_ref, w2_ref, g_ref, o_ref, os_ref):
    def proj(a1, a2, res):
        y = _dot(a1[...].astype(BF16), w1_ref[...]) + _dot(a2[...].astype(BF16), w2_ref[...])
        return res[...] + _rms(y, g_ref[...])

    o_ref[...] = proj(a1_ref, a2_ref, res_ref)

    @pl.when(pl.program_id(0) == 0)
    def _():
        os_ref[...] = proj(a1s_ref, a2s_ref, ress_ref)


def _out_proj(a1, a2, res, a1s, a2s, ress, w1, w2, g, tm_pref=512):
    M, K1 = a1.shape
    Ms = a1s.shape[0]
    K2 = a2.shape[1]
    D = w1.shape[1]
    tm = _tile(M, tm_pref)
    whole = lambda r, c: pl.BlockSpec((r, c), lambda i: (0, 0))
    return pl.pallas_call(
        _out_proj_body,
        grid=(M // tm,),
        in_specs=[
            pl.BlockSpec((tm, K1), lambda i: (i, 0)),
            pl.BlockSpec((tm, K2), lambda i: (i, 0)),
            pl.BlockSpec((tm, D), lambda i: (i, 0)),
            whole(Ms, K1), whole(Ms, K2), whole(Ms, D),
            whole(K1, D), whole(K2, D), whole(1, D),
        ],
        out_specs=[pl.BlockSpec((tm, D), lambda i: (i, 0)), whole(Ms, D)],
        out_shape=[jax.ShapeDtypeStruct((M, D), F32), jax.ShapeDtypeStruct((Ms, D), F32)],
        compiler_params=_cparams("arbitrary"),
        name="out_proj",
    )(a1, a2, res, a1s, a2s, ress, w1, w2, g)


def _ffn_body(x_ref, xs_ref, g2_ref, g3_ref, w1_ref, w3_ref, w2_ref, o_ref, os_ref, xn_ref, acc_ref, xsn_ref, accs_ref):
    i = pl.program_id(0)
    j = pl.program_id(1)
    last = pl.num_programs(1) - 1
    w1 = w1_ref[...].astype(BF16)
    w3 = w3_ref[...].astype(BF16)
    w2 = w2_ref[...].astype(BF16)

    def step(x_r, o_r, xn_r, acc_r):
        @pl.when(j == 0)
        def _():
            xn_r[...] = _rms(x_r[...], g2_ref[...]).astype(BF16)
            acc_r[...] = jnp.zeros_like(acc_r)

        xn = xn_r[...]
        h1 = _dot(xn, w1)
        h = (h1 * jax.nn.sigmoid(h1) * _dot(xn, w3)).astype(BF16)
        acc_r[...] += _dot(h, w2)

        @pl.when(j == last)
        def _():
            o_r[...] = x_r[...] + _rms(acc_r[...], g3_ref[...])

    step(x_ref, o_ref, xn_ref, acc_ref)

    @pl.when(i == 0)
    def _():
        step(xs_ref, os_ref, xsn_ref, accs_ref)


def _ffn(x, xs, g2, g3, w1_stack, w3_stack, w2_stack, layer, tm_pref=1024, th_pref=256):
    M, D = x.shape
    Ms = xs.shape[0]
    H = w1_stack.shape[2]
    tm = _tile(M, tm_pref)
    th = _tile(H, th_pref)
    once = pl.Buffered(1)
    return pl.pallas_call(
        _ffn_body,
        grid=(M // tm, H // th),
        in_specs=[
            pl.BlockSpec((tm, D), lambda i, j: (i, 0), pipeline_mode=once),
            pl.BlockSpec((Ms, D), lambda i, j: (0, 0)),
            pl.BlockSpec((1, D), lambda i, j: (0, 0)),
            pl.BlockSpec((1, D), lambda i, j: (0, 0)),
            pl.BlockSpec((None, D, th), lambda i, j: (layer, 0, j)),
            pl.BlockSpec((None, D, th), lambda i, j: (layer, 0, j)),
            pl.BlockSpec((None, th, D), lambda i, j: (layer, j, 0)),
        ],
        out_specs=[
            pl.BlockSpec((tm, D), lambda i, j: (i, 0), pipeline_mode=once),
            pl.BlockSpec((Ms, D), lambda i, j: (0, 0)),
        ],
        out_shape=[jax.ShapeDtypeStruct((M, D), F32), jax.ShapeDtypeStruct((Ms, D), F32)],
        scratch_shapes=[pltpu.VMEM((tm, D), BF16), pltpu.VMEM((tm, D), F32),
                        pltpu.VMEM((Ms, D), BF16), pltpu.VMEM((Ms, D), F32)],
        compiler_params=_cparams("arbitrary", "arbitrary"),
        name="ffn",
    )(x, xs, g2, g3, w1_stack, w3_stack, w2_stack)


def _pool_body(u_ref, prev_ref, st_ref, pw_ref, ps_ref, o_ref, ext_ref, *, tT, t0):
    t = pl.program_id(1)
    ext_ref[0:HALO, :] = jnp.where(t == 0, st_ref[...], prev_ref[...])
    ext_ref[HALO:HALO + tT, :] = u_ref[...]
    pos = t0 + t * tT + lax.broadcasted_iota(jnp.int32, (tT, 1), 0)
    for g, w in enumerate(POOL_WINDOWS):
        cs = slice(g * POOL_DG, (g + 1) * POOL_DG)
        x_new = ext_ref[HALO:HALO + tT, cs]
        tot = x_new
        for i in range(1, w):
            tot = tot + ext_ref[HALO - i:HALO - i + tT, cs]
        cnt = jnp.minimum(w, pos + 1).astype(F32)
        y = tot / cnt - x_new
        o_ref[:, cs] = _dot(y.astype(BF16), pw_ref[g]) * ps_ref[:, cs]


def _pool_mix(z3, st, pool_w, pool_scale, t0, tT_pref=512):
    N, T = z3.shape[:2]
    tT = _tile(T, tT_pref)
    nT = T // tT
    if nT > 1:
        assert tT % HALO == 0
        prev, prev_spec = z3, pl.BlockSpec((None, HALO, D_HALF), lambda n, t: (n, jnp.maximum(t * (tT // HALO) - 1, 0), 0))
    else:
        prev, prev_spec = st, pl.BlockSpec((None, HALO, D_HALF), lambda n, t: (n, 0, 0))
    return pl.pallas_call(
        functools.partial(_pool_body, tT=tT, t0=t0),
        grid=(N, nT),
        in_specs=[
            pl.BlockSpec((None, tT, D_HALF), lambda n, t: (n, t, 0)),
            prev_spec,
            pl.BlockSpec((None, HALO, D_HALF), lambda n, t: (n, 0, 0)),
            pl.BlockSpec((len(POOL_WINDOWS), POOL_DG, POOL_DG), lambda n, t: (0, 0, 0)),
            pl.BlockSpec((1, D_HALF), lambda n, t: (0, 0)),
        ],
        out_specs=pl.BlockSpec((None, tT, D_HALF), lambda n, t: (n, t, 0)),
        out_shape=jax.ShapeDtypeStruct((N, T, D_HALF), F32),
        scratch_shapes=[pltpu.VMEM((HALO + tT, D_HALF), F32)],
        compiler_params=_cparams("parallel", "arbitrary"),
        name="pool_mix",
    )(z3, prev, st, pool_w, pool_scale)


def _mlstm_body(q_ref, k_ref, v_ref, o_ref, zg_ref, gb_ref, mg_ref, c0_ref, n0_ref, m0_ref,
                y_ref, cN_ref, nN_ref, mN_ref, C_s, n_s, m_s, *, L, t_valid):
    c = pl.program_id(1)

    @pl.when(c == 0)
    def _():
        C_s[...] = c0_ref[...]
        n_s[...] = n0_ref[...]
        m_s[...] = m0_ref[...]

    a = GATE_CAP * jnp.tanh((zg_ref[...] + gb_ref[...]) / GATE_CAP)
    lane = lax.broadcasted_iota(jnp.int32, (L, LANE), 1)
    logsig = jnp.minimum(a, 0.0) - jnp.log1p(jnp.exp(-jnp.abs(a)))
    A = jnp.where(lane < MLSTM_HEADS, a, logsig)
    if t_valid < L:
        row = lax.broadcasted_iota(jnp.int32, (L, LANE), 0)
        A = jnp.where(row < t_valid, A, jnp.where(lane < MLSTM_HEADS, NEG, 0.0))
    r_i = lax.broadcasted_iota(jnp.int32, (L, L), 0)
    c_i = lax.broadcasted_iota(jnp.int32, (L, L), 1)
    causal = r_i >= c_i
    Bc = jnp.dot(causal.astype(F32), A, preferred_element_type=F32, precision=lax.Precision.HIGHEST)
    At = A.T
    Bt = Bc.T

    for h in range(MLSTM_HEADS):
        ks = slice(h * MLSTM_DK, (h + 1) * MLSTM_DK)
        vs = slice(h * MLSTM_DV, (h + 1) * MLSTM_DV)
        qh = q_ref[:, ks] * (MLSTM_DK ** -0.5)
        kh = k_ref[:, ks]
        vh = v_ref[:, vs].astype(BF16)
        b_c = Bc[:, MLSTM_HEADS + h:MLSTM_HEADS + h + 1]
        b_r = Bt[MLSTM_HEADS + h:MLSTM_HEADS + h + 1, :]
        li_c = A[:, h:h + 1]
        li_r = At[h:h + 1, :]
        m = m_s[h]
        Ch = C_s[h]
        nh = n_s[h]

        d = jnp.where(causal, b_c - b_r + li_r, NEG)
        inter = b_c + m
        m_t = jnp.maximum(inter, jnp.max(d, axis=-1, keepdims=True))
        w_in = jnp.exp(inter - m_t)
        qb = qh.astype(BF16)
        s = _dot_nt(qb, kh.astype(BF16)) * jnp.exp(d - m_t)
        num = w_in * _dot(qb, Ch.astype(BF16)) + _dot(s.astype(BF16), vh)
        den = w_in * jnp.sum(qh * nh, axis=-1, keepdims=True) + jnp.sum(s, axis=-1, keepdims=True)
        hh = num / jnp.maximum(jnp.abs(den), jnp.exp(-m_t))

        b_end = b_c[L - 1:L, :]
        g_c = b_end - b_c + li_c
        m_new = jnp.maximum(b_end + m, jnp.max(g_c, axis=0, keepdims=True))
        w_c = jnp.exp(b_end + m - m_new)
        kw = kh * jnp.exp(g_c - m_new)
        C_s[h] = w_c * Ch + lax.dot_general(kw.astype(BF16), vh, TN_DIMS, preferred_element_type=F32)
        n_s[h] = w_c * nh + jnp.sum(kw, axis=0, keepdims=True)
        m_s[h] = m_new

        hn = _rms(hh, mg_ref[:, vs])
        y_ref[:, vs] = jax.nn.sigmoid(o_ref[:, vs]) * hn

    @pl.when(c == pl.num_programs(1) - 1)
    def _():
        cN_ref[...] = C_s[...]
        nN_ref[...] = n_s[...]
        mN_ref[...] = m_s[...]


def _mlstm(z3, zg3, gate_b, mnorm_g, C0, n0, m0, L, t_valid):
    N, Tp = z3.shape[:2]
    nc = Tp // L
    H = MLSTM_HEADS
    qk_w = H * MLSTM_DK
    v_w = H * MLSTM_DV
    st = lambda n, c: (n, 0, 0, 0)
    outs = pl.pallas_call(
        functools.partial(_mlstm_body, L=L, t_valid=t_valid),
        grid=(N, nc),
        in_specs=[
            pl.BlockSpec((None, L, qk_w), lambda n, c: (n, c, D_HALF // qk_w)),
            pl.BlockSpec((None, L, qk_w), lambda n, c: (n, c, D_HALF // qk_w + 1)),
            pl.BlockSpec((None, L, v_w), lambda n, c: (n, c, 2)),
            pl.BlockSpec((None, L, v_w), lambda n, c: (n, c, 3)),
            pl.BlockSpec((None, L, LANE), lambda n, c: (n, c, 0)),
            pl.BlockSpec((1, LANE), lambda n, c: (0, 0)),
            pl.BlockSpec((1, v_w), lambda n, c: (0, 0)),
            pl.BlockSpec((None, H, MLSTM_DK, MLSTM_DV), st),
            pl.BlockSpec((None, H, 1, MLSTM_DK), st),
            pl.BlockSpec((None, H, 1, 1), st),
        ],
        out_specs=[
            pl.BlockSpec((None, L, v_w), lambda n, c: (n, c, 0)),
            pl.BlockSpec((None, H, MLSTM_DK, MLSTM_DV), st),
            pl.BlockSpec((None, H, 1, MLSTM_DK), st),
            pl.BlockSpec((None, H, 1, 1), st),
        ],
        out_shape=[
            jax.ShapeDtypeStruct((N, Tp, v_w), F32),
            jax.ShapeDtypeStruct((N, H, MLSTM_DK, MLSTM_DV), F32),
            jax.ShapeDtypeStruct((N, H, 1, MLSTM_DK), F32),
            jax.ShapeDtypeStruct((N, H, 1, 1), F32),
        ],
        scratch_shapes=[
            pltpu.VMEM((H, MLSTM_DK, MLSTM_DV), F32),
            pltpu.VMEM((H, 1, MLSTM_DK), F32),
            pltpu.VMEM((H, 1, 1), F32),
        ],
        compiler_params=_cparams("parallel", "arbitrary"),
        name="mlstm",
    )(z3, z3, z3, z3, zg3, gate_b, mnorm_g, C0, n0.reshape(N, H, 1, MLSTM_DK), m0.reshape(N, H, 1, 1))
    y, C, n, m = outs
    return y, C, n.reshape(N, H, MLSTM_DK), m.reshape(N, H)


def _gmlp_body(u_ref, v_ref, g_ref, ws_ref, bs_ref, y_ref, vn_ref, *, tT):
    v = v_ref[...]
    vc = v - jnp.mean(v, axis=-1, keepdims=True)
    vn = vc * lax.rsqrt(jnp.mean(vc * vc, axis=-1, keepdims=True) + 1e-5) * g_ref[...]
    vn_ref[...] = vn
    r_i = lax.broadcasted_iota(jnp.int32, (GMLP_CHUNK, GMLP_CHUNK), 0)
    c_i = lax.broadcasted_iota(jnp.int32, (GMLP_CHUNK, GMLP_CHUNK), 1)
    for g in range(GMLP_GROUPS):
        cs = slice(g * GMLP_DG, (g + 1) * GMLP_DG)
        wm = jnp.where(r_i >= c_i, ws_ref[g], 0.0).astype(BF16)
        bias = bs_ref[:, g:g + 1]
        for ch in range(tT // GMLP_CHUNK):
            rs = slice(ch * GMLP_CHUNK, (ch + 1) * GMLP_CHUNK)
            mix = _dot(wm, vn[rs, cs].astype(BF16)) + bias
            y_ref[rs, cs] = u_ref[rs, cs] * mix


def _gmlp(z3, gnorm_g, ws, bs_t, tT_pref=512):
    N, Tp = z3.shape[:2]
    tT = _tile(Tp, tT_pref)
    assert tT % GMLP_CHUNK == 0
    return pl.pallas_call(
        functools.partial(_gmlp_body, tT=tT),
        grid=(N, Tp // tT),
        in_specs=[
            pl.BlockSpec((None, tT, D_HALF), lambda n, t: (n, t, 0)),
            pl.BlockSpec((None, tT, D_HALF), lambda n, t: (n, t, 1)),
            pl.BlockSpec((1, D_HALF), lambda n, t: (0, 0)),
            pl.BlockSpec((GMLP_GROUPS, GMLP_CHUNK, GMLP_CHUNK), lambda n, t: (0, 0, 0)),
            pl.BlockSpec((GMLP_CHUNK, LANE), lambda n, t: (0, 0)),
        ],
        out_specs=[
            pl.BlockSpec((None, tT, D_HALF), lambda n, t: (n, t, 0)),
            pl.BlockSpec((None, tT, D_HALF), lambda n, t: (n, t, 0)),
        ],
        out_shape=[jax.ShapeDtypeStruct((N, Tp, D_HALF), F32), jax.ShapeDtypeStruct((N, Tp, D_HALF), F32)],
        compiler_params=_cparams("parallel", "parallel"),
        name="gmlp",
    )(z3, z3, gnorm_g, ws, bs_t)


PAGES_PER_STEP = 8
BLK_PER_PAGE = PAGE // CMP_STRIDE


def _compress_body(pt_ref, *refs):
    page_refs = refs[:PAGES_PER_STEP]
    wab_ref, a_ref, b_ref = refs[PAGES_PER_STEP:]
    wa = wab_ref[0]
    wb = wab_ref[1]
    for p, pr in enumerate(page_refs):
        x = pr[...].reshape(BLK_PER_PAGE, CMP_STRIDE, 4 * NSA_DH)
        rs = slice(p * BLK_PER_PAGE, (p + 1) * BLK_PER_PAGE)
        a_ref[rs, :] = jnp.sum(x * wa[None], axis=1)
        b_ref[rs, :] = jnp.sum(x * wb[None], axis=1)


def _compress_ab(pages, page_table, wab, col_block):
    N, n_pages = page_table.shape
    assert n_pages % PAGES_PER_STEP == 0
    W = 4 * NSA_DH
    rows = PAGES_PER_STEP * BLK_PER_PAGE
    page_specs = [
        pl.BlockSpec((None, PAGE, W), functools.partial(
            lambda n, s, pt, r: (pt[n, s * PAGES_PER_STEP + r], 0, col_block), r=r))
        for r in range(PAGES_PER_STEP)
    ]
    grid_spec = pltpu.PrefetchScalarGridSpec(
        num_scalar_prefetch=1,
        grid=(N, n_pages // PAGES_PER_STEP),
        in_specs=page_specs + [pl.BlockSpec((2, CMP_STRIDE, W), lambda n, s, pt: (0, 0, 0))],
        out_specs=[pl.BlockSpec((None, rows, W), lambda n, s, pt: (n, s, 0))] * 2,
    )
    ns = n_pages * BLK_PER_PAGE
    return pl.pallas_call(
        _compress_body,
        grid_spec=grid_spec,
        out_shape=[jax.ShapeDtypeStruct((N, ns, W), F32)] * 2,
        compiler_params=_cparams("parallel", "parallel"),
        name="nsa_compress",
    )(page_table, *([pages] * PAGES_PER_STEP), wab)


ROWS_PER_TOKEN = 2 * NSA_KV
PAGE_ROWS = PAGE * ROWS_PER_TOKEN
BLK_ROWS = CMP_STRIDE * ROWS_PER_TOKEN


def _compress_rows_body(pt_ref, *refs):
    page_refs = refs[:PAGES_PER_STEP]
    wab_ref, a_ref, b_ref, z_s = refs[PAGES_PER_STEP:]
    for p, pr in enumerate(page_refs):
        x = pr[...].reshape(BLK_PER_PAGE, BLK_ROWS, NSA_DH)
        for half, out_ref in enumerate((a_ref, b_ref)):
            y = (x * wab_ref[half][None]).reshape(BLK_PER_PAGE, BLK_ROWS // 8, 8, NSA_DH).sum(axis=1)
            y = y.reshape(BLK_PER_PAGE * 8, NSA_DH)
            slot = 2 * p + half
            z_s[slot] = y + pltpu.roll(y, BLK_PER_PAGE * 8 - ROWS_PER_TOKEN, 0)
            for c in range(ROWS_PER_TOKEN):
                out_ref[p * BLK_PER_PAGE:(p + 1) * BLK_PER_PAGE, c * NSA_DH:(c + 1) * NSA_DH] = (
                    z_s[slot, pl.ds(c, BLK_PER_PAGE, stride=8), :])


def _compress_ab_rows(rows, page_table, wab_rows):
    N, n_pages = page_table.shape
    assert n_pages % PAGES_PER_STEP == 0
    W = ROWS_PER_TOKEN * NSA_DH
    out_rows = PAGES_PER_STEP * BLK_PER_PAGE
    page_specs = [
        pl.BlockSpec((PAGE_ROWS, NSA_DH), functools.partial(
            lambda n, s, pt, r: (pt[n, s * PAGES_PER_STEP + r], 0), r=r))
        for r in range(PAGES_PER_STEP)
    ]
    grid_spec = pltpu.PrefetchScalarGridSpec(
        num_scalar_prefetch=1,
        grid=(N, n_pages // PAGES_PER_STEP),
        in_specs=page_specs + [pl.BlockSpec((2, BLK_ROWS, NSA_DH), lambda n, s, pt: (0, 0, 0))],
        out_specs=[pl.BlockSpec((None, out_rows, W), lambda n, s, pt: (n, s, 0))] * 2,
        scratch_shapes=[pltpu.VMEM((2 * PAGES_PER_STEP, BLK_PER_PAGE * 8, NSA_DH), F32)],
    )
    ns = n_pages * BLK_PER_PAGE
    return pl.pallas_call(
        _compress_rows_body,
        grid_spec=grid_spec,
        out_shape=[jax.ShapeDtypeStruct((N, ns, W), F32)] * 2,
        compiler_params=_cparams("parallel", "arbitrary"),
        name="nsa_compress_rows",
    )(page_table, *([rows] * PAGES_PER_STEP), wab_rows)


def _cmp_proj_body(a_ref, b_ref, w_ref, kc_ref, vc_ref, *, ns):
    b_next = pltpu.roll(b_ref[...], ns - 1, 0)
    row = lax.broadcasted_iota(jnp.int32, (ns, 1), 0)
    blk = jnp.where(row < ns - 1, a_ref[...] + b_next, 0.0).astype(BF16)
    w0 = w_ref[0].astype(BF16)
    w1 = w_ref[1].astype(BF16)
    for kv in range(NSA_KV):
        cs = slice(kv * NSA_DH, (kv + 1) * NSA_DH)
        kc_ref[:, cs] = _dot(blk[:, kv * NSA_DH:(kv + 1) * NSA_DH], w0)
        vc_ref[:, cs] = _dot(blk[:, (NSA_KV + kv) * NSA_DH:(NSA_KV + kv + 1) * NSA_DH], w1)


def _cmp_proj(a, b, w_cmp):
    N, ns, W = a.shape
    return pl.pallas_call(
        functools.partial(_cmp_proj_body, ns=ns),
        grid=(N,),
        in_specs=[
            pl.BlockSpec((None, ns, W), lambda n: (n, 0, 0)),
            pl.BlockSpec((None, ns, W), lambda n: (n, 0, 0)),
            pl.BlockSpec((2, NSA_DH, NSA_DH), lambda n: (0, 0, 0)),
        ],
        out_specs=[pl.BlockSpec((None, ns, NSA_KV * NSA_DH), lambda n: (n, 0, 0))] * 2,
        out_shape=[jax.ShapeDtypeStruct((N, ns, NSA_KV * NSA_DH), F32)] * 2,
        compiler_params=_cparams("parallel"),
        name="nsa_cmp_proj",
    )(a, b, w_cmp)


def _split3(x):
    hi = x.astype(BF16)
    r1 = x - hi.astype(F32)
    mid = r1.astype(BF16)
    lo = (r1 - mid.astype(F32)).astype(BF16)
    return hi, mid, lo


def _cmp_to_sel(p_sum, a_mat):
    hi, mid, lo = _split3(p_sum)
    return _dot(hi, a_mat) + _dot(mid, a_mat) + _dot(lo, a_mat)


def _topk_mask(score, n_valid, k):
    lane = lax.broadcasted_iota(jnp.int32, score.shape, 1)
    rank = jnp.zeros(score.shape, F32)
    for jp in range(n_valid):
        col = score[:, jp:jp + 1]
        beats = (col > score) | ((col == score) & (lane > jp))
        rank = rank + beats.astype(F32)
    return ((rank < k) & (lane < n_valid)).astype(F32)


def _sel_scores(imp, qblk, n_valid):
    lane = lax.broadcasted_iota(jnp.int32, imp.shape, 1)
    forced = (lane == 0) | (lane == qblk) | (lane == qblk - 1)
    score = jnp.where(forced, FORCE_SCORE, jnp.where(lane > qblk, -1.0, imp))
    return jnp.where(lane < n_valid, score, -2.0)


def _masked_softmax_rows(s, mask):
    s = jnp.where(mask, s, NEG)
    e = jnp.exp(s - jnp.max(s, axis=-1, keepdims=True))
    return jnp.where(mask, e / jnp.sum(e, axis=-1, keepdims=True), 0.0)


def _stack_heads(q, scale):
    return (jnp.concatenate([q[:, g * NSA_DH:(g + 1) * NSA_DH] for g in range(NSA_G)], axis=0) * scale).astype(BF16)


SEL_TK = 256
WIN_KEYS = WINDOW + QBLK


def _nsa_prompt_body(q_ref, zg_ref, gb_ref, kc_ref, vc_ref, ks_ref, vs_ref, kw_ref, vw_ref, amat_t_ref, exp_ref,
                     o_ref, ks_bf, vs_bf, kw_bf, vw_bf, kc_bf, vc_bf, sc_ref, *, n_sel, nsr):
    qb = pl.program_id(1)

    @pl.when(qb == 0)
    def _():
        ks_bf[...] = ks_ref[...].astype(BF16)
        vs_bf[...] = vs_ref[...].astype(BF16)
        kw_bf[...] = kw_ref[...].astype(BF16)
        vw_bf[...] = vw_ref[...].astype(BF16)
        kc_bf[...] = kc_ref[...].astype(BF16)
        vc_bf[...] = vc_ref[...].astype(BF16)

    R = QBLK
    start = qb * R
    heads = range(NSA_HEADS)
    groups = range(NSA_KV)
    kvc = lambda kv: slice(kv * NSA_DH, (kv + 1) * NSA_DH)
    qh = [(q_ref[:, h * NSA_DH:(h + 1) * NSA_DH] * (NSA_DH ** -0.5)).astype(BF16) for h in heads]
    pos = start + lax.broadcasted_iota(jnp.int32, (R, 1), 0)

    ns = kc_bf.shape[0]
    cmp_end = (lax.broadcasted_iota(jnp.int32, (R, ns), 1) + 2) * CMP_STRIDE - 1
    bias_c = jnp.where(cmp_end <= pos, 0.0, NEG)
    any_c = jnp.where(pos >= 2 * CMP_STRIDE - 1, 1.0, 0.0)
    s_c = [_dot_nt(qh[h], kc_bf[:, kvc(h // NSA_G)]) for h in heads]
    o_cmp, p_sum = [], [None] * NSA_KV
    for h in heads:
        kv = h // NSA_G
        e_c = jnp.exp(s_c[h] + bias_c - jnp.max(s_c[h] + bias_c, axis=-1, keepdims=True))
        p_c = e_c * (any_c / jnp.sum(e_c, axis=-1, keepdims=True))
        o_cmp.append(_dot(p_c.astype(BF16), vc_bf[:, kvc(kv)]))
        p_sum[kv] = p_c if p_sum[kv] is None else p_sum[kv] + p_c

    a_t = amat_t_ref[...]
    blk = lax.broadcasted_iota(jnp.int32, (nsr, R), 0)
    qblk = (start + lax.broadcasted_iota(jnp.int32, (nsr, R), 1)) // SEL_BLOCK
    forced = (blk == 0) | (blk == qblk) | (blk == qblk - 1)
    score = []
    for kv in groups:
        hi, mid, lo = _split3(p_sum[kv])
        imp_t = _dot_nt(a_t, hi) + _dot_nt(a_t, mid) + _dot_nt(a_t, lo)
        sc = jnp.where(forced, FORCE_SCORE, jnp.where(blk > qblk, -1.0, imp_t))
        score.append(jnp.where(blk < n_sel, sc, -2.0))
        sc_ref[kv] = score[kv]

    def rank_step(i, ranks):
        ranks = list(ranks)
        for jp in (2 * i, 2 * i + 1):
            for kv in groups:
                row = sc_ref[kv, pl.ds(jp, 1), :]
                beats = (row > score[kv]) | ((row == score[kv]) & (blk > jp))
                ranks[kv] = ranks[kv] + beats.astype(F32)
        return tuple(ranks)

    n_rank = jnp.minimum(start // SEL_BLOCK + QBLK // SEL_BLOCK, n_sel)
    ranks = lax.fori_loop(0, (n_rank + 1) // 2, rank_step, tuple(jnp.zeros((nsr, R), F32) for _ in groups))
    pad = jnp.zeros((LANE - nsr, R), F32)
    sel = [jnp.concatenate([jnp.where(ranks[kv] < N_SEL, 1.0, 0.0), pad], axis=0).T.astype(BF16) for kv in groups]

    lane_tk = lax.broadcasted_iota(jnp.int32, (R, SEL_TK), 1)

    def sel_tile(kt, carry, causal):
        k0 = pl.multiple_of(kt * SEL_TK, SEL_TK)
        k_t = ks_bf[pl.ds(k0, SEL_TK), :]
        v_t = vs_bf[pl.ds(k0, SEL_TK), :]
        bias = [(_dot(sel[kv], exp_ref[kt]) - 1.0) * (-NEG) for kv in groups]
        if causal:
            bias = [jnp.where(k0 + lane_tk <= pos, b, NEG) for b in bias]
        scores = [_dot_nt(qh[h], k_t[:, kvc(h // NSA_G)]) for h in heads]
        stats = []
        for h in heads:
            m_prev, l_prev, _ = carry[h]
            s = scores[h] + bias[h // NSA_G]
            m_new = jnp.maximum(m_prev, jnp.max(s, axis=-1, keepdims=True))
            alpha = jnp.exp(m_prev - m_new)
            p = jnp.exp(s - m_new)
            stats.append((m_new, alpha, alpha * l_prev + jnp.sum(p, axis=-1, keepdims=True), p.astype(BF16)))
        return tuple((m_new, l_new, alpha * carry[h][2] + _dot(p, v_t[:, kvc(h // NSA_G)]))
                     for h, (m_new, alpha, l_new, p) in enumerate(stats))

    n_tiles = (start + R - 1) // SEL_TK + 1
    init = tuple((jnp.full((R, 1), NEG, F32), jnp.zeros((R, 1), F32), jnp.zeros((R, NSA_DH), F32)) for _ in heads)
    carry = lax.fori_loop(0, n_tiles - 1, lambda kt, c: sel_tile(kt, c, False), init)
    o_sel = [acc / l for _, l, acc in sel_tile(n_tiles - 1, carry, True)]

    w0 = pl.multiple_of(jnp.maximum(start - WINDOW, 0), QBLK)
    tok_w = w0 + lax.broadcasted_iota(jnp.int32, (R, WIN_KEYS), 1)
    bias_w = jnp.where((tok_w <= pos) & (tok_w > pos - WINDOW), 0.0, NEG)
    k_w = kw_bf[pl.ds(w0, WIN_KEYS), :]
    v_w = vw_bf[pl.ds(w0, WIN_KEYS), :]
    s_w = [_dot_nt(qh[h], k_w[:, kvc(h // NSA_G)]) for h in heads]

    gates = jax.nn.sigmoid(zg_ref[...] + gb_ref[...])
    gate = lambda branch, h: gates[:, branch * NSA_HEADS + h:branch * NSA_HEADS + h + 1]
    for h in heads:
        e_w = jnp.exp(s_w[h] + bias_w - jnp.max(s_w[h] + bias_w, axis=-1, keepdims=True))
        o_win = _dot(e_w.astype(BF16), v_w[:, kvc(h // NSA_G)]) / jnp.sum(e_w, axis=-1, keepdims=True)
        o_ref[:, h * NSA_DH:(h + 1) * NSA_DH] = gate(0, h) * o_cmp[h] + gate(1, h) * o_sel[h] + gate(2, h) * o_win


def _sel_map(ns, n_sel, nsb):
    i = np.arange(ns)[:, None]
    j = np.arange(nsb)[None, :]
    r = SEL_BLOCK // CMP_STRIDE
    return jnp.asarray(((i >= r * j - 1) & (i <= r * j + r - 1) & (j < n_sel)).astype(np.float32), BF16)


def _nsa_prompt(z3, zg3, gate_b, kc, vc):
    N, T = z3.shape[:2]
    assert T % SEL_TK == 0 and T >= WIN_KEYS
    ns = kc.shape[1]
    n_sel = T // SEL_BLOCK
    nsr = -(-n_sel // 8) * 8
    assert nsr <= LANE
    amat_t = _sel_map(ns, n_sel, nsr).T
    tok = np.arange(T).reshape(T // SEL_TK, 1, SEL_TK)
    expand = jnp.asarray((tok // SEL_BLOCK == np.arange(LANE)[None, :, None]).astype(np.float32), BF16)
    KVW = NSA_KV * NSA_DH
    qcol = 2 * D_HALF // (NSA_HEADS * NSA_DH)
    kvs_col = (3 * D_HALF + 2 * KVW) // KVW
    kvw_col = kvs_col + 2
    full = lambda off: pl.BlockSpec((None, T, KVW), lambda n, qb: (n, 0, off))
    return pl.pallas_call(
        functools.partial(_nsa_prompt_body, n_sel=n_sel, nsr=nsr),
        grid=(N, T // QBLK),
        in_specs=[
            pl.BlockSpec((None, QBLK, NSA_HEADS * NSA_DH), lambda n, qb: (n, qb, qcol)),
            pl.BlockSpec((None, QBLK, LANE), lambda n, qb: (n, qb, 0)),
            pl.BlockSpec((1, LANE), lambda n, qb: (0, 0)),
            pl.BlockSpec((None, ns, KVW), lambda n, qb: (n, 0, 0)),
            pl.BlockSpec((None, ns, KVW), lambda n, qb: (n, 0, 0)),
            full(kvs_col), full(kvs_col + 1), full(kvw_col), full(kvw_col + 1),
            pl.BlockSpec((nsr, ns), lambda n, qb: (0, 0)),
            pl.BlockSpec((T // SEL_TK, LANE, SEL_TK), lambda n, qb: (0, 0, 0)),
        ],
        out_specs=pl.BlockSpec((None, QBLK, NSA_HEADS * NSA_DH), lambda n, qb: (n, qb, 0)),
        out_shape=jax.ShapeDtypeStruct((N, T, NSA_HEADS * NSA_DH), F32),
        scratch_shapes=([pltpu.VMEM((T, KVW), BF16)] * 4 + [pltpu.VMEM((ns, KVW), BF16)] * 2
                        + [pltpu.VMEM((NSA_KV, nsr, QBLK), F32)]),
        compiler_params=_cparams("parallel", "arbitrary"),
        name="nsa_prompt",
    )(z3, zg3, gate_b, kc, vc, z3, z3, z3, z3, amat_t, expand)


def _nsa_sample_a_body(q_ref, kc_ref, vc_ref, kw_ref, vw_ref, amat_ref, ocmp_ref, owin_ref, sel_ref,
                       *, Tq, past_len, n_sel, wb):
    qs = _stack_heads(q_ref[...], NSA_DH ** -0.5)
    pos = past_len + lax.broadcasted_iota(jnp.int32, (Tq, 1), 0)
    pos4 = jnp.concatenate([pos] * NSA_G, axis=0)

    ns = kc_ref.shape[0]
    cmp_i = lax.broadcasted_iota(jnp.int32, (NSA_G * Tq, ns), 1)
    m_c = ((cmp_i + 2) * CMP_STRIDE - 1 <= pos4) & (cmp_i < ns - 1)
    p_c = _masked_softmax_rows(_dot_nt(qs, kc_ref[...].astype(BF16)), m_c)
    ocmp_ref[...] = _dot(p_c.astype(BF16), vc_ref[...].astype(BF16))
    p_sum = p_c[0:Tq] + p_c[Tq:2 * Tq] + p_c[2 * Tq:3 * Tq] + p_c[3 * Tq:4 * Tq]
    imp = _cmp_to_sel(p_sum, amat_ref[...])
    sel_ref[...] = _topk_mask(_sel_scores(imp, pos // SEL_BLOCK, n_sel), n_sel, N_SEL)

    nw = kw_ref.shape[0]
    tok_w = past_len - wb + lax.broadcasted_iota(jnp.int32, (NSA_G * Tq, nw), 1)
    m_w = (tok_w >= 0) & (tok_w <= pos4) & (tok_w > pos4 - WINDOW)
    p_w = _masked_softmax_rows(_dot_nt(qs, kw_ref[...].astype(BF16)), m_w)
    owin_ref[...] = _dot(p_w.astype(BF16), vw_ref[...].astype(BF16))


def _nsa_sample_a(z3, kc, vc, kw_full, past_len, wb):
    N, Tq = z3.shape[:2]
    ns = kc.shape[1]
    nw = kw_full.shape[1]
    n_sel = -(-(past_len + Tq) // SEL_BLOCK)
    nsb = -(-n_sel // LANE) * LANE
    amat = _sel_map(ns, n_sel, nsb)
    qcol = 2 * D_HALF // (NSA_G * NSA_DH)
    R4 = NSA_G * Tq
    return pl.pallas_call(
        functools.partial(_nsa_sample_a_body, Tq=Tq, past_len=past_len, n_sel=n_sel, wb=wb),
        grid=(N, NSA_KV),
        in_specs=[
            pl.BlockSpec((None, Tq, NSA_G * NSA_DH), lambda n, kv: (n, 0, qcol + kv)),
            pl.BlockSpec((None, ns, NSA_DH), lambda n, kv: (n, 0, kv)),
            pl.BlockSpec((None, ns, NSA_DH), lambda n, kv: (n, 0, kv)),
            pl.BlockSpec((None, nw, NSA_DH), lambda n, kv: (n, 0, kv)),
            pl.BlockSpec((None, nw, NSA_DH), lambda n, kv: (n, 0, NSA_KV + kv)),
            pl.BlockSpec((ns, nsb), lambda n, kv: (0, 0)),
        ],
        out_specs=[
            pl.BlockSpec((None, None, R4, NSA_DH), lambda n, kv: (n, kv, 0, 0)),
            pl.BlockSpec((None, None, R4, NSA_DH), lambda n, kv: (n, kv, 0, 0)),
            pl.BlockSpec((None, None, Tq, nsb), lambda n, kv: (n, kv, 0, 0)),
        ],
        out_shape=[
            jax.ShapeDtypeStruct((N, NSA_KV, R4, NSA_DH), F32),
            jax.ShapeDtypeStruct((N, NSA_KV, R4, NSA_DH), F32),
            jax.ShapeDtypeStruct((N, NSA_KV, Tq, nsb), F32),
        ],
        compiler_params=_cparams("parallel", "parallel"),
        name="nsa_sample_cmp_win",
    )(z3, kc, vc, kw_full, kw_full, amat)


BLK_PER_STEP = PAGES_PER_STEP * PAGE // SEL_BLOCK


def _nsa_sample_b_body(pt_ref, *refs, Tq, past_len):
    page_refs = refs[:PAGES_PER_STEP]
    (q_ref, new_ref, sel_ref, exp_ref, ocmp_ref, owin_ref, zg_ref, gb_ref, o_ref, m_s, l_s, acc_s) = refs[PAGES_PER_STEP:]
    s_id = pl.program_id(1)
    R4 = NSA_G * Tq
    rep = lambda x: jnp.concatenate([x] * NSA_G, axis=0)
    pos = past_len + lax.broadcasted_iota(jnp.int32, (Tq, 1), 0)

    @pl.when(s_id == 0)
    def _():
        m_s[...] = jnp.full(m_s.shape, NEG, F32)
        l_s[...] = jnp.zeros(l_s.shape, F32)
        acc_s[...] = jnp.zeros(acc_s.shape, F32)

    def update(kv, s, mask, pv):
        s = jnp.where(mask, s, NEG)
        m_prev = m_s[kv]
        m_new = jnp.maximum(m_prev, jnp.max(s, axis=-1, keepdims=True))
        alpha = jnp.exp(m_prev - m_new)
        p = jnp.where(mask, jnp.exp(s - m_new), 0.0)
        l_s[kv] = alpha * l_s[kv] + jnp.sum(p, axis=-1, keepdims=True)
        acc_s[kv] = alpha * acc_s[kv] + pv(p.astype(BF16))
        m_s[kv] = m_new

    qs = [_stack_heads(q_ref[:, kv * NSA_G * NSA_DH:(kv + 1) * NSA_G * NSA_DH], NSA_DH ** -0.5) for kv in range(NSA_KV)]
    comp = lambda pr, c: pr[pl.ds(c, PAGE, stride=ROWS_PER_TOKEN), :].astype(BF16)
    for kv in range(NSA_KV):
        s = jnp.concatenate([_dot_nt(qs[kv], comp(pr, kv)) for pr in page_refs], axis=1)
        mask = rep(_dot(sel_ref[kv].astype(BF16), exp_ref[...])) > 0.5

        def pv(p, kv=kv):
            out = _dot(p[:, 0:PAGE], comp(page_refs[0], NSA_KV + kv))
            for i in range(1, PAGES_PER_STEP):
                out = out + _dot(p[:, i * PAGE:(i + 1) * PAGE], comp(page_refs[i], NSA_KV + kv))
            return out

        update(kv, s, mask, pv)

    @pl.when(s_id == pl.num_programs(1) - 1)
    def _():
        new = new_ref[...].astype(BF16)
        tok_n = past_len + lax.broadcasted_iota(jnp.int32, (R4, Tq), 1)
        gates = jax.nn.sigmoid(zg_ref[...] + gb_ref[...])
        for kv in range(NSA_KV):
            k_new = new[:, kv * NSA_DH:(kv + 1) * NSA_DH]
            v_new = new[:, (NSA_KV + kv) * NSA_DH:(NSA_KV + kv + 1) * NSA_DH]
            update(kv, _dot_nt(qs[kv], k_new), tok_n <= rep(pos), lambda p, v_new=v_new: _dot(p, v_new))
            o_sel = acc_s[kv] / l_s[kv]
            o_cmp = ocmp_ref[kv]
            o_win = owin_ref[kv]
            for g in range(NSA_G):
                rs = slice(g * Tq, (g + 1) * Tq)
                head = kv * NSA_G + g
                o_ref[:, head * NSA_DH:(head + 1) * NSA_DH] = (
                    gates[:, head:head + 1] * o_cmp[rs]
                    + gates[:, NSA_HEADS + head:NSA_HEADS + head + 1] * o_sel[rs]
                    + gates[:, 2 * NSA_HEADS + head:2 * NSA_HEADS + head + 1] * o_win[rs])


def _nsa_sample_b(z3, zg3, gate_b, pool_sel, page_table, sel, o_cmp, o_win, past_len):
    N, Tq = z3.shape[:2]
    n_pages = page_table.shape[1]
    assert n_pages % PAGES_PER_STEP == 0 and past_len % SEL_BLOCK == 0
    n_steps = n_pages // PAGES_PER_STEP
    sel_steps = sel[..., :past_len // SEL_BLOCK].reshape(N, NSA_KV, Tq, n_steps, BLK_PER_STEP).transpose(0, 3, 1, 2, 4)
    tok = np.arange(PAGES_PER_STEP * PAGE)[None, :]
    expand = jnp.asarray((tok // SEL_BLOCK == np.arange(BLK_PER_STEP)[:, None]).astype(np.float32), BF16)
    W = 4 * NSA_DH
    R4 = NSA_G * Tq
    qcol = 2 * D_HALF // (NSA_HEADS * NSA_DH)
    kvs_col = (3 * D_HALF + W) // W
    page_specs = [
        pl.BlockSpec((PAGE_ROWS, NSA_DH), functools.partial(
            lambda n, s, pt, r: (pt[n, s * PAGES_PER_STEP + r], 0), r=r))
        for r in range(PAGES_PER_STEP)
    ]
    per_n4 =lambda shape: pl.BlockSpec((None,) + shape, lambda n, s, pt: (n, 0, 0, 0))
    grid_spec = pltpu.PrefetchScalarGridSpec(
        num_scalar_prefetch=1,
        grid=(N, n_pages // PAGES_PER_STEP),
        in_specs=page_specs + [
            pl.BlockSpec((None, Tq, NSA_HEADS * NSA_DH), lambda n, s, pt: (n, 0, qcol)),
            pl.BlockSpec((None, Tq, W), lambda n, s, pt: (n, 0, kvs_col)),
            pl.BlockSpec((None, None, NSA_KV, Tq, BLK_PER_STEP), lambda n, s, pt: (n, s, 0, 0, 0)),
            pl.BlockSpec((BLK_PER_STEP, PAGES_PER_STEP * PAGE), lambda n, s, pt: (0, 0)),
            per_n4((NSA_KV, R4, NSA_DH)),
            per_n4((NSA_KV, R4, NSA_DH)),
            pl.BlockSpec((None, Tq, LANE), lambda n, s, pt: (n, 0, 0)),
            pl.BlockSpec((1, LANE), lambda n, s, pt: (0, 0)),
        ],
        out_specs=pl.BlockSpec((None, Tq, NSA_HEADS * NSA_DH), lambda n, s, pt: (n, 0, 0)),
        scratch_shapes=[
            pltpu.VMEM((NSA_KV, R4, 1), F32),
            pltpu.VMEM((NSA_KV, R4, 1), F32),
            pltpu.VMEM((NSA_KV, R4, NSA_DH), F32),
        ],
    )
    return pl.pallas_call(
        functools.partial(_nsa_sample_b_body, Tq=Tq, past_len=past_len),
        grid_spec=grid_spec,
        out_shape=jax.ShapeDtypeStruct((N, Tq, NSA_HEADS * NSA_DH), F32),
        compiler_params=_cparams("parallel", "arbitrary"),
        name="nsa_sample_sel",
    )(page_table, *([pool_sel] * PAGES_PER_STEP), z3, z3, sel_steps, expand, o_cmp, o_win, zg3, gate_b)


MLSTM_CHUNK_PROMPT = 256
SAMPLE_PAD = 128


def _pad_rows(x, rows):
    return jnp.pad(x, ((0, 0), (0, rows - x.shape[1]), (0, 0)))


def _lane_pad(v):
    return jnp.pad(v.astype(F32), (0, LANE - v.shape[0])).reshape(1, LANE)


def _gate_cols(w_in, n_main):
    return jnp.pad(w_in[:, n_main:], ((0, 0), (0, LANE - (w_in.shape[1] - n_main))))


def _even_mixers(z3, zg3, t0, pool_buf, C0, n0, m0, w):
    N, T = z3.shape[:2]
    st = jnp.pad(pool_buf, ((0, 0), (HALO - POOL_PAD, 0), (0, 0)))
    y_a = _pool_mix(z3, st, w["pool_w"], w["pool_scale"], t0)
    u_ext_tail = jnp.concatenate([pool_buf, z3[:, :, :D_HALF]], axis=1)[:, -POOL_PAD:] if T < POOL_PAD else z3[:, -POOL_PAD:, :D_HALF]

    if T % MLSTM_CHUNK_PROMPT == 0:
        L, zm, zgm = MLSTM_CHUNK_PROMPT, z3, zg3
    else:
        L, zm, zgm = SAMPLE_PAD, _pad_rows(z3, SAMPLE_PAD), _pad_rows(zg3, SAMPLE_PAD)
    y_b, C, n, m = _mlstm(zm, zgm, w["gate_b"], w["mnorm_g"], C0, n0, m0, L, min(T, L))
    return y_a, y_b[:, :T], u_ext_tail, C, n, m


def _odd_mixers(z3, zg3, w, past=None):
    N, T, n_main = z3.shape
    W = 4 * NSA_DH
    kvc = z3[:, :, 3 * D_HALF:3 * D_HALF + W]
    kvs = z3[:, :, 3 * D_HALF + W:3 * D_HALF + 2 * W]
    kvw = z3[:, :, 3 * D_HALF + 2 * W:3 * D_HALF + 3 * W]

    if past is None:
        y_c, _ = _gmlp(z3, w["gnorm_g"], w["ws"], w["bs_t"])
        vn = None
        pt = jnp.arange(N * (T // PAGE), dtype=jnp.int32).reshape(N, T // PAGE)
        a, b = _compress_ab(z3.reshape(N * (T // PAGE), PAGE, n_main), pt, w["wab"], 3 * D_HALF // W)
        kc, vc = _cmp_proj(a, b, w["cmp_w"])
        o = _nsa_prompt(z3, zg3, w["nsa_gate_b"], kc, vc)
        win_state = kvw[:, -min(WINDOW, T):]
    else:
        win_buf, pool_cmp, pool_sel, page_table = past
        past_len = page_table.shape[1] * PAGE
        wb = win_buf.shape[1]
        y_c, vn = _gmlp(_pad_rows(z3[:, :, :2 * D_HALF], SAMPLE_PAD), w["gnorm_g"], w["ws"], w["bs_t"])
        y_c, vn = y_c[:, :T], vn[:, :T]
        assert (past_len + T) // CMP_STRIDE == past_len // CMP_STRIDE
        a, b = _compress_ab_rows(pool_cmp, page_table, w["wab_rows"])
        kc, vc = _cmp_proj(a, b, w["cmp_w"])
        kw_all = jnp.concatenate([win_buf.reshape(N, wb, W), kvw], axis=1)
        nw = -(-(wb + T) // LANE) * LANE
        o_cmp, o_win, sel = _nsa_sample_a(z3, kc, vc, _pad_rows(kw_all, nw), past_len, wb)
        o = _nsa_sample_b(z3, zg3, w["nsa_gate_b"], pool_sel, page_table, sel, o_cmp, o_win, past_len)
        win_state = kw_all[:, -wb:]

    kv5 = lambda t: t.reshape(N, t.shape[1], 2, NSA_KV, NSA_DH)
    return y_c, o, vn, kv5(kvc), kv5(kvs), kv5(win_state)


def _layer(xp, xs, w, ffn_stacks, layer, n_main, mixers_p, mixers_s):
    B, T, D = xp.shape
    Ns, Ts, _ = xs.shape
    xp2, xs2 = xp.reshape(B * T, D), xs.reshape(Ns * Ts, D)
    zp, zgp, zs, zgs = _norm_matmul(xp2, xs2, w["ng"][0:1], w["w_in"], w["w_in_layer"], n_main, w["w_gate"])
    a1p, a2p, *extra_p = mixers_p(zp.reshape(B, T, n_main), zgp.reshape(B, T, LANE))
    a1s, a2s, *extra_s = mixers_s(zs.reshape(Ns, Ts, n_main), zgs.reshape(Ns, Ts, LANE))
    flat = lambda t: t.reshape(-1, D_HALF)
    xp2, xs2 = _out_proj(flat(a1p), flat(a2p), xp2, flat(a1s), flat(a2s), xs2, w["w_out_a"], w["w_out_b"], w["ng"][1:2])
    xp2, xs2 = _ffn(xp2, xs2, w["ng"][2:3], w["ng"][3:4], *ffn_stacks, layer)
    return xp2.reshape(B, T, D), xs2.reshape(Ns, Ts, D), extra_p, extra_s


def kernel(x_prompt, x_sample, state_pool, state_mlstm_c, state_mlstm_n, state_mlstm_m, state_win_kv, cache_cmp_kv, cache_sel_kv, page_table, norm_g, w_in_even, w_out_even, pool_w, pool_scale, mlstm_gate_b, mlstm_norm_g, w_in_odd, w_out_odd, gmlp_norm_g, gmlp_ws, gmlp_bs, nsa_cmp_pos, nsa_cmp_w, nsa_gate_b, ffn_w1, ffn_w3, ffn_w2):
    B = x_prompt.shape[0]
    depth = norm_g.shape[0]
    past_len = page_table.shape[1] * PAGE
    xp, xs = x_prompt, x_sample
    pool_p, pool_s, c_p, c_s, n_p, n_s, m_p, m_s = [], [], [], [], [], [], [], []
    gv_s, cmp_p, cmp_s, sel_p, sel_s, win_p, win_s = [], [], [], [], [], [], []
    ffn_stacks = (ffn_w1, ffn_w3, ffn_w2)
    for l in range(depth):
        j = l // 2
        if l % 2 == 0:
            n_main = 4 * D_HALF
            w_out = w_out_even[j].astype(BF16)
            w = dict(ng=norm_g[l], w_in=w_in_even, w_in_layer=j, w_gate=_gate_cols(w_in_even[j], n_main),
                     w_out_a=w_out[:D_HALF], w_out_b=w_out[D_HALF:],
                     pool_w=pool_w[j].astype(BF16), pool_scale=pool_scale[j].reshape(1, D_HALF),
                     gate_b=_lane_pad(mlstm_gate_b[j].reshape(-1)), mnorm_g=mlstm_norm_g[j].reshape(1, D_HALF))
            zp = jnp.zeros((B, POOL_PAD, D_HALF), F32)
            zc = jnp.zeros((B, MLSTM_HEADS, MLSTM_DK, MLSTM_DV), F32)
            zn = jnp.zeros((B, MLSTM_HEADS, MLSTM_DK), F32)
            zm = jnp.zeros((B, MLSTM_HEADS), F32)
            xp, xs, (pb, c, n, m), (pbs, cs, ns_, ms) = _layer(
                xp, xs, w, ffn_stacks, l, n_main,
                lambda z3, zg3: _even_mixers(z3, zg3, 0, zp, zc, zn, zm, w),
                lambda z3, zg3: _even_mixers(z3, zg3, past_len, state_pool[j], state_mlstm_c[j], state_mlstm_n[j],
                                             state_mlstm_m[j], w))
            pool_p.append(pb); c_p.append(c); n_p.append(n); m_p.append(m)
            pool_s.append(pbs); c_s.append(cs); n_s.append(ns_); m_s.append(ms)
        else:
            n_main = 4 * D_HALF + D_HALF // 2
            w_out = w_out_odd[j].astype(BF16)
            cp = nsa_cmp_pos[j]
            wcol = jnp.repeat(cp, NSA_KV * NSA_DH, axis=1)
            wab = jnp.stack([wcol[:CMP_STRIDE], wcol[CMP_STRIDE:]])
            wab_rows = jnp.broadcast_to(jnp.repeat(cp, NSA_KV, axis=1).reshape(2, BLK_ROWS, 1), (2, BLK_ROWS, NSA_DH))
            w = dict(ng=norm_g[l], w_in=w_in_odd, w_in_layer=j, w_gate=_gate_cols(w_in_odd[j], n_main),
                     w_out_a=w_out[:D_HALF], w_out_b=w_out[D_HALF:],
                     gnorm_g=gmlp_norm_g[j].reshape(1, D_HALF), ws=gmlp_ws[j],
                     bs_t=jnp.pad(gmlp_bs[j].T, ((0, 0), (0, LANE - GMLP_GROUPS))),
                     wab=wab, wab_rows=wab_rows, cmp_w=nsa_cmp_w[j], nsa_gate_b=_lane_pad(nsa_gate_b[j]))
            n_pool = cache_cmp_kv.shape[1]
            flat = lambda c: c.reshape(c.shape[0] * n_pool * PAGE_ROWS, NSA_DH)
            past = (state_win_kv[j], flat(cache_cmp_kv), flat(cache_sel_kv), page_table + j * n_pool)
            xp, xs, (_, kc, ksl, wv), (vn, kcs, ksls, wvs) = _layer(
                xp, xs, w, ffn_stacks, l, n_main,
                lambda z3, zg3: _odd_mixers(z3, zg3, w),
                lambda z3, zg3: _odd_mixers(z3, zg3, w, past=past))
            cmp_p.append(kc); sel_p.append(ksl); win_p.append(wv)
            gv_s.append(vn); cmp_s.append(kcs); sel_s.append(ksls); win_s.append(wvs)
    st = jnp.stack
    return (xp, xs, st(pool_p), st(pool_s), st(c_p), st(c_s), st(n_p), st(n_s), st(m_p), st(m_s),
            st(gv_s), st(cmp_p), st(cmp_s), st(sel_p), st(sel_s), st(win_p), st(win_s))
```

```python
import functools
import math

import numpy as np
import jax
import jax.numpy as jnp
from jax import lax
from jax.experimental import pallas as pl
from jax.experimental.pallas import tpu as pltpu

F32 = jnp.float32
BF16 = jnp.bfloat16
NEG = -1e30

D_HALF = 1024
POOL_WINDOWS = (2, 4, 8, 16)
POOL_DG = 256
POOL_PAD = 15
HALO = 16
MLSTM_HEADS = 4
MLSTM_DK = 128
MLSTM_DV = 256
GATE_CAP = 15.0
GMLP_CHUNK = 128
GMLP_GROUPS = 4
GMLP_DG = 256
NSA_HEADS = 8
NSA_DH = 128
NSA_KV = 2
NSA_G = 4
CMP_STRIDE = 16
SEL_BLOCK = 64
N_SEL = 16
WINDOW = 512
QBLK = 128
PAGE = 128
FORCE_SCORE = 1e9
LANE = 128
VMEM_LIMIT = 56 * 1024 * 1024

NT_DIMS = (((1,), (1,)), ((), ()))
TN_DIMS = (((0,), (0,)), ((), ()))


def _cparams(*sem):
    return pltpu.CompilerParams(dimension_semantics=sem, vmem_limit_bytes=VMEM_LIMIT)


def _tile(m, pref):
    if m <= pref:
        return m
    for t in range(pref, 7, -1):
        if m % t == 0 and t % 8 == 0:
            return t
    return m


def _dot(a, b):
    return jnp.dot(a, b, preferred_element_type=F32)


def _dot_nt(a, b):
    return lax.dot_general(a, b, NT_DIMS, preferred_element_type=F32)


def _rms(x, g, eps=1e-6):
    return x * lax.rsqrt(jnp.mean(x * x, axis=-1, keepdims=True) + eps) * g


def _norm_mm_body(x_ref, xs_ref, g_ref, w_ref, wg_ref, o_ref, og_ref, os_ref, ogs_ref, xn_ref, xsn_ref):
    i = pl.program_id(0)
    j = pl.program_id(1)
    w = w_ref[...].astype(BF16)

    @pl.when(j == 0)
    def _():
        xn = _rms(x_ref[...], g_ref[...]).astype(BF16)
        xn_ref[...] = xn
        og_ref[...] = _dot(xn, wg_ref[...].astype(BF16))

    o_ref[...] = _dot(xn_ref[...], w)

    @pl.when(i == 0)
    def _():
        @pl.when(j == 0)
        def _():
            xsn = _rms(xs_ref[...], g_ref[...]).astype(BF16)
            xsn_ref[...] = xsn
            ogs_ref[...] = _dot(xsn, wg_ref[...].astype(BF16))

        os_ref[...] = _dot(xsn_ref[...], w)


def _norm_matmul(x, xs, g, w_stack, layer, n_main, wg, tm_pref=1024, tn_pref=512):
    M, K = x.shape
    Ms = xs.shape[0]
    tm = _tile(M, tm_pref)
    tn = _tile(n_main, tn_pref)
    nj = n_main // tn
    return pl.pallas_call(
        _norm_mm_body,
        grid=(M // tm, nj),
        in_specs=[
            pl.BlockSpec((tm, K), lambda i, j: (i, 0)),
            pl.BlockSpec((Ms, K), lambda i, j: (0, 0)),
            pl.BlockSpec((1, K), lambda i, j: (0, 0)),
            pl.BlockSpec((None, K, tn), lambda i, j: (layer, 0, j)),
            pl.BlockSpec((K, LANE), lambda i, j: (0, 0)),
        ],
        out_specs=[
            pl.BlockSpec((tm, tn), lambda i, j: (i, j)),
            pl.BlockSpec((tm, LANE), lambda i, j: (i, 0)),
            pl.BlockSpec((Ms, tn), lambda i, j: (0, jnp.where(i == 0, j, nj - 1))),
            pl.BlockSpec((Ms, LANE), lambda i, j: (0, 0)),
        ],
        out_shape=[jax.ShapeDtypeStruct((M, n_main), F32), jax.ShapeDtypeStruct((M, LANE), F32),
                   jax.ShapeDtypeStruct((Ms, n_main), F32), jax.ShapeDtypeStruct((Ms, LANE), F32)],
        scratch_shapes=[pltpu.VMEM((tm, K), BF16), pltpu.VMEM((Ms, K), BF16)],
        compiler_params=_cparams("arbitrary", "arbitrary"),
        name="norm_matmul",
    )(x, xs, g, w_stack, wg)


def _out_proj_body(a1_ref, a2_ref, res_ref, a1s_ref, a2s_ref, ress_ref, w1_ref, w2_ref, g_ref, o_ref, os_ref):
    def proj(a1, a2, res):
        y = _dot(a1[...].astype(BF16), w1_ref[...]) + _dot(a2[...].astype(BF16), w2_ref[...])
        return res[...] + _rms(y, g_ref[...])

    o_ref[...] = proj(a1_ref, a2_ref, res_ref)

    @pl.when(pl.program_id(0) == 0)
    def _():
        os_ref[...] = proj(a1s_ref, a2s_ref, ress_ref)


def _out_proj(a1, a2, res, a1s, a2s, ress, w1, w2, g, tm_pref=512):
    M, K1 = a1.shape
    Ms = a1s.shape[0]
    K2 = a2.shape[1]
    D = w1.shape[1]
    tm = _tile(M, tm_pref)
    whole = lambda r, c: pl.BlockSpec((r, c), lambda i: (0, 0))
    return pl.pallas_call(
        _out_proj_body,
        grid=(M // tm,),
        in_specs=[
            pl.BlockSpec((tm, K1), lambda i: (i, 0)),
            pl.BlockSpec((tm, K2), lambda i: (i, 0)),
            pl.BlockSpec((tm, D), lambda i: (i, 0)),
            whole(Ms, K1), whole(Ms, K2), whole(Ms, D),
            whole(K1, D), whole(K2, D), whole(1, D),
        ],
        out_specs=[pl.BlockSpec((tm, D), lambda i: (i, 0)), whole(Ms, D)],
        out_shape=[jax.ShapeDtypeStruct((M, D), F32), jax.ShapeDtypeStruct((Ms, D), F32)],
        compiler_params=_cparams("arbitrary"),
        name="out_proj",
    )(a1, a2, res, a1s, a2s, ress, w1, w2, g)


def _ffn_body(x_ref, xs_ref, g2_ref, g3_ref, w1_ref, w3_ref, w2_ref, o_ref, os_ref, xn_ref, acc_ref, xsn_ref, accs_ref):
    i = pl.program_id(0)
    j = pl.program_id(1)
    last = pl.num_programs(1) - 1
    w1 = w1_ref[...]
    w3 = w3_ref[...]
    w2 = w2_ref[...]

    def step(x_r, o_r, xn_r, acc_r):
        @pl.when(j == 0)
        def _():
            xn_r[...] = _rms(x_r[...], g2_ref[...]).astype(BF16)
            acc_r[...] = jnp.zeros_like(acc_r)

        xn = xn_r[...]
        h1 = _dot(xn, w1)
        h = (h1 * jax.nn.sigmoid(h1) * _dot(xn, w3)).astype(BF16)
        acc_r[...] += _dot(h, w2)

        @pl.when(j == last)
        def _():
            o_r[...] = x_r[...] + _rms(acc_r[...], g3_ref[...])

    step(x_ref, o_ref, xn_ref, acc_ref)

    @pl.when(i == 0)
    def _():
        step(xs_ref, os_ref, xsn_ref, accs_ref)


def _ffn(x, xs, g2, g3, w1_stack, w3_stack, w2_stack, layer, tm_pref=512, th_pref=512):
    M, D = x.shape
    Ms = xs.shape[0]
    H = w1_stack.shape[2]
    tm = _tile(M, tm_pref)
    th = _tile(H, th_pref)
    return pl.pallas_call(
        _ffn_body,
        grid=(M // tm, H // th),
        in_specs=[
            pl.BlockSpec((tm, D), lambda i, j: (i, 0)),
            pl.BlockSpec((Ms, D), lambda i, j: (0, 0)),
            pl.BlockSpec((1, D), lambda i, j: (0, 0)),
            pl.BlockSpec((1, D), lambda i, j: (0, 0)),
            pl.BlockSpec((None, D, th), lambda i, j: (layer, 0, j)),
            pl.BlockSpec((None, D, th), lambda i, j: (layer, 0, j)),
            pl.BlockSpec((None, th, D), lambda i, j: (layer, j, 0)),
        ],
        out_specs=[
            pl.BlockSpec((tm, D), lambda i, j: (i, 0)),
            pl.BlockSpec((Ms, D), lambda i, j: (0, 0)),
        ],
        out_shape=[jax.ShapeDtypeStruct((M, D), F32), jax.ShapeDtypeStruct((Ms, D), F32)],
        scratch_shapes=[pltpu.VMEM((tm, D), BF16), pltpu.VMEM((tm, D), F32),
                        pltpu.VMEM((Ms, D), BF16), pltpu.VMEM((Ms, D), F32)],
        compiler_params=_cparams("arbitrary", "arbitrary"),
        name="ffn",
    )(x, xs, g2, g3, w1_stack, w3_stack, w2_stack)


def _pool_body(u_ref, prev_ref, st_ref, pw_ref, ps_ref, o_ref, ext_ref, *, tT, t0):
    t = pl.program_id(1)
    ext_ref[0:HALO, :] = jnp.where(t == 0, st_ref[...], prev_ref[...])
    ext_ref[HALO:HALO + tT, :] = u_ref[...]
    pos = t0 + t * tT + lax.broadcasted_iota(jnp.int32, (tT, 1), 0)
    for g, w in enumerate(POOL_WINDOWS):
        cs = slice(g * POOL_DG, (g + 1) * POOL_DG)
        x_new = ext_ref[HALO:HALO + tT, cs]
        tot = x_new
        for i in range(1, w):
            tot = tot + ext_ref[HALO - i:HALO - i + tT, cs]
        cnt = jnp.minimum(w, pos + 1).astype(F32)
        y = tot / cnt - x_new
        o_ref[:, cs] = _dot(y.astype(BF16), pw_ref[g]) * ps_ref[:, cs]


def _pool_mix(z3, st, pool_w, pool_scale, t0, tT_pref=512):
    N, T = z3.shape[:2]
    tT = _tile(T, tT_pref)
    nT = T // tT
    if nT > 1:
        assert tT % HALO == 0
        prev, prev_spec = z3, pl.BlockSpec((None, HALO, D_HALF), lambda n, t: (n, jnp.maximum(t * (tT // HALO) - 1, 0), 0))
    else:
        prev, prev_spec = st, pl.BlockSpec((None, HALO, D_HALF), lambda n, t: (n, 0, 0))
    return pl.pallas_call(
        functools.partial(_pool_body, tT=tT, t0=t0),
        grid=(N, nT),
        in_specs=[
            pl.BlockSpec((None, tT, D_HALF), lambda n, t: (n, t, 0)),
            prev_spec,
            pl.BlockSpec((None, HALO, D_HALF), lambda n, t: (n, 0, 0)),
            pl.BlockSpec((len(POOL_WINDOWS), POOL_DG, POOL_DG), lambda n, t: (0, 0, 0)),
            pl.BlockSpec((1, D_HALF), lambda n, t: (0, 0)),
        ],
        out_specs=pl.BlockSpec((None, tT, D_HALF), lambda n, t: (n, t, 0)),
        out_shape=jax.ShapeDtypeStruct((N, T, D_HALF), F32),
        scratch_shapes=[pltpu.VMEM((HALO + tT, D_HALF), F32)],
        compiler_params=_cparams("parallel", "arbitrary"),
        name="pool_mix",
    )(z3, prev, st, pool_w, pool_scale)


def _mlstm_body(q_ref, k_ref, v_ref, o_ref, zg_ref, gb_ref, mg_ref, c0_ref, n0_ref, m0_ref,
                y_ref, cN_ref, nN_ref, mN_ref, C_s, n_s, m_s, *, L, t_valid):
    c = pl.program_id(1)

    @pl.when(c == 0)
    def _():
        C_s[...] = c0_ref[...]
        n_s[...] = n0_ref[...]
        m_s[...] = m0_ref[...]

    a = GATE_CAP * jnp.tanh((zg_ref[...] + gb_ref[...]) / GATE_CAP)
    lane = lax.broadcasted_iota(jnp.int32, (L, LANE), 1)
    logsig = jnp.minimum(a, 0.0) - jnp.log1p(jnp.exp(-jnp.abs(a)))
    A = jnp.where(lane < MLSTM_HEADS, a, logsig)
    if t_valid < L:
        row = lax.broadcasted_iota(jnp.int32, (L, LANE), 0)
        A = jnp.where(row < t_valid, A, jnp.where(lane < MLSTM_HEADS, NEG, 0.0))
    r_i = lax.broadcasted_iota(jnp.int32, (L, L), 0)
    c_i = lax.broadcasted_iota(jnp.int32, (L, L), 1)
    causal = r_i >= c_i
    Bc = jnp.dot(causal.astype(F32), A, preferred_element_type=F32, precision=lax.Precision.HIGHEST)
    At = A.T
    Bt = Bc.T

    for h in range(MLSTM_HEADS):
        ks = slice(h * MLSTM_DK, (h + 1) * MLSTM_DK)
        vs = slice(h * MLSTM_DV, (h + 1) * MLSTM_DV)
        qh = q_ref[:, ks] * (MLSTM_DK ** -0.5)
        kh = k_ref[:, ks]
        vh = v_ref[:, vs].astype(BF16)
        b_c = Bc[:, MLSTM_HEADS + h:MLSTM_HEADS + h + 1]
        b_r = Bt[MLSTM_HEADS + h:MLSTM_HEADS + h + 1, :]
        li_c = A[:, h:h + 1]
        li_r = At[h:h + 1, :]
        m = m_s[h]
        Ch = C_s[h]
        nh = n_s[h]

        d = jnp.where(causal, b_c - b_r + li_r, NEG)
        inter = b_c + m
        m_t = jnp.maximum(inter, jnp.max(d, axis=-1, keepdims=True))
        w_in = jnp.exp(inter - m_t)
        qb = qh.astype(BF16)
        s = _dot_nt(qb, kh.astype(BF16)) * jnp.exp(d - m_t)
        num = w_in * _dot(qb, Ch.astype(BF16)) + _dot(s.astype(BF16), vh)
        den = w_in * jnp.sum(qh * nh, axis=-1, keepdims=True) + jnp.sum(s, axis=-1, keepdims=True)
        hh = num / jnp.maximum(jnp.abs(den), jnp.exp(-m_t))

        b_end = b_c[L - 1:L, :]
        g_c = b_end - b_c + li_c
        m_new = jnp.maximum(b_end + m, jnp.max(g_c, axis=0, keepdims=True))
        w_c = jnp.exp(b_end + m - m_new)
        kw = kh * jnp.exp(g_c - m_new)
        C_s[h] = w_c * Ch + lax.dot_general(kw.astype(BF16), vh, TN_DIMS, preferred_element_type=F32)
        n_s[h] = w_c * nh + jnp.sum(kw, axis=0, keepdims=True)
        m_s[h] = m_new

        hn = _rms(hh, mg_ref[:, vs])
        y_ref[:, vs] = jax.nn.sigmoid(o_ref[:, vs]) * hn

    @pl.when(c == pl.num_programs(1) - 1)
    def _():
        cN_ref[...] = C_s[...]
        nN_ref[...] = n_s[...]
        mN_ref[...] = m_s[...]


def _mlstm(z3, zg3, gate_b, mnorm_g, C0, n0, m0, L, t_valid):
    N, Tp = z3.shape[:2]
    nc = Tp // L
    H = MLSTM_HEADS
    qk_w = H * MLSTM_DK
    v_w = H * MLSTM_DV
    st = lambda n, c: (n, 0, 0, 0)
    outs = pl.pallas_call(
        functools.partial(_mlstm_body, L=L, t_valid=t_valid),
        grid=(N, nc),
        in_specs=[
            pl.BlockSpec((None, L, qk_w), lambda n, c: (n, c, D_HALF // qk_w)),
            pl.BlockSpec((None, L, qk_w), lambda n, c: (n, c, D_HALF // qk_w + 1)),
            pl.BlockSpec((None, L, v_w), lambda n, c: (n, c, 2)),
            pl.BlockSpec((None, L, v_w), lambda n, c: (n, c, 3)),
            pl.BlockSpec((None, L, LANE), lambda n, c: (n, c, 0)),
            pl.BlockSpec((1, LANE), lambda n, c: (0, 0)),
            pl.BlockSpec((1, v_w), lambda n, c: (0, 0)),
            pl.BlockSpec((None, H, MLSTM_DK, MLSTM_DV), st),
            pl.BlockSpec((None, H, 1, MLSTM_DK), st),
            pl.BlockSpec((None, H, 1, 1), st),
        ],
        out_specs=[
            pl.BlockSpec((None, L, v_w), lambda n, c: (n, c, 0)),
            pl.BlockSpec((None, H, MLSTM_DK, MLSTM_DV), st),
            pl.BlockSpec((None, H, 1, MLSTM_DK), st),
            pl.BlockSpec((None, H, 1, 1), st),
        ],
        out_shape=[
            jax.ShapeDtypeStruct((N, Tp, v_w), F32),
            jax.ShapeDtypeStruct((N, H, MLSTM_DK, MLSTM_DV), F32),
            jax.ShapeDtypeStruct((N, H, 1, MLSTM_DK), F32),
            jax.ShapeDtypeStruct((N, H, 1, 1), F32),
        ],
        scratch_shapes=[
            pltpu.VMEM((H, MLSTM_DK, MLSTM_DV), F32),
            pltpu.VMEM((H, 1, MLSTM_DK), F32),
            pltpu.VMEM((H, 1, 1), F32),
        ],
        compiler_params=_cparams("parallel", "arbitrary"),
        name="mlstm",
    )(z3, z3, z3, z3, zg3, gate_b, mnorm_g, C0, n0.reshape(N, H, 1, MLSTM_DK), m0.reshape(N, H, 1, 1))
    y, C, n, m = outs
    return y, C, n.reshape(N, H, MLSTM_DK), m.reshape(N, H)


def _gmlp_body(u_ref, v_ref, g_ref, ws_ref, bs_ref, y_ref, vn_ref, *, tT):
    v = v_ref[...]
    vc = v - jnp.mean(v, axis=-1, keepdims=True)
    vn = vc * lax.rsqrt(jnp.mean(vc * vc, axis=-1, keepdims=True) + 1e-5) * g_ref[...]
    vn_ref[...] = vn
    r_i = lax.broadcasted_iota(jnp.int32, (GMLP_CHUNK, GMLP_CHUNK), 0)
    c_i = lax.broadcasted_iota(jnp.int32, (GMLP_CHUNK, GMLP_CHUNK), 1)
    for g in range(GMLP_GROUPS):
        cs = slice(g * GMLP_DG, (g + 1) * GMLP_DG)
        wm = jnp.where(r_i >= c_i, ws_ref[g], 0.0).astype(BF16)
        bias = bs_ref[:, g:g + 1]
        for ch in range(tT // GMLP_CHUNK):
            rs = slice(ch * GMLP_CHUNK, (ch + 1) * GMLP_CHUNK)
            mix = _dot(wm, vn[rs, cs].astype(BF16)) + bias
            y_ref[rs, cs] = u_ref[rs, cs] * mix


def _gmlp(z3, gnorm_g, ws, bs_t, tT_pref=512):
    N, Tp = z3.shape[:2]
    tT = _tile(Tp, tT_pref)
    assert tT % GMLP_CHUNK == 0
    return pl.pallas_call(
        functools.partial(_gmlp_body, tT=tT),
        grid=(N, Tp // tT),
        in_specs=[
            pl.BlockSpec((None, tT, D_HALF), lambda n, t: (n, t, 0)),
            pl.BlockSpec((None, tT, D_HALF), lambda n, t: (n, t, 1)),
            pl.BlockSpec((1, D_HALF), lambda n, t: (0, 0)),
            pl.BlockSpec((GMLP_GROUPS, GMLP_CHUNK, GMLP_CHUNK), lambda n, t: (0, 0, 0)),
            pl.BlockSpec((GMLP_CHUNK, LANE), lambda n, t: (0, 0)),
        ],
        out_specs=[
            pl.BlockSpec((None, tT, D_HALF), lambda n, t: (n, t, 0)),
            pl.BlockSpec((None, tT, D_HALF), lambda n, t: (n, t, 0)),
        ],
        out_shape=[jax.ShapeDtypeStruct((N, Tp, D_HALF), F32), jax.ShapeDtypeStruct((N, Tp, D_HALF), F32)],
        compiler_params=_cparams("parallel", "parallel"),
        name="gmlp",
    )(z3, z3, gnorm_g, ws, bs_t)


PAGES_PER_STEP = 16
BLK_PER_PAGE = PAGE // CMP_STRIDE


def _compress_body(pt_ref, *refs):
    page_refs = refs[:PAGES_PER_STEP]
    wab_ref, a_ref, b_ref = refs[PAGES_PER_STEP:]
    wa = wab_ref[0]
    wb = wab_ref[1]
    for p, pr in enumerate(page_refs):
        x = pr[...].reshape(BLK_PER_PAGE, CMP_STRIDE, 4 * NSA_DH)
        rs = slice(p * BLK_PER_PAGE, (p + 1) * BLK_PER_PAGE)
        a_ref[rs, :] = jnp.sum(x * wa[None], axis=1)
        b_ref[rs, :] = jnp.sum(x * wb[None], axis=1)


def _compress_ab(pages, page_table, wab, col_block):
    N, n_pages = page_table.shape
    assert n_pages % PAGES_PER_STEP == 0
    W = 4 * NSA_DH
    rows = PAGES_PER_STEP * BLK_PER_PAGE
    page_specs = [
        pl.BlockSpec((None, PAGE, W), functools.partial(
            lambda n, s, pt, r: (pt[n, s * PAGES_PER_STEP + r], 0, col_block), r=r))
        for r in range(PAGES_PER_STEP)
    ]
    grid_spec = pltpu.PrefetchScalarGridSpec(
        num_scalar_prefetch=1,
        grid=(N, n_pages // PAGES_PER_STEP),
        in_specs=page_specs + [pl.BlockSpec((2, CMP_STRIDE, W), lambda n, s, pt: (0, 0, 0))],
        out_specs=[pl.BlockSpec((None, rows, W), lambda n, s, pt: (n, s, 0))] * 2,
    )
    ns = n_pages * BLK_PER_PAGE
    return pl.pallas_call(
        _compress_body,
        grid_spec=grid_spec,
        out_shape=[jax.ShapeDtypeStruct((N, ns, W), F32)] * 2,
        compiler_params=_cparams("parallel", "parallel"),
        name="nsa_compress",
    )(page_table, *([pages] * PAGES_PER_STEP), wab)


ROWS_PER_TOKEN = 2 * NSA_KV
PAGE_ROWS = PAGE * ROWS_PER_TOKEN
BLK_ROWS = CMP_STRIDE * ROWS_PER_TOKEN


def _compress_rows_body(pt_ref, *refs):
    page_refs = refs[:PAGES_PER_STEP]
    wab_ref, a_ref, b_ref, z_s = refs[PAGES_PER_STEP:]
    for p, pr in enumerate(page_refs):
        x = pr[...].reshape(BLK_PER_PAGE, BLK_ROWS, NSA_DH)
        for half, out_ref in enumerate((a_ref, b_ref)):
            y = (x * wab_ref[half][None]).reshape(BLK_PER_PAGE, BLK_ROWS // 8, 8, NSA_DH).sum(axis=1)
            y = y.reshape(BLK_PER_PAGE * 8, NSA_DH)
            slot = 2 * p + half
            z_s[slot] = y + pltpu.roll(y, BLK_PER_PAGE * 8 - ROWS_PER_TOKEN, 0)
            for c in range(ROWS_PER_TOKEN):
                out_ref[p * BLK_PER_PAGE:(p + 1) * BLK_PER_PAGE, c * NSA_DH:(c + 1) * NSA_DH] = (
                    z_s[slot, pl.ds(c, BLK_PER_PAGE, stride=8), :])


def _compress_ab_rows(rows, page_table, wab_rows):
    N, n_pages = page_table.shape
    assert n_pages % PAGES_PER_STEP == 0
    W = ROWS_PER_TOKEN * NSA_DH
    out_rows = PAGES_PER_STEP * BLK_PER_PAGE
    page_specs = [
        pl.BlockSpec((PAGE_ROWS, NSA_DH), functools.partial(
            lambda n, s, pt, r: (pt[n, s * PAGES_PER_STEP + r], 0), r=r))
        for r in range(PAGES_PER_STEP)
    ]
    grid_spec = pltpu.PrefetchScalarGridSpec(
        num_scalar_prefetch=1,
        grid=(N, n_pages // PAGES_PER_STEP),
        in_specs=page_specs + [pl.BlockSpec((2, BLK_ROWS, NSA_DH), lambda n, s, pt: (0, 0, 0))],
        out_specs=[pl.BlockSpec((None, out_rows, W), lambda n, s, pt: (n, s, 0))] * 2,
        scratch_shapes=[pltpu.VMEM((2 * PAGES_PER_STEP, BLK_PER_PAGE * 8, NSA_DH), F32)],
    )
    ns = n_pages * BLK_PER_PAGE
    return pl.pallas_call(
        _compress_rows_body,
        grid_spec=grid_spec,
        out_shape=[jax.ShapeDtypeStruct((N, ns, W), F32)] * 2,
        compiler_params=_cparams("parallel", "arbitrary"),
        name="nsa_compress_rows",
    )(page_table, *([rows] * PAGES_PER_STEP), wab_rows)


def _cmp_proj_body(a_ref, b_ref, w_ref, kc_ref, vc_ref, *, ns):
    b_next = pltpu.roll(b_ref[...], ns - 1, 0)
    row = lax.broadcasted_iota(jnp.int32, (ns, 1), 0)
    blk = jnp.where(row < ns - 1, a_ref[...] + b_next, 0.0).astype(BF16)
    w0 = w_ref[0].astype(BF16)
    w1 = w_ref[1].astype(BF16)
    for kv in range(NSA_KV):
        cs = slice(kv * NSA_DH, (kv + 1) * NSA_DH)
        kc_ref[:, cs] = _dot(blk[:, kv * NSA_DH:(kv + 1) * NSA_DH], w0)
        vc_ref[:, cs] = _dot(blk[:, (NSA_KV + kv) * NSA_DH:(NSA_KV + kv + 1) * NSA_DH], w1)


def _cmp_proj(a, b, w_cmp):
    N, ns, W = a.shape
    return pl.pallas_call(
        functools.partial(_cmp_proj_body, ns=ns),
        grid=(N,),
        in_specs=[
            pl.BlockSpec((None, ns, W), lambda n: (n, 0, 0)),
            pl.BlockSpec((None, ns, W), lambda n: (n, 0, 0)),
            pl.BlockSpec((2, NSA_DH, NSA_DH), lambda n: (0, 0, 0)),
        ],
        out_specs=[pl.BlockSpec((None, ns, NSA_KV * NSA_DH), lambda n: (n, 0, 0))] * 2,
        out_shape=[jax.ShapeDtypeStruct((N, ns, NSA_KV * NSA_DH), F32)] * 2,
        compiler_params=_cparams("parallel"),
        name="nsa_cmp_proj",
    )(a, b, w_cmp)


def _split3(x):
    hi = x.astype(BF16)
    r1 = x - hi.astype(F32)
    mid = r1.astype(BF16)
    lo = (r1 - mid.astype(F32)).astype(BF16)
    return hi, mid, lo


def _cmp_to_sel(p_sum, a_mat):
    hi, mid, lo = _split3(p_sum)
    return _dot(hi, a_mat) + _dot(mid, a_mat) + _dot(lo, a_mat)


def _topk_mask(score, n_valid, k):
    lane = lax.broadcasted_iota(jnp.int32, score.shape, 1)
    rank = jnp.zeros(score.shape, F32)
    for jp in range(n_valid):
        col = score[:, jp:jp + 1]
        beats = (col > score) | ((col == score) & (lane > jp))
        rank = rank + beats.astype(F32)
    return ((rank < k) & (lane < n_valid)).astype(F32)


def _sel_scores(imp, qblk, n_valid):
    lane = lax.broadcasted_iota(jnp.int32, imp.shape, 1)
    forced = (lane == 0) | (lane == qblk) | (lane == qblk - 1)
    score = jnp.where(forced, FORCE_SCORE, jnp.where(lane > qblk, -1.0, imp))
    return jnp.where(lane < n_valid, score, -2.0)


def _masked_softmax_rows(s, mask):
    s = jnp.where(mask, s, NEG)
    e = jnp.exp(s - jnp.max(s, axis=-1, keepdims=True))
    return jnp.where(mask, e / jnp.sum(e, axis=-1, keepdims=True), 0.0)


def _stack_heads(q, scale):
    return (jnp.concatenate([q[:, g * NSA_DH:(g + 1) * NSA_DH] for g in range(NSA_G)], axis=0) * scale).astype(BF16)


SEL_TK = 512
WIN_KEYS = WINDOW + QBLK


def _nsa_prompt_body(q_ref, zg_ref, gb_ref, kc_ref, vc_ref, ks_ref, vs_ref, kw_ref, vw_ref, amat_t_ref,
                     o_ref, ks_bf, kw_bf, vs_t, vw_t, kc_bf, vc_t, sc_ref, *, n_sel, nsr):
    qb = pl.program_id(1)
    T = ks_ref.shape[0]
    kvc = lambda kv: slice(kv * NSA_DH, (kv + 1) * NSA_DH)

    @pl.when(qb == 0)
    def _():
        ks_bf[...] = ks_ref[...].astype(BF16)
        kw_bf[...] = kw_ref[...].astype(BF16)
        kc_bf[...] = kc_ref[...].astype(BF16)
        vc_t[...] = vc_ref[...].T.astype(BF16)

        def transpose_values(i, c):
            r0 = pl.multiple_of(i * LANE, LANE)
            vs_t[i] = vs_ref[pl.ds(r0, LANE), :].T.astype(BF16)
            vw_t[i] = vw_ref[pl.ds(r0, LANE), :].T.astype(BF16)
            return c

        lax.fori_loop(0, T // LANE, transpose_values, 0)

    R = QBLK
    start = qb * R
    groups = range(NSA_KV)
    tile4 = lambda x: jnp.concatenate([x] * NSA_G, axis=1)
    q_t = [jnp.concatenate([(q_ref[:, (kv * NSA_G + g) * NSA_DH:(kv * NSA_G + g + 1) * NSA_DH] * (NSA_DH ** -0.5)).T
                            for g in range(NSA_G)], axis=1).astype(BF16) for kv in groups]
    pos = start + lax.broadcasted_iota(jnp.int32, (1, R), 1)

    ns = kc_bf.shape[0]
    cmp_end = (lax.broadcasted_iota(jnp.int32, (ns, R), 0) + 2) * CMP_STRIDE - 1
    bias_c = tile4(jnp.where(cmp_end <= pos, 0.0, NEG))
    any_c = tile4(jnp.where(pos >= 2 * CMP_STRIDE - 1, 1.0, 0.0))
    a_t = amat_t_ref[...]
    blk = lax.broadcasted_iota(jnp.int32, (nsr, R), 0)
    qblk = (start + lax.broadcasted_iota(jnp.int32, (nsr, R), 1)) // SEL_BLOCK
    forced = (blk == 0) | (blk == qblk) | (blk == qblk - 1)
    o_cmp, score = [], []
    for kv in groups:
        s_c = _dot(kc_bf[:, kvc(kv)], q_t[kv]) + bias_c
        e_c = jnp.exp(s_c - jnp.max(s_c, axis=0, keepdims=True))
        p_c = e_c * (any_c / jnp.sum(e_c, axis=0, keepdims=True))
        o_cmp.append(_dot(vc_t[kvc(kv), :], p_c.astype(BF16)))
        p_sum = p_c[:, 0:R] + p_c[:, R:2 * R] + p_c[:, 2 * R:3 * R] + p_c[:, 3 * R:4 * R]
        hi, mid, lo = _split3(p_sum)
        imp_t = _dot(a_t, hi) + _dot(a_t, mid) + _dot(a_t, lo)
        sc = jnp.where(forced, FORCE_SCORE, jnp.where(blk > qblk, -1.0, imp_t))
        score.append(jnp.where(blk < n_sel, sc, -2.0))
        sc_ref[kv] = score[kv]

    def rank_step(i, ranks):
        ranks = list(ranks)
        for jp in (2 * i, 2 * i + 1):
            for kv in groups:
                row = sc_ref[kv, pl.ds(jp, 1), :]
                beats = (row > score[kv]) | ((row == score[kv]) & (blk > jp))
                ranks[kv] = ranks[kv] + beats.astype(F32)
        return tuple(ranks)

    n_rank = jnp.minimum(start // SEL_BLOCK + QBLK // SEL_BLOCK, n_sel)
    ranks = lax.fori_loop(0, (n_rank + 1) // 2, rank_step, tuple(jnp.zeros((nsr, R), F32) for _ in groups))
    for kv in groups:
        sc_ref[kv] = jnp.where(ranks[kv] < N_SEL, 0.0, NEG)

    def values_product(v_ref, tile0, n_sub, kv, p):
        out = _dot(v_ref[tile0, kvc(kv), :], p[0:LANE])
        for i in range(1, n_sub):
            out = out + _dot(v_ref[tile0 + i, kvc(kv), :], p[i * LANE:(i + 1) * LANE])
        return out

    key_tk = lax.broadcasted_iota(jnp.int32, (SEL_TK, R), 0)

    def sel_tile(kt, carry, causal):
        k0 = pl.multiple_of(kt * SEL_TK, SEL_TK)
        scores = [_dot(ks_bf[pl.ds(k0, SEL_TK), kvc(kv)], q_t[kv]) for kv in groups]
        stats = []
        for kv in groups:
            m_prev, l_prev, _ = carry[kv]
            bias = jnp.concatenate(
                [jnp.broadcast_to(sc_ref[kv, pl.ds(kt * (SEL_TK // SEL_BLOCK) + b, 1), :], (SEL_BLOCK, R))
                 for b in range(SEL_TK // SEL_BLOCK)], axis=0)
            if causal:
                bias = jnp.where(k0 + key_tk <= pos, bias, NEG)
            s = scores[kv] + tile4(bias)
            m_new = jnp.maximum(m_prev, jnp.max(s, axis=0, keepdims=True))
            alpha = jnp.exp(m_prev - m_new)
            p = jnp.exp(s - m_new)
            stats.append((m_new, alpha, alpha * l_prev + jnp.sum(p, axis=0, keepdims=True), p.astype(BF16)))
        return tuple((m_new, l_new, alpha * carry[kv][2]
                      + values_product(vs_t, kt * (SEL_TK // LANE), SEL_TK // LANE, kv, p))
                     for kv, (m_new, alpha, l_new, p) in enumerate(stats))

    n_tiles = (start + R - 1) // SEL_TK + 1
    init = tuple((jnp.full((1, NSA_G * R), NEG, F32), jnp.zeros((1, NSA_G * R), F32),
                  jnp.zeros((NSA_DH, NSA_G * R), F32)) for _ in groups)
    carry = lax.fori_loop(0, n_tiles - 1, lambda kt, c: sel_tile(kt, c, False), init)
    o_sel = [acc / l for _, l, acc in sel_tile(n_tiles - 1, carry, True)]

    w0 = pl.multiple_of(jnp.maximum(start - WINDOW, 0), QBLK)
    key_w = w0 + lax.broadcasted_iota(jnp.int32, (WIN_KEYS, R), 0)
    bias_w = tile4(jnp.where((key_w <= pos) & (key_w > pos - WINDOW), 0.0, NEG))
    gates_t = jax.nn.sigmoid(zg_ref[...] + gb_ref[...]).T
    for kv in groups:
        s_w = _dot(kw_bf[pl.ds(w0, WIN_KEYS), kvc(kv)], q_t[kv]) + bias_w
        e_w = jnp.exp(s_w - jnp.max(s_w, axis=0, keepdims=True))
        o_win = (values_product(vw_t, w0 // LANE, WIN_KEYS // LANE, kv, e_w.astype(BF16))
                 / jnp.sum(e_w, axis=0, keepdims=True))
        gate = lambda branch: jnp.concatenate(
            [gates_t[branch * NSA_HEADS + kv * NSA_G + g:branch * NSA_HEADS + kv * NSA_G + g + 1, :]
             for g in range(NSA_G)], axis=1)
        out_t = gate(0) * o_cmp[kv] + gate(1) * o_sel[kv] + gate(2) * o_win
        for g in range(NSA_G):
            h = kv * NSA_G + g
            o_ref[:, h * NSA_DH:(h + 1) * NSA_DH] = out_t[:, g * R:(g + 1) * R].T


def _sel_map(ns, n_sel, nsb):
    i = np.arange(ns)[:, None]
    j = np.arange(nsb)[None, :]
    r = SEL_BLOCK // CMP_STRIDE
    return jnp.asarray(((i >= r * j - 1) & (i <= r * j + r - 1) & (j < n_sel)).astype(np.float32), BF16)


def _nsa_prompt(z3, zg3, gate_b, kc, vc):
    N, T = z3.shape[:2]
    assert T % SEL_TK == 0 and T >= WIN_KEYS
    ns = kc.shape[1]
    n_sel = T // SEL_BLOCK
    nsr = -(-n_sel // 8) * 8
    assert nsr <= LANE
    amat_t = _sel_map(ns, n_sel, nsr).T
    KVW = NSA_KV * NSA_DH
    qcol = 2 * D_HALF // (NSA_HEADS * NSA_DH)
    kvs_col = (3 * D_HALF + 2 * KVW) // KVW
    kvw_col = kvs_col + 2
    full = lambda off: pl.BlockSpec((None, T, KVW), lambda n, qb: (n, 0, off))
    return pl.pallas_call(
        functools.partial(_nsa_prompt_body, n_sel=n_sel, nsr=nsr),
        grid=(N, T // QBLK),
        in_specs=[
            pl.BlockSpec((None, QBLK, NSA_HEADS * NSA_DH), lambda n, qb: (n, qb, qcol)),
            pl.BlockSpec((None, QBLK, LANE), lambda n, qb: (n, qb, 0)),
            pl.BlockSpec((1, LANE), lambda n, qb: (0, 0)),
            pl.BlockSpec((None, ns, KVW), lambda n, qb: (n, 0, 0)),
            pl.BlockSpec((None, ns, KVW), lambda n, qb: (n, 0, 0)),
            full(kvs_col), full(kvs_col + 1), full(kvw_col), full(kvw_col + 1),
            pl.BlockSpec((nsr, ns), lambda n, qb: (0, 0)),
        ],
        out_specs=pl.BlockSpec((None, QBLK, NSA_HEADS * NSA_DH), lambda n, qb: (n, qb, 0)),
        out_shape=jax.ShapeDtypeStruct((N, T, NSA_HEADS * NSA_DH), F32),
        scratch_shapes=[
            pltpu.VMEM((T, KVW), BF16), pltpu.VMEM((T, KVW), BF16),
            pltpu.VMEM((T // LANE, KVW, LANE), BF16), pltpu.VMEM((T // LANE, KVW, LANE), BF16),
            pltpu.VMEM((ns, KVW), BF16), pltpu.VMEM((KVW, ns), BF16),
            pltpu.VMEM((NSA_KV, nsr, QBLK), F32),
        ],
        compiler_params=_cparams("parallel", "arbitrary"),
        name="nsa_prompt",
    )(z3, zg3, gate_b, kc, vc, z3, z3, z3, z3, amat_t)


def _nsa_sample_a_body(q_ref, kc_ref, vc_ref, kw_ref, vw_ref, amat_ref, ocmp_ref, owin_ref, sel_ref,
                       *, Tq, past_len, n_sel, wb):
    qs = _stack_heads(q_ref[...], NSA_DH ** -0.5)
    pos = past_len + lax.broadcasted_iota(jnp.int32, (Tq, 1), 0)
    pos4 = jnp.concatenate([pos] * NSA_G, axis=0)

    ns = kc_ref.shape[0]
    cmp_i = lax.broadcasted_iota(jnp.int32, (NSA_G * Tq, ns), 1)
    m_c = ((cmp_i + 2) * CMP_STRIDE - 1 <= pos4) & (cmp_i < ns - 1)
    p_c = _masked_softmax_rows(_dot_nt(qs, kc_ref[...].astype(BF16)), m_c)
    ocmp_ref[...] = _dot(p_c.astype(BF16), vc_ref[...].astype(BF16))
    p_sum = p_c[0:Tq] + p_c[Tq:2 * Tq] + p_c[2 * Tq:3 * Tq] + p_c[3 * Tq:4 * Tq]
    imp = _cmp_to_sel(p_sum, amat_ref[...])
    sel_ref[...] = _topk_mask(_sel_scores(imp, pos // SEL_BLOCK, n_sel), n_sel, N_SEL)

    nw = kw_ref.shape[0]
    tok_w = past_len - wb + lax.broadcasted_iota(jnp.int32, (NSA_G * Tq, nw), 1)
    m_w = (tok_w >= 0) & (tok_w <= pos4) & (tok_w > pos4 - WINDOW)
    p_w = _masked_softmax_rows(_dot_nt(qs, kw_ref[...].astype(BF16)), m_w)
    owin_ref[...] = _dot(p_w.astype(BF16), vw_ref[...].astype(BF16))


def _nsa_sample_a(z3, kc, vc, kw_full, past_len, wb):
    N, Tq = z3.shape[:2]
    ns = kc.shape[1]
    nw = kw_full.shape[1]
    n_sel = -(-(past_len + Tq) // SEL_BLOCK)
    nsb = -(-n_sel // LANE) * LANE
    amat = _sel_map(ns, n_sel, nsb)
    qcol = 2 * D_HALF // (NSA_G * NSA_DH)
    R4 = NSA_G * Tq
    return pl.pallas_call(
        functools.partial(_nsa_sample_a_body, Tq=Tq, past_len=past_len, n_sel=n_sel, wb=wb),
        grid=(N, NSA_KV),
        in_specs=[
            pl.BlockSpec((None, Tq, NSA_G * NSA_DH), lambda n, kv: (n, 0, qcol + kv)),
            pl.BlockSpec((None, ns, NSA_DH), lambda n, kv: (n, 0, kv)),
            pl.BlockSpec((None, ns, NSA_DH), lambda n, kv: (n, 0, kv)),
            pl.BlockSpec((None, nw, NSA_DH), lambda n, kv: (n, 0, kv)),
            pl.BlockSpec((None, nw, NSA_DH), lambda n, kv: (n, 0, NSA_KV + kv)),
            pl.BlockSpec((ns, nsb), lambda n, kv: (0, 0)),
        ],
        out_specs=[
            pl.BlockSpec((None, None, R4, NSA_DH), lambda n, kv: (n, kv, 0, 0)),
            pl.BlockSpec((None, None, R4, NSA_DH), lambda n, kv: (n, kv, 0, 0)),
            pl.BlockSpec((None, None, Tq, nsb), lambda n, kv: (n, kv, 0, 0)),
        ],
        out_shape=[
            jax.ShapeDtypeStruct((N, NSA_KV, R4, NSA_DH), F32),
            jax.ShapeDtypeStruct((N, NSA_KV, R4, NSA_DH), F32),
            jax.ShapeDtypeStruct((N, NSA_KV, Tq, nsb), F32),
        ],
        compiler_params=_cparams("parallel", "parallel"),
        name="nsa_sample_cmp_win",
    )(z3, kc, vc, kw_full, kw_full, amat)


BLK_PER_STEP = PAGES_PER_STEP * PAGE // SEL_BLOCK


def _nsa_sample_b_body(pt_ref, *refs, Tq, past_len):
    page_refs = refs[:PAGES_PER_STEP]
    (q_ref, new_ref, sel_ref, exp_ref, ocmp_ref, owin_ref, zg_ref, gb_ref, o_ref, m_s, l_s, acc_s) = refs[PAGES_PER_STEP:]
    s_id = pl.program_id(1)
    R4 = NSA_G * Tq
    rep = lambda x: jnp.concatenate([x] * NSA_G, axis=0)
    pos = past_len + lax.broadcasted_iota(jnp.int32, (Tq, 1), 0)

    @pl.when(s_id == 0)
    def _():
        m_s[...] = jnp.full(m_s.shape, NEG, F32)
        l_s[...] = jnp.zeros(l_s.shape, F32)
        acc_s[...] = jnp.zeros(acc_s.shape, F32)

    def update(kv, s, mask, pv):
        s = jnp.where(mask, s, NEG)
        m_prev = m_s[kv]
        m_new = jnp.maximum(m_prev, jnp.max(s, axis=-1, keepdims=True))
        alpha = jnp.exp(m_prev - m_new)
        p = jnp.where(mask, jnp.exp(s - m_new), 0.0)
        l_s[kv] = alpha * l_s[kv] + jnp.sum(p, axis=-1, keepdims=True)
        acc_s[kv] = alpha * acc_s[kv] + pv(p.astype(BF16))
        m_s[kv] = m_new

    qs = [_stack_heads(q_ref[:, kv * NSA_G * NSA_DH:(kv + 1) * NSA_G * NSA_DH], NSA_DH ** -0.5) for kv in range(NSA_KV)]
    comp = lambda pr, c: pr[pl.ds(c, PAGE, stride=ROWS_PER_TOKEN), :].astype(BF16)
    for kv in range(NSA_KV):
        s = jnp.concatenate([_dot_nt(qs[kv], comp(pr, kv)) for pr in page_refs], axis=1)
        mask = rep(_dot(sel_ref[kv].astype(BF16), exp_ref[...])) > 0.5

        def pv(p, kv=kv):
            out = _dot(p[:, 0:PAGE], comp(page_refs[0], NSA_KV + kv))
            for i in range(1, PAGES_PER_STEP):
                out = out + _dot(p[:, i * PAGE:(i + 1) * PAGE], comp(page_refs[i], NSA_KV + kv))
            return out

        update(kv, s, mask, pv)

    @pl.when(s_id == pl.num_programs(1) - 1)
    def _():
        new = new_ref[...].astype(BF16)
        tok_n = past_len + lax.broadcasted_iota(jnp.int32, (R4, Tq), 1)
        gates = jax.nn.sigmoid(zg_ref[...] + gb_ref[...])
        for kv in range(NSA_KV):
            k_new = new[:, kv * NSA_DH:(kv + 1) * NSA_DH]
            v_new = new[:, (NSA_KV + kv) * NSA_DH:(NSA_KV + kv + 1) * NSA_DH]
            update(kv, _dot_nt(qs[kv], k_new), tok_n <= rep(pos), lambda p, v_new=v_new: _dot(p, v_new))
            o_sel = acc_s[kv] / l_s[kv]
            o_cmp = ocmp_ref[kv]
            o_win = owin_ref[kv]
            for g in range(NSA_G):
                rs = slice(g * Tq, (g + 1) * Tq)
                head = kv * NSA_G + g
                o_ref[:, head * NSA_DH:(head + 1) * NSA_DH] = (
                    gates[:, head:head + 1] * o_cmp[rs]
                    + gates[:, NSA_HEADS + head:NSA_HEADS + head + 1] * o_sel[rs]
                    + gates[:, 2 * NSA_HEADS + head:2 * NSA_HEADS + head + 1] * o_win[rs])


def _nsa_sample_b(z3, zg3, gate_b, pool_sel, page_table, sel, o_cmp, o_win, past_len):
    N, Tq = z3.shape[:2]
    n_pages = page_table.shape[1]
    assert n_pages % PAGES_PER_STEP == 0 and past_len % SEL_BLOCK == 0
    n_steps = n_pages // PAGES_PER_STEP
    sel_steps = sel[..., :past_len // SEL_BLOCK].reshape(N, NSA_KV, Tq, n_steps, BLK_PER_STEP).transpose(0, 3, 1, 2, 4)
    tok = np.arange(PAGES_PER_STEP * PAGE)[None, :]
    expand = jnp.asarray((tok // SEL_BLOCK == np.arange(BLK_PER_STEP)[:, None]).astype(np.float32), BF16)
    W = 4 * NSA_DH
    R4 = NSA_G * Tq
    qcol = 2 * D_HALF // (NSA_HEADS * NSA_DH)
    kvs_col = (3 * D_HALF + W) // W
    page_specs = [
        pl.BlockSpec((PAGE_ROWS, NSA_DH), functools.partial(
            lambda n, s, pt, r: (pt[n, s * PAGES_PER_STEP + r], 0), r=r))
        for r in range(PAGES_PER_STEP)
    ]
    per_n4 =lambda shape: pl.BlockSpec((None,) + shape, lambda n, s, pt: (n, 0, 0, 0))
    grid_spec = pltpu.PrefetchScalarGridSpec(
        num_scalar_prefetch=1,
        grid=(N, n_pages // PAGES_PER_STEP),
        in_specs=page_specs + [
            pl.BlockSpec((None, Tq, NSA_HEADS * NSA_DH), lambda n, s, pt: (n, 0, qcol)),
            pl.BlockSpec((None, Tq, W), lambda n, s, pt: (n, 0, kvs_col)),
            pl.BlockSpec((None, None, NSA_KV, Tq, BLK_PER_STEP), lambda n, s, pt: (n, s, 0, 0, 0)),
            pl.BlockSpec((BLK_PER_STEP, PAGES_PER_STEP * PAGE), lambda n, s, pt: (0, 0)),
            per_n4((NSA_KV, R4, NSA_DH)),
            per_n4((NSA_KV, R4, NSA_DH)),
            pl.BlockSpec((None, Tq, LANE), lambda n, s, pt: (n, 0, 0)),
            pl.BlockSpec((1, LANE), lambda n, s, pt: (0, 0)),
        ],
        out_specs=pl.BlockSpec((None, Tq, NSA_HEADS * NSA_DH), lambda n, s, pt: (n, 0, 0)),
        scratch_shapes=[
            pltpu.VMEM((NSA_KV, R4, 1), F32),
            pltpu.VMEM((NSA_KV, R4, 1), F32),
            pltpu.VMEM((NSA_KV, R4, NSA_DH), F32),
        ],
    )
    return pl.pallas_call(
        functools.partial(_nsa_sample_b_body, Tq=Tq, past_len=past_len),
        grid_spec=grid_spec,
        out_shape=jax.ShapeDtypeStruct((N, Tq, NSA_HEADS * NSA_DH), F32),
        compiler_params=_cparams("parallel", "arbitrary"),
        name="nsa_sample_sel",
    )(page_table, *([pool_sel] * PAGES_PER_STEP), z3, z3, sel_steps, expand, o_cmp, o_win, zg3, gate_b)


MLSTM_CHUNK_PROMPT = 256
SAMPLE_PAD = 128


def _pad_rows(x, rows):
    return jnp.pad(x, ((0, 0), (0, rows - x.shape[1]), (0, 0)))


def _lane_pad(v):
    return jnp.pad(v.astype(F32), (0, LANE - v.shape[0])).reshape(1, LANE)


def _gate_cols(w_in, n_main):
    return jnp.pad(w_in[:, n_main:], ((0, 0), (0, LANE - (w_in.shape[1] - n_main))))


def _even_mixers(z3, zg3, t0, pool_buf, C0, n0, m0, w):
    N, T = z3.shape[:2]
    st = jnp.pad(pool_buf, ((0, 0), (HALO - POOL_PAD, 0), (0, 0)))
    y_a = _pool_mix(z3, st, w["pool_w"], w["pool_scale"], t0)
    u_ext_tail = jnp.concatenate([pool_buf, z3[:, :, :D_HALF]], axis=1)[:, -POOL_PAD:] if T < POOL_PAD else z3[:, -POOL_PAD:, :D_HALF]

    if T % MLSTM_CHUNK_PROMPT == 0:
        L, zm, zgm = MLSTM_CHUNK_PROMPT, z3, zg3
    else:
        L, zm, zgm = SAMPLE_PAD, _pad_rows(z3, SAMPLE_PAD), _pad_rows(zg3, SAMPLE_PAD)
    y_b, C, n, m = _mlstm(zm, zgm, w["gate_b"], w["mnorm_g"], C0, n0, m0, L, min(T, L))
    return y_a, y_b[:, :T], u_ext_tail, C, n, m


def _odd_mixers(z3, zg3, w, past=None):
    N, T, n_main = z3.shape
    W = 4 * NSA_DH
    kvc = z3[:, :, 3 * D_HALF:3 * D_HALF + W]
    kvs = z3[:, :, 3 * D_HALF + W:3 * D_HALF + 2 * W]
    kvw = z3[:, :, 3 * D_HALF + 2 * W:3 * D_HALF + 3 * W]

    if past is None:
        y_c, _ = _gmlp(z3, w["gnorm_g"], w["ws"], w["bs_t"])
        vn = None
        pt = jnp.arange(N * (T // PAGE), dtype=jnp.int32).reshape(N, T // PAGE)
        a, b = _compress_ab(z3.reshape(N * (T // PAGE), PAGE, n_main), pt, w["wab"], 3 * D_HALF // W)
        kc, vc = _cmp_proj(a, b, w["cmp_w"])
        o = _nsa_prompt(z3, zg3, w["nsa_gate_b"], kc, vc)
        win_state = kvw[:, -min(WINDOW, T):]
    else:
        win_buf, pool_cmp, pool_sel, page_table = past
        past_len = page_table.shape[1] * PAGE
        wb = win_buf.shape[1]
        y_c, vn = _gmlp(_pad_rows(z3[:, :, :2 * D_HALF], SAMPLE_PAD), w["gnorm_g"], w["ws"], w["bs_t"])
        y_c, vn = y_c[:, :T], vn[:, :T]
        assert (past_len + T) // CMP_STRIDE == past_len // CMP_STRIDE
        a, b = _compress_ab_rows(pool_cmp, page_table, w["wab_rows"])
        kc, vc = _cmp_proj(a, b, w["cmp_w"])
        kw_all = jnp.concatenate([win_buf.reshape(N, wb, W), kvw], axis=1)
        nw = -(-(wb + T) // LANE) * LANE
        o_cmp, o_win, sel = _nsa_sample_a(z3, kc, vc, _pad_rows(kw_all, nw), past_len, wb)
        o = _nsa_sample_b(z3, zg3, w["nsa_gate_b"], pool_sel, page_table, sel, o_cmp, o_win, past_len)
        win_state = kw_all[:, -wb:]

    kv5 = lambda t: t.reshape(N, t.shape[1], 2, NSA_KV, NSA_DH)
    return y_c, o, vn, kv5(kvc), kv5(kvs), kv5(win_state)


def _layer(xp, xs, w, ffn_stacks, layer, n_main, mixers_p, mixers_s):
    B, T, D = xp.shape
    Ns, Ts, _ = xs.shape
    xp2, xs2 = xp.reshape(B * T, D), xs.reshape(Ns * Ts, D)
    zp, zgp, zs, zgs = _norm_matmul(xp2, xs2, w["ng"][0:1], w["w_in"], w["w_in_layer"], n_main, w["w_gate"])
    a1p, a2p, *extra_p = mixers_p(zp.reshape(B, T, n_main), zgp.reshape(B, T, LANE))
    a1s, a2s, *extra_s = mixers_s(zs.reshape(Ns, Ts, n_main), zgs.reshape(Ns, Ts, LANE))
    flat = lambda t: t.reshape(-1, D_HALF)
    xp2, xs2 = _out_proj(flat(a1p), flat(a2p), xp2, flat(a1s), flat(a2s), xs2, w["w_out_a"], w["w_out_b"], w["ng"][1:2])
    xp2, xs2 = _ffn(xp2, xs2, w["ng"][2:3], w["ng"][3:4], *ffn_stacks, layer)
    return xp2.reshape(B, T, D), xs2.reshape(Ns, Ts, D), extra_p, extra_s


def kernel(x_prompt, x_sample, state_pool, state_mlstm_c, state_mlstm_n, state_mlstm_m, state_win_kv, cache_cmp_kv, cache_sel_kv, page_table, norm_g, w_in_even, w_out_even, pool_w, pool_scale, mlstm_gate_b, mlstm_norm_g, w_in_odd, w_out_odd, gmlp_norm_g, gmlp_ws, gmlp_bs, nsa_cmp_pos, nsa_cmp_w, nsa_gate_b, ffn_w1, ffn_w3, ffn_w2):
    B = x_prompt.shape[0]
    depth = norm_g.shape[0]
    past_len = page_table.shape[1] * PAGE
    xp, xs = x_prompt, x_sample
    pool_p, pool_s, c_p, c_s, n_p, n_s, m_p, m_s = [], [], [], [], [], [], [], []
    gv_s, cmp_p, cmp_s, sel_p, sel_s, win_p, win_s = [], [], [], [], [], [], []
    ffn_stacks = (ffn_w1.astype(BF16), ffn_w3.astype(BF16), ffn_w2.astype(BF16))
    for l in range(depth):
        j = l // 2
        if l % 2 == 0:
            n_main = 4 * D_HALF
            w_out = w_out_even[j].astype(BF16)
            w = dict(ng=norm_g[l], w_in=w_in_even, w_in_layer=j, w_gate=_gate_cols(w_in_even[j], n_main),
                     w_out_a=w_out[:D_HALF], w_out_b=w_out[D_HALF:],
                     pool_w=pool_w[j].astype(BF16), pool_scale=pool_scale[j].reshape(1, D_HALF),
                     gate_b=_lane_pad(mlstm_gate_b[j].reshape(-1)), mnorm_g=mlstm_norm_g[j].reshape(1, D_HALF))
            zp = jnp.zeros((B, POOL_PAD, D_HALF), F32)
            zc = jnp.zeros((B, MLSTM_HEADS, MLSTM_DK, MLSTM_DV), F32)
            zn = jnp.zeros((B, MLSTM_HEADS, MLSTM_DK), F32)
            zm = jnp.zeros((B, MLSTM_HEADS), F32)
            xp, xs, (pb, c, n, m), (pbs, cs, ns_, ms) = _layer(
                xp, xs, w, ffn_stacks, l, n_main,
                lambda z3, zg3: _even_mixers(z3, zg3, 0, zp, zc, zn, zm, w),
                lambda z3, zg3: _even_mixers(z3, zg3, past_len, state_pool[j], state_mlstm_c[j], state_mlstm_n[j],
                                             state_mlstm_m[j], w))
            pool_p.append(pb); c_p.append(c); n_p.append(n); m_p.append(m)
            pool_s.append(pbs); c_s.append(cs); n_s.append(ns_); m_s.append(ms)
        else:
            n_main = 4 * D_HALF + D_HALF // 2
            w_out = w_out_odd[j].astype(BF16)
            cp = nsa_cmp_pos[j]
            wcol = jnp.repeat(cp, NSA_KV * NSA_DH, axis=1)
            wab = jnp.stack([wcol[:CMP_STRIDE], wcol[CMP_STRIDE:]])
            wab_rows = jnp.broadcast_to(jnp.repeat(cp, NSA_KV, axis=1).reshape(2, BLK_ROWS, 1), (2, BLK_ROWS, NSA_DH))
            w = dict(ng=norm_g[l], w_in=w_in_odd, w_in_layer=j, w_gate=_gate_cols(w_in_odd[j], n_main),
                     w_out_a=w_out[:D_HALF], w_out_b=w_out[D_HALF:],
                     gnorm_g=gmlp_norm_g[j].reshape(1, D_HALF), ws=gmlp_ws[j],
                     bs_t=jnp.pad(gmlp_bs[j].T, ((0, 0), (0, LANE - GMLP_GROUPS))),
                     wab=wab, wab_rows=wab_rows, cmp_w=nsa_cmp_w[j], nsa_gate_b=_lane_pad(nsa_gate_b[j]))
            n_pool = cache_cmp_kv.shape[1]
            flat = lambda c: c.reshape(c.shape[0] * n_pool * PAGE_ROWS, NSA_DH)
            past = (state_win_kv[j], flat(cache_cmp_kv), flat(cache_sel_kv), page_table + j * n_pool)
            xp, xs, (_, kc, ksl, wv), (vn, kcs, ksls, wvs) = _layer(
                xp, xs, w, ffn_stacks, l, n_main,
                lambda z3, zg3: _odd_mixers(z3, zg3, w),
                lambda z3, zg3: _odd_mixers(z3, zg3, w, past=past))
            cmp_p.append(kc); sel_p.append(ksl); win_p.append(wv)
            gv_s.append(vn); cmp_s.append(kcs); sel_s.append(ksls); win_s.append(wvs)
    st = jnp.stack
    return (xp, xs, st(pool_p), st(pool_s), st(c_p), st(c_s), st(n_p), st(n_s), st(m_p), st(m_s),
            st(gv_s), st(cmp_p), st(cmp_s), st(sel_p), st(sel_s), st(win_p), st(win_s))
```

```python
import functools
import math

import numpy as np
import jax
import jax.numpy as jnp
from jax import lax
from jax.experimental import pallas as pl
from jax.experimental.pallas import tpu as pltpu

F32 = jnp.float32
BF16 = jnp.bfloat16
NEG = -1e30

D_HALF = 1024
POOL_WINDOWS = (2, 4, 8, 16)
POOL_DG = 256
POOL_PAD = 15
HALO = 16
MLSTM_HEADS = 4
MLSTM_DK = 128
MLSTM_DV = 256
GATE_CAP = 15.0
GMLP_CHUNK = 128
GMLP_GROUPS = 4
GMLP_DG = 256
NSA_HEADS = 8
NSA_DH = 128
NSA_KV = 2
NSA_G = 4
CMP_STRIDE = 16
SEL_BLOCK = 64
N_SEL = 16
WINDOW = 512
QBLK = 128
PAGE = 128
FORCE_SCORE = 1e9
LANE = 128
VMEM_LIMIT = 56 * 1024 * 1024

NT_DIMS = (((1,), (1,)), ((), ()))
TN_DIMS = (((0,), (0,)), ((), ()))


def _cparams(*sem):
    return pltpu.CompilerParams(dimension_semantics=sem, vmem_limit_bytes=VMEM_LIMIT)


def _tile(m, pref):
    if m <= pref:
        return m
    for t in range(pref, 7, -1):
        if m % t == 0 and t % 8 == 0:
            return t
    return m


def _dot(a, b):
    return jnp.dot(a, b, preferred_element_type=F32)


def _dot_nt(a, b):
    return lax.dot_general(a, b, NT_DIMS, preferred_element_type=F32)


def _rms(x, g, eps=1e-6):
    return x * lax.rsqrt(jnp.mean(x * x, axis=-1, keepdims=True) + eps) * g


def _norm_mm_body(x_ref, xs_ref, g_ref, w_ref, wg_ref, o_ref, og_ref, os_ref, ogs_ref, xn_ref, xsn_ref):
    i = pl.program_id(0)
    j = pl.program_id(1)
    w = w_ref[...].astype(BF16)

    @pl.when(j == 0)
    def _():
        xn = _rms(x_ref[...], g_ref[...]).astype(BF16)
        xn_ref[...] = xn
        og_ref[...] = _dot(xn, wg_ref[...].astype(BF16))

    o_ref[...] = _dot(xn_ref[...], w)

    @pl.when(i == 0)
    def _():
        @pl.when(j == 0)
        def _():
            xsn = _rms(xs_ref[...], g_ref[...]).astype(BF16)
            xsn_ref[...] = xsn
            ogs_ref[...] = _dot(xsn, wg_ref[...].astype(BF16))

        os_ref[...] = _dot(xsn_ref[...], w)


def _norm_matmul(x, xs, g, w_stack, layer, n_main, wg, tm_pref=1024, tn_pref=512):
    M, K = x.shape
    Ms = xs.shape[0]
    tm = _tile(M, tm_pref)
    tn = _tile(n_main, tn_pref)
    nj = n_main // tn
    return pl.pallas_call(
        _norm_mm_body,
        grid=(M // tm, nj),
        in_specs=[
            pl.BlockSpec((tm, K), lambda i, j: (i, 0)),
            pl.BlockSpec((Ms, K), lambda i, j: (0, 0)),
            pl.BlockSpec((1, K), lambda i, j: (0, 0)),
            pl.BlockSpec((None, K, tn), lambda i, j: (layer, 0, j)),
            pl.BlockSpec((K, LANE), lambda i, j: (0, 0)),
        ],
        out_specs=[
            pl.BlockSpec((tm, tn), lambda i, j: (i, j)),
            pl.BlockSpec((tm, LANE), lambda i, j: (i, 0)),
            pl.BlockSpec((Ms, tn), lambda i, j: (0, jnp.where(i == 0, j, nj - 1))),
            pl.BlockSpec((Ms, LANE), lambda i, j: (0, 0)),
        ],
        out_shape=[jax.ShapeDtypeStruct((M, n_main), F32), jax.ShapeDtypeStruct((M, LANE), F32),
                   jax.ShapeDtypeStruct((Ms, n_main), F32), jax.ShapeDtypeStruct((Ms, LANE), F32)],
        scratch_shapes=[pltpu.VMEM((tm, K), BF16), pltpu.VMEM((Ms, K), BF16)],
        compiler_params=_cparams("arbitrary", "arbitrary"),
        name="norm_matmul",
    )(x, xs, g, w_stack, wg)


def _out_proj_body(a1_ref, a2_ref, res_ref, a1s_ref, a2s_ref, ress_ref, w1_ref, w2_ref, g_ref, gn_ref,
                   o_ref, on_ref, os_ref, osn_ref):
    def proj(a1, a2, res, o_r, on_r):
        y = _dot(a1[...].astype(BF16), w1_ref[...]) + _dot(a2[...].astype(BF16), w2_ref[...])
        x = res[...] + _rms(y, g_ref[...])
        o_r[...] = x
        on_r[...] = _rms(x, gn_ref[...]).astype(BF16)

    proj(a1_ref, a2_ref, res_ref, o_ref, on_ref)

    @pl.when(pl.program_id(0) == 0)
    def _():
        proj(a1s_ref, a2s_ref, ress_ref, os_ref, osn_ref)


def _out_proj(a1, a2, res, a1s, a2s, ress, w1, w2, g, g_next, tm_pref=512):
    M, K1 = a1.shape
    Ms = a1s.shape[0]
    K2 = a2.shape[1]
    D = w1.shape[1]
    tm = _tile(M, tm_pref)
    whole = lambda r, c: pl.BlockSpec((r, c), lambda i: (0, 0))
    rows = lambda c: pl.BlockSpec((tm, c), lambda i: (i, 0))
    return pl.pallas_call(
        _out_proj_body,
        grid=(M // tm,),
        in_specs=[
            rows(K1), rows(K2), rows(D),
            whole(Ms, K1), whole(Ms, K2), whole(Ms, D),
            whole(K1, D), whole(K2, D), whole(1, D), whole(1, D),
        ],
        out_specs=[rows(D), rows(D), whole(Ms, D), whole(Ms, D)],
        out_shape=[jax.ShapeDtypeStruct((M, D), F32), jax.ShapeDtypeStruct((M, D), BF16),
                   jax.ShapeDtypeStruct((Ms, D), F32), jax.ShapeDtypeStruct((Ms, D), BF16)],
        compiler_params=_cparams("arbitrary"),
        name="out_proj",
    )(a1, a2, res, a1s, a2s, ress, w1, w2, g, g_next)


def _ffn_up_body(xn_ref, xsn_ref, w1_ref, w3_ref, h_ref, hs_ref, w1_bf, w3_bf):
    def swiglu(xn):
        h1 = _dot(xn, w1_bf[...])
        return (h1 * jax.nn.sigmoid(h1) * _dot(xn, w3_bf[...])).astype(BF16)

    @pl.when(pl.program_id(1) == 0)
    def _():
        w1_bf[...] = w1_ref[...].astype(BF16)
        w3_bf[...] = w3_ref[...].astype(BF16)
        hs_ref[...] = swiglu(xsn_ref[...])

    h_ref[...] = swiglu(xn_ref[...])


def _ffn_up(xn, xsn, w1_stack, w3_stack, layer, tm_pref=1024, th_pref=512):
    M, D = xn.shape
    Ms = xsn.shape[0]
    H = w1_stack.shape[2]
    tm = _tile(M, tm_pref)
    th = _tile(H, th_pref)
    return pl.pallas_call(
        _ffn_up_body,
        grid=(H // th, M // tm),
        in_specs=[
            pl.BlockSpec((tm, D), lambda j, i: (i, 0)),
            pl.BlockSpec((Ms, D), lambda j, i: (0, 0)),
            pl.BlockSpec((None, D, th), lambda j, i: (layer, 0, j)),
            pl.BlockSpec((None, D, th), lambda j, i: (layer, 0, j)),
        ],
        out_specs=[
            pl.BlockSpec((tm, th), lambda j, i: (i, j)),
            pl.BlockSpec((Ms, th), lambda j, i: (0, j)),
        ],
        out_shape=[jax.ShapeDtypeStruct((M, H), BF16), jax.ShapeDtypeStruct((Ms, H), BF16)],
        scratch_shapes=[pltpu.VMEM((D, th), BF16), pltpu.VMEM((D, th), BF16)],
        compiler_params=_cparams("arbitrary", "arbitrary"),
        name="ffn_up",
    )(xn, xsn, w1_stack, w3_stack)


def _ffn_down_body(h_ref, hs_ref, x_ref, xs_ref, g3_ref, w2_ref, o_ref, os_ref, acc_ref, accs_ref):
    i = pl.program_id(0)
    j = pl.program_id(1)
    last = pl.num_programs(1) - 1
    w2 = w2_ref[...]

    def step(h_r, x_r, o_r, acc_r):
        @pl.when(j == 0)
        def _():
            acc_r[...] = jnp.zeros_like(acc_r)

        acc_r[...] += _dot(h_r[...], w2)

        @pl.when(j == last)
        def _():
            o_r[...] = x_r[...] + _rms(acc_r[...], g3_ref[...])

    step(h_ref, x_ref, o_ref, acc_ref)

    @pl.when(i == 0)
    def _():
        step(hs_ref, xs_ref, os_ref, accs_ref)


def _ffn_down(h, hs, x, xs, g3, w2_stack, layer, tm_pref=1024, tk_pref=512):
    M, D = x.shape
    Ms = xs.shape[0]
    H = h.shape[1]
    tm = _tile(M, tm_pref)
    tk = _tile(H, tk_pref)
    return pl.pallas_call(
        _ffn_down_body,
        grid=(M // tm, H // tk),
        in_specs=[
            pl.BlockSpec((tm, tk), lambda i, j: (i, j)),
            pl.BlockSpec((Ms, tk), lambda i, j: (0, j)),
            pl.BlockSpec((tm, D), lambda i, j: (i, 0), pipeline_mode=pl.Buffered(1)),
            pl.BlockSpec((Ms, D), lambda i, j: (0, 0)),
            pl.BlockSpec((1, D), lambda i, j: (0, 0)),
            pl.BlockSpec((None, tk, D), lambda i, j: (layer, j, 0)),
        ],
        out_specs=[
            pl.BlockSpec((tm, D), lambda i, j: (i, 0)),
            pl.BlockSpec((Ms, D), lambda i, j: (0, 0)),
        ],
        out_shape=[jax.ShapeDtypeStruct((M, D), F32), jax.ShapeDtypeStruct((Ms, D), F32)],
        scratch_shapes=[pltpu.VMEM((tm, D), F32), pltpu.VMEM((Ms, D), F32)],
        compiler_params=_cparams("arbitrary", "arbitrary"),
        name="ffn_down",
    )(h, hs, x, xs, g3, w2_stack)


def _pool_body(u_ref, prev_ref, st_ref, pw_ref, ps_ref, o_ref, ext_ref, *, tT, t0):
    t = pl.program_id(1)
    ext_ref[0:HALO, :] = jnp.where(t == 0, st_ref[...], prev_ref[...])
    ext_ref[HALO:HALO + tT, :] = u_ref[...]
    pos = t0 + t * tT + lax.broadcasted_iota(jnp.int32, (tT, 1), 0)
    for g, w in enumerate(POOL_WINDOWS):
        cs = slice(g * POOL_DG, (g + 1) * POOL_DG)
        x_new = ext_ref[HALO:HALO + tT, cs]
        tot = x_new
        for i in range(1, w):
            tot = tot + ext_ref[HALO - i:HALO - i + tT, cs]
        cnt = jnp.minimum(w, pos + 1).astype(F32)
        y = tot / cnt - x_new
        o_ref[:, cs] = _dot(y.astype(BF16), pw_ref[g]) * ps_ref[:, cs]


def _pool_mix(z3, st, pool_w, pool_scale, t0, tT_pref=512):
    N, T = z3.shape[:2]
    tT = _tile(T, tT_pref)
    nT = T // tT
    if nT > 1:
        assert tT % HALO == 0
        prev, prev_spec = z3, pl.BlockSpec((None, HALO, D_HALF), lambda n, t: (n, jnp.maximum(t * (tT // HALO) - 1, 0), 0))
    else:
        prev, prev_spec = st, pl.BlockSpec((None, HALO, D_HALF), lambda n, t: (n, 0, 0))
    return pl.pallas_call(
        functools.partial(_pool_body, tT=tT, t0=t0),
        grid=(N, nT),
        in_specs=[
            pl.BlockSpec((None, tT, D_HALF), lambda n, t: (n, t, 0)),
            prev_spec,
            pl.BlockSpec((None, HALO, D_HALF), lambda n, t: (n, 0, 0)),
            pl.BlockSpec((len(POOL_WINDOWS), POOL_DG, POOL_DG), lambda n, t: (0, 0, 0)),
            pl.BlockSpec((1, D_HALF), lambda n, t: (0, 0)),
        ],
        out_specs=pl.BlockSpec((None, tT, D_HALF), lambda n, t: (n, t, 0)),
        out_shape=jax.ShapeDtypeStruct((N, T, D_HALF), F32),
        scratch_shapes=[pltpu.VMEM((HALO + tT, D_HALF), F32)],
        compiler_params=_cparams("parallel", "arbitrary"),
        name="pool_mix",
    )(z3, prev, st, pool_w, pool_scale)


def _mlstm_body(q_ref, k_ref, v_ref, o_ref, zg_ref, gb_ref, mg_ref, c0_ref, n0_ref, m0_ref,
                y_ref, cN_ref, nN_ref, mN_ref, C_s, n_s, m_s, *, L, t_valid):
    c = pl.program_id(1)

    @pl.when(c == 0)
    def _():
        C_s[...] = c0_ref[...]
        n_s[...] = n0_ref[...]
        m_s[...] = m0_ref[...]

    a = GATE_CAP * jnp.tanh((zg_ref[...] + gb_ref[...]) / GATE_CAP)
    lane = lax.broadcasted_iota(jnp.int32, (L, LANE), 1)
    logsig = jnp.minimum(a, 0.0) - jnp.log1p(jnp.exp(-jnp.abs(a)))
    A = jnp.where(lane < MLSTM_HEADS, a, logsig)
    if t_valid < L:
        row = lax.broadcasted_iota(jnp.int32, (L, LANE), 0)
        A = jnp.where(row < t_valid, A, jnp.where(lane < MLSTM_HEADS, NEG, 0.0))
    r_i = lax.broadcasted_iota(jnp.int32, (L, L), 0)
    c_i = lax.broadcasted_iota(jnp.int32, (L, L), 1)
    causal = r_i >= c_i
    Bc = jnp.dot(causal.astype(F32), A, preferred_element_type=F32, precision=lax.Precision.HIGHEST)
    At = A.T
    Bt = Bc.T

    for h in range(MLSTM_HEADS):
        ks = slice(h * MLSTM_DK, (h + 1) * MLSTM_DK)
        vs = slice(h * MLSTM_DV, (h + 1) * MLSTM_DV)
        qh = q_ref[:, ks] * (MLSTM_DK ** -0.5)
        kh = k_ref[:, ks]
        vh = v_ref[:, vs].astype(BF16)
        b_c = Bc[:, MLSTM_HEADS + h:MLSTM_HEADS + h + 1]
        b_r = Bt[MLSTM_HEADS + h:MLSTM_HEADS + h + 1, :]
        li_c = A[:, h:h + 1]
        li_r = At[h:h + 1, :]
        m = m_s[h]
        Ch = C_s[h]
        nh = n_s[h]

        d = jnp.where(causal, b_c - b_r + li_r, NEG)
        inter = b_c + m
        m_t = jnp.maximum(inter, jnp.max(d, axis=-1, keepdims=True))
        w_in = jnp.exp(inter - m_t)
        qb = qh.astype(BF16)
        s = _dot_nt(qb, kh.astype(BF16)) * jnp.exp(d - m_t)
        num = w_in * _dot(qb, Ch.astype(BF16)) + _dot(s.astype(BF16), vh)
        den = w_in * jnp.sum(qh * nh, axis=-1, keepdims=True) + jnp.sum(s, axis=-1, keepdims=True)
        hh = num / jnp.maximum(jnp.abs(den), jnp.exp(-m_t))

        b_end = b_c[L - 1:L, :]
        g_c = b_end - b_c + li_c
        m_new = jnp.maximum(b_end + m, jnp.max(g_c, axis=0, keepdims=True))
        w_c = jnp.exp(b_end + m - m_new)
        kw = kh * jnp.exp(g_c - m_new)
        C_s[h] = w_c * Ch + lax.dot_general(kw.astype(BF16), vh, TN_DIMS, preferred_element_type=F32)
        n_s[h] = w_c * nh + jnp.sum(kw, axis=0, keepdims=True)
        m_s[h] = m_new

        hn = _rms(hh, mg_ref[:, vs])
        y_ref[:, vs] = jax.nn.sigmoid(o_ref[:, vs]) * hn

    @pl.when(c == pl.num_programs(1) - 1)
    def _():
        cN_ref[...] = C_s[...]
        nN_ref[...] = n_s[...]
        mN_ref[...] = m_s[...]


def _mlstm(z3, zg3, gate_b, mnorm_g, C0, n0, m0, L, t_valid):
    N, Tp = z3.shape[:2]
    nc = Tp // L
    H = MLSTM_HEADS
    qk_w = H * MLSTM_DK
    v_w = H * MLSTM_DV
    st = lambda n, c: (n, 0, 0, 0)
    outs = pl.pallas_call(
        functools.partial(_mlstm_body, L=L, t_valid=t_valid),
        grid=(N, nc),
        in_specs=[
            pl.BlockSpec((None, L, qk_w), lambda n, c: (n, c, D_HALF // qk_w)),
            pl.BlockSpec((None, L, qk_w), lambda n, c: (n, c, D_HALF // qk_w + 1)),
            pl.BlockSpec((None, L, v_w), lambda n, c: (n, c, 2)),
            pl.BlockSpec((None, L, v_w), lambda n, c: (n, c, 3)),
            pl.BlockSpec((None, L, LANE), lambda n, c: (n, c, 0)),
            pl.BlockSpec((1, LANE), lambda n, c: (0, 0)),
            pl.BlockSpec((1, v_w), lambda n, c: (0, 0)),
            pl.BlockSpec((None, H, MLSTM_DK, MLSTM_DV), st),
            pl.BlockSpec((None, H, 1, MLSTM_DK), st),
            pl.BlockSpec((None, H, 1, 1), st),
        ],
        out_specs=[
            pl.BlockSpec((None, L, v_w), lambda n, c: (n, c, 0)),
            pl.BlockSpec((None, H, MLSTM_DK, MLSTM_DV), st),
            pl.BlockSpec((None, H, 1, MLSTM_DK), st),
            pl.BlockSpec((None, H, 1, 1), st),
        ],
        out_shape=[
            jax.ShapeDtypeStruct((N, Tp, v_w), F32),
            jax.ShapeDtypeStruct((N, H, MLSTM_DK, MLSTM_DV), F32),
            jax.ShapeDtypeStruct((N, H, 1, MLSTM_DK), F32),
            jax.ShapeDtypeStruct((N, H, 1, 1), F32),
        ],
        scratch_shapes=[
            pltpu.VMEM((H, MLSTM_DK, MLSTM_DV), F32),
            pltpu.VMEM((H, 1, MLSTM_DK), F32),
            pltpu.VMEM((H, 1, 1), F32),
        ],
        compiler_params=_cparams("parallel", "arbitrary"),
        name="mlstm",
    )(z3, z3, z3, z3, zg3, gate_b, mnorm_g, C0, n0.reshape(N, H, 1, MLSTM_DK), m0.reshape(N, H, 1, 1))
    y, C, n, m = outs
    return y, C, n.reshape(N, H, MLSTM_DK), m.reshape(N, H)


def _gmlp_body(u_ref, v_ref, g_ref, ws_ref, bs_ref, y_ref, vn_ref, *, tT):
    v = v_ref[...]
    vc = v - jnp.mean(v, axis=-1, keepdims=True)
    vn = vc * lax.rsqrt(jnp.mean(vc * vc, axis=-1, keepdims=True) + 1e-5) * g_ref[...]
    vn_ref[...] = vn
    r_i = lax.broadcasted_iota(jnp.int32, (GMLP_CHUNK, GMLP_CHUNK), 0)
    c_i = lax.broadcasted_iota(jnp.int32, (GMLP_CHUNK, GMLP_CHUNK), 1)
    for g in range(GMLP_GROUPS):
        cs = slice(g * GMLP_DG, (g + 1) * GMLP_DG)
        wm = jnp.where(r_i >= c_i, ws_ref[g], 0.0).astype(BF16)
        bias = bs_ref[:, g:g + 1]
        for ch in range(tT // GMLP_CHUNK):
            rs = slice(ch * GMLP_CHUNK, (ch + 1) * GMLP_CHUNK)
            mix = _dot(wm, vn[rs, cs].astype(BF16)) + bias
            y_ref[rs, cs] = u_ref[rs, cs] * mix


def _gmlp(z3, gnorm_g, ws, bs_t, tT_pref=512):
    N, Tp = z3.shape[:2]
    tT = _tile(Tp, tT_pref)
    assert tT % GMLP_CHUNK == 0
    return pl.pallas_call(
        functools.partial(_gmlp_body, tT=tT),
        grid=(N, Tp // tT),
        in_specs=[
            pl.BlockSpec((None, tT, D_HALF), lambda n, t: (n, t, 0)),
            pl.BlockSpec((None, tT, D_HALF), lambda n, t: (n, t, 1)),
            pl.BlockSpec((1, D_HALF), lambda n, t: (0, 0)),
            pl.BlockSpec((GMLP_GROUPS, GMLP_CHUNK, GMLP_CHUNK), lambda n, t: (0, 0, 0)),
            pl.BlockSpec((GMLP_CHUNK, LANE), lambda n, t: (0, 0)),
        ],
        out_specs=[
            pl.BlockSpec((None, tT, D_HALF), lambda n, t: (n, t, 0)),
            pl.BlockSpec((None, tT, D_HALF), lambda n, t: (n, t, 0)),
        ],
        out_shape=[jax.ShapeDtypeStruct((N, Tp, D_HALF), F32), jax.ShapeDtypeStruct((N, Tp, D_HALF), F32)],
        compiler_params=_cparams("parallel", "parallel"),
        name="gmlp",
    )(z3, z3, gnorm_g, ws, bs_t)


PAGES_PER_STEP = 16
BLK_PER_PAGE = PAGE // CMP_STRIDE


def _compress_body(pt_ref, *refs):
    page_refs = refs[:PAGES_PER_STEP]
    wab_ref, a_ref, b_ref = refs[PAGES_PER_STEP:]
    wa = wab_ref[0]
    wb = wab_ref[1]
    for p, pr in enumerate(page_refs):
        x = pr[...].reshape(BLK_PER_PAGE, CMP_STRIDE, 4 * NSA_DH)
        rs = slice(p * BLK_PER_PAGE, (p + 1) * BLK_PER_PAGE)
        a_ref[rs, :] = jnp.sum(x * wa[None], axis=1)
        b_ref[rs, :] = jnp.sum(x * wb[None], axis=1)


def _compress_ab(pages, page_table, wab, col_block):
    N, n_pages = page_table.shape
    assert n_pages % PAGES_PER_STEP == 0
    W = 4 * NSA_DH
    rows = PAGES_PER_STEP * BLK_PER_PAGE
    page_specs = [
        pl.BlockSpec((None, PAGE, W), functools.partial(
            lambda n, s, pt, r: (pt[n, s * PAGES_PER_STEP + r], 0, col_block), r=r))
        for r in range(PAGES_PER_STEP)
    ]
    grid_spec = pltpu.PrefetchScalarGridSpec(
        num_scalar_prefetch=1,
        grid=(N, n_pages // PAGES_PER_STEP),
        in_specs=page_specs + [pl.BlockSpec((2, CMP_STRIDE, W), lambda n, s, pt: (0, 0, 0))],
        out_specs=[pl.BlockSpec((None, rows, W), lambda n, s, pt: (n, s, 0))] * 2,
    )
    ns = n_pages * BLK_PER_PAGE
    return pl.pallas_call(
        _compress_body,
        grid_spec=grid_spec,
        out_shape=[jax.ShapeDtypeStruct((N, ns, W), F32)] * 2,
        compiler_params=_cparams("parallel", "parallel"),
        name="nsa_compress",
    )(page_table, *([pages] * PAGES_PER_STEP), wab)


ROWS_PER_TOKEN = 2 * NSA_KV
PAGE_ROWS = PAGE * ROWS_PER_TOKEN
BLK_ROWS = CMP_STRIDE * ROWS_PER_TOKEN


def _compress_rows_body(pt_ref, *refs):
    page_refs = refs[:PAGES_PER_STEP]
    wab_ref, a_ref, b_ref, z_s = refs[PAGES_PER_STEP:]
    for p, pr in enumerate(page_refs):
        x = pr[...].reshape(BLK_PER_PAGE, BLK_ROWS, NSA_DH)
        for half, out_ref in enumerate((a_ref, b_ref)):
            y = (x * wab_ref[half][None]).reshape(BLK_PER_PAGE, BLK_ROWS // 8, 8, NSA_DH).sum(axis=1)
            y = y.reshape(BLK_PER_PAGE * 8, NSA_DH)
            slot = 2 * p + half
            z_s[slot] = y + pltpu.roll(y, BLK_PER_PAGE * 8 - ROWS_PER_TOKEN, 0)
            for c in range(ROWS_PER_TOKEN):
                out_ref[p * BLK_PER_PAGE:(p + 1) * BLK_PER_PAGE, c * NSA_DH:(c + 1) * NSA_DH] = (
                    z_s[slot, pl.ds(c, BLK_PER_PAGE, stride=8), :])


def _compress_ab_rows(rows, page_table, wab_rows):
    N, n_pages = page_table.shape
    assert n_pages % PAGES_PER_STEP == 0
    W = ROWS_PER_TOKEN * NSA_DH
    out_rows = PAGES_PER_STEP * BLK_PER_PAGE
    page_specs = [
        pl.BlockSpec((PAGE_ROWS, NSA_DH), functools.partial(
            lambda n, s, pt, r: (pt[n, s * PAGES_PER_STEP + r], 0), r=r))
        for r in range(PAGES_PER_STEP)
    ]
    grid_spec = pltpu.PrefetchScalarGridSpec(
        num_scalar_prefetch=1,
        grid=(N, n_pages // PAGES_PER_STEP),
        in_specs=page_specs + [pl.BlockSpec((2, BLK_ROWS, NSA_DH), lambda n, s, pt: (0, 0, 0))],
        out_specs=[pl.BlockSpec((None, out_rows, W), lambda n, s, pt: (n, s, 0))] * 2,
        scratch_shapes=[pltpu.VMEM((2 * PAGES_PER_STEP, BLK_PER_PAGE * 8, NSA_DH), F32)],
    )
    ns = n_pages * BLK_PER_PAGE
    return pl.pallas_call(
        _compress_rows_body,
        grid_spec=grid_spec,
        out_shape=[jax.ShapeDtypeStruct((N, ns, W), F32)] * 2,
        compiler_params=_cparams("parallel", "arbitrary"),
        name="nsa_compress_rows",
    )(page_table, *([rows] * PAGES_PER_STEP), wab_rows)


def _cmp_proj_body(a_ref, b_ref, w_ref, kc_ref, vc_ref, *, ns):
    b_next = pltpu.roll(b_ref[...], ns - 1, 0)
    row = lax.broadcasted_iota(jnp.int32, (ns, 1), 0)
    blk = jnp.where(row < ns - 1, a_ref[...] + b_next, 0.0).astype(BF16)
    w0 = w_ref[0].astype(BF16)
    w1 = w_ref[1].astype(BF16)
    for kv in range(NSA_KV):
        cs = slice(kv * NSA_DH, (kv + 1) * NSA_DH)
        kc_ref[:, cs] = _dot(blk[:, kv * NSA_DH:(kv + 1) * NSA_DH], w0)
        vc_ref[:, cs] = _dot(blk[:, (NSA_KV + kv) * NSA_DH:(NSA_KV + kv + 1) * NSA_DH], w1)


def _cmp_proj(a, b, w_cmp):
    N, ns, W = a.shape
    return pl.pallas_call(
        functools.partial(_cmp_proj_body, ns=ns),
        grid=(N,),
        in_specs=[
            pl.BlockSpec((None, ns, W), lambda n: (n, 0, 0)),
            pl.BlockSpec((None, ns, W), lambda n: (n, 0, 0)),
            pl.BlockSpec((2, NSA_DH, NSA_DH), lambda n: (0, 0, 0)),
        ],
        out_specs=[pl.BlockSpec((None, ns, NSA_KV * NSA_DH), lambda n: (n, 0, 0))] * 2,
        out_shape=[jax.ShapeDtypeStruct((N, ns, NSA_KV * NSA_DH), F32)] * 2,
        compiler_params=_cparams("parallel"),
        name="nsa_cmp_proj",
    )(a, b, w_cmp)


def _split3(x):
    hi = x.astype(BF16)
    r1 = x - hi.astype(F32)
    mid = r1.astype(BF16)
    lo = (r1 - mid.astype(F32)).astype(BF16)
    return hi, mid, lo


def _cmp_to_sel(p_sum, a_mat):
    hi, mid, lo = _split3(p_sum)
    return _dot(hi, a_mat) + _dot(mid, a_mat) + _dot(lo, a_mat)


def _topk_mask(score, n_valid, k):
    lane = lax.broadcasted_iota(jnp.int32, score.shape, 1)
    rank = jnp.zeros(score.shape, F32)
    for jp in range(n_valid):
        col = score[:, jp:jp + 1]
        beats = (col > score) | ((col == score) & (lane > jp))
        rank = rank + beats.astype(F32)
    return ((rank < k) & (lane < n_valid)).astype(F32)


def _sel_scores(imp, qblk, n_valid):
    lane = lax.broadcasted_iota(jnp.int32, imp.shape, 1)
    forced = (lane == 0) | (lane == qblk) | (lane == qblk - 1)
    score = jnp.where(forced, FORCE_SCORE, jnp.where(lane > qblk, -1.0, imp))
    return jnp.where(lane < n_valid, score, -2.0)


def _masked_softmax_rows(s, mask):
    s = jnp.where(mask, s, NEG)
    e = jnp.exp(s - jnp.max(s, axis=-1, keepdims=True))
    return jnp.where(mask, e / jnp.sum(e, axis=-1, keepdims=True), 0.0)


def _stack_heads(q, scale):
    return (jnp.concatenate([q[:, g * NSA_DH:(g + 1) * NSA_DH] for g in range(NSA_G)], axis=0) * scale).astype(BF16)


SEL_TK = 512
WIN_KEYS = WINDOW + QBLK


def _nsa_prompt_body(q_ref, zg_ref, gb_ref, kc_ref, vc_ref, ks_ref, vs_ref, kw_ref, vw_ref, amat_t_ref,
                     o_ref, ks_bf, kw_bf, vs_t, vw_t, kc_bf, vc_t, sc_ref, *, n_sel, nsr):
    qb = pl.program_id(1)
    T = ks_ref.shape[0]
    kvc = lambda kv: slice(kv * NSA_DH, (kv + 1) * NSA_DH)

    @pl.when(qb == 0)
    def _():
        ks_bf[...] = ks_ref[...].astype(BF16)
        kw_bf[...] = kw_ref[...].astype(BF16)
        kc_bf[...] = kc_ref[...].astype(BF16)
        vc_t[...] = vc_ref[...].T.astype(BF16)

        def transpose_values(i, c):
            r0 = pl.multiple_of(i * LANE, LANE)
            vs_t[i] = vs_ref[pl.ds(r0, LANE), :].T.astype(BF16)
            vw_t[i] = vw_ref[pl.ds(r0, LANE), :].T.astype(BF16)
            return c

        lax.fori_loop(0, T // LANE, transpose_values, 0)

    R = QBLK
    start = qb * R
    groups = range(NSA_KV)
    tile4 = lambda x: jnp.concatenate([x] * NSA_G, axis=1)
    q_t = [jnp.concatenate([(q_ref[:, (kv * NSA_G + g) * NSA_DH:(kv * NSA_G + g + 1) * NSA_DH] * (NSA_DH ** -0.5)).T
                            for g in range(NSA_G)], axis=1).astype(BF16) for kv in groups]
    pos = start + lax.broadcasted_iota(jnp.int32, (1, R), 1)

    ns = kc_bf.shape[0]
    cmp_end = (lax.broadcasted_iota(jnp.int32, (ns, R), 0) + 2) * CMP_STRIDE - 1
    bias_c = tile4(jnp.where(cmp_end <= pos, 0.0, NEG))
    any_c = tile4(jnp.where(pos >= 2 * CMP_STRIDE - 1, 1.0, 0.0))
    a_t = amat_t_ref[...]
    blk = lax.broadcasted_iota(jnp.int32, (nsr, R), 0)
    qblk = (start + lax.broadcasted_iota(jnp.int32, (nsr, R), 1)) // SEL_BLOCK
    forced = (blk == 0) | (blk == qblk) | (blk == qblk - 1)
    o_cmp, score = [], []
    for kv in groups:
        s_c = _dot(kc_bf[:, kvc(kv)], q_t[kv]) + bias_c
        e_c = jnp.exp(s_c - jnp.max(s_c, axis=0, keepdims=True))
        p_c = e_c * (any_c / jnp.sum(e_c, axis=0, keepdims=True))
        o_cmp.append(_dot(vc_t[kvc(kv), :], p_c.astype(BF16)))
        p_sum = p_c[:, 0:R] + p_c[:, R:2 * R] + p_c[:, 2 * R:3 * R] + p_c[:, 3 * R:4 * R]
        hi, mid, lo = _split3(p_sum)
        imp_t = _dot(a_t, hi) + _dot(a_t, mid) + _dot(a_t, lo)
        sc = jnp.where(forced, FORCE_SCORE, jnp.where(blk > qblk, -1.0, imp_t))
        score.append(jnp.where(blk < n_sel, sc, -2.0))
        sc_ref[kv] = score[kv]

    def rank_step(i, ranks):
        ranks = list(ranks)
        for jp in (2 * i, 2 * i + 1):
            for kv in groups:
                row = sc_ref[kv, pl.ds(jp, 1), :]
                beats = (row > score[kv]) | ((row == score[kv]) & (blk > jp))
                ranks[kv] = ranks[kv] + beats.astype(F32)
        return tuple(ranks)

    n_rank = jnp.minimum(start // SEL_BLOCK + QBLK // SEL_BLOCK, n_sel)
    ranks = lax.fori_loop(0, (n_rank + 1) // 2, rank_step, tuple(jnp.zeros((nsr, R), F32) for _ in groups))
    for kv in groups:
        sc_ref[kv] = jnp.where(ranks[kv] < N_SEL, 0.0, NEG)

    def values_product(v_ref, tile0, n_sub, kv, p):
        out = _dot(v_ref[tile0, kvc(kv), :], p[0:LANE])
        for i in range(1, n_sub):
            out = out + _dot(v_ref[tile0 + i, kvc(kv), :], p[i * LANE:(i + 1) * LANE])
        return out

    key_tk = lax.broadcasted_iota(jnp.int32, (SEL_TK, R), 0)

    def sel_tile(kt, carry, causal):
        k0 = pl.multiple_of(kt * SEL_TK, SEL_TK)
        scores = [_dot(ks_bf[pl.ds(k0, SEL_TK), kvc(kv)], q_t[kv]) for kv in groups]
        stats = []
        for kv in groups:
            m_prev, l_prev, _ = carry[kv]
            bias = jnp.concatenate(
                [jnp.broadcast_to(sc_ref[kv, pl.ds(kt * (SEL_TK // SEL_BLOCK) + b, 1), :], (SEL_BLOCK, R))
                 for b in range(SEL_TK // SEL_BLOCK)], axis=0)
            if causal:
                bias = jnp.where(k0 + key_tk <= pos, bias, NEG)
            s = scores[kv] + tile4(bias)
            m_new = jnp.maximum(m_prev, jnp.max(s, axis=0, keepdims=True))
            alpha = jnp.exp(m_prev - m_new)
            p = jnp.exp(s - m_new)
            stats.append((m_new, alpha, alpha * l_prev + jnp.sum(p, axis=0, keepdims=True), p.astype(BF16)))
        return tuple((m_new, l_new, alpha * carry[kv][2]
                      + values_product(vs_t, kt * (SEL_TK // LANE), SEL_TK // LANE, kv, p))
                     for kv, (m_new, alpha, l_new, p) in enumerate(stats))

    n_tiles = (start + R - 1) // SEL_TK + 1
    init = tuple((jnp.full((1, NSA_G * R), NEG, F32), jnp.zeros((1, NSA_G * R), F32),
                  jnp.zeros((NSA_DH, NSA_G * R), F32)) for _ in groups)
    carry = lax.fori_loop(0, n_tiles - 1, lambda kt, c: sel_tile(kt, c, False), init)
    o_sel = [acc / l for _, l, acc in sel_tile(n_tiles - 1, carry, True)]

    w0 = pl.multiple_of(jnp.maximum(start - WINDOW, 0), QBLK)
    key_w = w0 + lax.broadcasted_iota(jnp.int32, (WIN_KEYS, R), 0)
    bias_w = tile4(jnp.where((key_w <= pos) & (key_w > pos - WINDOW), 0.0, NEG))
    gates_t = jax.nn.sigmoid(zg_ref[...] + gb_ref[...]).T
    for kv in groups:
        s_w = _dot(kw_bf[pl.ds(w0, WIN_KEYS), kvc(kv)], q_t[kv]) + bias_w
        e_w = jnp.exp(s_w - jnp.max(s_w, axis=0, keepdims=True))
        o_win = (values_product(vw_t, w0 // LANE, WIN_KEYS // LANE, kv, e_w.astype(BF16))
                 / jnp.sum(e_w, axis=0, keepdims=True))
        gate = lambda branch: jnp.concatenate(
            [gates_t[branch * NSA_HEADS + kv * NSA_G + g:branch * NSA_HEADS + kv * NSA_G + g + 1, :]
             for g in range(NSA_G)], axis=1)
        out_t = gate(0) * o_cmp[kv] + gate(1) * o_sel[kv] + gate(2) * o_win
        for g in range(NSA_G):
            h = kv * NSA_G + g
            o_ref[:, h * NSA_DH:(h + 1) * NSA_DH] = out_t[:, g * R:(g + 1) * R].T


def _sel_map(ns, n_sel, nsb):
    i = np.arange(ns)[:, None]
    j = np.arange(nsb)[None, :]
    r = SEL_BLOCK // CMP_STRIDE
    return jnp.asarray(((i >= r * j - 1) & (i <= r * j + r - 1) & (j < n_sel)).astype(np.float32), BF16)


def _nsa_prompt(z3, zg3, gate_b, kc, vc):
    N, T = z3.shape[:2]
    assert T % SEL_TK == 0 and T >= WIN_KEYS
    ns = kc.shape[1]
    n_sel = T // SEL_BLOCK
    nsr = -(-n_sel // 8) * 8
    assert nsr <= LANE
    amat_t = _sel_map(ns, n_sel, nsr).T
    KVW = NSA_KV * NSA_DH
    qcol = 2 * D_HALF // (NSA_HEADS * NSA_DH)
    kvs_col = (3 * D_HALF + 2 * KVW) // KVW
    kvw_col = kvs_col + 2
    full = lambda off: pl.BlockSpec((None, T, KVW), lambda n, qb: (n, 0, off))
    return pl.pallas_call(
        functools.partial(_nsa_prompt_body, n_sel=n_sel, nsr=nsr),
        grid=(N, T // QBLK),
        in_specs=[
            pl.BlockSpec((None, QBLK, NSA_HEADS * NSA_DH), lambda n, qb: (n, qb, qcol)),
            pl.BlockSpec((None, QBLK, LANE), lambda n, qb: (n, qb, 0)),
            pl.BlockSpec((1, LANE), lambda n, qb: (0, 0)),
            pl.BlockSpec((None, ns, KVW), lambda n, qb: (n, 0, 0)),
            pl.BlockSpec((None, ns, KVW), lambda n, qb: (n, 0, 0)),
            full(kvs_col), full(kvs_col + 1), full(kvw_col), full(kvw_col + 1),
            pl.BlockSpec((nsr, ns), lambda n, qb: (0, 0)),
        ],
        out_specs=pl.BlockSpec((None, QBLK, NSA_HEADS * NSA_DH), lambda n, qb: (n, qb, 0)),
        out_shape=jax.ShapeDtypeStruct((N, T, NSA_HEADS * NSA_DH), F32),
        scratch_shapes=[
            pltpu.VMEM((T, KVW), BF16), pltpu.VMEM((T, KVW), BF16),
            pltpu.VMEM((T // LANE, KVW, LANE), BF16), pltpu.VMEM((T // LANE, KVW, LANE), BF16),
            pltpu.VMEM((ns, KVW), BF16), pltpu.VMEM((KVW, ns), BF16),
            pltpu.VMEM((NSA_KV, nsr, QBLK), F32),
        ],
        compiler_params=_cparams("parallel", "arbitrary"),
        name="nsa_prompt",
    )(z3, zg3, gate_b, kc, vc, z3, z3, z3, z3, amat_t)


def _nsa_sample_a_body(q_ref, kc_ref, vc_ref, kw_ref, vw_ref, amat_ref, ocmp_ref, owin_ref, sel_ref,
                       *, Tq, past_len, n_sel, wb):
    qs = _stack_heads(q_ref[...], NSA_DH ** -0.5)
    pos = past_len + lax.broadcasted_iota(jnp.int32, (Tq, 1), 0)
    pos4 = jnp.concatenate([pos] * NSA_G, axis=0)

    ns = kc_ref.shape[0]
    cmp_i = lax.broadcasted_iota(jnp.int32, (NSA_G * Tq, ns), 1)
    m_c = ((cmp_i + 2) * CMP_STRIDE - 1 <= pos4) & (cmp_i < ns - 1)
    p_c = _masked_softmax_rows(_dot_nt(qs, kc_ref[...].astype(BF16)), m_c)
    ocmp_ref[...] = _dot(p_c.astype(BF16), vc_ref[...].astype(BF16))
    p_sum = p_c[0:Tq] + p_c[Tq:2 * Tq] + p_c[2 * Tq:3 * Tq] + p_c[3 * Tq:4 * Tq]
    imp = _cmp_to_sel(p_sum, amat_ref[...])
    sel_ref[...] = _topk_mask(_sel_scores(imp, pos // SEL_BLOCK, n_sel), n_sel, N_SEL)

    nw = kw_ref.shape[0]
    tok_w = past_len - wb + lax.broadcasted_iota(jnp.int32, (NSA_G * Tq, nw), 1)
    m_w = (tok_w >= 0) & (tok_w <= pos4) & (tok_w > pos4 - WINDOW)
    p_w = _masked_softmax_rows(_dot_nt(qs, kw_ref[...].astype(BF16)), m_w)
    owin_ref[...] = _dot(p_w.astype(BF16), vw_ref[...].astype(BF16))


def _nsa_sample_a(z3, kc, vc, kw_full, past_len, wb):
    N, Tq = z3.shape[:2]
    ns = kc.shape[1]
    nw = kw_full.shape[1]
    n_sel = -(-(past_len + Tq) // SEL_BLOCK)
    nsb = -(-n_sel // LANE) * LANE
    amat = _sel_map(ns, n_sel, nsb)
    qcol = 2 * D_HALF // (NSA_G * NSA_DH)
    R4 = NSA_G * Tq
    return pl.pallas_call(
        functools.partial(_nsa_sample_a_body, Tq=Tq, past_len=past_len, n_sel=n_sel, wb=wb),
        grid=(N, NSA_KV),
        in_specs=[
            pl.BlockSpec((None, Tq, NSA_G * NSA_DH), lambda n, kv: (n, 0, qcol + kv)),
            pl.BlockSpec((None, ns, NSA_DH), lambda n, kv: (n, 0, kv)),
            pl.BlockSpec((None, ns, NSA_DH), lambda n, kv: (n, 0, kv)),
            pl.BlockSpec((None, nw, NSA_DH), lambda n, kv: (n, 0, kv)),
            pl.BlockSpec((None, nw, NSA_DH), lambda n, kv: (n, 0, NSA_KV + kv)),
            pl.BlockSpec((ns, nsb), lambda n, kv: (0, 0)),
        ],
        out_specs=[
            pl.BlockSpec((None, None, R4, NSA_DH), lambda n, kv: (n, kv, 0, 0)),
            pl.BlockSpec((None, None, R4, NSA_DH), lambda n, kv: (n, kv, 0, 0)),
            pl.BlockSpec((None, None, Tq, nsb), lambda n, kv: (n, kv, 0, 0)),
        ],
        out_shape=[
            jax.ShapeDtypeStruct((N, NSA_KV, R4, NSA_DH), F32),
            jax.ShapeDtypeStruct((N, NSA_KV, R4, NSA_DH), F32),
            jax.ShapeDtypeStruct((N, NSA_KV, Tq, nsb), F32),
        ],
        compiler_params=_cparams("parallel", "parallel"),
        name="nsa_sample_cmp_win",
    )(z3, kc, vc, kw_full, kw_full, amat)


BLK_PER_STEP = PAGES_PER_STEP * PAGE // SEL_BLOCK


def _nsa_sample_b_body(pt_ref, *refs, Tq, past_len):
    page_refs = refs[:PAGES_PER_STEP]
    (q_ref, new_ref, sel_ref, exp_ref, ocmp_ref, owin_ref, zg_ref, gb_ref, o_ref, m_s, l_s, acc_s) = refs[PAGES_PER_STEP:]
    s_id = pl.program_id(1)
    R4 = NSA_G * Tq
    rep = lambda x: jnp.concatenate([x] * NSA_G, axis=0)
    pos = past_len + lax.broadcasted_iota(jnp.int32, (Tq, 1), 0)

    @pl.when(s_id == 0)
    def _():
        m_s[...] = jnp.full(m_s.shape, NEG, F32)
        l_s[...] = jnp.zeros(l_s.shape, F32)
        acc_s[...] = jnp.zeros(acc_s.shape, F32)

    def update(kv, s, mask, pv):
        s = jnp.where(mask, s, NEG)
        m_prev = m_s[kv]
        m_new = jnp.maximum(m_prev, jnp.max(s, axis=-1, keepdims=True))
        alpha = jnp.exp(m_prev - m_new)
        p = jnp.where(mask, jnp.exp(s - m_new), 0.0)
        l_s[kv] = alpha * l_s[kv] + jnp.sum(p, axis=-1, keepdims=True)
        acc_s[kv] = alpha * acc_s[kv] + pv(p.astype(BF16))
        m_s[kv] = m_new

    qs = [_stack_heads(q_ref[:, kv * NSA_G * NSA_DH:(kv + 1) * NSA_G * NSA_DH], NSA_DH ** -0.5) for kv in range(NSA_KV)]
    comp = lambda pr, c: pr[pl.ds(c, PAGE, stride=ROWS_PER_TOKEN), :].astype(BF16)
    for kv in range(NSA_KV):
        s = jnp.concatenate([_dot_nt(qs[kv], comp(pr, kv)) for pr in page_refs], axis=1)
        mask = rep(_dot(sel_ref[kv].astype(BF16), exp_ref[...])) > 0.5

        def pv(p, kv=kv):
            out = _dot(p[:, 0:PAGE], comp(page_refs[0], NSA_KV + kv))
            for i in range(1, PAGES_PER_STEP):
                out = out + _dot(p[:, i * PAGE:(i + 1) * PAGE], comp(page_refs[i], NSA_KV + kv))
            return out

        update(kv, s, mask, pv)

    @pl.when(s_id == pl.num_programs(1) - 1)
    def _():
        new = new_ref[...].astype(BF16)
        tok_n = past_len + lax.broadcasted_iota(jnp.int32, (R4, Tq), 1)
        gates = jax.nn.sigmoid(zg_ref[...] + gb_ref[...])
        for kv in range(NSA_KV):
            k_new = new[:, kv * NSA_DH:(kv + 1) * NSA_DH]
            v_new = new[:, (NSA_KV + kv) * NSA_DH:(NSA_KV + kv + 1) * NSA_DH]
            update(kv, _dot_nt(qs[kv], k_new), tok_n <= rep(pos), lambda p, v_new=v_new: _dot(p, v_new))
            o_sel = acc_s[kv] / l_s[kv]
            o_cmp = ocmp_ref[kv]
            o_win = owin_ref[kv]
            for g in range(NSA_G):
                rs = slice(g * Tq, (g + 1) * Tq)
                head = kv * NSA_G + g
                o_ref[:, head * NSA_DH:(head + 1) * NSA_DH] = (
                    gates[:, head:head + 1] * o_cmp[rs]
                    + gates[:, NSA_HEADS + head:NSA_HEADS + head + 1] * o_sel[rs]
                    + gates[:, 2 * NSA_HEADS + head:2 * NSA_HEADS + head + 1] * o_win[rs])


def _nsa_sample_b(z3, zg3, gate_b, pool_sel, page_table, sel, o_cmp, o_win, past_len):
    N, Tq = z3.shape[:2]
    n_pages = page_table.shape[1]
    assert n_pages % PAGES_PER_STEP == 0 and past_len % SEL_BLOCK == 0
    n_steps = n_pages // PAGES_PER_STEP
    sel_steps = sel[..., :past_len // SEL_BLOCK].reshape(N, NSA_KV, Tq, n_steps, BLK_PER_STEP).transpose(0, 3, 1, 2, 4)
    tok = np.arange(PAGES_PER_STEP * PAGE)[None, :]
    expand = jnp.asarray((tok // SEL_BLOCK == np.arange(BLK_PER_STEP)[:, None]).astype(np.float32), BF16)
    W = 4 * NSA_DH
    R4 = NSA_G * Tq
    qcol = 2 * D_HALF // (NSA_HEADS * NSA_DH)
    kvs_col = (3 * D_HALF + W) // W
    page_specs = [
        pl.BlockSpec((PAGE_ROWS, NSA_DH), functools.partial(
            lambda n, s, pt, r: (pt[n, s * PAGES_PER_STEP + r], 0), r=r))
        for r in range(PAGES_PER_STEP)
    ]
    per_n4 =lambda shape: pl.BlockSpec((None,) + shape, lambda n, s, pt: (n, 0, 0, 0))
    grid_spec = pltpu.PrefetchScalarGridSpec(
        num_scalar_prefetch=1,
        grid=(N, n_pages // PAGES_PER_STEP),
        in_specs=page_specs + [
            pl.BlockSpec((None, Tq, NSA_HEADS * NSA_DH), lambda n, s, pt: (n, 0, qcol)),
            pl.BlockSpec((None, Tq, W), lambda n, s, pt: (n, 0, kvs_col)),
            pl.BlockSpec((None, None, NSA_KV, Tq, BLK_PER_STEP), lambda n, s, pt: (n, s, 0, 0, 0)),
            pl.BlockSpec((BLK_PER_STEP, PAGES_PER_STEP * PAGE), lambda n, s, pt: (0, 0)),
            per_n4((NSA_KV, R4, NSA_DH)),
            per_n4((NSA_KV, R4, NSA_DH)),
            pl.BlockSpec((None, Tq, LANE), lambda n, s, pt: (n, 0, 0)),
            pl.BlockSpec((1, LANE), lambda n, s, pt: (0, 0)),
        ],
        out_specs=pl.BlockSpec((None, Tq, NSA_HEADS * NSA_DH), lambda n, s, pt: (n, 0, 0)),
        scratch_shapes=[
            pltpu.VMEM((NSA_KV, R4, 1), F32),
            pltpu.VMEM((NSA_KV, R4, 1), F32),
            pltpu.VMEM((NSA_KV, R4, NSA_DH), F32),
        ],
    )
    return pl.pallas_call(
        functools.partial(_nsa_sample_b_body, Tq=Tq, past_len=past_len),
        grid_spec=grid_spec,
        out_shape=jax.ShapeDtypeStruct((N, Tq, NSA_HEADS * NSA_DH), F32),
        compiler_params=_cparams("parallel", "arbitrary"),
        name="nsa_sample_sel",
    )(page_table, *([pool_sel] * PAGES_PER_STEP), z3, z3, sel_steps, expand, o_cmp, o_win, zg3, gate_b)


MLSTM_CHUNK_PROMPT = 256
SAMPLE_PAD = 128


def _pad_rows(x, rows):
    return jnp.pad(x, ((0, 0), (0, rows - x.shape[1]), (0, 0)))


def _lane_pad(v):
    return jnp.pad(v.astype(F32), (0, LANE - v.shape[0])).reshape(1, LANE)


def _gate_cols(w_in, n_main):
    return jnp.pad(w_in[:, n_main:], ((0, 0), (0, LANE - (w_in.shape[1] - n_main))))


def _even_mixers(z3, zg3, t0, pool_buf, C0, n0, m0, w):
    N, T = z3.shape[:2]
    st = jnp.pad(pool_buf, ((0, 0), (HALO - POOL_PAD, 0), (0, 0)))
    y_a = _pool_mix(z3, st, w["pool_w"], w["pool_scale"], t0)
    u_ext_tail = jnp.concatenate([pool_buf, z3[:, :, :D_HALF]], axis=1)[:, -POOL_PAD:] if T < POOL_PAD else z3[:, -POOL_PAD:, :D_HALF]

    if T % MLSTM_CHUNK_PROMPT == 0:
        L, zm, zgm = MLSTM_CHUNK_PROMPT, z3, zg3
    else:
        L, zm, zgm = SAMPLE_PAD, _pad_rows(z3, SAMPLE_PAD), _pad_rows(zg3, SAMPLE_PAD)
    y_b, C, n, m = _mlstm(zm, zgm, w["gate_b"], w["mnorm_g"], C0, n0, m0, L, min(T, L))
    return y_a, y_b[:, :T], u_ext_tail, C, n, m


def _odd_mixers(z3, zg3, w, past=None):
    N, T, n_main = z3.shape
    W = 4 * NSA_DH
    kvc = z3[:, :, 3 * D_HALF:3 * D_HALF + W]
    kvs = z3[:, :, 3 * D_HALF + W:3 * D_HALF + 2 * W]
    kvw = z3[:, :, 3 * D_HALF + 2 * W:3 * D_HALF + 3 * W]

    if past is None:
        y_c, _ = _gmlp(z3, w["gnorm_g"], w["ws"], w["bs_t"])
        vn = None
        pt = jnp.arange(N * (T // PAGE), dtype=jnp.int32).reshape(N, T // PAGE)
        a, b = _compress_ab(z3.reshape(N * (T // PAGE), PAGE, n_main), pt, w["wab"], 3 * D_HALF // W)
        kc, vc = _cmp_proj(a, b, w["cmp_w"])
        o = _nsa_prompt(z3, zg3, w["nsa_gate_b"], kc, vc)
        win_state = kvw[:, -min(WINDOW, T):]
    else:
        win_buf, pool_cmp, pool_sel, page_table = past
        past_len = page_table.shape[1] * PAGE
        wb = win_buf.shape[1]
        y_c, vn = _gmlp(_pad_rows(z3[:, :, :2 * D_HALF], SAMPLE_PAD), w["gnorm_g"], w["ws"], w["bs_t"])
        y_c, vn = y_c[:, :T], vn[:, :T]
        assert (past_len + T) // CMP_STRIDE == past_len // CMP_STRIDE
        a, b = _compress_ab_rows(pool_cmp, page_table, w["wab_rows"])
        kc, vc = _cmp_proj(a, b, w["cmp_w"])
        kw_all = jnp.concatenate([win_buf.reshape(N, wb, W), kvw], axis=1)
        nw = -(-(wb + T) // LANE) * LANE
        o_cmp, o_win, sel = _nsa_sample_a(z3, kc, vc, _pad_rows(kw_all, nw), past_len, wb)
        o = _nsa_sample_b(z3, zg3, w["nsa_gate_b"], pool_sel, page_table, sel, o_cmp, o_win, past_len)
        win_state = kw_all[:, -wb:]

    kv5 = lambda t: t.reshape(N, t.shape[1], 2, NSA_KV, NSA_DH)
    return y_c, o, vn, kv5(kvc), kv5(kvs), kv5(win_state)


def _layer(xp, xs, w, ffn_stacks, layer, n_main, mixers_p, mixers_s):
    B, T, D = xp.shape
    Ns, Ts, _ = xs.shape
    xp2, xs2 = xp.reshape(B * T, D), xs.reshape(Ns * Ts, D)
    zp, zgp, zs, zgs = _norm_matmul(xp2, xs2, w["ng"][0:1], w["w_in"], w["w_in_layer"], n_main, w["w_gate"])
    a1p, a2p, *extra_p = mixers_p(zp.reshape(B, T, n_main), zgp.reshape(B, T, LANE))
    a1s, a2s, *extra_s = mixers_s(zs.reshape(Ns, Ts, n_main), zgs.reshape(Ns, Ts, LANE))
    flat = lambda t: t.reshape(-1, D_HALF)
    xp2, xpn, xs2, xsn = _out_proj(flat(a1p), flat(a2p), xp2, flat(a1s), flat(a2s), xs2, w["w_out_a"], w["w_out_b"],
                                   w["ng"][1:2], w["ng"][2:3])
    w1_stack, w3_stack, w2_stack = ffn_stacks
    hp, hs = _ffn_up(xpn, xsn, w1_stack, w3_stack, layer)
    xp2, xs2 = _ffn_down(hp, hs, xp2, xs2, w["ng"][3:4], w2_stack, layer)
    return xp2.reshape(B, T, D), xs2.reshape(Ns, Ts, D), extra_p, extra_s


def kernel(x_prompt, x_sample, state_pool, state_mlstm_c, state_mlstm_n, state_mlstm_m, state_win_kv, cache_cmp_kv, cache_sel_kv, page_table, norm_g, w_in_even, w_out_even, pool_w, pool_scale, mlstm_gate_b, mlstm_norm_g, w_in_odd, w_out_odd, gmlp_norm_g, gmlp_ws, gmlp_bs, nsa_cmp_pos, nsa_cmp_w, nsa_gate_b, ffn_w1, ffn_w3, ffn_w2):
    B = x_prompt.shape[0]
    depth = norm_g.shape[0]
    past_len = page_table.shape[1] * PAGE
    xp, xs = x_prompt, x_sample
    pool_p, pool_s, c_p, c_s, n_p, n_s, m_p, m_s = [], [], [], [], [], [], [], []
    gv_s, cmp_p, cmp_s, sel_p, sel_s, win_p, win_s = [], [], [], [], [], [], []
    ffn_stacks = (ffn_w1, ffn_w3, ffn_w2.astype(BF16))
    for l in range(depth):
        j = l // 2
        if l % 2 == 0:
            n_main = 4 * D_HALF
            w_out = w_out_even[j].astype(BF16)
            w = dict(ng=norm_g[l], w_in=w_in_even, w_in_layer=j, w_gate=_gate_cols(w_in_even[j], n_main),
                     w_out_a=w_out[:D_HALF], w_out_b=w_out[D_HALF:],
                     pool_w=pool_w[j].astype(BF16), pool_scale=pool_scale[j].reshape(1, D_HALF),
                     gate_b=_lane_pad(mlstm_gate_b[j].reshape(-1)), mnorm_g=mlstm_norm_g[j].reshape(1, D_HALF))
            zp = jnp.zeros((B, POOL_PAD, D_HALF), F32)
            zc = jnp.zeros((B, MLSTM_HEADS, MLSTM_DK, MLSTM_DV), F32)
            zn = jnp.zeros((B, MLSTM_HEADS, MLSTM_DK), F32)
            zm = jnp.zeros((B, MLSTM_HEADS), F32)
            xp, xs, (pb, c, n, m), (pbs, cs, ns_, ms) = _layer(
                xp, xs, w, ffn_stacks, l, n_main,
                lambda z3, zg3: _even_mixers(z3, zg3, 0, zp, zc, zn, zm, w),
                lambda z3, zg3: _even_mixers(z3, zg3, past_len, state_pool[j], state_mlstm_c[j], state_mlstm_n[j],
                                             state_mlstm_m[j], w))
            pool_p.append(pb); c_p.append(c); n_p.append(n); m_p.append(m)
            pool_s.append(pbs); c_s.append(cs); n_s.append(ns_); m_s.append(ms)
        else:
            n_main = 4 * D_HALF + D_HALF // 2
            w_out = w_out_odd[j].astype(BF16)
            cp = nsa_cmp_pos[j]
            wcol = jnp.repeat(cp, NSA_KV * NSA_DH, axis=1)
            wab = jnp.stack([wcol[:CMP_STRIDE], wcol[CMP_STRIDE:]])
            wab_rows = jnp.broadcast_to(jnp.repeat(cp, NSA_KV, axis=1).reshape(2, BLK_ROWS, 1), (2, BLK_ROWS, NSA_DH))
            w = dict(ng=norm_g[l], w_in=w_in_odd, w_in_layer=j, w_gate=_gate_cols(w_in_odd[j], n_main),
                     w_out_a=w_out[:D_HALF], w_out_b=w_out[D_HALF:],
                     gnorm_g=gmlp_norm_g[j].reshape(1, D_HALF), ws=gmlp_ws[j],
                     bs_t=jnp.pad(gmlp_bs[j].T, ((0, 0), (0, LANE - GMLP_GROUPS))),
                     wab=wab, wab_rows=wab_rows, cmp_w=nsa_cmp_w[j], nsa_gate_b=_lane_pad(nsa_gate_b[j]))
            n_pool = cache_cmp_kv.shape[1]
            flat = lambda c: c.reshape(c.shape[0] * n_pool * PAGE_ROWS, NSA_DH)
            past = (state_win_kv[j], flat(cache_cmp_kv), flat(cache_sel_kv), page_table + j * n_pool)
            xp, xs, (_, kc, ksl, wv), (vn, kcs, ksls, wvs) = _layer(
                xp, xs, w, ffn_stacks, l, n_main,
                lambda z3, zg3: _odd_mixers(z3, zg3, w),
                lambda z3, zg3: _odd_mixers(z3, zg3, w, past=past))
            cmp_p.append(kc); sel_p.append(ksl); win_p.append(wv)
            gv_s.append(vn); cmp_s.append(kcs); sel_s.append(ksls); win_s.append(wvs)
    st = jnp.stack
    return (xp, xs, st(pool_p), st(pool_s), st(c_p), st(c_s), st(n_p), st(n_s), st(m_p), st(m_s),
            st(gv_s), st(cmp_p), st(cmp_s), st(sel_p), st(sel_s), st(win_p), st(win_s))
```

```python
import functools
import math

import numpy as np
import jax
import jax.numpy as jnp
from jax import lax
from jax.experimental import pallas as pl
from jax.experimental.pallas import tpu as pltpu

F32 = jnp.float32
BF16 = jnp.bfloat16
NEG = -1e30

D_HALF = 1024
POOL_WINDOWS = (2, 4, 8, 16)
POOL_DG = 256
POOL_PAD = 15
HALO = 16
MLSTM_HEADS = 4
MLSTM_DK = 128
MLSTM_DV = 256
GATE_CAP = 15.0
GMLP_CHUNK = 128
GMLP_GROUPS = 4
GMLP_DG = 256
NSA_HEADS = 8
NSA_DH = 128
NSA_KV = 2
NSA_G = 4
CMP_STRIDE = 16
SEL_BLOCK = 64
N_SEL = 16
WINDOW = 512
QBLK = 128
PAGE = 128
FORCE_SCORE = 1e9
LANE = 128
VMEM_LIMIT = 56 * 1024 * 1024

NT_DIMS = (((1,), (1,)), ((), ()))
TN_DIMS = (((0,), (0,)), ((), ()))


def _cparams(*sem):
    return pltpu.CompilerParams(dimension_semantics=sem, vmem_limit_bytes=VMEM_LIMIT)


def _tile(m, pref):
    if m <= pref:
        return m
    for t in range(pref, 7, -1):
        if m % t == 0 and t % 8 == 0:
            return t
    return m


def _dot(a, b):
    return jnp.dot(a, b, preferred_element_type=F32)


def _dot_nt(a, b):
    return lax.dot_general(a, b, NT_DIMS, preferred_element_type=F32)


def _rms(x, g, eps=1e-6):
    return x * lax.rsqrt(jnp.mean(x * x, axis=-1, keepdims=True) + eps) * g


def _rms_cast_body(x_ref, xs_ref, g_ref, o_ref, os_ref):
    o_ref[...] = _rms(x_ref[...], g_ref[...]).astype(BF16)

    @pl.when(pl.program_id(0) == 0)
    def _():
        os_ref[...] = _rms(xs_ref[...], g_ref[...]).astype(BF16)


def _rms_cast(x, xs, g, tm_pref=1024):
    M, K = x.shape
    Ms = xs.shape[0]
    tm = _tile(M, tm_pref)
    return pl.pallas_call(
        _rms_cast_body,
        grid=(M // tm,),
        in_specs=[pl.BlockSpec((tm, K), lambda i: (i, 0)), pl.BlockSpec((Ms, K), lambda i: (0, 0)),
                  pl.BlockSpec((1, K), lambda i: (0, 0))],
        out_specs=[pl.BlockSpec((tm, K), lambda i: (i, 0)), pl.BlockSpec((Ms, K), lambda i: (0, 0))],
        out_shape=[jax.ShapeDtypeStruct((M, K), BF16), jax.ShapeDtypeStruct((Ms, K), BF16)],
        compiler_params=_cparams("arbitrary"),
        name="rms_cast",
    )(x, xs, g)


def _in_proj_body(xn_ref, xsn_ref, w_ref, wg_ref, o_ref, og_ref, os_ref, ogs_ref, w_bf):
    j = pl.program_id(0)
    i = pl.program_id(1)

    @pl.when(i == 0)
    def _():
        w_bf[...] = w_ref[...].astype(BF16)
        os_ref[...] = _dot_nt(xsn_ref[...], w_bf[...])

    o_ref[...] = _dot_nt(xn_ref[...], w_bf[...])

    @pl.when(j == 0)
    def _():
        wg = wg_ref[...].astype(BF16)
        og_ref[...] = _dot_nt(xn_ref[...], wg)

        @pl.when(i == 0)
        def _():
            ogs_ref[...] = _dot_nt(xsn_ref[...], wg)


def _in_proj(xn, xsn, wt_stack, layer, n_main, wg_t, tm_pref=1024, tn_pref=1152):
    M, K = xn.shape
    Ms = xsn.shape[0]
    tm = _tile(M, tm_pref)
    tn = max(t for t in range(LANE, tn_pref + 1, LANE) if n_main % t == 0)
    ni = M // tm
    return pl.pallas_call(
        _in_proj_body,
        grid=(n_main // tn, ni),
        in_specs=[
            pl.BlockSpec((tm, K), lambda j, i: (i, 0)),
            pl.BlockSpec((Ms, K), lambda j, i: (0, 0)),
            pl.BlockSpec((None, tn, K), lambda j, i: (layer, j, 0)),
            pl.BlockSpec((LANE, K), lambda j, i: (0, 0)),
        ],
        out_specs=[
            pl.BlockSpec((tm, tn), lambda j, i: (i, j)),
            pl.BlockSpec((tm, LANE), lambda j, i: (jnp.where(j == 0, i, ni - 1), 0)),
            pl.BlockSpec((Ms, tn), lambda j, i: (0, j)),
            pl.BlockSpec((Ms, LANE), lambda j, i: (0, 0)),
        ],
        out_shape=[jax.ShapeDtypeStruct((M, n_main), F32), jax.ShapeDtypeStruct((M, LANE), F32),
                   jax.ShapeDtypeStruct((Ms, n_main), F32), jax.ShapeDtypeStruct((Ms, LANE), F32)],
        scratch_shapes=[pltpu.VMEM((tn, K), BF16)],
        compiler_params=_cparams("arbitrary", "arbitrary"),
        name="in_proj",
    )(xn, xsn, wt_stack, wg_t)


def _out_proj_body(a1_ref, a2_ref, res_ref, a1s_ref, a2s_ref, ress_ref, w1_ref, w2_ref, g_ref, gn_ref,
                   o_ref, on_ref, os_ref, osn_ref):
    def proj(a1, a2, res, o_r, on_r):
        y = _dot(a1[...].astype(BF16), w1_ref[...]) + _dot(a2[...].astype(BF16), w2_ref[...])
        x = res[...] + _rms(y, g_ref[...])
        o_r[...] = x
        on_r[...] = _rms(x, gn_ref[...]).astype(BF16)

    proj(a1_ref, a2_ref, res_ref, o_ref, on_ref)

    @pl.when(pl.program_id(0) == 0)
    def _():
        proj(a1s_ref, a2s_ref, ress_ref, os_ref, osn_ref)


def _out_proj(a1, a2, res, a1s, a2s, ress, w1, w2, g, g_next, tm_pref=512):
    M, K1 = a1.shape
    Ms = a1s.shape[0]
    K2 = a2.shape[1]
    D = w1.shape[1]
    tm = _tile(M, tm_pref)
    whole = lambda r, c: pl.BlockSpec((r, c), lambda i: (0, 0))
    rows = lambda c: pl.BlockSpec((tm, c), lambda i: (i, 0))
    return pl.pallas_call(
        _out_proj_body,
        grid=(M // tm,),
        in_specs=[
            rows(K1), rows(K2), rows(D),
            whole(Ms, K1), whole(Ms, K2), whole(Ms, D),
            whole(K1, D), whole(K2, D), whole(1, D), whole(1, D),
        ],
        out_specs=[rows(D), rows(D), whole(Ms, D), whole(Ms, D)],
        out_shape=[jax.ShapeDtypeStruct((M, D), F32), jax.ShapeDtypeStruct((M, D), BF16),
                   jax.ShapeDtypeStruct((Ms, D), F32), jax.ShapeDtypeStruct((Ms, D), BF16)],
        compiler_params=_cparams("arbitrary"),
        name="out_proj",
    )(a1, a2, res, a1s, a2s, ress, w1, w2, g, g_next)


def _ffn_up_body(xn_ref, xsn_ref, w1_ref, w3_ref, h_ref, hs_ref, w1_bf, w3_bf):
    def swiglu(xn):
        h1 = _dot(xn, w1_bf[...])
        return (h1 * jax.nn.sigmoid(h1) * _dot(xn, w3_bf[...])).astype(BF16)

    @pl.when(pl.program_id(1) == 0)
    def _():
        w1_bf[...] = w1_ref[...].astype(BF16)
        w3_bf[...] = w3_ref[...].astype(BF16)
        hs_ref[...] = swiglu(xsn_ref[...])

    h_ref[...] = swiglu(xn_ref[...])


def _ffn_up(xn, xsn, w1_stack, w3_stack, layer, tm_pref=1024, th_pref=512):
    M, D = xn.shape
    Ms = xsn.shape[0]
    H = w1_stack.shape[2]
    tm = _tile(M, tm_pref)
    th = _tile(H, th_pref)
    return pl.pallas_call(
        _ffn_up_body,
        grid=(H // th, M // tm),
        in_specs=[
            pl.BlockSpec((tm, D), lambda j, i: (i, 0)),
            pl.BlockSpec((Ms, D), lambda j, i: (0, 0)),
            pl.BlockSpec((None, D, th), lambda j, i: (layer, 0, j)),
            pl.BlockSpec((None, D, th), lambda j, i: (layer, 0, j)),
        ],
        out_specs=[
            pl.BlockSpec((tm, th), lambda j, i: (i, j)),
            pl.BlockSpec((Ms, th), lambda j, i: (0, j)),
        ],
        out_shape=[jax.ShapeDtypeStruct((M, H), BF16), jax.ShapeDtypeStruct((Ms, H), BF16)],
        scratch_shapes=[pltpu.VMEM((D, th), BF16), pltpu.VMEM((D, th), BF16)],
        compiler_params=_cparams("arbitrary", "arbitrary"),
        name="ffn_up",
    )(xn, xsn, w1_stack, w3_stack)


def _ffn_down_body(h_ref, hs_ref, x_ref, xs_ref, g3_ref, gn_ref, w2_ref, o_ref, os_ref, on_ref, osn_ref,
                   acc_ref, accs_ref):
    i = pl.program_id(0)
    j = pl.program_id(1)
    last = pl.num_programs(1) - 1
    w2 = w2_ref[...]

    def step(h_r, x_r, o_r, on_r, acc_r):
        @pl.when(j == 0)
        def _():
            acc_r[...] = jnp.zeros_like(acc_r)

        acc_r[...] += _dot(h_r[...], w2)

        @pl.when(j == last)
        def _():
            x = x_r[...] + _rms(acc_r[...], g3_ref[...])
            o_r[...] = x
            on_r[...] = _rms(x, gn_ref[...]).astype(BF16)

    step(h_ref, x_ref, o_ref, on_ref, acc_ref)

    @pl.when(i == 0)
    def _():
        step(hs_ref, xs_ref, os_ref, osn_ref, accs_ref)


def _ffn_down(h, hs, x, xs, g3, g_next, w2_stack, layer, tm_pref=1024, tk_pref=704):
    M, D = x.shape
    Ms = xs.shape[0]
    H = h.shape[1]
    tm = _tile(M, tm_pref)
    tk = max(t for t in range(LANE, tk_pref + 1, LANE) if H % t == 0)
    return pl.pallas_call(
        _ffn_down_body,
        grid=(M // tm, H // tk),
        in_specs=[
            pl.BlockSpec((tm, tk), lambda i, j: (i, j)),
            pl.BlockSpec((Ms, tk), lambda i, j: (0, j)),
            pl.BlockSpec((tm, D), lambda i, j: (i, 0), pipeline_mode=pl.Buffered(1)),
            pl.BlockSpec((Ms, D), lambda i, j: (0, 0)),
            pl.BlockSpec((1, D), lambda i, j: (0, 0)),
            pl.BlockSpec((1, D), lambda i, j: (0, 0)),
            pl.BlockSpec((None, tk, D), lambda i, j: (layer, j, 0)),
        ],
        out_specs=[
            pl.BlockSpec((tm, D), lambda i, j: (i, 0)),
            pl.BlockSpec((Ms, D), lambda i, j: (0, 0)),
            pl.BlockSpec((tm, D), lambda i, j: (i, 0)),
            pl.BlockSpec((Ms, D), lambda i, j: (0, 0)),
        ],
        out_shape=[jax.ShapeDtypeStruct((M, D), F32), jax.ShapeDtypeStruct((Ms, D), F32),
                   jax.ShapeDtypeStruct((M, D), BF16), jax.ShapeDtypeStruct((Ms, D), BF16)],
        scratch_shapes=[pltpu.VMEM((tm, D), F32), pltpu.VMEM((Ms, D), F32)],
        compiler_params=_cparams("arbitrary", "arbitrary"),
        name="ffn_down",
    )(h, hs, x, xs, g3, g_next, w2_stack)


def _pool_body(u_ref, prev_ref, st_ref, pw_ref, ps_ref, o_ref, ext_ref, *, tT, t0):
    t = pl.program_id(1)
    ext_ref[0:HALO, :] = jnp.where(t == 0, st_ref[...], prev_ref[...])
    ext_ref[HALO:HALO + tT, :] = u_ref[...]
    pos = t0 + t * tT + lax.broadcasted_iota(jnp.int32, (tT, 1), 0)
    for g, w in enumerate(POOL_WINDOWS):
        cs = slice(g * POOL_DG, (g + 1) * POOL_DG)
        x_new = ext_ref[HALO:HALO + tT, cs]
        tot = x_new
        for i in range(1, w):
            tot = tot + ext_ref[HALO - i:HALO - i + tT, cs]
        cnt = jnp.minimum(w, pos + 1).astype(F32)
        y = tot / cnt - x_new
        o_ref[:, cs] = _dot(y.astype(BF16), pw_ref[g]) * ps_ref[:, cs]


def _pool_mix(z3, st, pool_w, pool_scale, t0, tT_pref=512):
    N, T = z3.shape[:2]
    tT = _tile(T, tT_pref)
    nT = T // tT
    if nT > 1:
        assert tT % HALO == 0
        prev, prev_spec = z3, pl.BlockSpec((None, HALO, D_HALF), lambda n, t: (n, jnp.maximum(t * (tT // HALO) - 1, 0), 0))
    else:
        prev, prev_spec = st, pl.BlockSpec((None, HALO, D_HALF), lambda n, t: (n, 0, 0))
    return pl.pallas_call(
        functools.partial(_pool_body, tT=tT, t0=t0),
        grid=(N, nT),
        in_specs=[
            pl.BlockSpec((None, tT, D_HALF), lambda n, t: (n, t, 0)),
            prev_spec,
            pl.BlockSpec((None, HALO, D_HALF), lambda n, t: (n, 0, 0)),
            pl.BlockSpec((len(POOL_WINDOWS), POOL_DG, POOL_DG), lambda n, t: (0, 0, 0)),
            pl.BlockSpec((1, D_HALF), lambda n, t: (0, 0)),
        ],
        out_specs=pl.BlockSpec((None, tT, D_HALF), lambda n, t: (n, t, 0)),
        out_shape=jax.ShapeDtypeStruct((N, T, D_HALF), F32),
        scratch_shapes=[pltpu.VMEM((HALO + tT, D_HALF), F32)],
        compiler_params=_cparams("parallel", "arbitrary"),
        name="pool_mix",
    )(z3, prev, st, pool_w, pool_scale)


def _mlstm_body(q_ref, k_ref, v_ref, o_ref, zg_ref, gb_ref, mg_ref, c0_ref, n0_ref, m0_ref,
                y_ref, cN_ref, nN_ref, mN_ref, C_s, n_s, m_s, *, L, t_valid):
    c = pl.program_id(1)

    @pl.when(c == 0)
    def _():
        C_s[...] = c0_ref[...]
        n_s[...] = n0_ref[...]
        m_s[...] = m0_ref[...]

    a = GATE_CAP * jnp.tanh((zg_ref[...] + gb_ref[...]) / GATE_CAP)
    lane = lax.broadcasted_iota(jnp.int32, (L, LANE), 1)
    logsig = jnp.minimum(a, 0.0) - jnp.log1p(jnp.exp(-jnp.abs(a)))
    A = jnp.where(lane < MLSTM_HEADS, a, logsig)
    if t_valid < L:
        row = lax.broadcasted_iota(jnp.int32, (L, LANE), 0)
        A = jnp.where(row < t_valid, A, jnp.where(lane < MLSTM_HEADS, NEG, 0.0))
    r_i = lax.broadcasted_iota(jnp.int32, (L, L), 0)
    c_i = lax.broadcasted_iota(jnp.int32, (L, L), 1)
    causal = r_i >= c_i
    Bc = jnp.dot(causal.astype(F32), A, preferred_element_type=F32, precision=lax.Precision.HIGHEST)
    At = A.T
    Bt = Bc.T

    for h in range(MLSTM_HEADS):
        ks = slice(h * MLSTM_DK, (h + 1) * MLSTM_DK)
        vs = slice(h * MLSTM_DV, (h + 1) * MLSTM_DV)
        qh = q_ref[:, ks] * (MLSTM_DK ** -0.5)
        kh = k_ref[:, ks]
        vh = v_ref[:, vs].astype(BF16)
        b_c = Bc[:, MLSTM_HEADS + h:MLSTM_HEADS + h + 1]
        b_r = Bt[MLSTM_HEADS + h:MLSTM_HEADS + h + 1, :]
        li_c = A[:, h:h + 1]
        li_r = At[h:h + 1, :]
        m = m_s[h]
        Ch = C_s[h]
        nh = n_s[h]

        d = jnp.where(causal, b_c - b_r + li_r, NEG)
        inter = b_c + m
        m_t = jnp.maximum(inter, jnp.max(d, axis=-1, keepdims=True))
        w_in = jnp.exp(inter - m_t)
        qb = qh.astype(BF16)
        s = _dot_nt(qb, kh.astype(BF16)) * jnp.exp(d - m_t)
        num = w_in * _dot(qb, Ch.astype(BF16)) + _dot(s.astype(BF16), vh)
        den = w_in * jnp.sum(qh * nh, axis=-1, keepdims=True) + jnp.sum(s, axis=-1, keepdims=True)
        hh = num / jnp.maximum(jnp.abs(den), jnp.exp(-m_t))

        b_end = b_c[L - 1:L, :]
        g_c = b_end - b_c + li_c
        m_new = jnp.maximum(b_end + m, jnp.max(g_c, axis=0, keepdims=True))
        w_c = jnp.exp(b_end + m - m_new)
        kw = kh * jnp.exp(g_c - m_new)
        C_s[h] = w_c * Ch + lax.dot_general(kw.astype(BF16), vh, TN_DIMS, preferred_element_type=F32)
        n_s[h] = w_c * nh + jnp.sum(kw, axis=0, keepdims=True)
        m_s[h] = m_new

        hn = _rms(hh, mg_ref[:, vs])
        y_ref[:, vs] = jax.nn.sigmoid(o_ref[:, vs]) * hn

    @pl.when(c == pl.num_programs(1) - 1)
    def _():
        cN_ref[...] = C_s[...]
        nN_ref[...] = n_s[...]
        mN_ref[...] = m_s[...]


def _mlstm(z3, zg3, gate_b, mnorm_g, C0, n0, m0, L, t_valid):
    N, Tp = z3.shape[:2]
    nc = Tp // L
    H = MLSTM_HEADS
    qk_w = H * MLSTM_DK
    v_w = H * MLSTM_DV
    st = lambda n, c: (n, 0, 0, 0)
    outs = pl.pallas_call(
        functools.partial(_mlstm_body, L=L, t_valid=t_valid),
        grid=(N, nc),
        in_specs=[
            pl.BlockSpec((None, L, qk_w), lambda n, c: (n, c, D_HALF // qk_w)),
            pl.BlockSpec((None, L, qk_w), lambda n, c: (n, c, D_HALF // qk_w + 1)),
            pl.BlockSpec((None, L, v_w), lambda n, c: (n, c, 2)),
            pl.BlockSpec((None, L, v_w), lambda n, c: (n, c, 3)),
            pl.BlockSpec((None, L, LANE), lambda n, c: (n, c, 0)),
            pl.BlockSpec((1, LANE), lambda n, c: (0, 0)),
            pl.BlockSpec((1, v_w), lambda n, c: (0, 0)),
            pl.BlockSpec((None, H, MLSTM_DK, MLSTM_DV), st),
            pl.BlockSpec((None, H, 1, MLSTM_DK), st),
            pl.BlockSpec((None, H, 1, 1), st),
        ],
        out_specs=[
            pl.BlockSpec((None, L, v_w), lambda n, c: (n, c, 0)),
            pl.BlockSpec((None, H, MLSTM_DK, MLSTM_DV), st),
            pl.BlockSpec((None, H, 1, MLSTM_DK), st),
            pl.BlockSpec((None, H, 1, 1), st),
        ],
        out_shape=[
            jax.ShapeDtypeStruct((N, Tp, v_w), F32),
            jax.ShapeDtypeStruct((N, H, MLSTM_DK, MLSTM_DV), F32),
            jax.ShapeDtypeStruct((N, H, 1, MLSTM_DK), F32),
            jax.ShapeDtypeStruct((N, H, 1, 1), F32),
        ],
        scratch_shapes=[
            pltpu.VMEM((H, MLSTM_DK, MLSTM_DV), F32),
            pltpu.VMEM((H, 1, MLSTM_DK), F32),
            pltpu.VMEM((H, 1, 1), F32),
        ],
        compiler_params=_cparams("parallel", "arbitrary"),
        name="mlstm",
    )(z3, z3, z3, z3, zg3, gate_b, mnorm_g, C0, n0.reshape(N, H, 1, MLSTM_DK), m0.reshape(N, H, 1, 1))
    y, C, n, m = outs
    return y, C, n.reshape(N, H, MLSTM_DK), m.reshape(N, H)


def _gmlp_body(u_ref, v_ref, g_ref, ws_ref, bs_ref, y_ref, vn_ref, *, tT):
    v = v_ref[...]
    vc = v - jnp.mean(v, axis=-1, keepdims=True)
    vn = vc * lax.rsqrt(jnp.mean(vc * vc, axis=-1, keepdims=True) + 1e-5) * g_ref[...]
    vn_ref[...] = vn
    r_i = lax.broadcasted_iota(jnp.int32, (GMLP_CHUNK, GMLP_CHUNK), 0)
    c_i = lax.broadcasted_iota(jnp.int32, (GMLP_CHUNK, GMLP_CHUNK), 1)
    for g in range(GMLP_GROUPS):
        cs = slice(g * GMLP_DG, (g + 1) * GMLP_DG)
        wm = jnp.where(r_i >= c_i, ws_ref[g], 0.0).astype(BF16)
        bias = bs_ref[:, g:g + 1]
        for ch in range(tT // GMLP_CHUNK):
            rs = slice(ch * GMLP_CHUNK, (ch + 1) * GMLP_CHUNK)
            mix = _dot(wm, vn[rs, cs].astype(BF16)) + bias
            y_ref[rs, cs] = u_ref[rs, cs] * mix


def _gmlp(z3, gnorm_g, ws, bs_t, tT_pref=512):
    N, Tp = z3.shape[:2]
    tT = _tile(Tp, tT_pref)
    assert tT % GMLP_CHUNK == 0
    return pl.pallas_call(
        functools.partial(_gmlp_body, tT=tT),
        grid=(N, Tp // tT),
        in_specs=[
            pl.BlockSpec((None, tT, D_HALF), lambda n, t: (n, t, 0)),
            pl.BlockSpec((None, tT, D_HALF), lambda n, t: (n, t, 1)),
            pl.BlockSpec((1, D_HALF), lambda n, t: (0, 0)),
            pl.BlockSpec((GMLP_GROUPS, GMLP_CHUNK, GMLP_CHUNK), lambda n, t: (0, 0, 0)),
            pl.BlockSpec((GMLP_CHUNK, LANE), lambda n, t: (0, 0)),
        ],
        out_specs=[
            pl.BlockSpec((None, tT, D_HALF), lambda n, t: (n, t, 0)),
            pl.BlockSpec((None, tT, D_HALF), lambda n, t: (n, t, 0)),
        ],
        out_shape=[jax.ShapeDtypeStruct((N, Tp, D_HALF), F32), jax.ShapeDtypeStruct((N, Tp, D_HALF), F32)],
        compiler_params=_cparams("parallel", "parallel"),
        name="gmlp",
    )(z3, z3, gnorm_g, ws, bs_t)


PAGES_PER_STEP = 16
BLK_PER_PAGE = PAGE // CMP_STRIDE


def _compress_body(pt_ref, *refs):
    page_refs = refs[:PAGES_PER_STEP]
    wab_ref, a_ref, b_ref = refs[PAGES_PER_STEP:]
    wa = wab_ref[0]
    wb = wab_ref[1]
    for p, pr in enumerate(page_refs):
        x = pr[...].reshape(BLK_PER_PAGE, CMP_STRIDE, 4 * NSA_DH)
        rs = slice(p * BLK_PER_PAGE, (p + 1) * BLK_PER_PAGE)
        a_ref[rs, :] = jnp.sum(x * wa[None], axis=1)
        b_ref[rs, :] = jnp.sum(x * wb[None], axis=1)


def _compress_ab(pages, page_table, wab, col_block):
    N, n_pages = page_table.shape
    assert n_pages % PAGES_PER_STEP == 0
    W = 4 * NSA_DH
    rows = PAGES_PER_STEP * BLK_PER_PAGE
    page_specs = [
        pl.BlockSpec((None, PAGE, W), functools.partial(
            lambda n, s, pt, r: (pt[n, s * PAGES_PER_STEP + r], 0, col_block), r=r))
        for r in range(PAGES_PER_STEP)
    ]
    grid_spec = pltpu.PrefetchScalarGridSpec(
        num_scalar_prefetch=1,
        grid=(N, n_pages // PAGES_PER_STEP),
        in_specs=page_specs + [pl.BlockSpec((2, CMP_STRIDE, W), lambda n, s, pt: (0, 0, 0))],
        out_specs=[pl.BlockSpec((None, rows, W), lambda n, s, pt: (n, s, 0))] * 2,
    )
    ns = n_pages * BLK_PER_PAGE
    return pl.pallas_call(
        _compress_body,
        grid_spec=grid_spec,
        out_shape=[jax.ShapeDtypeStruct((N, ns, W), F32)] * 2,
        compiler_params=_cparams("parallel", "parallel"),
        name="nsa_compress",
    )(page_table, *([pages] * PAGES_PER_STEP), wab)


ROWS_PER_TOKEN = 2 * NSA_KV
PAGE_ROWS = PAGE * ROWS_PER_TOKEN
BLK_ROWS = CMP_STRIDE * ROWS_PER_TOKEN


def _compress_rows_body(pt_ref, *refs):
    page_refs = refs[:PAGES_PER_STEP]
    wab_ref, a_ref, b_ref, z_s = refs[PAGES_PER_STEP:]
    for p, pr in enumerate(page_refs):
        x = pr[...].reshape(BLK_PER_PAGE, BLK_ROWS, NSA_DH)
        for half, out_ref in enumerate((a_ref, b_ref)):
            y = (x * wab_ref[half][None]).reshape(BLK_PER_PAGE, BLK_ROWS // 8, 8, NSA_DH).sum(axis=1)
            y = y.reshape(BLK_PER_PAGE * 8, NSA_DH)
            slot = 2 * p + half
            z_s[slot] = y + pltpu.roll(y, BLK_PER_PAGE * 8 - ROWS_PER_TOKEN, 0)
            for c in range(ROWS_PER_TOKEN):
                out_ref[p * BLK_PER_PAGE:(p + 1) * BLK_PER_PAGE, c * NSA_DH:(c + 1) * NSA_DH] = (
                    z_s[slot, pl.ds(c, BLK_PER_PAGE, stride=8), :])


def _compress_ab_rows(rows, page_table, wab_rows):
    N, n_pages = page_table.shape
    assert n_pages % PAGES_PER_STEP == 0
    W = ROWS_PER_TOKEN * NSA_DH
    out_rows = PAGES_PER_STEP * BLK_PER_PAGE
    page_specs = [
        pl.BlockSpec((PAGE_ROWS, NSA_DH), functools.partial(
            lambda n, s, pt, r: (pt[n, s * PAGES_PER_STEP + r], 0), r=r))
        for r in range(PAGES_PER_STEP)
    ]
    grid_spec = pltpu.PrefetchScalarGridSpec(
        num_scalar_prefetch=1,
        grid=(N, n_pages // PAGES_PER_STEP),
        in_specs=page_specs + [pl.BlockSpec((2, BLK_ROWS, NSA_DH), lambda n, s, pt: (0, 0, 0))],
        out_specs=[pl.BlockSpec((None, out_rows, W), lambda n, s, pt: (n, s, 0))] * 2,
        scratch_shapes=[pltpu.VMEM((2 * PAGES_PER_STEP, BLK_PER_PAGE * 8, NSA_DH), F32)],
    )
    ns = n_pages * BLK_PER_PAGE
    return pl.pallas_call(
        _compress_rows_body,
        grid_spec=grid_spec,
        out_shape=[jax.ShapeDtypeStruct((N, ns, W), F32)] * 2,
        compiler_params=_cparams("parallel", "arbitrary"),
        name="nsa_compress_rows",
    )(page_table, *([rows] * PAGES_PER_STEP), wab_rows)


def _cmp_proj_body(a_ref, b_ref, w_ref, kc_ref, vc_ref, *, ns):
    b_next = pltpu.roll(b_ref[...], ns - 1, 0)
    row = lax.broadcasted_iota(jnp.int32, (ns, 1), 0)
    blk = jnp.where(row < ns - 1, a_ref[...] + b_next, 0.0).astype(BF16)
    w0 = w_ref[0].astype(BF16)
    w1 = w_ref[1].astype(BF16)
    for kv in range(NSA_KV):
        cs = slice(kv * NSA_DH, (kv + 1) * NSA_DH)
        kc_ref[:, cs] = _dot(blk[:, kv * NSA_DH:(kv + 1) * NSA_DH], w0)
        vc_ref[:, cs] = _dot(blk[:, (NSA_KV + kv) * NSA_DH:(NSA_KV + kv + 1) * NSA_DH], w1)


def _cmp_proj(a, b, w_cmp):
    N, ns, W = a.shape
    return pl.pallas_call(
        functools.partial(_cmp_proj_body, ns=ns),
        grid=(N,),
        in_specs=[
            pl.BlockSpec((None, ns, W), lambda n: (n, 0, 0)),
            pl.BlockSpec((None, ns, W), lambda n: (n, 0, 0)),
            pl.BlockSpec((2, NSA_DH, NSA_DH), lambda n: (0, 0, 0)),
        ],
        out_specs=[pl.BlockSpec((None, ns, NSA_KV * NSA_DH), lambda n: (n, 0, 0))] * 2,
        out_shape=[jax.ShapeDtypeStruct((N, ns, NSA_KV * NSA_DH), F32)] * 2,
        compiler_params=_cparams("parallel"),
        name="nsa_cmp_proj",
    )(a, b, w_cmp)


def _split3(x):
    hi = x.astype(BF16)
    r1 = x - hi.astype(F32)
    mid = r1.astype(BF16)
    lo = (r1 - mid.astype(F32)).astype(BF16)
    return hi, mid, lo


def _cmp_to_sel(p_sum, a_mat):
    hi, mid, lo = _split3(p_sum)
    return _dot(hi, a_mat) + _dot(mid, a_mat) + _dot(lo, a_mat)


def _topk_mask(score, n_valid, k):
    lane = lax.broadcasted_iota(jnp.int32, score.shape, 1)
    rank = jnp.zeros(score.shape, F32)
    for jp in range(n_valid):
        col = score[:, jp:jp + 1]
        beats = (col > score) | ((col == score) & (lane > jp))
        rank = rank + beats.astype(F32)
    return ((rank < k) & (lane < n_valid)).astype(F32)


def _sel_scores(imp, qblk, n_valid):
    lane = lax.broadcasted_iota(jnp.int32, imp.shape, 1)
    forced = (lane == 0) | (lane == qblk) | (lane == qblk - 1)
    score = jnp.where(forced, FORCE_SCORE, jnp.where(lane > qblk, -1.0, imp))
    return jnp.where(lane < n_valid, score, -2.0)


def _masked_softmax_rows(s, mask):
    s = jnp.where(mask, s, NEG)
    e = jnp.exp(s - jnp.max(s, axis=-1, keepdims=True))
    return jnp.where(mask, e / jnp.sum(e, axis=-1, keepdims=True), 0.0)


def _stack_heads(q, scale):
    return (jnp.concatenate([q[:, g * NSA_DH:(g + 1) * NSA_DH] for g in range(NSA_G)], axis=0) * scale).astype(BF16)


SEL_TK = 512
WIN_KEYS = WINDOW + QBLK


def _nsa_prompt_body(q_ref, zg_ref, gb_ref, kc_ref, vc_ref, ks_ref, vs_ref, kw_ref, vw_ref, amat_t_ref,
                     o_ref, ks_bf, kw_bf, vs_t, vw_t, kc_bf, vc_t, sc_ref, *, n_sel, nsr):
    qb = pl.program_id(1)
    T = ks_ref.shape[0]
    kvc = lambda kv: slice(kv * NSA_DH, (kv + 1) * NSA_DH)

    @pl.when(qb == 0)
    def _():
        ks_bf[...] = ks_ref[...].astype(BF16)
        kw_bf[...] = kw_ref[...].astype(BF16)
        kc_bf[...] = kc_ref[...].astype(BF16)
        vc_t[...] = vc_ref[...].T.astype(BF16)

        def transpose_values(i, c):
            r0 = pl.multiple_of(i * LANE, LANE)
            vs_t[i] = vs_ref[pl.ds(r0, LANE), :].T.astype(BF16)
            vw_t[i] = vw_ref[pl.ds(r0, LANE), :].T.astype(BF16)
            return c

        lax.fori_loop(0, T // LANE, transpose_values, 0)

    R = QBLK
    start = qb * R
    groups = range(NSA_KV)
    tile4 = lambda x: jnp.concatenate([x] * NSA_G, axis=1)
    q_t = [jnp.concatenate([(q_ref[:, (kv * NSA_G + g) * NSA_DH:(kv * NSA_G + g + 1) * NSA_DH] * (NSA_DH ** -0.5)).T
                            for g in range(NSA_G)], axis=1).astype(BF16) for kv in groups]
    pos = start + lax.broadcasted_iota(jnp.int32, (1, R), 1)

    ns = kc_bf.shape[0]
    cmp_end = (lax.broadcasted_iota(jnp.int32, (ns, R), 0) + 2) * CMP_STRIDE - 1
    bias_c = tile4(jnp.where(cmp_end <= pos, 0.0, NEG))
    any_c = tile4(jnp.where(pos >= 2 * CMP_STRIDE - 1, 1.0, 0.0))
    a_t = amat_t_ref[...]
    blk = lax.broadcasted_iota(jnp.int32, (nsr, R), 0)
    qblk = (start + lax.broadcasted_iota(jnp.int32, (nsr, R), 1)) // SEL_BLOCK
    forced = (blk == 0) | (blk == qblk) | (blk == qblk - 1)
    o_cmp, score = [], []
    for kv in groups:
        s_c = _dot(kc_bf[:, kvc(kv)], q_t[kv]) + bias_c
        e_c = jnp.exp(s_c - jnp.max(s_c, axis=0, keepdims=True))
        p_c = e_c * (any_c / jnp.sum(e_c, axis=0, keepdims=True))
        o_cmp.append(_dot(vc_t[kvc(kv), :], p_c.astype(BF16)))
        p_sum = p_c[:, 0:R] + p_c[:, R:2 * R] + p_c[:, 2 * R:3 * R] + p_c[:, 3 * R:4 * R]
        hi, mid, lo = _split3(p_sum)
        imp_t = _dot(a_t, hi) + _dot(a_t, mid) + _dot(a_t, lo)
        sc = jnp.where(forced, FORCE_SCORE, jnp.where(blk > qblk, -1.0, imp_t))
        score.append(jnp.where(blk < n_sel, sc, -2.0))
        sc_ref[kv] = score[kv]

    def rank_step(i, ranks):
        ranks = list(ranks)
        for jp in (2 * i, 2 * i + 1):
            for kv in groups:
                row = sc_ref[kv, pl.ds(jp, 1), :]
                beats = (row > score[kv]) | ((row == score[kv]) & (blk > jp))
                ranks[kv] = ranks[kv] + beats.astype(F32)
        return tuple(ranks)

    n_rank = jnp.minimum(start // SEL_BLOCK + QBLK // SEL_BLOCK, n_sel)
    ranks = lax.fori_loop(0, (n_rank + 1) // 2, rank_step, tuple(jnp.zeros((nsr, R), F32) for _ in groups))
    for kv in groups:
        sc_ref[kv] = jnp.where(ranks[kv] < N_SEL, 0.0, NEG)

    def values_product(v_ref, tile0, n_sub, kv, p):
        out = _dot(v_ref[tile0, kvc(kv), :], p[0:LANE])
        for i in range(1, n_sub):
            out = out + _dot(v_ref[tile0 + i, kvc(kv), :], p[i * LANE:(i + 1) * LANE])
        return out

    key_tk = lax.broadcasted_iota(jnp.int32, (SEL_TK, R), 0)

    def sel_tile(kt, carry, causal):
        k0 = pl.multiple_of(kt * SEL_TK, SEL_TK)
        scores = [_dot(ks_bf[pl.ds(k0, SEL_TK), kvc(kv)], q_t[kv]) for kv in groups]
        stats = []
        for kv in groups:
            m_prev, l_prev, _ = carry[kv]
            bias = jnp.concatenate(
                [jnp.broadcast_to(sc_ref[kv, pl.ds(kt * (SEL_TK // SEL_BLOCK) + b, 1), :], (SEL_BLOCK, R))
                 for b in range(SEL_TK // SEL_BLOCK)], axis=0)
            if causal:
                bias = jnp.where(k0 + key_tk <= pos, bias, NEG)
            s = scores[kv] + tile4(bias)
            m_new = jnp.maximum(m_prev, jnp.max(s, axis=0, keepdims=True))
            alpha = jnp.exp(m_prev - m_new)
            p = jnp.exp(s - m_new)
            stats.append((m_new, alpha, alpha * l_prev + jnp.sum(p, axis=0, keepdims=True), p.astype(BF16)))
        return tuple((m_new, l_new, alpha * carry[kv][2]
                      + values_product(vs_t, kt * (SEL_TK // LANE), SEL_TK // LANE, kv, p))
                     for kv, (m_new, alpha, l_new, p) in enumerate(stats))

    n_tiles = (start + R - 1) // SEL_TK + 1
    init = tuple((jnp.full((1, NSA_G * R), NEG, F32), jnp.zeros((1, NSA_G * R), F32),
                  jnp.zeros((NSA_DH, NSA_G * R), F32)) for _ in groups)
    carry = lax.fori_loop(0, n_tiles - 1, lambda kt, c: sel_tile(kt, c, False), init)
    o_sel = [acc / l for _, l, acc in sel_tile(n_tiles - 1, carry, True)]

    w0 = pl.multiple_of(jnp.maximum(start - WINDOW, 0), QBLK)
    key_w = w0 + lax.broadcasted_iota(jnp.int32, (WIN_KEYS, R), 0)
    bias_w = tile4(jnp.where((key_w <= pos) & (key_w > pos - WINDOW), 0.0, NEG))
    gates_t = jax.nn.sigmoid(zg_ref[...] + gb_ref[...]).T
    for kv in groups:
        s_w = _dot(kw_bf[pl.ds(w0, WIN_KEYS), kvc(kv)], q_t[kv]) + bias_w
        e_w = jnp.exp(s_w - jnp.max(s_w, axis=0, keepdims=True))
        o_win = (values_product(vw_t, w0 // LANE, WIN_KEYS // LANE, kv, e_w.astype(BF16))
                 / jnp.sum(e_w, axis=0, keepdims=True))
        gate = lambda branch: jnp.concatenate(
            [gates_t[branch * NSA_HEADS + kv * NSA_G + g:branch * NSA_HEADS + kv * NSA_G + g + 1, :]
             for g in range(NSA_G)], axis=1)
        out_t = gate(0) * o_cmp[kv] + gate(1) * o_sel[kv] + gate(2) * o_win
        for g in range(NSA_G):
            h = kv * NSA_G + g
            o_ref[:, h * NSA_DH:(h + 1) * NSA_DH] = out_t[:, g * R:(g + 1) * R].T


def _sel_map(ns, n_sel, nsb):
    i = np.arange(ns)[:, None]
    j = np.arange(nsb)[None, :]
    r = SEL_BLOCK // CMP_STRIDE
    return jnp.asarray(((i >= r * j - 1) & (i <= r * j + r - 1) & (j < n_sel)).astype(np.float32), BF16)


def _nsa_prompt(z3, zg3, gate_b, kc, vc):
    N, T = z3.shape[:2]
    assert T % SEL_TK == 0 and T >= WIN_KEYS
    ns = kc.shape[1]
    n_sel = T // SEL_BLOCK
    nsr = -(-n_sel // 8) * 8
    assert nsr <= LANE
    amat_t = _sel_map(ns, n_sel, nsr).T
    KVW = NSA_KV * NSA_DH
    qcol = 2 * D_HALF // (NSA_HEADS * NSA_DH)
    kvs_col = (3 * D_HALF + 2 * KVW) // KVW
    kvw_col = kvs_col + 2
    full = lambda off: pl.BlockSpec((None, T, KVW), lambda n, qb: (n, 0, off))
    return pl.pallas_call(
        functools.partial(_nsa_prompt_body, n_sel=n_sel, nsr=nsr),
        grid=(N, T // QBLK),
        in_specs=[
            pl.BlockSpec((None, QBLK, NSA_HEADS * NSA_DH), lambda n, qb: (n, qb, qcol)),
            pl.BlockSpec((None, QBLK, LANE), lambda n, qb: (n, qb, 0)),
            pl.BlockSpec((1, LANE), lambda n, qb: (0, 0)),
            pl.BlockSpec((None, ns, KVW), lambda n, qb: (n, 0, 0)),
            pl.BlockSpec((None, ns, KVW), lambda n, qb: (n, 0, 0)),
            full(kvs_col), full(kvs_col + 1), full(kvw_col), full(kvw_col + 1),
            pl.BlockSpec((nsr, ns), lambda n, qb: (0, 0)),
        ],
        out_specs=pl.BlockSpec((None, QBLK, NSA_HEADS * NSA_DH), lambda n, qb: (n, qb, 0)),
        out_shape=jax.ShapeDtypeStruct((N, T, NSA_HEADS * NSA_DH), F32),
        scratch_shapes=[
            pltpu.VMEM((T, KVW), BF16), pltpu.VMEM((T, KVW), BF16),
            pltpu.VMEM((T // LANE, KVW, LANE), BF16), pltpu.VMEM((T // LANE, KVW, LANE), BF16),
            pltpu.VMEM((ns, KVW), BF16), pltpu.VMEM((KVW, ns), BF16),
            pltpu.VMEM((NSA_KV, nsr, QBLK), F32),
        ],
        compiler_params=_cparams("parallel", "arbitrary"),
        name="nsa_prompt",
    )(z3, zg3, gate_b, kc, vc, z3, z3, z3, z3, amat_t)


def _nsa_sample_a_body(q_ref, kc_ref, vc_ref, kw_ref, vw_ref, amat_ref, ocmp_ref, owin_ref, sel_ref,
                       *, Tq, past_len, n_sel, wb):
    qs = _stack_heads(q_ref[...], NSA_DH ** -0.5)
    pos = past_len + lax.broadcasted_iota(jnp.int32, (Tq, 1), 0)
    pos4 = jnp.concatenate([pos] * NSA_G, axis=0)

    ns = kc_ref.shape[0]
    cmp_i = lax.broadcasted_iota(jnp.int32, (NSA_G * Tq, ns), 1)
    m_c = ((cmp_i + 2) * CMP_STRIDE - 1 <= pos4) & (cmp_i < ns - 1)
    p_c = _masked_softmax_rows(_dot_nt(qs, kc_ref[...].astype(BF16)), m_c)
    ocmp_ref[...] = _dot(p_c.astype(BF16), vc_ref[...].astype(BF16))
    p_sum = p_c[0:Tq] + p_c[Tq:2 * Tq] + p_c[2 * Tq:3 * Tq] + p_c[3 * Tq:4 * Tq]
    imp = _cmp_to_sel(p_sum, amat_ref[...])
    sel_ref[...] = _topk_mask(_sel_scores(imp, pos // SEL_BLOCK, n_sel), n_sel, N_SEL)

    nw = kw_ref.shape[0]
    tok_w = past_len - wb + lax.broadcasted_iota(jnp.int32, (NSA_G * Tq, nw), 1)
    m_w = (tok_w >= 0) & (tok_w <= pos4) & (tok_w > pos4 - WINDOW)
    p_w = _masked_softmax_rows(_dot_nt(qs, kw_ref[...].astype(BF16)), m_w)
    owin_ref[...] = _dot(p_w.astype(BF16), vw_ref[...].astype(BF16))


def _nsa_sample_a(z3, kc, vc, kw_full, past_len, wb):
    N, Tq = z3.shape[:2]
    ns = kc.shape[1]
    nw = kw_full.shape[1]
    n_sel = -(-(past_len + Tq) // SEL_BLOCK)
    nsb = -(-n_sel // LANE) * LANE
    amat = _sel_map(ns, n_sel, nsb)
    qcol = 2 * D_HALF // (NSA_G * NSA_DH)
    R4 = NSA_G * Tq
    return pl.pallas_call(
        functools.partial(_nsa_sample_a_body, Tq=Tq, past_len=past_len, n_sel=n_sel, wb=wb),
        grid=(N, NSA_KV),
        in_specs=[
            pl.BlockSpec((None, Tq, NSA_G * NSA_DH), lambda n, kv: (n, 0, qcol + kv)),
            pl.BlockSpec((None, ns, NSA_DH), lambda n, kv: (n, 0, kv)),
            pl.BlockSpec((None, ns, NSA_DH), lambda n, kv: (n, 0, kv)),
            pl.BlockSpec((None, nw, NSA_DH), lambda n, kv: (n, 0, kv)),
            pl.BlockSpec((None, nw, NSA_DH), lambda n, kv: (n, 0, NSA_KV + kv)),
            pl.BlockSpec((ns, nsb), lambda n, kv: (0, 0)),
        ],
        out_specs=[
            pl.BlockSpec((None, None, R4, NSA_DH), lambda n, kv: (n, kv, 0, 0)),
            pl.BlockSpec((None, None, R4, NSA_DH), lambda n, kv: (n, kv, 0, 0)),
            pl.BlockSpec((None, None, Tq, nsb), lambda n, kv: (n, kv, 0, 0)),
        ],
        out_shape=[
            jax.ShapeDtypeStruct((N, NSA_KV, R4, NSA_DH), F32),
            jax.ShapeDtypeStruct((N, NSA_KV, R4, NSA_DH), F32),
            jax.ShapeDtypeStruct((N, NSA_KV, Tq, nsb), F32),
        ],
        compiler_params=_cparams("parallel", "parallel"),
        name="nsa_sample_cmp_win",
    )(z3, kc, vc, kw_full, kw_full, amat)


BLK_PER_STEP = PAGES_PER_STEP * PAGE // SEL_BLOCK


def _nsa_sample_b_body(pt_ref, *refs, Tq, past_len):
    page_refs = refs[:PAGES_PER_STEP]
    (q_ref, new_ref, sel_ref, exp_ref, ocmp_ref, owin_ref, zg_ref, gb_ref, o_ref, m_s, l_s, acc_s) = refs[PAGES_PER_STEP:]
    s_id = pl.program_id(1)
    R4 = NSA_G * Tq
    rep = lambda x: jnp.concatenate([x] * NSA_G, axis=0)
    pos = past_len + lax.broadcasted_iota(jnp.int32, (Tq, 1), 0)

    @pl.when(s_id == 0)
    def _():
        m_s[...] = jnp.full(m_s.shape, NEG, F32)
        l_s[...] = jnp.zeros(l_s.shape, F32)
        acc_s[...] = jnp.zeros(acc_s.shape, F32)

    def update(kv, s, mask, pv):
        s = jnp.where(mask, s, NEG)
        m_prev = m_s[kv]
        m_new = jnp.maximum(m_prev, jnp.max(s, axis=-1, keepdims=True))
        alpha = jnp.exp(m_prev - m_new)
        p = jnp.where(mask, jnp.exp(s - m_new), 0.0)
        l_s[kv] = alpha * l_s[kv] + jnp.sum(p, axis=-1, keepdims=True)
        acc_s[kv] = alpha * acc_s[kv] + pv(p.astype(BF16))
        m_s[kv] = m_new

    qs = [_stack_heads(q_ref[:, kv * NSA_G * NSA_DH:(kv + 1) * NSA_G * NSA_DH], NSA_DH ** -0.5) for kv in range(NSA_KV)]
    comp = lambda pr, c: pr[pl.ds(c, PAGE, stride=ROWS_PER_TOKEN), :].astype(BF16)
    for kv in range(NSA_KV):
        s = jnp.concatenate([_dot_nt(qs[kv], comp(pr, kv)) for pr in page_refs], axis=1)
        mask = rep(_dot(sel_ref[kv].astype(BF16), exp_ref[...])) > 0.5

        def pv(p, kv=kv):
            out = _dot(p[:, 0:PAGE], comp(page_refs[0], NSA_KV + kv))
            for i in range(1, PAGES_PER_STEP):
                out = out + _dot(p[:, i * PAGE:(i + 1) * PAGE], comp(page_refs[i], NSA_KV + kv))
            return out

        update(kv, s, mask, pv)

    @pl.when(s_id == pl.num_programs(1) - 1)
    def _():
        new = new_ref[...].astype(BF16)
        tok_n = past_len + lax.broadcasted_iota(jnp.int32, (R4, Tq), 1)
        gates = jax.nn.sigmoid(zg_ref[...] + gb_ref[...])
        for kv in range(NSA_KV):
            k_new = new[:, kv * NSA_DH:(kv + 1) * NSA_DH]
            v_new = new[:, (NSA_KV + kv) * NSA_DH:(NSA_KV + kv + 1) * NSA_DH]
            update(kv, _dot_nt(qs[kv], k_new), tok_n <= rep(pos), lambda p, v_new=v_new: _dot(p, v_new))
            o_sel = acc_s[kv] / l_s[kv]
            o_cmp = ocmp_ref[kv]
            o_win = owin_ref[kv]
            for g in range(NSA_G):
                rs = slice(g * Tq, (g + 1) * Tq)
                head = kv * NSA_G + g
                o_ref[:, head * NSA_DH:(head + 1) * NSA_DH] = (
                    gates[:, head:head + 1] * o_cmp[rs]
                    + gates[:, NSA_HEADS + head:NSA_HEADS + head + 1] * o_sel[rs]
                    + gates[:, 2 * NSA_HEADS + head:2 * NSA_HEADS + head + 1] * o_win[rs])


def _nsa_sample_b(z3, zg3, gate_b, pool_sel, page_table, sel, o_cmp, o_win, past_len):
    N, Tq = z3.shape[:2]
    n_pages = page_table.shape[1]
    assert n_pages % PAGES_PER_STEP == 0 and past_len % SEL_BLOCK == 0
    n_steps = n_pages // PAGES_PER_STEP
    sel_steps = sel[..., :past_len // SEL_BLOCK].reshape(N, NSA_KV, Tq, n_steps, BLK_PER_STEP).transpose(0, 3, 1, 2, 4)
    tok = np.arange(PAGES_PER_STEP * PAGE)[None, :]
    expand = jnp.asarray((tok // SEL_BLOCK == np.arange(BLK_PER_STEP)[:, None]).astype(np.float32), BF16)
    W = 4 * NSA_DH
    R4 = NSA_G * Tq
    qcol = 2 * D_HALF // (NSA_HEADS * NSA_DH)
    kvs_col = (3 * D_HALF + W) // W
    page_specs = [
        pl.BlockSpec((PAGE_ROWS, NSA_DH), functools.partial(
            lambda n, s, pt, r: (pt[n, s * PAGES_PER_STEP + r], 0), r=r))
        for r in range(PAGES_PER_STEP)
    ]
    per_n4 =lambda shape: pl.BlockSpec((None,) + shape, lambda n, s, pt: (n, 0, 0, 0))
    grid_spec = pltpu.PrefetchScalarGridSpec(
        num_scalar_prefetch=1,
        grid=(N, n_pages // PAGES_PER_STEP),
        in_specs=page_specs + [
            pl.BlockSpec((None, Tq, NSA_HEADS * NSA_DH), lambda n, s, pt: (n, 0, qcol)),
            pl.BlockSpec((None, Tq, W), lambda n, s, pt: (n, 0, kvs_col)),
            pl.BlockSpec((None, None, NSA_KV, Tq, BLK_PER_STEP), lambda n, s, pt: (n, s, 0, 0, 0)),
            pl.BlockSpec((BLK_PER_STEP, PAGES_PER_STEP * PAGE), lambda n, s, pt: (0, 0)),
            per_n4((NSA_KV, R4, NSA_DH)),
            per_n4((NSA_KV, R4, NSA_DH)),
            pl.BlockSpec((None, Tq, LANE), lambda n, s, pt: (n, 0, 0)),
            pl.BlockSpec((1, LANE), lambda n, s, pt: (0, 0)),
        ],
        out_specs=pl.BlockSpec((None, Tq, NSA_HEADS * NSA_DH), lambda n, s, pt: (n, 0, 0)),
        scratch_shapes=[
            pltpu.VMEM((NSA_KV, R4, 1), F32),
            pltpu.VMEM((NSA_KV, R4, 1), F32),
            pltpu.VMEM((NSA_KV, R4, NSA_DH), F32),
        ],
    )
    return pl.pallas_call(
        functools.partial(_nsa_sample_b_body, Tq=Tq, past_len=past_len),
        grid_spec=grid_spec,
        out_shape=jax.ShapeDtypeStruct((N, Tq, NSA_HEADS * NSA_DH), F32),
        compiler_params=_cparams("parallel", "arbitrary"),
        name="nsa_sample_sel",
    )(page_table, *([pool_sel] * PAGES_PER_STEP), z3, z3, sel_steps, expand, o_cmp, o_win, zg3, gate_b)


MLSTM_CHUNK_PROMPT = 256
SAMPLE_PAD = 128


def _pad_rows(x, rows):
    return jnp.pad(x, ((0, 0), (0, rows - x.shape[1]), (0, 0)))


def _lane_pad(v):
    return jnp.pad(v.astype(F32), (0, LANE - v.shape[0])).reshape(1, LANE)


def _gate_rows(w_in_t, n_main):
    return jnp.pad(w_in_t[n_main:], ((0, LANE - (w_in_t.shape[0] - n_main)), (0, 0)))


def _even_mixers(z3, zg3, t0, pool_buf, C0, n0, m0, w):
    N, T = z3.shape[:2]
    st = jnp.pad(pool_buf, ((0, 0), (HALO - POOL_PAD, 0), (0, 0)))
    y_a = _pool_mix(z3, st, w["pool_w"], w["pool_scale"], t0)
    u_ext_tail = jnp.concatenate([pool_buf, z3[:, :, :D_HALF]], axis=1)[:, -POOL_PAD:] if T < POOL_PAD else z3[:, -POOL_PAD:, :D_HALF]

    if T % MLSTM_CHUNK_PROMPT == 0:
        L, zm, zgm = MLSTM_CHUNK_PROMPT, z3, zg3
    else:
        L, zm, zgm = SAMPLE_PAD, _pad_rows(z3, SAMPLE_PAD), _pad_rows(zg3, SAMPLE_PAD)
    y_b, C, n, m = _mlstm(zm, zgm, w["gate_b"], w["mnorm_g"], C0, n0, m0, L, min(T, L))
    return y_a, y_b[:, :T], u_ext_tail, C, n, m


def _odd_mixers(z3, zg3, w, past=None):
    N, T, n_main = z3.shape
    W = 4 * NSA_DH
    kvc = z3[:, :, 3 * D_HALF:3 * D_HALF + W]
    kvs = z3[:, :, 3 * D_HALF + W:3 * D_HALF + 2 * W]
    kvw = z3[:, :, 3 * D_HALF + 2 * W:3 * D_HALF + 3 * W]

    if past is None:
        y_c, _ = _gmlp(z3, w["gnorm_g"], w["ws"], w["bs_t"])
        vn = None
        pt = jnp.arange(N * (T // PAGE), dtype=jnp.int32).reshape(N, T // PAGE)
        a, b = _compress_ab(z3.reshape(N * (T // PAGE), PAGE, n_main), pt, w["wab"], 3 * D_HALF // W)
        kc, vc = _cmp_proj(a, b, w["cmp_w"])
        o = _nsa_prompt(z3, zg3, w["nsa_gate_b"], kc, vc)
        win_state = kvw[:, -min(WINDOW, T):]
    else:
        win_buf, pool_cmp, pool_sel, page_table = past
        past_len = page_table.shape[1] * PAGE
        wb = win_buf.shape[1]
        y_c, vn = _gmlp(_pad_rows(z3[:, :, :2 * D_HALF], SAMPLE_PAD), w["gnorm_g"], w["ws"], w["bs_t"])
        y_c, vn = y_c[:, :T], vn[:, :T]
        assert (past_len + T) // CMP_STRIDE == past_len // CMP_STRIDE
        a, b = _compress_ab_rows(pool_cmp, page_table, w["wab_rows"])
        kc, vc = _cmp_proj(a, b, w["cmp_w"])
        kw_all = jnp.concatenate([win_buf.reshape(N, wb, W), kvw], axis=1)
        nw = -(-(wb + T) // LANE) * LANE
        o_cmp, o_win, sel = _nsa_sample_a(z3, kc, vc, _pad_rows(kw_all, nw), past_len, wb)
        o = _nsa_sample_b(z3, zg3, w["nsa_gate_b"], pool_sel, page_table, sel, o_cmp, o_win, past_len)
        win_state = kw_all[:, -wb:]

    kv5 = lambda t: t.reshape(N, t.shape[1], 2, NSA_KV, NSA_DH)
    return y_c, o, vn, kv5(kvc), kv5(kvs), kv5(win_state)


def _layer(xp, xs, xpn, xsn, w, ffn_stacks, layer, n_main, mixers_p, mixers_s):
    B, T, D = xp.shape
    Ns, Ts, _ = xs.shape
    xp2, xs2 = xp.reshape(B * T, D), xs.reshape(Ns * Ts, D)
    zp, zgp, zs, zgs = _in_proj(xpn, xsn, w["w_in"], w["w_in_layer"], n_main, w["w_gate"])
    a1p, a2p, *extra_p = mixers_p(zp.reshape(B, T, n_main), zgp.reshape(B, T, LANE))
    a1s, a2s, *extra_s = mixers_s(zs.reshape(Ns, Ts, n_main), zgs.reshape(Ns, Ts, LANE))
    flat = lambda t: t.reshape(-1, D_HALF)
    xp2, xpn, xs2, xsn = _out_proj(flat(a1p), flat(a2p), xp2, flat(a1s), flat(a2s), xs2, w["w_out_a"], w["w_out_b"],
                                   w["ng"][1:2], w["ng"][2:3])
    w1_stack, w3_stack, w2_stack = ffn_stacks
    hp, hs = _ffn_up(xpn, xsn, w1_stack, w3_stack, layer)
    xp2, xs2, xpn, xsn = _ffn_down(hp, hs, xp2, xs2, w["ng"][3:4], w["ng_next"], w2_stack, layer)
    return xp2.reshape(B, T, D), xs2.reshape(Ns, Ts, D), xpn, xsn, extra_p, extra_s


def kernel(x_prompt, x_sample, state_pool, state_mlstm_c, state_mlstm_n, state_mlstm_m, state_win_kv, cache_cmp_kv, cache_sel_kv, page_table, norm_g, w_in_even, w_out_even, pool_w, pool_scale, mlstm_gate_b, mlstm_norm_g, w_in_odd, w_out_odd, gmlp_norm_g, gmlp_ws, gmlp_bs, nsa_cmp_pos, nsa_cmp_w, nsa_gate_b, ffn_w1, ffn_w3, ffn_w2):
    B = x_prompt.shape[0]
    depth = norm_g.shape[0]
    past_len = page_table.shape[1] * PAGE
    xp, xs = x_prompt, x_sample
    pool_p, pool_s, c_p, c_s, n_p, n_s, m_p, m_s = [], [], [], [], [], [], [], []
    gv_s, cmp_p, cmp_s, sel_p, sel_s, win_p, win_s = [], [], [], [], [], [], []
    ffn_stacks = (ffn_w1, ffn_w3, ffn_w2.astype(BF16))
    D = x_prompt.shape[-1]
    xpn, xsn = _rms_cast(x_prompt.reshape(-1, D), x_sample.reshape(-1, D), norm_g[0][0:1])
    for l in range(depth):
        j = l // 2
        ng_next = norm_g[min(l + 1, depth - 1)][0:1]
        if l % 2 == 0:
            n_main = 4 * D_HALF
            w_out = w_out_even[j].astype(BF16)
            w_in_t = jnp.swapaxes(w_in_even, 1, 2)
            w = dict(ng=norm_g[l], ng_next=ng_next, w_in=w_in_t, w_in_layer=j, w_gate=_gate_rows(w_in_t[j], n_main),
                     w_out_a=w_out[:D_HALF], w_out_b=w_out[D_HALF:],
                     pool_w=pool_w[j].astype(BF16), pool_scale=pool_scale[j].reshape(1, D_HALF),
                     gate_b=_lane_pad(mlstm_gate_b[j].reshape(-1)), mnorm_g=mlstm_norm_g[j].reshape(1, D_HALF))
            zp = jnp.zeros((B, POOL_PAD, D_HALF), F32)
            zc = jnp.zeros((B, MLSTM_HEADS, MLSTM_DK, MLSTM_DV), F32)
            zn = jnp.zeros((B, MLSTM_HEADS, MLSTM_DK), F32)
            zm = jnp.zeros((B, MLSTM_HEADS), F32)
            xp, xs, xpn, xsn, (pb, c, n, m), (pbs, cs, ns_, ms) = _layer(
                xp, xs, xpn, xsn, w, ffn_stacks, l, n_main,
                lambda z3, zg3: _even_mixers(z3, zg3, 0, zp, zc, zn, zm, w),
                lambda z3, zg3: _even_mixers(z3, zg3, past_len, state_pool[j], state_mlstm_c[j], state_mlstm_n[j],
                                             state_mlstm_m[j], w))
            pool_p.append(pb); c_p.append(c); n_p.append(n); m_p.append(m)
            pool_s.append(pbs); c_s.append(cs); n_s.append(ns_); m_s.append(ms)
        else:
            n_main = 4 * D_HALF + D_HALF // 2
            w_out = w_out_odd[j].astype(BF16)
            cp = nsa_cmp_pos[j]
            wcol = jnp.repeat(cp, NSA_KV * NSA_DH, axis=1)
            wab = jnp.stack([wcol[:CMP_STRIDE], wcol[CMP_STRIDE:]])
            wab_rows = jnp.broadcast_to(jnp.repeat(cp, NSA_KV, axis=1).reshape(2, BLK_ROWS, 1), (2, BLK_ROWS, NSA_DH))
            w_in_t = jnp.swapaxes(w_in_odd, 1, 2)
            w = dict(ng=norm_g[l], ng_next=ng_next, w_in=w_in_t, w_in_layer=j, w_gate=_gate_rows(w_in_t[j], n_main),
                     w_out_a=w_out[:D_HALF], w_out_b=w_out[D_HALF:],
                     gnorm_g=gmlp_norm_g[j].reshape(1, D_HALF), ws=gmlp_ws[j],
                     bs_t=jnp.pad(gmlp_bs[j].T, ((0, 0), (0, LANE - GMLP_GROUPS))),
                     wab=wab, wab_rows=wab_rows, cmp_w=nsa_cmp_w[j], nsa_gate_b=_lane_pad(nsa_gate_b[j]))
            n_pool = cache_cmp_kv.shape[1]
            flat = lambda c: c.reshape(c.shape[0] * n_pool * PAGE_ROWS, NSA_DH)
            past = (state_win_kv[j], flat(cache_cmp_kv), flat(cache_sel_kv), page_table + j * n_pool)
            xp, xs, xpn, xsn, (_, kc, ksl, wv), (vn, kcs, ksls, wvs) = _layer(
                xp, xs, xpn, xsn, w, ffn_stacks, l, n_main,
                lambda z3, zg3: _odd_mixers(z3, zg3, w),
                lambda z3, zg3: _odd_mixers(z3, zg3, w, past=past))
            cmp_p.append(kc); sel_p.append(ksl); win_p.append(wv)
            gv_s.append(vn); cmp_s.append(kcs); sel_s.append(ksls); win_s.append(wvs)
    st = jnp.stack
    return (xp, xs, st(pool_p), st(pool_s), st(c_p), st(c_s), st(n_p), st(n_s), st(m_p), st(m_s),
            st(gv_s), st(cmp_p), st(cmp_s), st(sel_p), st(sel_s), st(win_p), st(win_s))
```

```python
import functools
import math

import numpy as np
import jax
import jax.numpy as jnp
from jax import lax
from jax.experimental import pallas as pl
from jax.experimental.pallas import tpu as pltpu

F32 = jnp.float32
BF16 = jnp.bfloat16
NEG = -1e30

D_HALF = 1024
POOL_WINDOWS = (2, 4, 8, 16)
POOL_DG = 256
POOL_PAD = 15
HALO = 16
MLSTM_HEADS = 4
MLSTM_DK = 128
MLSTM_DV = 256
GATE_CAP = 15.0
GMLP_CHUNK = 128
GMLP_GROUPS = 4
GMLP_DG = 256
NSA_HEADS = 8
NSA_DH = 128
NSA_KV = 2
NSA_G = 4
CMP_STRIDE = 16
SEL_BLOCK = 64
N_SEL = 16
WINDOW = 512
QBLK = 128
PAGE = 128
FORCE_SCORE = 1e9
LANE = 128
VMEM_LIMIT = 56 * 1024 * 1024

NT_DIMS = (((1,), (1,)), ((), ()))
TN_DIMS = (((0,), (0,)), ((), ()))


def _cparams(*sem):
    return pltpu.CompilerParams(dimension_semantics=sem, vmem_limit_bytes=VMEM_LIMIT)


def _tile(m, pref):
    if m <= pref:
        return m
    for t in range(pref, 7, -1):
        if m % t == 0 and t % 8 == 0:
            return t
    return m


def _dot(a, b):
    return jnp.dot(a, b, preferred_element_type=F32)


def _dot_nt(a, b):
    return lax.dot_general(a, b, NT_DIMS, preferred_element_type=F32)


def _rms(x, g, eps=1e-6):
    return x * lax.rsqrt(jnp.mean(x * x, axis=-1, keepdims=True) + eps) * g


def _rms_cast_body(x_ref, xs_ref, g_ref, o_ref, os_ref):
    o_ref[...] = _rms(x_ref[...], g_ref[...]).astype(BF16)

    @pl.when(pl.program_id(0) == 0)
    def _():
        os_ref[...] = _rms(xs_ref[...], g_ref[...]).astype(BF16)


def _rms_cast(x, xs, g, tm_pref=1024):
    M, K = x.shape
    Ms = xs.shape[0]
    tm = _tile(M, tm_pref)
    return pl.pallas_call(
        _rms_cast_body,
        grid=(M // tm,),
        in_specs=[pl.BlockSpec((tm, K), lambda i: (i, 0)), pl.BlockSpec((Ms, K), lambda i: (0, 0)),
                  pl.BlockSpec((1, K), lambda i: (0, 0))],
        out_specs=[pl.BlockSpec((tm, K), lambda i: (i, 0)), pl.BlockSpec((Ms, K), lambda i: (0, 0))],
        out_shape=[jax.ShapeDtypeStruct((M, K), BF16), jax.ShapeDtypeStruct((Ms, K), BF16)],
        compiler_params=_cparams("arbitrary"),
        name="rms_cast",
    )(x, xs, g)


def _in_proj_body(xn_ref, xsn_ref, w_ref, wg_ref, o_ref, og_ref, os_ref, ogs_ref, w_bf):
    j = pl.program_id(0)
    i = pl.program_id(1)

    @pl.when(i == 0)
    def _():
        w_bf[...] = w_ref[...].astype(BF16)
        os_ref[...] = _dot_nt(xsn_ref[...], w_bf[...])

    o_ref[...] = _dot_nt(xn_ref[...], w_bf[...])

    @pl.when(j == 0)
    def _():
        wg = wg_ref[...].astype(BF16)
        og_ref[...] = _dot_nt(xn_ref[...], wg)

        @pl.when(i == 0)
        def _():
            ogs_ref[...] = _dot_nt(xsn_ref[...], wg)


def _in_proj(xn, xsn, wt_stack, layer, n_main, wg_t, tm_pref=1024, tn_pref=1152):
    M, K = xn.shape
    Ms = xsn.shape[0]
    tm = _tile(M, tm_pref)
    tn = max(t for t in range(LANE, tn_pref + 1, LANE) if n_main % t == 0)
    ni = M // tm
    return pl.pallas_call(
        _in_proj_body,
        grid=(n_main // tn, ni),
        in_specs=[
            pl.BlockSpec((tm, K), lambda j, i: (i, 0)),
            pl.BlockSpec((Ms, K), lambda j, i: (0, 0)),
            pl.BlockSpec((None, tn, K), lambda j, i: (layer, j, 0)),
            pl.BlockSpec((LANE, K), lambda j, i: (0, 0)),
        ],
        out_specs=[
            pl.BlockSpec((tm, tn), lambda j, i: (i, j)),
            pl.BlockSpec((tm, LANE), lambda j, i: (jnp.where(j == 0, i, ni - 1), 0)),
            pl.BlockSpec((Ms, tn), lambda j, i: (0, j)),
            pl.BlockSpec((Ms, LANE), lambda j, i: (0, 0)),
        ],
        out_shape=[jax.ShapeDtypeStruct((M, n_main), F32), jax.ShapeDtypeStruct((M, LANE), F32),
                   jax.ShapeDtypeStruct((Ms, n_main), F32), jax.ShapeDtypeStruct((Ms, LANE), F32)],
        scratch_shapes=[pltpu.VMEM((tn, K), BF16)],
        compiler_params=_cparams("arbitrary", "arbitrary"),
        name="in_proj",
    )(xn, xsn, wt_stack, wg_t)


def _out_proj_body(a1_ref, a2_ref, res_ref, a1s_ref, a2s_ref, ress_ref, w1_ref, w2_ref, g_ref, gn_ref,
                   o_ref, on_ref, os_ref, osn_ref):
    def proj(a1, a2, res, o_r, on_r):
        y = _dot(a1[...].astype(BF16), w1_ref[...]) + _dot(a2[...].astype(BF16), w2_ref[...])
        x = res[...] + _rms(y, g_ref[...])
        o_r[...] = x
        on_r[...] = _rms(x, gn_ref[...]).astype(BF16)

    proj(a1_ref, a2_ref, res_ref, o_ref, on_ref)

    @pl.when(pl.program_id(0) == 0)
    def _():
        proj(a1s_ref, a2s_ref, ress_ref, os_ref, osn_ref)


def _out_proj(a1, a2, res, a1s, a2s, ress, w1, w2, g, g_next, tm_pref=512):
    M, K1 = a1.shape
    Ms = a1s.shape[0]
    K2 = a2.shape[1]
    D = w1.shape[1]
    tm = _tile(M, tm_pref)
    whole = lambda r, c: pl.BlockSpec((r, c), lambda i: (0, 0))
    rows = lambda c: pl.BlockSpec((tm, c), lambda i: (i, 0))
    return pl.pallas_call(
        _out_proj_body,
        grid=(M // tm,),
        in_specs=[
            rows(K1), rows(K2), rows(D),
            whole(Ms, K1), whole(Ms, K2), whole(Ms, D),
            whole(K1, D), whole(K2, D), whole(1, D), whole(1, D),
        ],
        out_specs=[rows(D), rows(D), whole(Ms, D), whole(Ms, D)],
        out_shape=[jax.ShapeDtypeStruct((M, D), F32), jax.ShapeDtypeStruct((M, D), BF16),
                   jax.ShapeDtypeStruct((Ms, D), F32), jax.ShapeDtypeStruct((Ms, D), BF16)],
        compiler_params=_cparams("arbitrary"),
        name="out_proj",
    )(a1, a2, res, a1s, a2s, ress, w1, w2, g, g_next)


def _ffn_up_body(xn_ref, xsn_ref, w1_ref, w3_ref, w2_ref, h_ref, hs_ref, w2_out, w1_bf, w3_bf):
    def swiglu(xn):
        h1 = _dot(xn, w1_bf[...])
        return (h1 * jax.nn.sigmoid(h1) * _dot(xn, w3_bf[...])).astype(BF16)

    @pl.when(pl.program_id(1) == 0)
    def _():
        w1_bf[...] = w1_ref[...].astype(BF16)
        w3_bf[...] = w3_ref[...].astype(BF16)
        w2_out[...] = w2_ref[...].astype(BF16)
        hs_ref[...] = swiglu(xsn_ref[...])

    h_ref[...] = swiglu(xn_ref[...])


def _ffn_up(xn, xsn, w1_stack, w3_stack, w2_stack, layer, tm_pref=1024, th_pref=512):
    M, D = xn.shape
    Ms = xsn.shape[0]
    H = w1_stack.shape[2]
    tm = _tile(M, tm_pref)
    th = _tile(H, th_pref)
    return pl.pallas_call(
        _ffn_up_body,
        grid=(H // th, M // tm),
        in_specs=[
            pl.BlockSpec((tm, D), lambda j, i: (i, 0)),
            pl.BlockSpec((Ms, D), lambda j, i: (0, 0)),
            pl.BlockSpec((None, D, th), lambda j, i: (layer, 0, j)),
            pl.BlockSpec((None, D, th), lambda j, i: (layer, 0, j)),
            pl.BlockSpec((None, th, D), lambda j, i: (layer, j, 0)),
        ],
        out_specs=[
            pl.BlockSpec((tm, th), lambda j, i: (i, j)),
            pl.BlockSpec((Ms, th), lambda j, i: (0, j)),
            pl.BlockSpec((th, D), lambda j, i: (j, 0)),
        ],
        out_shape=[jax.ShapeDtypeStruct((M, H), BF16), jax.ShapeDtypeStruct((Ms, H), BF16),
                   jax.ShapeDtypeStruct((H, D), BF16)],
        scratch_shapes=[pltpu.VMEM((D, th), BF16), pltpu.VMEM((D, th), BF16)],
        compiler_params=_cparams("arbitrary", "arbitrary"),
        name="ffn_up",
    )(xn, xsn, w1_stack, w3_stack, w2_stack)


def _ffn_down_body(h_ref, hs_ref, x_ref, xs_ref, g3_ref, gn_ref, w2_ref, o_ref, os_ref, on_ref, osn_ref,
                   acc_ref, accs_ref):
    i = pl.program_id(0)
    j = pl.program_id(1)
    last = pl.num_programs(1) - 1
    w2 = w2_ref[...]

    def step(h_r, x_r, o_r, on_r, acc_r):
        @pl.when(j == 0)
        def _():
            acc_r[...] = jnp.zeros_like(acc_r)

        acc_r[...] += _dot(h_r[...], w2)

        @pl.when(j == last)
        def _():
            x = x_r[...] + _rms(acc_r[...], g3_ref[...])
            o_r[...] = x
            on_r[...] = _rms(x, gn_ref[...]).astype(BF16)

    step(h_ref, x_ref, o_ref, on_ref, acc_ref)

    @pl.when(i == 0)
    def _():
        step(hs_ref, xs_ref, os_ref, osn_ref, accs_ref)


def _ffn_down(h, hs, x, xs, g3, g_next, w2, tm_pref=1024, tk_pref=512):
    M, D = x.shape
    Ms = xs.shape[0]
    H = h.shape[1]
    tm = _tile(M, tm_pref)
    tk = max(t for t in range(LANE, tk_pref + 1, LANE) if H % t == 0)
    return pl.pallas_call(
        _ffn_down_body,
        grid=(M // tm, H // tk),
        in_specs=[
            pl.BlockSpec((tm, tk), lambda i, j: (i, j)),
            pl.BlockSpec((Ms, tk), lambda i, j: (0, j)),
            pl.BlockSpec((tm, D), lambda i, j: (i, 0), pipeline_mode=pl.Buffered(1)),
            pl.BlockSpec((Ms, D), lambda i, j: (0, 0)),
            pl.BlockSpec((1, D), lambda i, j: (0, 0)),
            pl.BlockSpec((1, D), lambda i, j: (0, 0)),
            pl.BlockSpec((tk, D), lambda i, j: (j, 0)),
        ],
        out_specs=[
            pl.BlockSpec((tm, D), lambda i, j: (i, 0)),
            pl.BlockSpec((Ms, D), lambda i, j: (0, 0)),
            pl.BlockSpec((tm, D), lambda i, j: (i, 0)),
            pl.BlockSpec((Ms, D), lambda i, j: (0, 0)),
        ],
        out_shape=[jax.ShapeDtypeStruct((M, D), F32), jax.ShapeDtypeStruct((Ms, D), F32),
                   jax.ShapeDtypeStruct((M, D), BF16), jax.ShapeDtypeStruct((Ms, D), BF16)],
        scratch_shapes=[pltpu.VMEM((tm, D), F32), pltpu.VMEM((Ms, D), F32)],
        compiler_params=_cparams("arbitrary", "arbitrary"),
        name="ffn_down",
    )(h, hs, x, xs, g3, g_next, w2)


def _pool_body(u_ref, prev_ref, st_ref, pw_ref, ps_ref, o_ref, ext_ref, *, tT, t0):
    t = pl.program_id(1)
    ext_ref[0:HALO, :] = jnp.where(t == 0, st_ref[...], prev_ref[...])
    ext_ref[HALO:HALO + tT, :] = u_ref[...]
    pos = t0 + t * tT + lax.broadcasted_iota(jnp.int32, (tT, 1), 0)
    for g, w in enumerate(POOL_WINDOWS):
        cs = slice(g * POOL_DG, (g + 1) * POOL_DG)
        x_new = ext_ref[HALO:HALO + tT, cs]
        tot = x_new
        for i in range(1, w):
            tot = tot + ext_ref[HALO - i:HALO - i + tT, cs]
        cnt = jnp.minimum(w, pos + 1).astype(F32)
        y = tot / cnt - x_new
        o_ref[:, cs] = (_dot(y.astype(BF16), pw_ref[g]) * ps_ref[:, cs]).astype(o_ref.dtype)


def _pool_mix(z3, st, pool_w, pool_scale, t0, tT_pref=512):
    N, T = z3.shape[:2]
    tT = _tile(T, tT_pref)
    nT = T // tT
    if nT > 1:
        assert tT % HALO == 0
        prev, prev_spec = z3, pl.BlockSpec((None, HALO, D_HALF), lambda n, t: (n, jnp.maximum(t * (tT // HALO) - 1, 0), 0))
    else:
        prev, prev_spec = st, pl.BlockSpec((None, HALO, D_HALF), lambda n, t: (n, 0, 0))
    return pl.pallas_call(
        functools.partial(_pool_body, tT=tT, t0=t0),
        grid=(N, nT),
        in_specs=[
            pl.BlockSpec((None, tT, D_HALF), lambda n, t: (n, t, 0)),
            prev_spec,
            pl.BlockSpec((None, HALO, D_HALF), lambda n, t: (n, 0, 0)),
            pl.BlockSpec((len(POOL_WINDOWS), POOL_DG, POOL_DG), lambda n, t: (0, 0, 0)),
            pl.BlockSpec((1, D_HALF), lambda n, t: (0, 0)),
        ],
        out_specs=pl.BlockSpec((None, tT, D_HALF), lambda n, t: (n, t, 0)),
        out_shape=jax.ShapeDtypeStruct((N, T, D_HALF), BF16 if tT % 16 == 0 else F32),
        scratch_shapes=[pltpu.VMEM((HALO + tT, D_HALF), F32)],
        compiler_params=_cparams("parallel", "arbitrary"),
        name="pool_mix",
    )(z3, prev, st, pool_w, pool_scale)


def _mlstm_body(q_ref, k_ref, v_ref, o_ref, zg_ref, gb_ref, mg_ref, c0_ref, n0_ref, m0_ref,
                y_ref, cN_ref, nN_ref, mN_ref, C_s, n_s, m_s, *, L, t_valid):
    c = pl.program_id(1)

    @pl.when(c == 0)
    def _():
        C_s[...] = c0_ref[...]
        n_s[...] = n0_ref[...]
        m_s[...] = m0_ref[...]

    a = GATE_CAP * jnp.tanh((zg_ref[...] + gb_ref[...]) / GATE_CAP)
    lane = lax.broadcasted_iota(jnp.int32, (L, LANE), 1)
    logsig = jnp.minimum(a, 0.0) - jnp.log1p(jnp.exp(-jnp.abs(a)))
    A = jnp.where(lane < MLSTM_HEADS, a, logsig)
    if t_valid < L:
        row = lax.broadcasted_iota(jnp.int32, (L, LANE), 0)
        A = jnp.where(row < t_valid, A, jnp.where(lane < MLSTM_HEADS, NEG, 0.0))
    r_i = lax.broadcasted_iota(jnp.int32, (L, L), 0)
    c_i = lax.broadcasted_iota(jnp.int32, (L, L), 1)
    causal = r_i >= c_i
    Bc = jnp.dot(causal.astype(F32), A, preferred_element_type=F32, precision=lax.Precision.HIGHEST)
    At = A.T
    Bt = Bc.T

    heads = range(MLSTM_HEADS)
    ks = lambda h: slice(h * MLSTM_DK, (h + 1) * MLSTM_DK)
    vs = lambda h: slice(h * MLSTM_DV, (h + 1) * MLSTM_DV)
    m0 = [m_s[h] for h in heads]
    C0 = [C_s[h] for h in heads]
    n0 = [n_s[h] for h in heads]
    qf = [q_ref[:, ks(h)] * (MLSTM_DK ** -0.5) for h in heads]
    qb = [q.astype(BF16) for q in qf]
    kf = [k_ref[:, ks(h)] for h in heads]
    vb = [v_ref[:, vs(h)].astype(BF16) for h in heads]
    qk = [_dot_nt(qb[h], kf[h].astype(BF16)) for h in heads]
    qc = [_dot(qb[h], C0[h].astype(BF16)) for h in heads]

    b_c = [Bc[:, MLSTM_HEADS + h:MLSTM_HEADS + h + 1] for h in heads]
    m_t, w_in, s = [], [], []
    for h in heads:
        b_r = Bt[MLSTM_HEADS + h:MLSTM_HEADS + h + 1, :]
        d = jnp.where(causal, b_c[h] - b_r + At[h:h + 1, :], NEG)
        inter = b_c[h] + m0[h]
        m_t.append(jnp.maximum(inter, jnp.max(d, axis=-1, keepdims=True)))
        w_in.append(jnp.exp(inter - m_t[h]))
        s.append(qk[h] * jnp.exp(d - m_t[h]))
    sv = [_dot(s[h].astype(BF16), vb[h]) for h in heads]

    kw, w_c, m_new = [], [], []
    for h in heads:
        b_end = b_c[h][L - 1:L, :]
        g_c = b_end - b_c[h] + A[:, h:h + 1]
        m_new.append(jnp.maximum(b_end + m0[h], jnp.max(g_c, axis=0, keepdims=True)))
        w_c.append(jnp.exp(b_end + m0[h] - m_new[h]))
        kw.append(kf[h] * jnp.exp(g_c - m_new[h]))
    kv = [lax.dot_general(kw[h].astype(BF16), vb[h], TN_DIMS, preferred_element_type=F32) for h in heads]

    for h in heads:
        num = w_in[h] * qc[h] + sv[h]
        den = w_in[h] * jnp.sum(qf[h] * n0[h], axis=-1, keepdims=True) + jnp.sum(s[h], axis=-1, keepdims=True)
        hh = num / jnp.maximum(jnp.abs(den), jnp.exp(-m_t[h]))
        y_ref[:, vs(h)] = (jax.nn.sigmoid(o_ref[:, vs(h)]) * _rms(hh, mg_ref[:, vs(h)])).astype(BF16)
    for h in heads:
        C_s[h] = w_c[h] * C0[h] + kv[h]
        n_s[h] = w_c[h] * n0[h] + jnp.sum(kw[h], axis=0, keepdims=True)
        m_s[h] = m_new[h]

    @pl.when(c == pl.num_programs(1) - 1)
    def _():
        cN_ref[...] = C_s[...]
        nN_ref[...] = n_s[...]
        mN_ref[...] = m_s[...]


def _mlstm(z3, zg3, gate_b, mnorm_g, C0, n0, m0, L, t_valid):
    N, Tp = z3.shape[:2]
    nc = Tp // L
    H = MLSTM_HEADS
    qk_w = H * MLSTM_DK
    v_w = H * MLSTM_DV
    st = lambda n, c: (n, 0, 0, 0)
    outs = pl.pallas_call(
        functools.partial(_mlstm_body, L=L, t_valid=t_valid),
        grid=(N, nc),
        in_specs=[
            pl.BlockSpec((None, L, qk_w), lambda n, c: (n, c, D_HALF // qk_w)),
            pl.BlockSpec((None, L, qk_w), lambda n, c: (n, c, D_HALF // qk_w + 1)),
            pl.BlockSpec((None, L, v_w), lambda n, c: (n, c, 2)),
            pl.BlockSpec((None, L, v_w), lambda n, c: (n, c, 3)),
            pl.BlockSpec((None, L, LANE), lambda n, c: (n, c, 0)),
            pl.BlockSpec((1, LANE), lambda n, c: (0, 0)),
            pl.BlockSpec((1, v_w), lambda n, c: (0, 0)),
            pl.BlockSpec((None, H, MLSTM_DK, MLSTM_DV), st),
            pl.BlockSpec((None, H, 1, MLSTM_DK), st),
            pl.BlockSpec((None, H, 1, 1), st),
        ],
        out_specs=[
            pl.BlockSpec((None, L, v_w), lambda n, c: (n, c, 0)),
            pl.BlockSpec((None, H, MLSTM_DK, MLSTM_DV), st),
            pl.BlockSpec((None, H, 1, MLSTM_DK), st),
            pl.BlockSpec((None, H, 1, 1), st),
        ],
        out_shape=[
            jax.ShapeDtypeStruct((N, Tp, v_w), BF16),
            jax.ShapeDtypeStruct((N, H, MLSTM_DK, MLSTM_DV), F32),
            jax.ShapeDtypeStruct((N, H, 1, MLSTM_DK), F32),
            jax.ShapeDtypeStruct((N, H, 1, 1), F32),
        ],
        scratch_shapes=[
            pltpu.VMEM((H, MLSTM_DK, MLSTM_DV), F32),
            pltpu.VMEM((H, 1, MLSTM_DK), F32),
            pltpu.VMEM((H, 1, 1), F32),
        ],
        compiler_params=_cparams("parallel", "arbitrary"),
        name="mlstm",
    )(z3, z3, z3, z3, zg3, gate_b, mnorm_g, C0, n0.reshape(N, H, 1, MLSTM_DK), m0.reshape(N, H, 1, 1))
    y, C, n, m = outs
    return y, C, n.reshape(N, H, MLSTM_DK), m.reshape(N, H)


def _gmlp_body(u_ref, v_ref, g_ref, ws_ref, bs_ref, y_ref, vn_ref, *, tT):
    v = v_ref[...]
    vc = v - jnp.mean(v, axis=-1, keepdims=True)
    vn = vc * lax.rsqrt(jnp.mean(vc * vc, axis=-1, keepdims=True) + 1e-5) * g_ref[...]
    vn_ref[...] = vn
    r_i = lax.broadcasted_iota(jnp.int32, (GMLP_CHUNK, GMLP_CHUNK), 0)
    c_i = lax.broadcasted_iota(jnp.int32, (GMLP_CHUNK, GMLP_CHUNK), 1)
    for g in range(GMLP_GROUPS):
        cs = slice(g * GMLP_DG, (g + 1) * GMLP_DG)
        wm = jnp.where(r_i >= c_i, ws_ref[g], 0.0).astype(BF16)
        bias = bs_ref[:, g:g + 1]
        for ch in range(tT // GMLP_CHUNK):
            rs = slice(ch * GMLP_CHUNK, (ch + 1) * GMLP_CHUNK)
            mix = _dot(wm, vn[rs, cs].astype(BF16)) + bias
            y_ref[rs, cs] = (u_ref[rs, cs] * mix).astype(BF16)


def _gmlp(z3, gnorm_g, ws, bs_t, tT_pref=512):
    N, Tp = z3.shape[:2]
    tT = _tile(Tp, tT_pref)
    assert tT % GMLP_CHUNK == 0
    return pl.pallas_call(
        functools.partial(_gmlp_body, tT=tT),
        grid=(N, Tp // tT),
        in_specs=[
            pl.BlockSpec((None, tT, D_HALF), lambda n, t: (n, t, 0)),
            pl.BlockSpec((None, tT, D_HALF), lambda n, t: (n, t, 1)),
            pl.BlockSpec((1, D_HALF), lambda n, t: (0, 0)),
            pl.BlockSpec((GMLP_GROUPS, GMLP_CHUNK, GMLP_CHUNK), lambda n, t: (0, 0, 0)),
            pl.BlockSpec((GMLP_CHUNK, LANE), lambda n, t: (0, 0)),
        ],
        out_specs=[
            pl.BlockSpec((None, tT, D_HALF), lambda n, t: (n, t, 0)),
            pl.BlockSpec((None, tT, D_HALF), lambda n, t: (n, t, 0)),
        ],
        out_shape=[jax.ShapeDtypeStruct((N, Tp, D_HALF), BF16), jax.ShapeDtypeStruct((N, Tp, D_HALF), F32)],
        compiler_params=_cparams("parallel", "parallel"),
        name="gmlp",
    )(z3, z3, gnorm_g, ws, bs_t)


PAGES_PER_STEP = 16
BLK_PER_PAGE = PAGE // CMP_STRIDE


def _compress_body(pt_ref, *refs):
    page_refs = refs[:PAGES_PER_STEP]
    wab_ref, a_ref, b_ref = refs[PAGES_PER_STEP:]
    wa = wab_ref[0]
    wb = wab_ref[1]
    for p, pr in enumerate(page_refs):
        x = pr[...].reshape(BLK_PER_PAGE, CMP_STRIDE, 4 * NSA_DH)
        rs = slice(p * BLK_PER_PAGE, (p + 1) * BLK_PER_PAGE)
        a_ref[rs, :] = jnp.sum(x * wa[None], axis=1)
        b_ref[rs, :] = jnp.sum(x * wb[None], axis=1)


def _compress_ab(pages, page_table, wab, col_block):
    N, n_pages = page_table.shape
    assert n_pages % PAGES_PER_STEP == 0
    W = 4 * NSA_DH
    rows = PAGES_PER_STEP * BLK_PER_PAGE
    page_specs = [
        pl.BlockSpec((None, PAGE, W), functools.partial(
            lambda n, s, pt, r: (pt[n, s * PAGES_PER_STEP + r], 0, col_block), r=r))
        for r in range(PAGES_PER_STEP)
    ]
    grid_spec = pltpu.PrefetchScalarGridSpec(
        num_scalar_prefetch=1,
        grid=(N, n_pages // PAGES_PER_STEP),
        in_specs=page_specs + [pl.BlockSpec((2, CMP_STRIDE, W), lambda n, s, pt: (0, 0, 0))],
        out_specs=[pl.BlockSpec((None, rows, W), lambda n, s, pt: (n, s, 0))] * 2,
    )
    ns = n_pages * BLK_PER_PAGE
    return pl.pallas_call(
        _compress_body,
        grid_spec=grid_spec,
        out_shape=[jax.ShapeDtypeStruct((N, ns, W), F32)] * 2,
        compiler_params=_cparams("parallel", "parallel"),
        name="nsa_compress",
    )(page_table, *([pages] * PAGES_PER_STEP), wab)


ROWS_PER_TOKEN = 2 * NSA_KV
PAGE_ROWS = PAGE * ROWS_PER_TOKEN
BLK_ROWS = CMP_STRIDE * ROWS_PER_TOKEN


def _compress_rows_body(pt_ref, *refs):
    page_refs = refs[:PAGES_PER_STEP]
    wab_ref, a_ref, b_ref, z_s = refs[PAGES_PER_STEP:]
    for p, pr in enumerate(page_refs):
        x = pr[...].reshape(BLK_PER_PAGE, BLK_ROWS, NSA_DH)
        for half, out_ref in enumerate((a_ref, b_ref)):
            y = (x * wab_ref[half][None]).reshape(BLK_PER_PAGE, BLK_ROWS // 8, 8, NSA_DH).sum(axis=1)
            y = y.reshape(BLK_PER_PAGE * 8, NSA_DH)
            slot = 2 * p + half
            z_s[slot] = y + pltpu.roll(y, BLK_PER_PAGE * 8 - ROWS_PER_TOKEN, 0)
            for c in range(ROWS_PER_TOKEN):
                out_ref[p * BLK_PER_PAGE:(p + 1) * BLK_PER_PAGE, c * NSA_DH:(c + 1) * NSA_DH] = (
                    z_s[slot, pl.ds(c, BLK_PER_PAGE, stride=8), :])


def _compress_ab_rows(rows, page_table, wab_rows):
    N, n_pages = page_table.shape
    assert n_pages % PAGES_PER_STEP == 0
    W = ROWS_PER_TOKEN * NSA_DH
    out_rows = PAGES_PER_STEP * BLK_PER_PAGE
    page_specs = [
        pl.BlockSpec((PAGE_ROWS, NSA_DH), functools.partial(
            lambda n, s, pt, r: (pt[n, s * PAGES_PER_STEP + r], 0), r=r))
        for r in range(PAGES_PER_STEP)
    ]
    grid_spec = pltpu.PrefetchScalarGridSpec(
        num_scalar_prefetch=1,
        grid=(N, n_pages // PAGES_PER_STEP),
        in_specs=page_specs + [pl.BlockSpec((2, BLK_ROWS, NSA_DH), lambda n, s, pt: (0, 0, 0))],
        out_specs=[pl.BlockSpec((None, out_rows, W), lambda n, s, pt: (n, s, 0))] * 2,
        scratch_shapes=[pltpu.VMEM((2 * PAGES_PER_STEP, BLK_PER_PAGE * 8, NSA_DH), F32)],
    )
    ns = n_pages * BLK_PER_PAGE
    return pl.pallas_call(
        _compress_rows_body,
        grid_spec=grid_spec,
        out_shape=[jax.ShapeDtypeStruct((N, ns, W), F32)] * 2,
        compiler_params=_cparams("parallel", "arbitrary"),
        name="nsa_compress_rows",
    )(page_table, *([rows] * PAGES_PER_STEP), wab_rows)


def _cmp_proj_body(a_ref, b_ref, w_ref, kc_ref, vc_ref, *, ns):
    b_next = pltpu.roll(b_ref[...], ns - 1, 0)
    row = lax.broadcasted_iota(jnp.int32, (ns, 1), 0)
    blk = jnp.where(row < ns - 1, a_ref[...] + b_next, 0.0).astype(BF16)
    w0 = w_ref[0].astype(BF16)
    w1 = w_ref[1].astype(BF16)
    for kv in range(NSA_KV):
        cs = slice(kv * NSA_DH, (kv + 1) * NSA_DH)
        kc_ref[:, cs] = _dot(blk[:, kv * NSA_DH:(kv + 1) * NSA_DH], w0)
        vc_ref[:, cs] = _dot(blk[:, (NSA_KV + kv) * NSA_DH:(NSA_KV + kv + 1) * NSA_DH], w1)


def _cmp_proj(a, b, w_cmp):
    N, ns, W = a.shape
    return pl.pallas_call(
        functools.partial(_cmp_proj_body, ns=ns),
        grid=(N,),
        in_specs=[
            pl.BlockSpec((None, ns, W), lambda n: (n, 0, 0)),
            pl.BlockSpec((None, ns, W), lambda n: (n, 0, 0)),
            pl.BlockSpec((2, NSA_DH, NSA_DH), lambda n: (0, 0, 0)),
        ],
        out_specs=[pl.BlockSpec((None, ns, NSA_KV * NSA_DH), lambda n: (n, 0, 0))] * 2,
        out_shape=[jax.ShapeDtypeStruct((N, ns, NSA_KV * NSA_DH), F32)] * 2,
        compiler_params=_cparams("parallel"),
        name="nsa_cmp_proj",
    )(a, b, w_cmp)


def _split3(x):
    hi = x.astype(BF16)
    r1 = x - hi.astype(F32)
    mid = r1.astype(BF16)
    lo = (r1 - mid.astype(F32)).astype(BF16)
    return hi, mid, lo


def _cmp_to_sel(p_sum, a_mat):
    hi, mid, lo = _split3(p_sum)
    return _dot(hi, a_mat) + _dot(mid, a_mat) + _dot(lo, a_mat)


def _topk_mask(score, n_valid, k):
    lane = lax.broadcasted_iota(jnp.int32, score.shape, 1)
    rank = jnp.zeros(score.shape, F32)
    for jp in range(n_valid):
        col = score[:, jp:jp + 1]
        beats = (col > score) | ((col == score) & (lane > jp))
        rank = rank + beats.astype(F32)
    return ((rank < k) & (lane < n_valid)).astype(F32)


def _sel_scores(imp, qblk, n_valid):
    lane = lax.broadcasted_iota(jnp.int32, imp.shape, 1)
    forced = (lane == 0) | (lane == qblk) | (lane == qblk - 1)
    score = jnp.where(forced, FORCE_SCORE, jnp.where(lane > qblk, -1.0, imp))
    return jnp.where(lane < n_valid, score, -2.0)


def _masked_softmax_rows(s, mask):
    s = jnp.where(mask, s, NEG)
    e = jnp.exp(s - jnp.max(s, axis=-1, keepdims=True))
    return jnp.where(mask, e / jnp.sum(e, axis=-1, keepdims=True), 0.0)


def _stack_heads(q, scale):
    return (jnp.concatenate([q[:, g * NSA_DH:(g + 1) * NSA_DH] for g in range(NSA_G)], axis=0) * scale).astype(BF16)


SEL_TK = 512
WIN_KEYS = WINDOW + QBLK


def _nsa_prompt_body(q_ref, zg_ref, gb_ref, kc_ref, vc_ref, ks_ref, vs_ref, kw_ref, vw_ref, amat_t_ref,
                     o_ref, ks_bf, kw_bf, vs_t, vw_t, kc_bf, vc_t, sc_ref, *, n_sel, nsr):
    qb = pl.program_id(1)
    T = ks_ref.shape[0]
    kvc = lambda kv: slice(kv * NSA_DH, (kv + 1) * NSA_DH)

    @pl.when(qb == 0)
    def _():
        ks_bf[...] = ks_ref[...].astype(BF16)
        kw_bf[...] = kw_ref[...].astype(BF16)
        kc_bf[...] = kc_ref[...].astype(BF16)
        vc_t[...] = vc_ref[...].T.astype(BF16)

        def transpose_values(i, c):
            r0 = pl.multiple_of(i * LANE, LANE)
            vs_t[i] = vs_ref[pl.ds(r0, LANE), :].T.astype(BF16)
            vw_t[i] = vw_ref[pl.ds(r0, LANE), :].T.astype(BF16)
            return c

        lax.fori_loop(0, T // LANE, transpose_values, 0)

    R = QBLK
    start = qb * R
    groups = range(NSA_KV)
    tile4 = lambda x: jnp.concatenate([x] * NSA_G, axis=1)
    q_t = [jnp.concatenate([(q_ref[:, (kv * NSA_G + g) * NSA_DH:(kv * NSA_G + g + 1) * NSA_DH] * (NSA_DH ** -0.5)).T
                            for g in range(NSA_G)], axis=1).astype(BF16) for kv in groups]
    pos = start + lax.broadcasted_iota(jnp.int32, (1, R), 1)

    ns = kc_bf.shape[0]
    cmp_end = (lax.broadcasted_iota(jnp.int32, (ns, R), 0) + 2) * CMP_STRIDE - 1
    bias_c = tile4(jnp.where(cmp_end <= pos, 0.0, NEG))
    any_c = tile4(jnp.where(pos >= 2 * CMP_STRIDE - 1, 1.0, 0.0))
    a_t = amat_t_ref[...]
    blk = lax.broadcasted_iota(jnp.int32, (nsr, R), 0)
    qblk = (start + lax.broadcasted_iota(jnp.int32, (nsr, R), 1)) // SEL_BLOCK
    forced = (blk == 0) | (blk == qblk) | (blk == qblk - 1)
    o_cmp, score = [], []
    for kv in groups:
        s_c = _dot(kc_bf[:, kvc(kv)], q_t[kv]) + bias_c
        e_c = jnp.exp(s_c - jnp.max(s_c, axis=0, keepdims=True))
        p_c = e_c * (any_c / jnp.sum(e_c, axis=0, keepdims=True))
        o_cmp.append(_dot(vc_t[kvc(kv), :], p_c.astype(BF16)))
        p_sum = p_c[:, 0:R] + p_c[:, R:2 * R] + p_c[:, 2 * R:3 * R] + p_c[:, 3 * R:4 * R]
        hi, mid, lo = _split3(p_sum)
        imp_t = _dot(a_t, hi) + _dot(a_t, mid) + _dot(a_t, lo)
        sc = jnp.where(forced, FORCE_SCORE, jnp.where(blk > qblk, -1.0, imp_t))
        score.append(jnp.where(blk < n_sel, sc, -2.0))
        sc_ref[kv] = score[kv]

    def rank_step(i, ranks):
        ranks = list(ranks)
        for jp in (2 * i, 2 * i + 1):
            for kv in groups:
                row = sc_ref[kv, pl.ds(jp, 1), :]
                beats = (row > score[kv]) | ((row == score[kv]) & (blk > jp))
                ranks[kv] = ranks[kv] + beats.astype(F32)
        return tuple(ranks)

    n_rank = jnp.minimum(start // SEL_BLOCK + QBLK // SEL_BLOCK, n_sel)
    ranks = lax.fori_loop(0, (n_rank + 1) // 2, rank_step, tuple(jnp.zeros((nsr, R), F32) for _ in groups))
    for kv in groups:
        sc_ref[kv] = jnp.where(ranks[kv] < N_SEL, 0.0, NEG)

    def values_product(v_ref, tile0, n_sub, kv, p):
        out = _dot(v_ref[tile0, kvc(kv), :], p[0:LANE])
        for i in range(1, n_sub):
            out = out + _dot(v_ref[tile0 + i, kvc(kv), :], p[i * LANE:(i + 1) * LANE])
        return out

    key_tk = lax.broadcasted_iota(jnp.int32, (SEL_TK, R), 0)

    def sel_tile(kt, carry, causal):
        k0 = pl.multiple_of(kt * SEL_TK, SEL_TK)
        scores = [_dot(ks_bf[pl.ds(k0, SEL_TK), kvc(kv)], q_t[kv]) for kv in groups]
        stats = []
        for kv in groups:
            m_prev, l_prev, _ = carry[kv]
            bias = jnp.concatenate(
                [jnp.broadcast_to(sc_ref[kv, pl.ds(kt * (SEL_TK // SEL_BLOCK) + b, 1), :], (SEL_BLOCK, R))
                 for b in range(SEL_TK // SEL_BLOCK)], axis=0)
            if causal:
                bias = jnp.where(k0 + key_tk <= pos, bias, NEG)
            s = scores[kv] + tile4(bias)
            m_new = jnp.maximum(m_prev, jnp.max(s, axis=0, keepdims=True))
            alpha = jnp.exp(m_prev - m_new)
            p = jnp.exp(s - m_new)
            stats.append((m_new, alpha, alpha * l_prev + jnp.sum(p, axis=0, keepdims=True), p.astype(BF16)))
        return tuple((m_new, l_new, alpha * carry[kv][2]
                      + values_product(vs_t, kt * (SEL_TK // LANE), SEL_TK // LANE, kv, p))
                     for kv, (m_new, alpha, l_new, p) in enumerate(stats))

    n_tiles = (start + R - 1) // SEL_TK + 1
    init = tuple((jnp.full((1, NSA_G * R), NEG, F32), jnp.zeros((1, NSA_G * R), F32),
                  jnp.zeros((NSA_DH, NSA_G * R), F32)) for _ in groups)
    carry = lax.fori_loop(0, n_tiles - 1, lambda kt, c: sel_tile(kt, c, False), init)
    o_sel = [acc / l for _, l, acc in sel_tile(n_tiles - 1, carry, True)]

    w0 = pl.multiple_of(jnp.maximum(start - WINDOW, 0), QBLK)
    key_w = w0 + lax.broadcasted_iota(jnp.int32, (WIN_KEYS, R), 0)
    bias_w = tile4(jnp.where((key_w <= pos) & (key_w > pos - WINDOW), 0.0, NEG))
    gates_t = jax.nn.sigmoid(zg_ref[...] + gb_ref[...]).T
    for kv in groups:
        s_w = _dot(kw_bf[pl.ds(w0, WIN_KEYS), kvc(kv)], q_t[kv]) + bias_w
        e_w = jnp.exp(s_w - jnp.max(s_w, axis=0, keepdims=True))
        o_win = (values_product(vw_t, w0 // LANE, WIN_KEYS // LANE, kv, e_w.astype(BF16))
                 / jnp.sum(e_w, axis=0, keepdims=True))
        gate = lambda branch: jnp.concatenate(
            [gates_t[branch * NSA_HEADS + kv * NSA_G + g:branch * NSA_HEADS + kv * NSA_G + g + 1, :]
             for g in range(NSA_G)], axis=1)
        out_t = gate(0) * o_cmp[kv] + gate(1) * o_sel[kv] + gate(2) * o_win
        for g in range(NSA_G):
            h = kv * NSA_G + g
            o_ref[:, h * NSA_DH:(h + 1) * NSA_DH] = out_t[:, g * R:(g + 1) * R].T.astype(BF16)


def _sel_map(ns, n_sel, nsb):
    i = np.arange(ns)[:, None]
    j = np.arange(nsb)[None, :]
    r = SEL_BLOCK // CMP_STRIDE
    return jnp.asarray(((i >= r * j - 1) & (i <= r * j + r - 1) & (j < n_sel)).astype(np.float32), BF16)


def _nsa_prompt(z3, zg3, gate_b, kc, vc):
    N, T = z3.shape[:2]
    assert T % SEL_TK == 0 and T >= WIN_KEYS
    ns = kc.shape[1]
    n_sel = T // SEL_BLOCK
    nsr = -(-n_sel // 8) * 8
    assert nsr <= LANE
    amat_t = _sel_map(ns, n_sel, nsr).T
    KVW = NSA_KV * NSA_DH
    qcol = 2 * D_HALF // (NSA_HEADS * NSA_DH)
    kvs_col = (3 * D_HALF + 2 * KVW) // KVW
    kvw_col = kvs_col + 2
    full = lambda off: pl.BlockSpec((None, T, KVW), lambda n, qb: (n, 0, off))
    return pl.pallas_call(
        functools.partial(_nsa_prompt_body, n_sel=n_sel, nsr=nsr),
        grid=(N, T // QBLK),
        in_specs=[
            pl.BlockSpec((None, QBLK, NSA_HEADS * NSA_DH), lambda n, qb: (n, qb, qcol)),
            pl.BlockSpec((None, QBLK, LANE), lambda n, qb: (n, qb, 0)),
            pl.BlockSpec((1, LANE), lambda n, qb: (0, 0)),
            pl.BlockSpec((None, ns, KVW), lambda n, qb: (n, 0, 0)),
            pl.BlockSpec((None, ns, KVW), lambda n, qb: (n, 0, 0)),
            full(kvs_col), full(kvs_col + 1), full(kvw_col), full(kvw_col + 1),
            pl.BlockSpec((nsr, ns), lambda n, qb: (0, 0)),
        ],
        out_specs=pl.BlockSpec((None, QBLK, NSA_HEADS * NSA_DH), lambda n, qb: (n, qb, 0)),
        out_shape=jax.ShapeDtypeStruct((N, T, NSA_HEADS * NSA_DH), BF16),
        scratch_shapes=[
            pltpu.VMEM((T, KVW), BF16), pltpu.VMEM((T, KVW), BF16),
            pltpu.VMEM((T // LANE, KVW, LANE), BF16), pltpu.VMEM((T // LANE, KVW, LANE), BF16),
            pltpu.VMEM((ns, KVW), BF16), pltpu.VMEM((KVW, ns), BF16),
            pltpu.VMEM((NSA_KV, nsr, QBLK), F32),
        ],
        compiler_params=_cparams("parallel", "arbitrary"),
        name="nsa_prompt",
    )(z3, zg3, gate_b, kc, vc, z3, z3, z3, z3, amat_t)


def _nsa_sample_a_body(q_ref, kc_ref, vc_ref, kw_ref, vw_ref, amat_ref, ocmp_ref, owin_ref, sel_ref,
                       *, Tq, past_len, n_sel, wb):
    qs = _stack_heads(q_ref[...], NSA_DH ** -0.5)
    pos = past_len + lax.broadcasted_iota(jnp.int32, (Tq, 1), 0)
    pos4 = jnp.concatenate([pos] * NSA_G, axis=0)

    ns = kc_ref.shape[0]
    cmp_i = lax.broadcasted_iota(jnp.int32, (NSA_G * Tq, ns), 1)
    m_c = ((cmp_i + 2) * CMP_STRIDE - 1 <= pos4) & (cmp_i < ns - 1)
    p_c = _masked_softmax_rows(_dot_nt(qs, kc_ref[...].astype(BF16)), m_c)
    ocmp_ref[...] = _dot(p_c.astype(BF16), vc_ref[...].astype(BF16))
    p_sum = p_c[0:Tq] + p_c[Tq:2 * Tq] + p_c[2 * Tq:3 * Tq] + p_c[3 * Tq:4 * Tq]
    imp = _cmp_to_sel(p_sum, amat_ref[...])
    sel_ref[...] = _topk_mask(_sel_scores(imp, pos // SEL_BLOCK, n_sel), n_sel, N_SEL)

    nw = kw_ref.shape[0]
    tok_w = past_len - wb + lax.broadcasted_iota(jnp.int32, (NSA_G * Tq, nw), 1)
    m_w = (tok_w >= 0) & (tok_w <= pos4) & (tok_w > pos4 - WINDOW)
    p_w = _masked_softmax_rows(_dot_nt(qs, kw_ref[...].astype(BF16)), m_w)
    owin_ref[...] = _dot(p_w.astype(BF16), vw_ref[...].astype(BF16))


def _nsa_sample_a(z3, kc, vc, kw_full, past_len, wb):
    N, Tq = z3.shape[:2]
    ns = kc.shape[1]
    nw = kw_full.shape[1]
    n_sel = -(-(past_len + Tq) // SEL_BLOCK)
    nsb = -(-n_sel // LANE) * LANE
    amat = _sel_map(ns, n_sel, nsb)
    qcol = 2 * D_HALF // (NSA_G * NSA_DH)
    R4 = NSA_G * Tq
    return pl.pallas_call(
        functools.partial(_nsa_sample_a_body, Tq=Tq, past_len=past_len, n_sel=n_sel, wb=wb),
        grid=(N, NSA_KV),
        in_specs=[
            pl.BlockSpec((None, Tq, NSA_G * NSA_DH), lambda n, kv: (n, 0, qcol + kv)),
            pl.BlockSpec((None, ns, NSA_DH), lambda n, kv: (n, 0, kv)),
            pl.BlockSpec((None, ns, NSA_DH), lambda n, kv: (n, 0, kv)),
            pl.BlockSpec((None, nw, NSA_DH), lambda n, kv: (n, 0, kv)),
            pl.BlockSpec((None, nw, NSA_DH), lambda n, kv: (n, 0, NSA_KV + kv)),
            pl.BlockSpec((ns, nsb), lambda n, kv: (0, 0)),
        ],
        out_specs=[
            pl.BlockSpec((None, None, R4, NSA_DH), lambda n, kv: (n, kv, 0, 0)),
            pl.BlockSpec((None, None, R4, NSA_DH), lambda n, kv: (n, kv, 0, 0)),
            pl.BlockSpec((None, None, Tq, nsb), lambda n, kv: (n, kv, 0, 0)),
        ],
        out_shape=[
            jax.ShapeDtypeStruct((N, NSA_KV, R4, NSA_DH), F32),
            jax.ShapeDtypeStruct((N, NSA_KV, R4, NSA_DH), F32),
            jax.ShapeDtypeStruct((N, NSA_KV, Tq, nsb), F32),
        ],
        compiler_params=_cparams("parallel", "parallel"),
        name="nsa_sample_cmp_win",
    )(z3, kc, vc, kw_full, kw_full, amat)


BLK_PER_STEP = PAGES_PER_STEP * PAGE // SEL_BLOCK


def _nsa_sample_b_body(pt_ref, *refs, Tq, past_len):
    page_refs = refs[:PAGES_PER_STEP]
    (q_ref, new_ref, sel_ref, exp_ref, ocmp_ref, owin_ref, zg_ref, gb_ref, o_ref, m_s, l_s, acc_s) = refs[PAGES_PER_STEP:]
    s_id = pl.program_id(1)
    R4 = NSA_G * Tq
    rep = lambda x: jnp.concatenate([x] * NSA_G, axis=0)
    pos = past_len + lax.broadcasted_iota(jnp.int32, (Tq, 1), 0)

    @pl.when(s_id == 0)
    def _():
        m_s[...] = jnp.full(m_s.shape, NEG, F32)
        l_s[...] = jnp.zeros(l_s.shape, F32)
        acc_s[...] = jnp.zeros(acc_s.shape, F32)

    def update(kv, s, mask, pv):
        s = jnp.where(mask, s, NEG)
        m_prev = m_s[kv]
        m_new = jnp.maximum(m_prev, jnp.max(s, axis=-1, keepdims=True))
        alpha = jnp.exp(m_prev - m_new)
        p = jnp.where(mask, jnp.exp(s - m_new), 0.0)
        l_s[kv] = alpha * l_s[kv] + jnp.sum(p, axis=-1, keepdims=True)
        acc_s[kv] = alpha * acc_s[kv] + pv(p.astype(BF16))
        m_s[kv] = m_new

    qs = [_stack_heads(q_ref[:, kv * NSA_G * NSA_DH:(kv + 1) * NSA_G * NSA_DH], NSA_DH ** -0.5) for kv in range(NSA_KV)]
    comp = lambda pr, c: pr[pl.ds(c, PAGE, stride=ROWS_PER_TOKEN), :].astype(BF16)
    for kv in range(NSA_KV):
        s = jnp.concatenate([_dot_nt(qs[kv], comp(pr, kv)) for pr in page_refs], axis=1)
        mask = rep(_dot(sel_ref[kv].astype(BF16), exp_ref[...])) > 0.5

        def pv(p, kv=kv):
            out = _dot(p[:, 0:PAGE], comp(page_refs[0], NSA_KV + kv))
            for i in range(1, PAGES_PER_STEP):
                out = out + _dot(p[:, i * PAGE:(i + 1) * PAGE], comp(page_refs[i], NSA_KV + kv))
            return out

        update(kv, s, mask, pv)

    @pl.when(s_id == pl.num_programs(1) - 1)
    def _():
        new = new_ref[...].astype(BF16)
        tok_n = past_len + lax.broadcasted_iota(jnp.int32, (R4, Tq), 1)
        gates = jax.nn.sigmoid(zg_ref[...] + gb_ref[...])
        for kv in range(NSA_KV):
            k_new = new[:, kv * NSA_DH:(kv + 1) * NSA_DH]
            v_new = new[:, (NSA_KV + kv) * NSA_DH:(NSA_KV + kv + 1) * NSA_DH]
            update(kv, _dot_nt(qs[kv], k_new), tok_n <= rep(pos), lambda p, v_new=v_new: _dot(p, v_new))
            o_sel = acc_s[kv] / l_s[kv]
            o_cmp = ocmp_ref[kv]
            o_win = owin_ref[kv]
            for g in range(NSA_G):
                rs = slice(g * Tq, (g + 1) * Tq)
                head = kv * NSA_G + g
                o_ref[:, head * NSA_DH:(head + 1) * NSA_DH] = (
                    gates[:, head:head + 1] * o_cmp[rs]
                    + gates[:, NSA_HEADS + head:NSA_HEADS + head + 1] * o_sel[rs]
                    + gates[:, 2 * NSA_HEADS + head:2 * NSA_HEADS + head + 1] * o_win[rs])


def _nsa_sample_b(z3, zg3, gate_b, pool_sel, page_table, sel, o_cmp, o_win, past_len):
    N, Tq = z3.shape[:2]
    n_pages = page_table.shape[1]
    assert n_pages % PAGES_PER_STEP == 0 and past_len % SEL_BLOCK == 0
    n_steps = n_pages // PAGES_PER_STEP
    sel_steps = sel[..., :past_len // SEL_BLOCK].reshape(N, NSA_KV, Tq, n_steps, BLK_PER_STEP).transpose(0, 3, 1, 2, 4)
    tok = np.arange(PAGES_PER_STEP * PAGE)[None, :]
    expand = jnp.asarray((tok // SEL_BLOCK == np.arange(BLK_PER_STEP)[:, None]).astype(np.float32), BF16)
    W = 4 * NSA_DH
    R4 = NSA_G * Tq
    qcol = 2 * D_HALF // (NSA_HEADS * NSA_DH)
    kvs_col = (3 * D_HALF + W) // W
    page_specs = [
        pl.BlockSpec((PAGE_ROWS, NSA_DH), functools.partial(
            lambda n, s, pt, r: (pt[n, s * PAGES_PER_STEP + r], 0), r=r))
        for r in range(PAGES_PER_STEP)
    ]
    per_n4 =lambda shape: pl.BlockSpec((None,) + shape, lambda n, s, pt: (n, 0, 0, 0))
    grid_spec = pltpu.PrefetchScalarGridSpec(
        num_scalar_prefetch=1,
        grid=(N, n_pages // PAGES_PER_STEP),
        in_specs=page_specs + [
            pl.BlockSpec((None, Tq, NSA_HEADS * NSA_DH), lambda n, s, pt: (n, 0, qcol)),
            pl.BlockSpec((None, Tq, W), lambda n, s, pt: (n, 0, kvs_col)),
            pl.BlockSpec((None, None, NSA_KV, Tq, BLK_PER_STEP), lambda n, s, pt: (n, s, 0, 0, 0)),
            pl.BlockSpec((BLK_PER_STEP, PAGES_PER_STEP * PAGE), lambda n, s, pt: (0, 0)),
            per_n4((NSA_KV, R4, NSA_DH)),
            per_n4((NSA_KV, R4, NSA_DH)),
            pl.BlockSpec((None, Tq, LANE), lambda n, s, pt: (n, 0, 0)),
            pl.BlockSpec((1, LANE), lambda n, s, pt: (0, 0)),
        ],
        out_specs=pl.BlockSpec((None, Tq, NSA_HEADS * NSA_DH), lambda n, s, pt: (n, 0, 0)),
        scratch_shapes=[
            pltpu.VMEM((NSA_KV, R4, 1), F32),
            pltpu.VMEM((NSA_KV, R4, 1), F32),
            pltpu.VMEM((NSA_KV, R4, NSA_DH), F32),
        ],
    )
    return pl.pallas_call(
        functools.partial(_nsa_sample_b_body, Tq=Tq, past_len=past_len),
        grid_spec=grid_spec,
        out_shape=jax.ShapeDtypeStruct((N, Tq, NSA_HEADS * NSA_DH), F32),
        compiler_params=_cparams("parallel", "arbitrary"),
        name="nsa_sample_sel",
    )(page_table, *([pool_sel] * PAGES_PER_STEP), z3, z3, sel_steps, expand, o_cmp, o_win, zg3, gate_b)


MLSTM_CHUNK_PROMPT = 256
SAMPLE_PAD = 128


def _pad_rows(x, rows):
    return jnp.pad(x, ((0, 0), (0, rows - x.shape[1]), (0, 0)))


def _lane_pad(v):
    return jnp.pad(v.astype(F32), (0, LANE - v.shape[0])).reshape(1, LANE)


def _gate_rows(w_in_t, n_main):
    return jnp.pad(w_in_t[n_main:], ((0, LANE - (w_in_t.shape[0] - n_main)), (0, 0)))


def _even_mixers(z3, zg3, t0, pool_buf, C0, n0, m0, w):
    N, T = z3.shape[:2]
    st = jnp.pad(pool_buf, ((0, 0), (HALO - POOL_PAD, 0), (0, 0)))
    y_a = _pool_mix(z3, st, w["pool_w"], w["pool_scale"], t0)
    u_ext_tail = jnp.concatenate([pool_buf, z3[:, :, :D_HALF]], axis=1)[:, -POOL_PAD:] if T < POOL_PAD else z3[:, -POOL_PAD:, :D_HALF]

    if T % MLSTM_CHUNK_PROMPT == 0:
        L, zm, zgm = MLSTM_CHUNK_PROMPT, z3, zg3
    else:
        L, zm, zgm = SAMPLE_PAD, _pad_rows(z3, SAMPLE_PAD), _pad_rows(zg3, SAMPLE_PAD)
    y_b, C, n, m = _mlstm(zm, zgm, w["gate_b"], w["mnorm_g"], C0, n0, m0, L, min(T, L))
    return y_a, y_b[:, :T], u_ext_tail, C, n, m


def _odd_mixers(z3, zg3, w, past=None):
    N, T, n_main = z3.shape
    W = 4 * NSA_DH
    kvc = z3[:, :, 3 * D_HALF:3 * D_HALF + W]
    kvs = z3[:, :, 3 * D_HALF + W:3 * D_HALF + 2 * W]
    kvw = z3[:, :, 3 * D_HALF + 2 * W:3 * D_HALF + 3 * W]

    if past is None:
        y_c, _ = _gmlp(z3, w["gnorm_g"], w["ws"], w["bs_t"])
        vn = None
        pt = jnp.arange(N * (T // PAGE), dtype=jnp.int32).reshape(N, T // PAGE)
        a, b = _compress_ab(z3.reshape(N * (T // PAGE), PAGE, n_main), pt, w["wab"], 3 * D_HALF // W)
        kc, vc = _cmp_proj(a, b, w["cmp_w"])
        o = _nsa_prompt(z3, zg3, w["nsa_gate_b"], kc, vc)
        win_state = kvw[:, -min(WINDOW, T):]
    else:
        win_buf, pool_cmp, pool_sel, page_table = past
        past_len = page_table.shape[1] * PAGE
        wb = win_buf.shape[1]
        y_c, vn = _gmlp(_pad_rows(z3[:, :, :2 * D_HALF], SAMPLE_PAD), w["gnorm_g"], w["ws"], w["bs_t"])
        y_c, vn = y_c[:, :T], vn[:, :T]
        assert (past_len + T) // CMP_STRIDE == past_len // CMP_STRIDE
        a, b = _compress_ab_rows(pool_cmp, page_table, w["wab_rows"])
        kc, vc = _cmp_proj(a, b, w["cmp_w"])
        kw_all = jnp.concatenate([win_buf.reshape(N, wb, W), kvw], axis=1)
        nw = -(-(wb + T) // LANE) * LANE
        o_cmp, o_win, sel = _nsa_sample_a(z3, kc, vc, _pad_rows(kw_all, nw), past_len, wb)
        o = _nsa_sample_b(z3, zg3, w["nsa_gate_b"], pool_sel, page_table, sel, o_cmp, o_win, past_len)
        win_state = kw_all[:, -wb:]

    kv5 = lambda t: t.reshape(N, t.shape[1], 2, NSA_KV, NSA_DH)
    return y_c, o, vn, kv5(kvc), kv5(kvs), kv5(win_state)


def _layer(xp, xs, xpn, xsn, w, ffn_stacks, layer, n_main, mixers_p, mixers_s):
    B, T, D = xp.shape
    Ns, Ts, _ = xs.shape
    xp2, xs2 = xp.reshape(B * T, D), xs.reshape(Ns * Ts, D)
    zp, zgp, zs, zgs = _in_proj(xpn, xsn, w["w_in"], w["w_in_layer"], n_main, w["w_gate"])
    a1p, a2p, *extra_p = mixers_p(zp.reshape(B, T, n_main), zgp.reshape(B, T, LANE))
    a1s, a2s, *extra_s = mixers_s(zs.reshape(Ns, Ts, n_main), zgs.reshape(Ns, Ts, LANE))
    flat = lambda t: t.reshape(-1, D_HALF)
    xp2, xpn, xs2, xsn = _out_proj(flat(a1p), flat(a2p), xp2, flat(a1s), flat(a2s), xs2, w["w_out_a"], w["w_out_b"],
                                   w["ng"][1:2], w["ng"][2:3])
    w1_stack, w3_stack, w2_stack = ffn_stacks
    hp, hs, w2_bf = _ffn_up(xpn, xsn, w1_stack, w3_stack, w2_stack, layer)
    xp2, xs2, xpn, xsn = _ffn_down(hp, hs, xp2, xs2, w["ng"][3:4], w["ng_next"], w2_bf)
    return xp2.reshape(B, T, D), xs2.reshape(Ns, Ts, D), xpn, xsn, extra_p, extra_s


def kernel(x_prompt, x_sample, state_pool, state_mlstm_c, state_mlstm_n, state_mlstm_m, state_win_kv, cache_cmp_kv, cache_sel_kv, page_table, norm_g, w_in_even, w_out_even, pool_w, pool_scale, mlstm_gate_b, mlstm_norm_g, w_in_odd, w_out_odd, gmlp_norm_g, gmlp_ws, gmlp_bs, nsa_cmp_pos, nsa_cmp_w, nsa_gate_b, ffn_w1, ffn_w3, ffn_w2):
    B = x_prompt.shape[0]
    depth = norm_g.shape[0]
    past_len = page_table.shape[1] * PAGE
    xp, xs = x_prompt, x_sample
    pool_p, pool_s, c_p, c_s, n_p, n_s, m_p, m_s = [], [], [], [], [], [], [], []
    gv_s, cmp_p, cmp_s, sel_p, sel_s, win_p, win_s = [], [], [], [], [], [], []
    ffn_stacks = (ffn_w1, ffn_w3, ffn_w2)
    D = x_prompt.shape[-1]
    xpn, xsn = _rms_cast(x_prompt.reshape(-1, D), x_sample.reshape(-1, D), norm_g[0][0:1])
    for l in range(depth):
        j = l // 2
        ng_next = norm_g[min(l + 1, depth - 1)][0:1]
        if l % 2 == 0:
            n_main = 4 * D_HALF
            w_out = w_out_even[j].astype(BF16)
            w_in_t = jnp.swapaxes(w_in_even, 1, 2)
            w = dict(ng=norm_g[l], ng_next=ng_next, w_in=w_in_t, w_in_layer=j, w_gate=_gate_rows(w_in_t[j], n_main),
                     w_out_a=w_out[:D_HALF], w_out_b=w_out[D_HALF:],
                     pool_w=pool_w[j].astype(BF16), pool_scale=pool_scale[j].reshape(1, D_HALF),
                     gate_b=_lane_pad(mlstm_gate_b[j].reshape(-1)), mnorm_g=mlstm_norm_g[j].reshape(1, D_HALF))
            zp = jnp.zeros((B, POOL_PAD, D_HALF), F32)
            zc = jnp.zeros((B, MLSTM_HEADS, MLSTM_DK, MLSTM_DV), F32)
            zn = jnp.zeros((B, MLSTM_HEADS, MLSTM_DK), F32)
            zm = jnp.zeros((B, MLSTM_HEADS), F32)
            xp, xs, xpn, xsn, (pb, c, n, m), (pbs, cs, ns_, ms) = _layer(
                xp, xs, xpn, xsn, w, ffn_stacks, l, n_main,
                lambda z3, zg3: _even_mixers(z3, zg3, 0, zp, zc, zn, zm, w),
                lambda z3, zg3: _even_mixers(z3, zg3, past_len, state_pool[j], state_mlstm_c[j], state_mlstm_n[j],
                                             state_mlstm_m[j], w))
            pool_p.append(pb); c_p.append(c); n_p.append(n); m_p.append(m)
            pool_s.append(pbs); c_s.append(cs); n_s.append(ns_); m_s.append(ms)
        else:
            n_main = 4 * D_HALF + D_HALF // 2
            w_out = w_out_odd[j].astype(BF16)
            cp = nsa_cmp_pos[j]
            wcol = jnp.repeat(cp, NSA_KV * NSA_DH, axis=1)
            wab = jnp.stack([wcol[:CMP_STRIDE], wcol[CMP_STRIDE:]])
            wab_rows = jnp.broadcast_to(jnp.repeat(cp, NSA_KV, axis=1).reshape(2, BLK_ROWS, 1), (2, BLK_ROWS, NSA_DH))
            w_in_t = jnp.swapaxes(w_in_odd, 1, 2)
            w = dict(ng=norm_g[l], ng_next=ng_next, w_in=w_in_t, w_in_layer=j, w_gate=_gate_rows(w_in_t[j], n_main),
                     w_out_a=w_out[:D_HALF], w_out_b=w_out[D_HALF:],
                     gnorm_g=gmlp_norm_g[j].reshape(1, D_HALF), ws=gmlp_ws[j],
                     bs_t=jnp.pad(gmlp_bs[j].T, ((0, 0), (0, LANE - GMLP_GROUPS))),
                     wab=wab, wab_rows=wab_rows, cmp_w=nsa_cmp_w[j], nsa_gate_b=_lane_pad(nsa_gate_b[j]))
            n_pool = cache_cmp_kv.shape[1]
            flat = lambda c: c.reshape(c.shape[0] * n_pool * PAGE_ROWS, NSA_DH)
            past = (state_win_kv[j], flat(cache_cmp_kv), flat(cache_sel_kv), page_table + j * n_pool)
            xp, xs, xpn, xsn, (_, kc, ksl, wv), (vn, kcs, ksls, wvs) = _layer(
                xp, xs, xpn, xsn, w, ffn_stacks, l, n_main,
                lambda z3, zg3: _odd_mixers(z3, zg3, w),
                lambda z3, zg3: _odd_mixers(z3, zg3, w, past=past))
            cmp_p.append(kc); sel_p.append(ksl); win_p.append(wv)
            gv_s.append(vn); cmp_s.append(kcs); sel_s.append(ksls); win_s.append(wvs)
    st = jnp.stack
    return (xp, xs, st(pool_p), st(pool_s), st(c_p), st(c_s), st(n_p), st(n_s), st(m_p), st(m_s),
            st(gv_s), st(cmp_p), st(cmp_s), st(sel_p), st(sel_s), st(win_p), st(win_s))
```

```python
import functools
import math

import numpy as np
import jax
import jax.numpy as jnp
from jax import lax
from jax.experimental import pallas as pl
from jax.experimental.pallas import tpu as pltpu

F32 = jnp.float32
BF16 = jnp.bfloat16
NEG = -1e30

D_HALF = 1024
POOL_WINDOWS = (2, 4, 8, 16)
POOL_DG = 256
POOL_PAD = 15
HALO = 16
MLSTM_HEADS = 4
MLSTM_DK = 128
MLSTM_DV = 256
GATE_CAP = 15.0
GMLP_CHUNK = 128
GMLP_GROUPS = 4
GMLP_DG = 256
NSA_HEADS = 8
NSA_DH = 128
NSA_KV = 2
NSA_G = 4
CMP_STRIDE = 16
SEL_BLOCK = 64
N_SEL = 16
WINDOW = 512
QBLK = 128
PAGE = 128
FORCE_SCORE = 1e9
LANE = 128
VMEM_LIMIT = 56 * 1024 * 1024

NT_DIMS = (((1,), (1,)), ((), ()))
TN_DIMS = (((0,), (0,)), ((), ()))


def _cparams(*sem):
    return pltpu.CompilerParams(dimension_semantics=sem, vmem_limit_bytes=VMEM_LIMIT)


def _tile(m, pref):
    if m <= pref:
        return m
    for t in range(pref, 7, -1):
        if m % t == 0 and t % 8 == 0:
            return t
    return m


def _dot(a, b):
    return jnp.dot(a, b, preferred_element_type=F32)


def _dot_nt(a, b):
    return lax.dot_general(a, b, NT_DIMS, preferred_element_type=F32)


def _rms(x, g, eps=1e-6):
    return x * lax.rsqrt(jnp.mean(x * x, axis=-1, keepdims=True) + eps) * g


def _rms_cast_body(x_ref, xs_ref, g_ref, o_ref, os_ref):
    o_ref[...] = _rms(x_ref[...], g_ref[...]).astype(BF16)

    @pl.when(pl.program_id(0) == 0)
    def _():
        os_ref[...] = _rms(xs_ref[...], g_ref[...]).astype(BF16)


def _rms_cast(x, xs, g, tm_pref=1024):
    M, K = x.shape
    Ms = xs.shape[0]
    tm = _tile(M, tm_pref)
    return pl.pallas_call(
        _rms_cast_body,
        grid=(M // tm,),
        in_specs=[pl.BlockSpec((tm, K), lambda i: (i, 0)), pl.BlockSpec((Ms, K), lambda i: (0, 0)),
                  pl.BlockSpec((1, K), lambda i: (0, 0))],
        out_specs=[pl.BlockSpec((tm, K), lambda i: (i, 0)), pl.BlockSpec((Ms, K), lambda i: (0, 0))],
        out_shape=[jax.ShapeDtypeStruct((M, K), BF16), jax.ShapeDtypeStruct((Ms, K), BF16)],
        compiler_params=_cparams("arbitrary"),
        name="rms_cast",
    )(x, xs, g)


def _in_proj_body(xn_ref, xsn_ref, w_ref, wg_ref, o_ref, og_ref, os_ref, ogs_ref, w_bf):
    j = pl.program_id(0)
    i = pl.program_id(1)

    @pl.when(i == 0)
    def _():
        w_bf[...] = w_ref[...].astype(BF16)
        os_ref[...] = _dot_nt(xsn_ref[...], w_bf[...])

    o_ref[...] = _dot_nt(xn_ref[...], w_bf[...])

    @pl.when(j == 0)
    def _():
        wg = wg_ref[...].astype(BF16)
        og_ref[...] = _dot_nt(xn_ref[...], wg)

        @pl.when(i == 0)
        def _():
            ogs_ref[...] = _dot_nt(xsn_ref[...], wg)


def _in_proj(xn, xsn, wt_stack, layer, n_main, wg_t, tm_pref=1024, tn_pref=1152):
    M, K = xn.shape
    Ms = xsn.shape[0]
    tm = _tile(M, tm_pref)
    tn = max(t for t in range(LANE, tn_pref + 1, LANE) if n_main % t == 0)
    ni = M // tm
    return pl.pallas_call(
        _in_proj_body,
        grid=(n_main // tn, ni),
        in_specs=[
            pl.BlockSpec((tm, K), lambda j, i: (i, 0)),
            pl.BlockSpec((Ms, K), lambda j, i: (0, 0)),
            pl.BlockSpec((None, tn, K), lambda j, i: (layer, j, 0)),
            pl.BlockSpec((LANE, K), lambda j, i: (0, 0)),
        ],
        out_specs=[
            pl.BlockSpec((tm, tn), lambda j, i: (i, j)),
            pl.BlockSpec((tm, LANE), lambda j, i: (jnp.where(j == 0, i, ni - 1), 0)),
            pl.BlockSpec((Ms, tn), lambda j, i: (0, j)),
            pl.BlockSpec((Ms, LANE), lambda j, i: (0, 0)),
        ],
        out_shape=[jax.ShapeDtypeStruct((M, n_main), F32), jax.ShapeDtypeStruct((M, LANE), F32),
                   jax.ShapeDtypeStruct((Ms, n_main), F32), jax.ShapeDtypeStruct((Ms, LANE), F32)],
        scratch_shapes=[pltpu.VMEM((tn, K), BF16)],
        compiler_params=_cparams("arbitrary", "arbitrary"),
        name="in_proj",
    )(xn, xsn, wt_stack, wg_t)


def _out_proj_body(a1_ref, a2_ref, res_ref, a1s_ref, a2s_ref, ress_ref, w1_ref, w2_ref, g_ref, gn_ref,
                   o_ref, on_ref, os_ref, osn_ref):
    def proj(a1, a2, res, o_r, on_r):
        y = _dot(a1[...].astype(BF16), w1_ref[...]) + _dot(a2[...].astype(BF16), w2_ref[...])
        x = res[...] + _rms(y, g_ref[...])
        o_r[...] = x
        on_r[...] = _rms(x, gn_ref[...]).astype(BF16)

    proj(a1_ref, a2_ref, res_ref, o_ref, on_ref)

    @pl.when(pl.program_id(0) == 0)
    def _():
        proj(a1s_ref, a2s_ref, ress_ref, os_ref, osn_ref)


def _out_proj(a1, a2, res, a1s, a2s, ress, w1, w2, g, g_next, tm_pref=512):
    M, K1 = a1.shape
    Ms = a1s.shape[0]
    K2 = a2.shape[1]
    D = w1.shape[1]
    tm = _tile(M, tm_pref)
    whole = lambda r, c: pl.BlockSpec((r, c), lambda i: (0, 0))
    rows = lambda c: pl.BlockSpec((tm, c), lambda i: (i, 0))
    return pl.pallas_call(
        _out_proj_body,
        grid=(M // tm,),
        in_specs=[
            rows(K1), rows(K2), rows(D),
            whole(Ms, K1), whole(Ms, K2), whole(Ms, D),
            whole(K1, D), whole(K2, D), whole(1, D), whole(1, D),
        ],
        out_specs=[rows(D), rows(D), whole(Ms, D), whole(Ms, D)],
        out_shape=[jax.ShapeDtypeStruct((M, D), F32), jax.ShapeDtypeStruct((M, D), BF16),
                   jax.ShapeDtypeStruct((Ms, D), F32), jax.ShapeDtypeStruct((Ms, D), BF16)],
        compiler_params=_cparams("arbitrary"),
        name="out_proj",
    )(a1, a2, res, a1s, a2s, ress, w1, w2, g, g_next)


def _ffn_up_body(xn_ref, xsn_ref, w1_ref, w3_ref, w2_ref, h_ref, hs_ref, w2_out, w1_bf, w3_bf):
    def swiglu(xn):
        h1 = _dot(xn, w1_bf[...])
        return (h1 * jax.nn.sigmoid(h1) * _dot(xn, w3_bf[...])).astype(BF16)

    @pl.when(pl.program_id(1) == 0)
    def _():
        w1_bf[...] = w1_ref[...].astype(BF16)
        w3_bf[...] = w3_ref[...].astype(BF16)
        w2_out[...] = w2_ref[...].astype(BF16)
        hs_ref[...] = swiglu(xsn_ref[...])

    h_ref[...] = swiglu(xn_ref[...])


def _ffn_up(xn, xsn, w1_stack, w3_stack, w2_stack, layer, tm_pref=1024, th_pref=512):
    M, D = xn.shape
    Ms = xsn.shape[0]
    H = w1_stack.shape[2]
    tm = _tile(M, tm_pref)
    th = _tile(H, th_pref)
    return pl.pallas_call(
        _ffn_up_body,
        grid=(H // th, M // tm),
        in_specs=[
            pl.BlockSpec((tm, D), lambda j, i: (i, 0)),
            pl.BlockSpec((Ms, D), lambda j, i: (0, 0)),
            pl.BlockSpec((None, D, th), lambda j, i: (layer, 0, j)),
            pl.BlockSpec((None, D, th), lambda j, i: (layer, 0, j)),
            pl.BlockSpec((None, th, D), lambda j, i: (layer, j, 0)),
        ],
        out_specs=[
            pl.BlockSpec((tm, th), lambda j, i: (i, j)),
            pl.BlockSpec((Ms, th), lambda j, i: (0, j)),
            pl.BlockSpec((th, D), lambda j, i: (j, 0)),
        ],
        out_shape=[jax.ShapeDtypeStruct((M, H), BF16), jax.ShapeDtypeStruct((Ms, H), BF16),
                   jax.ShapeDtypeStruct((H, D), BF16)],
        scratch_shapes=[pltpu.VMEM((D, th), BF16), pltpu.VMEM((D, th), BF16)],
        compiler_params=_cparams("arbitrary", "arbitrary"),
        name="ffn_up",
    )(xn, xsn, w1_stack, w3_stack, w2_stack)


def _ffn_down_body(h_ref, hs_ref, x_ref, xs_ref, g3_ref, gn_ref, w2_ref, o_ref, os_ref, on_ref, osn_ref):
    def rows(h_r, x_r, o_r, on_r):
        x = x_r[...] + _rms(_dot(h_r[...], w2_ref[...]), g3_ref[...])
        o_r[...] = x
        on_r[...] = _rms(x, gn_ref[...]).astype(BF16)

    rows(h_ref, x_ref, o_ref, on_ref)

    @pl.when(pl.program_id(0) == 0)
    def _():
        rows(hs_ref, xs_ref, os_ref, osn_ref)


def _ffn_down(h, hs, x, xs, g3, g_next, w2, tm_pref=256):
    M, D = x.shape
    Ms = xs.shape[0]
    H = h.shape[1]
    tm = _tile(M, tm_pref)
    whole = lambda r, c: pl.BlockSpec((r, c), lambda i: (0, 0))
    rows = lambda c: pl.BlockSpec((tm, c), lambda i: (i, 0))
    return pl.pallas_call(
        _ffn_down_body,
        grid=(M // tm,),
        in_specs=[rows(H), whole(Ms, H), rows(D), whole(Ms, D), whole(1, D), whole(1, D), whole(H, D)],
        out_specs=[rows(D), whole(Ms, D), rows(D), whole(Ms, D)],
        out_shape=[jax.ShapeDtypeStruct((M, D), F32), jax.ShapeDtypeStruct((Ms, D), F32),
                   jax.ShapeDtypeStruct((M, D), BF16), jax.ShapeDtypeStruct((Ms, D), BF16)],
        compiler_params=_cparams("arbitrary"),
        name="ffn_down",
    )(h, hs, x, xs, g3, g_next, w2)


def _pool_body(u_ref, prev_ref, st_ref, pw_ref, ps_ref, o_ref, ext_ref, *, tT, t0):
    t = pl.program_id(1)
    ext_ref[0:HALO, :] = jnp.where(t == 0, st_ref[...], prev_ref[...])
    ext_ref[HALO:HALO + tT, :] = u_ref[...]
    pos = t0 + t * tT + lax.broadcasted_iota(jnp.int32, (tT, 1), 0)
    for g, w in enumerate(POOL_WINDOWS):
        cs = slice(g * POOL_DG, (g + 1) * POOL_DG)
        x_new = ext_ref[HALO:HALO + tT, cs]
        tot = x_new
        for i in range(1, w):
            tot = tot + ext_ref[HALO - i:HALO - i + tT, cs]
        cnt = jnp.minimum(w, pos + 1).astype(F32)
        y = tot / cnt - x_new
        o_ref[:, cs] = (_dot(y.astype(BF16), pw_ref[g]) * ps_ref[:, cs]).astype(o_ref.dtype)


def _pool_mix(z3, st, pool_w, pool_scale, t0, tT_pref=512):
    N, T = z3.shape[:2]
    tT = _tile(T, tT_pref)
    nT = T // tT
    if nT > 1:
        assert tT % HALO == 0
        prev, prev_spec = z3, pl.BlockSpec((None, HALO, D_HALF), lambda n, t: (n, jnp.maximum(t * (tT // HALO) - 1, 0), 0))
    else:
        prev, prev_spec = st, pl.BlockSpec((None, HALO, D_HALF), lambda n, t: (n, 0, 0))
    return pl.pallas_call(
        functools.partial(_pool_body, tT=tT, t0=t0),
        grid=(N, nT),
        in_specs=[
            pl.BlockSpec((None, tT, D_HALF), lambda n, t: (n, t, 0)),
            prev_spec,
            pl.BlockSpec((None, HALO, D_HALF), lambda n, t: (n, 0, 0)),
            pl.BlockSpec((len(POOL_WINDOWS), POOL_DG, POOL_DG), lambda n, t: (0, 0, 0)),
            pl.BlockSpec((1, D_HALF), lambda n, t: (0, 0)),
        ],
        out_specs=pl.BlockSpec((None, tT, D_HALF), lambda n, t: (n, t, 0)),
        out_shape=jax.ShapeDtypeStruct((N, T, D_HALF), BF16 if tT % 16 == 0 else F32),
        scratch_shapes=[pltpu.VMEM((HALO + tT, D_HALF), F32)],
        compiler_params=_cparams("parallel", "arbitrary"),
        name="pool_mix",
    )(z3, prev, st, pool_w, pool_scale)


def _mlstm_body(q_ref, k_ref, v_ref, o_ref, zg_ref, gb_ref, mg_ref, c0_ref, n0_ref, m0_ref,
                y_ref, cN_ref, nN_ref, mN_ref, C_s, n_s, m_s, *, L, t_valid):
    c = pl.program_id(1)

    @pl.when(c == 0)
    def _():
        C_s[...] = c0_ref[...]
        n_s[...] = n0_ref[...]
        m_s[...] = m0_ref[...]

    a = GATE_CAP * jnp.tanh((zg_ref[...] + gb_ref[...]) / GATE_CAP)
    lane = lax.broadcasted_iota(jnp.int32, (L, LANE), 1)
    logsig = jnp.minimum(a, 0.0) - jnp.log1p(jnp.exp(-jnp.abs(a)))
    A = jnp.where(lane < MLSTM_HEADS, a, logsig)
    if t_valid < L:
        row = lax.broadcasted_iota(jnp.int32, (L, LANE), 0)
        A = jnp.where(row < t_valid, A, jnp.where(lane < MLSTM_HEADS, NEG, 0.0))
    r_i = lax.broadcasted_iota(jnp.int32, (L, L), 0)
    c_i = lax.broadcasted_iota(jnp.int32, (L, L), 1)
    causal = r_i >= c_i
    Bc = jnp.dot(causal.astype(F32), A, preferred_element_type=F32, precision=lax.Precision.HIGHEST)
    At = A.T
    Bt = Bc.T

    heads = range(MLSTM_HEADS)
    ks = lambda h: slice(h * MLSTM_DK, (h + 1) * MLSTM_DK)
    vs = lambda h: slice(h * MLSTM_DV, (h + 1) * MLSTM_DV)
    m0 = [m_s[h] for h in heads]
    C0 = [C_s[h] for h in heads]
    n0 = [n_s[h] for h in heads]
    qf = [q_ref[:, ks(h)] * (MLSTM_DK ** -0.5) for h in heads]
    qb = [q.astype(BF16) for q in qf]
    kf = [k_ref[:, ks(h)] for h in heads]
    vb = [v_ref[:, vs(h)].astype(BF16) for h in heads]
    qk = [_dot_nt(qb[h], kf[h].astype(BF16)) for h in heads]
    qc = [_dot(qb[h], C0[h].astype(BF16)) for h in heads]

    b_c = [Bc[:, MLSTM_HEADS + h:MLSTM_HEADS + h + 1] for h in heads]
    m_t, w_in, s = [], [], []
    for h in heads:
        b_r = Bt[MLSTM_HEADS + h:MLSTM_HEADS + h + 1, :]
        d = jnp.where(causal, b_c[h] - b_r + At[h:h + 1, :], NEG)
        inter = b_c[h] + m0[h]
        m_t.append(jnp.maximum(inter, jnp.max(d, axis=-1, keepdims=True)))
        w_in.append(jnp.exp(inter - m_t[h]))
        s.append(qk[h] * jnp.exp(d - m_t[h]))
    sv = [_dot(s[h].astype(BF16), vb[h]) for h in heads]

    kw, w_c, m_new = [], [], []
    for h in heads:
        b_end = b_c[h][L - 1:L, :]
        g_c = b_end - b_c[h] + A[:, h:h + 1]
        m_new.append(jnp.maximum(b_end + m0[h], jnp.max(g_c, axis=0, keepdims=True)))
        w_c.append(jnp.exp(b_end + m0[h] - m_new[h]))
        kw.append(kf[h] * jnp.exp(g_c - m_new[h]))
    kv = [lax.dot_general(kw[h].astype(BF16), vb[h], TN_DIMS, preferred_element_type=F32) for h in heads]

    for h in heads:
        num = w_in[h] * qc[h] + sv[h]
        den = w_in[h] * jnp.sum(qf[h] * n0[h], axis=-1, keepdims=True) + jnp.sum(s[h], axis=-1, keepdims=True)
        hh = num / jnp.maximum(jnp.abs(den), jnp.exp(-m_t[h]))
        y_ref[:, vs(h)] = (jax.nn.sigmoid(o_ref[:, vs(h)]) * _rms(hh, mg_ref[:, vs(h)])).astype(BF16)
    for h in heads:
        C_s[h] = w_c[h] * C0[h] + kv[h]
        n_s[h] = w_c[h] * n0[h] + jnp.sum(kw[h], axis=0, keepdims=True)
        m_s[h] = m_new[h]

    @pl.when(c == pl.num_programs(1) - 1)
    def _():
        cN_ref[...] = C_s[...]
        nN_ref[...] = n_s[...]
        mN_ref[...] = m_s[...]


def _mlstm(z3, zg3, gate_b, mnorm_g, C0, n0, m0, L, t_valid):
    N, Tp = z3.shape[:2]
    nc = Tp // L
    H = MLSTM_HEADS
    qk_w = H * MLSTM_DK
    v_w = H * MLSTM_DV
    st = lambda n, c: (n, 0, 0, 0)
    outs = pl.pallas_call(
        functools.partial(_mlstm_body, L=L, t_valid=t_valid),
        grid=(N, nc),
        in_specs=[
            pl.BlockSpec((None, L, qk_w), lambda n, c: (n, c, D_HALF // qk_w)),
            pl.BlockSpec((None, L, qk_w), lambda n, c: (n, c, D_HALF // qk_w + 1)),
            pl.BlockSpec((None, L, v_w), lambda n, c: (n, c, 2)),
            pl.BlockSpec((None, L, v_w), lambda n, c: (n, c, 3)),
            pl.BlockSpec((None, L, LANE), lambda n, c: (n, c, 0)),
            pl.BlockSpec((1, LANE), lambda n, c: (0, 0)),
            pl.BlockSpec((1, v_w), lambda n, c: (0, 0)),
            pl.BlockSpec((None, H, MLSTM_DK, MLSTM_DV), st),
            pl.BlockSpec((None, H, 1, MLSTM_DK), st),
            pl.BlockSpec((None, H, 1, 1), st),
        ],
        out_specs=[
            pl.BlockSpec((None, L, v_w), lambda n, c: (n, c, 0)),
            pl.BlockSpec((None, H, MLSTM_DK, MLSTM_DV), st),
            pl.BlockSpec((None, H, 1, MLSTM_DK), st),
            pl.BlockSpec((None, H, 1, 1), st),
        ],
        out_shape=[
            jax.ShapeDtypeStruct((N, Tp, v_w), BF16),
            jax.ShapeDtypeStruct((N, H, MLSTM_DK, MLSTM_DV), F32),
            jax.ShapeDtypeStruct((N, H, 1, MLSTM_DK), F32),
            jax.ShapeDtypeStruct((N, H, 1, 1), F32),
        ],
        scratch_shapes=[
            pltpu.VMEM((H, MLSTM_DK, MLSTM_DV), F32),
            pltpu.VMEM((H, 1, MLSTM_DK), F32),
            pltpu.VMEM((H, 1, 1), F32),
        ],
        compiler_params=_cparams("parallel", "arbitrary"),
        name="mlstm",
    )(z3, z3, z3, z3, zg3, gate_b, mnorm_g, C0, n0.reshape(N, H, 1, MLSTM_DK), m0.reshape(N, H, 1, 1))
    y, C, n, m = outs
    return y, C, n.reshape(N, H, MLSTM_DK), m.reshape(N, H)


def _gmlp_body(u_ref, v_ref, g_ref, ws_ref, bs_ref, y_ref, vn_ref, *, tT):
    v = v_ref[...]
    vc = v - jnp.mean(v, axis=-1, keepdims=True)
    vn = vc * lax.rsqrt(jnp.mean(vc * vc, axis=-1, keepdims=True) + 1e-5) * g_ref[...]
    vn_ref[...] = vn
    r_i = lax.broadcasted_iota(jnp.int32, (GMLP_CHUNK, GMLP_CHUNK), 0)
    c_i = lax.broadcasted_iota(jnp.int32, (GMLP_CHUNK, GMLP_CHUNK), 1)
    for g in range(GMLP_GROUPS):
        cs = slice(g * GMLP_DG, (g + 1) * GMLP_DG)
        wm = jnp.where(r_i >= c_i, ws_ref[g], 0.0).astype(BF16)
        bias = bs_ref[:, g:g + 1]
        for ch in range(tT // GMLP_CHUNK):
            rs = slice(ch * GMLP_CHUNK, (ch + 1) * GMLP_CHUNK)
            mix = _dot(wm, vn[rs, cs].astype(BF16)) + bias
            y_ref[rs, cs] = (u_ref[rs, cs] * mix).astype(BF16)


def _gmlp(z3, gnorm_g, ws, bs_t, tT_pref=512):
    N, Tp = z3.shape[:2]
    tT = _tile(Tp, tT_pref)
    assert tT % GMLP_CHUNK == 0
    return pl.pallas_call(
        functools.partial(_gmlp_body, tT=tT),
        grid=(N, Tp // tT),
        in_specs=[
            pl.BlockSpec((None, tT, D_HALF), lambda n, t: (n, t, 0)),
            pl.BlockSpec((None, tT, D_HALF), lambda n, t: (n, t, 1)),
            pl.BlockSpec((1, D_HALF), lambda n, t: (0, 0)),
            pl.BlockSpec((GMLP_GROUPS, GMLP_CHUNK, GMLP_CHUNK), lambda n, t: (0, 0, 0)),
            pl.BlockSpec((GMLP_CHUNK, LANE), lambda n, t: (0, 0)),
        ],
        out_specs=[
            pl.BlockSpec((None, tT, D_HALF), lambda n, t: (n, t, 0)),
            pl.BlockSpec((None, tT, D_HALF), lambda n, t: (n, t, 0)),
        ],
        out_shape=[jax.ShapeDtypeStruct((N, Tp, D_HALF), BF16), jax.ShapeDtypeStruct((N, Tp, D_HALF), F32)],
        compiler_params=_cparams("parallel", "parallel"),
        name="gmlp",
    )(z3, z3, gnorm_g, ws, bs_t)


PAGES_PER_STEP = 16
BLK_PER_PAGE = PAGE // CMP_STRIDE


def _compress_body(pt_ref, *refs):
    page_refs = refs[:PAGES_PER_STEP]
    wab_ref, a_ref, b_ref = refs[PAGES_PER_STEP:]
    wa = wab_ref[0]
    wb = wab_ref[1]
    for p, pr in enumerate(page_refs):
        x = pr[...].reshape(BLK_PER_PAGE, CMP_STRIDE, 4 * NSA_DH)
        rs = slice(p * BLK_PER_PAGE, (p + 1) * BLK_PER_PAGE)
        a_ref[rs, :] = jnp.sum(x * wa[None], axis=1)
        b_ref[rs, :] = jnp.sum(x * wb[None], axis=1)


def _compress_ab(pages, page_table, wab, col_block):
    N, n_pages = page_table.shape
    assert n_pages % PAGES_PER_STEP == 0
    W = 4 * NSA_DH
    rows = PAGES_PER_STEP * BLK_PER_PAGE
    page_specs = [
        pl.BlockSpec((None, PAGE, W), functools.partial(
            lambda n, s, pt, r: (pt[n, s * PAGES_PER_STEP + r], 0, col_block), r=r))
        for r in range(PAGES_PER_STEP)
    ]
    grid_spec = pltpu.PrefetchScalarGridSpec(
        num_scalar_prefetch=1,
        grid=(N, n_pages // PAGES_PER_STEP),
        in_specs=page_specs + [pl.BlockSpec((2, CMP_STRIDE, W), lambda n, s, pt: (0, 0, 0))],
        out_specs=[pl.BlockSpec((None, rows, W), lambda n, s, pt: (n, s, 0))] * 2,
    )
    ns = n_pages * BLK_PER_PAGE
    return pl.pallas_call(
        _compress_body,
        grid_spec=grid_spec,
        out_shape=[jax.ShapeDtypeStruct((N, ns, W), F32)] * 2,
        compiler_params=_cparams("parallel", "parallel"),
        name="nsa_compress",
    )(page_table, *([pages] * PAGES_PER_STEP), wab)


ROWS_PER_TOKEN = 2 * NSA_KV
PAGE_ROWS = PAGE * ROWS_PER_TOKEN
BLK_ROWS = CMP_STRIDE * ROWS_PER_TOKEN


def _compress_rows_body(pt_ref, *refs):
    page_refs = refs[:PAGES_PER_STEP]
    wab_ref, a_ref, b_ref, z_s = refs[PAGES_PER_STEP:]
    for p, pr in enumerate(page_refs):
        x = pr[...].reshape(BLK_PER_PAGE, BLK_ROWS, NSA_DH)
        for half, out_ref in enumerate((a_ref, b_ref)):
            y = (x * wab_ref[half][None]).reshape(BLK_PER_PAGE, BLK_ROWS // 8, 8, NSA_DH).sum(axis=1)
            y = y.reshape(BLK_PER_PAGE * 8, NSA_DH)
            slot = 2 * p + half
            z_s[slot] = y + pltpu.roll(y, BLK_PER_PAGE * 8 - ROWS_PER_TOKEN, 0)
            for c in range(ROWS_PER_TOKEN):
                out_ref[p * BLK_PER_PAGE:(p + 1) * BLK_PER_PAGE, c * NSA_DH:(c + 1) * NSA_DH] = (
                    z_s[slot, pl.ds(c, BLK_PER_PAGE, stride=8), :])


def _compress_ab_rows(rows, page_table, wab_rows):
    N, n_pages = page_table.shape
    assert n_pages % PAGES_PER_STEP == 0
    W = ROWS_PER_TOKEN * NSA_DH
    out_rows = PAGES_PER_STEP * BLK_PER_PAGE
    page_specs = [
        pl.BlockSpec((PAGE_ROWS, NSA_DH), functools.partial(
            lambda n, s, pt, r: (pt[n, s * PAGES_PER_STEP + r], 0), r=r))
        for r in range(PAGES_PER_STEP)
    ]
    grid_spec = pltpu.PrefetchScalarGridSpec(
        num_scalar_prefetch=1,
        grid=(N, n_pages // PAGES_PER_STEP),
        in_specs=page_specs + [pl.BlockSpec((2, BLK_ROWS, NSA_DH), lambda n, s, pt: (0, 0, 0))],
        out_specs=[pl.BlockSpec((None, out_rows, W), lambda n, s, pt: (n, s, 0))] * 2,
        scratch_shapes=[pltpu.VMEM((2 * PAGES_PER_STEP, BLK_PER_PAGE * 8, NSA_DH), F32)],
    )
    ns = n_pages * BLK_PER_PAGE
    return pl.pallas_call(
        _compress_rows_body,
        grid_spec=grid_spec,
        out_shape=[jax.ShapeDtypeStruct((N, ns, W), F32)] * 2,
        compiler_params=_cparams("parallel", "arbitrary"),
        name="nsa_compress_rows",
    )(page_table, *([rows] * PAGES_PER_STEP), wab_rows)


def _cmp_proj_body(a_ref, b_ref, w_ref, kc_ref, vc_ref, *, ns):
    b_next = pltpu.roll(b_ref[...], ns - 1, 0)
    row = lax.broadcasted_iota(jnp.int32, (ns, 1), 0)
    blk = jnp.where(row < ns - 1, a_ref[...] + b_next, 0.0).astype(BF16)
    w0 = w_ref[0].astype(BF16)
    w1 = w_ref[1].astype(BF16)
    for kv in range(NSA_KV):
        cs = slice(kv * NSA_DH, (kv + 1) * NSA_DH)
        kc_ref[:, cs] = _dot(blk[:, kv * NSA_DH:(kv + 1) * NSA_DH], w0)
        vc_ref[:, cs] = _dot(blk[:, (NSA_KV + kv) * NSA_DH:(NSA_KV + kv + 1) * NSA_DH], w1)


def _cmp_proj(a, b, w_cmp):
    N, ns, W = a.shape
    return pl.pallas_call(
        functools.partial(_cmp_proj_body, ns=ns),
        grid=(N,),
        in_specs=[
            pl.BlockSpec((None, ns, W), lambda n: (n, 0, 0)),
            pl.BlockSpec((None, ns, W), lambda n: (n, 0, 0)),
            pl.BlockSpec((2, NSA_DH, NSA_DH), lambda n: (0, 0, 0)),
        ],
        out_specs=[pl.BlockSpec((None, ns, NSA_KV * NSA_DH), lambda n: (n, 0, 0))] * 2,
        out_shape=[jax.ShapeDtypeStruct((N, ns, NSA_KV * NSA_DH), F32)] * 2,
        compiler_params=_cparams("parallel"),
        name="nsa_cmp_proj",
    )(a, b, w_cmp)


def _split3(x):
    hi = x.astype(BF16)
    r1 = x - hi.astype(F32)
    mid = r1.astype(BF16)
    lo = (r1 - mid.astype(F32)).astype(BF16)
    return hi, mid, lo


def _cmp_to_sel(p_sum, a_mat):
    hi, mid, lo = _split3(p_sum)
    return _dot(hi, a_mat) + _dot(mid, a_mat) + _dot(lo, a_mat)


def _topk_mask(score, n_valid, k):
    lane = lax.broadcasted_iota(jnp.int32, score.shape, 1)
    rank = jnp.zeros(score.shape, F32)
    for jp in range(n_valid):
        col = score[:, jp:jp + 1]
        beats = (col > score) | ((col == score) & (lane > jp))
        rank = rank + beats.astype(F32)
    return ((rank < k) & (lane < n_valid)).astype(F32)


def _sel_scores(imp, qblk, n_valid):
    lane = lax.broadcasted_iota(jnp.int32, imp.shape, 1)
    forced = (lane == 0) | (lane == qblk) | (lane == qblk - 1)
    score = jnp.where(forced, FORCE_SCORE, jnp.where(lane > qblk, -1.0, imp))
    return jnp.where(lane < n_valid, score, -2.0)


def _masked_softmax_rows(s, mask):
    s = jnp.where(mask, s, NEG)
    e = jnp.exp(s - jnp.max(s, axis=-1, keepdims=True))
    return jnp.where(mask, e / jnp.sum(e, axis=-1, keepdims=True), 0.0)


def _stack_heads(q, scale):
    return (jnp.concatenate([q[:, g * NSA_DH:(g + 1) * NSA_DH] for g in range(NSA_G)], axis=0) * scale).astype(BF16)


SEL_TK = 512
WIN_KEYS = WINDOW + QBLK


def _nsa_prompt_body(q_ref, zg_ref, gb_ref, kc_ref, vc_ref, ks_ref, vs_ref, kw_ref, vw_ref, amat_t_ref,
                     o_ref, ks_bf, kw_bf, vs_t, vw_t, kc_bf, vc_t, sc_ref, *, n_sel, nsr):
    qb = pl.program_id(1)
    T = ks_ref.shape[0]
    kvc = lambda kv: slice(kv * NSA_DH, (kv + 1) * NSA_DH)

    @pl.when(qb == 0)
    def _():
        ks_bf[...] = ks_ref[...].astype(BF16)
        kw_bf[...] = kw_ref[...].astype(BF16)
        kc_bf[...] = kc_ref[...].astype(BF16)
        vc_t[...] = vc_ref[...].T.astype(BF16)

        def transpose_values(i, c):
            r0 = pl.multiple_of(i * LANE, LANE)
            vs_t[i] = vs_ref[pl.ds(r0, LANE), :].T.astype(BF16)
            vw_t[i] = vw_ref[pl.ds(r0, LANE), :].T.astype(BF16)
            return c

        lax.fori_loop(0, T // LANE, transpose_values, 0)

    R = QBLK
    start = qb * R
    groups = range(NSA_KV)
    tile4 = lambda x: jnp.concatenate([x] * NSA_G, axis=1)
    q_t = [jnp.concatenate([(q_ref[:, (kv * NSA_G + g) * NSA_DH:(kv * NSA_G + g + 1) * NSA_DH] * (NSA_DH ** -0.5)).T
                            for g in range(NSA_G)], axis=1).astype(BF16) for kv in groups]
    pos = start + lax.broadcasted_iota(jnp.int32, (1, R), 1)

    ns = kc_bf.shape[0]
    cmp_end = (lax.broadcasted_iota(jnp.int32, (ns, R), 0) + 2) * CMP_STRIDE - 1
    bias_c = tile4(jnp.where(cmp_end <= pos, 0.0, NEG))
    any_c = tile4(jnp.where(pos >= 2 * CMP_STRIDE - 1, 1.0, 0.0))
    a_t = amat_t_ref[...]
    blk = lax.broadcasted_iota(jnp.int32, (nsr, R), 0)
    qblk = (start + lax.broadcasted_iota(jnp.int32, (nsr, R), 1)) // SEL_BLOCK
    forced = (blk == 0) | (blk == qblk) | (blk == qblk - 1)
    o_cmp, score = [], []
    for kv in groups:
        s_c = _dot(kc_bf[:, kvc(kv)], q_t[kv]) + bias_c
        e_c = jnp.exp(s_c - jnp.max(s_c, axis=0, keepdims=True))
        p_c = e_c * (any_c / jnp.sum(e_c, axis=0, keepdims=True))
        o_cmp.append(_dot(vc_t[kvc(kv), :], p_c.astype(BF16)))
        p_sum = p_c[:, 0:R] + p_c[:, R:2 * R] + p_c[:, 2 * R:3 * R] + p_c[:, 3 * R:4 * R]
        hi, mid, lo = _split3(p_sum)
        imp_t = _dot(a_t, hi) + _dot(a_t, mid) + _dot(a_t, lo)
        sc = jnp.where(forced, FORCE_SCORE, jnp.where(blk > qblk, -1.0, imp_t))
        score.append(jnp.where(blk < n_sel, sc, -2.0))
        sc_ref[kv] = score[kv]

    def rank_step(i, ranks):
        ranks = list(ranks)
        for jp in (2 * i, 2 * i + 1):
            for kv in groups:
                row = sc_ref[kv, pl.ds(jp, 1), :]
                beats = (row > score[kv]) | ((row == score[kv]) & (blk > jp))
                ranks[kv] = ranks[kv] + beats.astype(F32)
        return tuple(ranks)

    n_rank = jnp.minimum(start // SEL_BLOCK + QBLK // SEL_BLOCK, n_sel)
    ranks = lax.fori_loop(0, (n_rank + 1) // 2, rank_step, tuple(jnp.zeros((nsr, R), F32) for _ in groups))
    for kv in groups:
        sc_ref[kv] = jnp.where(ranks[kv] < N_SEL, 0.0, NEG)

    def values_product(v_ref, tile0, n_sub, kv, p):
        out = _dot(v_ref[tile0, kvc(kv), :], p[0:LANE])
        for i in range(1, n_sub):
            out = out + _dot(v_ref[tile0 + i, kvc(kv), :], p[i * LANE:(i + 1) * LANE])
        return out

    key_tk = lax.broadcasted_iota(jnp.int32, (SEL_TK, R), 0)

    def sel_tile(kt, carry, causal):
        k0 = pl.multiple_of(kt * SEL_TK, SEL_TK)
        scores = [_dot(ks_bf[pl.ds(k0, SEL_TK), kvc(kv)], q_t[kv]) for kv in groups]
        stats = []
        for kv in groups:
            m_prev, l_prev, _ = carry[kv]
            bias = jnp.concatenate(
                [jnp.broadcast_to(sc_ref[kv, pl.ds(kt * (SEL_TK // SEL_BLOCK) + b, 1), :], (SEL_BLOCK, R))
                 for b in range(SEL_TK // SEL_BLOCK)], axis=0)
            if causal:
                bias = jnp.where(k0 + key_tk <= pos, bias, NEG)
            s = scores[kv] + tile4(bias)
            m_new = jnp.maximum(m_prev, jnp.max(s, axis=0, keepdims=True))
            alpha = jnp.exp(m_prev - m_new)
            p = jnp.exp(s - m_new)
            stats.append((m_new, alpha, alpha * l_prev + jnp.sum(p, axis=0, keepdims=True), p.astype(BF16)))
        return tuple((m_new, l_new, alpha * carry[kv][2]
                      + values_product(vs_t, kt * (SEL_TK // LANE), SEL_TK // LANE, kv, p))
                     for kv, (m_new, alpha, l_new, p) in enumerate(stats))

    n_tiles = (start + R - 1) // SEL_TK + 1
    init = tuple((jnp.full((1, NSA_G * R), NEG, F32), jnp.zeros((1, NSA_G * R), F32),
                  jnp.zeros((NSA_DH, NSA_G * R), F32)) for _ in groups)
    carry = lax.fori_loop(0, n_tiles - 1, lambda kt, c: sel_tile(kt, c, False), init)
    o_sel = [acc / l for _, l, acc in sel_tile(n_tiles - 1, carry, True)]

    w0 = pl.multiple_of(jnp.maximum(start - WINDOW, 0), QBLK)
    key_w = w0 + lax.broadcasted_iota(jnp.int32, (WIN_KEYS, R), 0)
    bias_w = tile4(jnp.where((key_w <= pos) & (key_w > pos - WINDOW), 0.0, NEG))
    gates_t = jax.nn.sigmoid(zg_ref[...] + gb_ref[...]).T
    for kv in groups:
        s_w = _dot(kw_bf[pl.ds(w0, WIN_KEYS), kvc(kv)], q_t[kv]) + bias_w
        e_w = jnp.exp(s_w - jnp.max(s_w, axis=0, keepdims=True))
        o_win = (values_product(vw_t, w0 // LANE, WIN_KEYS // LANE, kv, e_w.astype(BF16))
                 / jnp.sum(e_w, axis=0, keepdims=True))
        gate = lambda branch: jnp.concatenate(
            [gates_t[branch * NSA_HEADS + kv * NSA_G + g:branch * NSA_HEADS + kv * NSA_G + g + 1, :]
             for g in range(NSA_G)], axis=1)
        out_t = gate(0) * o_cmp[kv] + gate(1) * o_sel[kv] + gate(2) * o_win
        for g in range(NSA_G):
            h = kv * NSA_G + g
            o_ref[:, h * NSA_DH:(h + 1) * NSA_DH] = out_t[:, g * R:(g + 1) * R].T.astype(BF16)


def _sel_map(ns, n_sel, nsb):
    i = np.arange(ns)[:, None]
    j = np.arange(nsb)[None, :]
    r = SEL_BLOCK // CMP_STRIDE
    return jnp.asarray(((i >= r * j - 1) & (i <= r * j + r - 1) & (j < n_sel)).astype(np.float32), BF16)


def _nsa_prompt(z3, zg3, gate_b, kc, vc):
    N, T = z3.shape[:2]
    assert T % SEL_TK == 0 and T >= WIN_KEYS
    ns = kc.shape[1]
    n_sel = T // SEL_BLOCK
    nsr = -(-n_sel // 8) * 8
    assert nsr <= LANE
    amat_t = _sel_map(ns, n_sel, nsr).T
    KVW = NSA_KV * NSA_DH
    qcol = 2 * D_HALF // (NSA_HEADS * NSA_DH)
    kvs_col = (3 * D_HALF + 2 * KVW) // KVW
    kvw_col = kvs_col + 2
    full = lambda off: pl.BlockSpec((None, T, KVW), lambda n, qb: (n, 0, off))
    return pl.pallas_call(
        functools.partial(_nsa_prompt_body, n_sel=n_sel, nsr=nsr),
        grid=(N, T // QBLK),
        in_specs=[
            pl.BlockSpec((None, QBLK, NSA_HEADS * NSA_DH), lambda n, qb: (n, qb, qcol)),
            pl.BlockSpec((None, QBLK, LANE), lambda n, qb: (n, qb, 0)),
            pl.BlockSpec((1, LANE), lambda n, qb: (0, 0)),
            pl.BlockSpec((None, ns, KVW), lambda n, qb: (n, 0, 0)),
            pl.BlockSpec((None, ns, KVW), lambda n, qb: (n, 0, 0)),
            full(kvs_col), full(kvs_col + 1), full(kvw_col), full(kvw_col + 1),
            pl.BlockSpec((nsr, ns), lambda n, qb: (0, 0)),
        ],
        out_specs=pl.BlockSpec((None, QBLK, NSA_HEADS * NSA_DH), lambda n, qb: (n, qb, 0)),
        out_shape=jax.ShapeDtypeStruct((N, T, NSA_HEADS * NSA_DH), BF16),
        scratch_shapes=[
            pltpu.VMEM((T, KVW), BF16), pltpu.VMEM((T, KVW), BF16),
            pltpu.VMEM((T // LANE, KVW, LANE), BF16), pltpu.VMEM((T // LANE, KVW, LANE), BF16),
            pltpu.VMEM((ns, KVW), BF16), pltpu.VMEM((KVW, ns), BF16),
            pltpu.VMEM((NSA_KV, nsr, QBLK), F32),
        ],
        compiler_params=_cparams("parallel", "arbitrary"),
        name="nsa_prompt",
    )(z3, zg3, gate_b, kc, vc, z3, z3, z3, z3, amat_t)


def _nsa_sample_a_body(q_ref, kc_ref, vc_ref, kw_ref, vw_ref, amat_ref, ocmp_ref, owin_ref, sel_ref,
                       *, Tq, past_len, n_sel, wb):
    qs = _stack_heads(q_ref[...], NSA_DH ** -0.5)
    pos = past_len + lax.broadcasted_iota(jnp.int32, (Tq, 1), 0)
    pos4 = jnp.concatenate([pos] * NSA_G, axis=0)

    ns = kc_ref.shape[0]
    cmp_i = lax.broadcasted_iota(jnp.int32, (NSA_G * Tq, ns), 1)
    m_c = ((cmp_i + 2) * CMP_STRIDE - 1 <= pos4) & (cmp_i < ns - 1)
    p_c = _masked_softmax_rows(_dot_nt(qs, kc_ref[...].astype(BF16)), m_c)
    ocmp_ref[...] = _dot(p_c.astype(BF16), vc_ref[...].astype(BF16))
    p_sum = p_c[0:Tq] + p_c[Tq:2 * Tq] + p_c[2 * Tq:3 * Tq] + p_c[3 * Tq:4 * Tq]
    imp = _cmp_to_sel(p_sum, amat_ref[...])
    sel_ref[...] = _topk_mask(_sel_scores(imp, pos // SEL_BLOCK, n_sel), n_sel, N_SEL)

    nw = kw_ref.shape[0]
    tok_w = past_len - wb + lax.broadcasted_iota(jnp.int32, (NSA_G * Tq, nw), 1)
    m_w = (tok_w >= 0) & (tok_w <= pos4) & (tok_w > pos4 - WINDOW)
    p_w = _masked_softmax_rows(_dot_nt(qs, kw_ref[...].astype(BF16)), m_w)
    owin_ref[...] = _dot(p_w.astype(BF16), vw_ref[...].astype(BF16))


def _nsa_sample_a(z3, kc, vc, kw_full, past_len, wb):
    N, Tq = z3.shape[:2]
    ns = kc.shape[1]
    nw = kw_full.shape[1]
    n_sel = -(-(past_len + Tq) // SEL_BLOCK)
    nsb = -(-n_sel // LANE) * LANE
    amat = _sel_map(ns, n_sel, nsb)
    qcol = 2 * D_HALF // (NSA_G * NSA_DH)
    R4 = NSA_G * Tq
    return pl.pallas_call(
        functools.partial(_nsa_sample_a_body, Tq=Tq, past_len=past_len, n_sel=n_sel, wb=wb),
        grid=(N, NSA_KV),
        in_specs=[
            pl.BlockSpec((None, Tq, NSA_G * NSA_DH), lambda n, kv: (n, 0, qcol + kv)),
            pl.BlockSpec((None, ns, NSA_DH), lambda n, kv: (n, 0, kv)),
            pl.BlockSpec((None, ns, NSA_DH), lambda n, kv: (n, 0, kv)),
            pl.BlockSpec((None, nw, NSA_DH), lambda n, kv: (n, 0, kv)),
            pl.BlockSpec((None, nw, NSA_DH), lambda n, kv: (n, 0, NSA_KV + kv)),
            pl.BlockSpec((ns, nsb), lambda n, kv: (0, 0)),
        ],
        out_specs=[
            pl.BlockSpec((None, None, R4, NSA_DH), lambda n, kv: (n, kv, 0, 0)),
            pl.BlockSpec((None, None, R4, NSA_DH), lambda n, kv: (n, kv, 0, 0)),
            pl.BlockSpec((None, None, Tq, nsb), lambda n, kv: (n, kv, 0, 0)),
        ],
        out_shape=[
            jax.ShapeDtypeStruct((N, NSA_KV, R4, NSA_DH), F32),
            jax.ShapeDtypeStruct((N, NSA_KV, R4, NSA_DH), F32),
            jax.ShapeDtypeStruct((N, NSA_KV, Tq, nsb), F32),
        ],
        compiler_params=_cparams("parallel", "parallel"),
        name="nsa_sample_cmp_win",
    )(z3, kc, vc, kw_full, kw_full, amat)


BLK_PER_STEP = PAGES_PER_STEP * PAGE // SEL_BLOCK


def _nsa_sample_b_body(pt_ref, *refs, Tq, past_len):
    page_refs = refs[:PAGES_PER_STEP]
    (q_ref, new_ref, sel_ref, exp_ref, ocmp_ref, owin_ref, zg_ref, gb_ref, o_ref, m_s, l_s, acc_s) = refs[PAGES_PER_STEP:]
    s_id = pl.program_id(1)
    R4 = NSA_G * Tq
    rep = lambda x: jnp.concatenate([x] * NSA_G, axis=0)
    pos = past_len + lax.broadcasted_iota(jnp.int32, (Tq, 1), 0)

    @pl.when(s_id == 0)
    def _():
        m_s[...] = jnp.full(m_s.shape, NEG, F32)
        l_s[...] = jnp.zeros(l_s.shape, F32)
        acc_s[...] = jnp.zeros(acc_s.shape, F32)

    def update(kv, s, mask, pv):
        s = jnp.where(mask, s, NEG)
        m_prev = m_s[kv]
        m_new = jnp.maximum(m_prev, jnp.max(s, axis=-1, keepdims=True))
        alpha = jnp.exp(m_prev - m_new)
        p = jnp.where(mask, jnp.exp(s - m_new), 0.0)
        l_s[kv] = alpha * l_s[kv] + jnp.sum(p, axis=-1, keepdims=True)
        acc_s[kv] = alpha * acc_s[kv] + pv(p.astype(BF16))
        m_s[kv] = m_new

    qs = [_stack_heads(q_ref[:, kv * NSA_G * NSA_DH:(kv + 1) * NSA_G * NSA_DH], NSA_DH ** -0.5) for kv in range(NSA_KV)]
    comp = lambda pr, c: pr[pl.ds(c, PAGE, stride=ROWS_PER_TOKEN), :].astype(BF16)
    for kv in range(NSA_KV):
        s = jnp.concatenate([_dot_nt(qs[kv], comp(pr, kv)) for pr in page_refs], axis=1)
        mask = rep(_dot(sel_ref[kv].astype(BF16), exp_ref[...])) > 0.5

        def pv(p, kv=kv):
            out = _dot(p[:, 0:PAGE], comp(page_refs[0], NSA_KV + kv))
            for i in range(1, PAGES_PER_STEP):
                out = out + _dot(p[:, i * PAGE:(i + 1) * PAGE], comp(page_refs[i], NSA_KV + kv))
            return out

        update(kv, s, mask, pv)

    @pl.when(s_id == pl.num_programs(1) - 1)
    def _():
        new = new_ref[...].astype(BF16)
        tok_n = past_len + lax.broadcasted_iota(jnp.int32, (R4, Tq), 1)
        gates = jax.nn.sigmoid(zg_ref[...] + gb_ref[...])
        for kv in range(NSA_KV):
            k_new = new[:, kv * NSA_DH:(kv + 1) * NSA_DH]
            v_new = new[:, (NSA_KV + kv) * NSA_DH:(NSA_KV + kv + 1) * NSA_DH]
            update(kv, _dot_nt(qs[kv], k_new), tok_n <= rep(pos), lambda p, v_new=v_new: _dot(p, v_new))
            o_sel = acc_s[kv] / l_s[kv]
            o_cmp = ocmp_ref[kv]
            o_win = owin_ref[kv]
            for g in range(NSA_G):
                rs = slice(g * Tq, (g + 1) * Tq)
                head = kv * NSA_G + g
                o_ref[:, head * NSA_DH:(head + 1) * NSA_DH] = (
                    gates[:, head:head + 1] * o_cmp[rs]
                    + gates[:, NSA_HEADS + head:NSA_HEADS + head + 1] * o_sel[rs]
                    + gates[:, 2 * NSA_HEADS + head:2 * NSA_HEADS + head + 1] * o_win[rs])


def _nsa_sample_b(z3, zg3, gate_b, pool_sel, page_table, sel, o_cmp, o_win, past_len):
    N, Tq = z3.shape[:2]
    n_pages = page_table.shape[1]
    assert n_pages % PAGES_PER_STEP == 0 and past_len % SEL_BLOCK == 0
    n_steps = n_pages // PAGES_PER_STEP
    sel_steps = sel[..., :past_len // SEL_BLOCK].reshape(N, NSA_KV, Tq, n_steps, BLK_PER_STEP).transpose(0, 3, 1, 2, 4)
    tok = np.arange(PAGES_PER_STEP * PAGE)[None, :]
    expand = jnp.asarray((tok // SEL_BLOCK == np.arange(BLK_PER_STEP)[:, None]).astype(np.float32), BF16)
    W = 4 * NSA_DH
    R4 = NSA_G * Tq
    qcol = 2 * D_HALF // (NSA_HEADS * NSA_DH)
    kvs_col = (3 * D_HALF + W) // W
    page_specs = [
        pl.BlockSpec((PAGE_ROWS, NSA_DH), functools.partial(
            lambda n, s, pt, r: (pt[n, s * PAGES_PER_STEP + r], 0), r=r))
        for r in range(PAGES_PER_STEP)
    ]
    per_n4 =lambda shape: pl.BlockSpec((None,) + shape, lambda n, s, pt: (n, 0, 0, 0))
    grid_spec = pltpu.PrefetchScalarGridSpec(
        num_scalar_prefetch=1,
        grid=(N, n_pages // PAGES_PER_STEP),
        in_specs=page_specs + [
            pl.BlockSpec((None, Tq, NSA_HEADS * NSA_DH), lambda n, s, pt: (n, 0, qcol)),
            pl.BlockSpec((None, Tq, W), lambda n, s, pt: (n, 0, kvs_col)),
            pl.BlockSpec((None, None, NSA_KV, Tq, BLK_PER_STEP), lambda n, s, pt: (n, s, 0, 0, 0)),
            pl.BlockSpec((BLK_PER_STEP, PAGES_PER_STEP * PAGE), lambda n, s, pt: (0, 0)),
            per_n4((NSA_KV, R4, NSA_DH)),
            per_n4((NSA_KV, R4, NSA_DH)),
            pl.BlockSpec((None, Tq, LANE), lambda n, s, pt: (n, 0, 0)),
            pl.BlockSpec((1, LANE), lambda n, s, pt: (0, 0)),
        ],
        out_specs=pl.BlockSpec((None, Tq, NSA_HEADS * NSA_DH), lambda n, s, pt: (n, 0, 0)),
        scratch_shapes=[
            pltpu.VMEM((NSA_KV, R4, 1), F32),
            pltpu.VMEM((NSA_KV, R4, 1), F32),
            pltpu.VMEM((NSA_KV, R4, NSA_DH), F32),
        ],
    )
    return pl.pallas_call(
        functools.partial(_nsa_sample_b_body, Tq=Tq, past_len=past_len),
        grid_spec=grid_spec,
        out_shape=jax.ShapeDtypeStruct((N, Tq, NSA_HEADS * NSA_DH), F32),
        compiler_params=_cparams("parallel", "arbitrary"),
        name="nsa_sample_sel",
    )(page_table, *([pool_sel] * PAGES_PER_STEP), z3, z3, sel_steps, expand, o_cmp, o_win, zg3, gate_b)


MLSTM_CHUNK_PROMPT = 256
SAMPLE_PAD = 128


def _pad_rows(x, rows):
    return jnp.pad(x, ((0, 0), (0, rows - x.shape[1]), (0, 0)))


def _lane_pad(v):
    return jnp.pad(v.astype(F32), (0, LANE - v.shape[0])).reshape(1, LANE)


def _gate_rows(w_in_t, n_main):
    return jnp.pad(w_in_t[n_main:], ((0, LANE - (w_in_t.shape[0] - n_main)), (0, 0)))


def _even_mixers(z3, zg3, t0, pool_buf, C0, n0, m0, w):
    N, T = z3.shape[:2]
    st = jnp.pad(pool_buf, ((0, 0), (HALO - POOL_PAD, 0), (0, 0)))
    y_a = _pool_mix(z3, st, w["pool_w"], w["pool_scale"], t0)
    u_ext_tail = jnp.concatenate([pool_buf, z3[:, :, :D_HALF]], axis=1)[:, -POOL_PAD:] if T < POOL_PAD else z3[:, -POOL_PAD:, :D_HALF]

    if T % MLSTM_CHUNK_PROMPT == 0:
        L, zm, zgm = MLSTM_CHUNK_PROMPT, z3, zg3
    else:
        L, zm, zgm = SAMPLE_PAD, _pad_rows(z3, SAMPLE_PAD), _pad_rows(zg3, SAMPLE_PAD)
    y_b, C, n, m = _mlstm(zm, zgm, w["gate_b"], w["mnorm_g"], C0, n0, m0, L, min(T, L))
    return y_a, y_b[:, :T], u_ext_tail, C, n, m


def _odd_mixers(z3, zg3, w, past=None):
    N, T, n_main = z3.shape
    W = 4 * NSA_DH
    kvc = z3[:, :, 3 * D_HALF:3 * D_HALF + W]
    kvs = z3[:, :, 3 * D_HALF + W:3 * D_HALF + 2 * W]
    kvw = z3[:, :, 3 * D_HALF + 2 * W:3 * D_HALF + 3 * W]

    if past is None:
        y_c, _ = _gmlp(z3, w["gnorm_g"], w["ws"], w["bs_t"])
        vn = None
        pt = jnp.arange(N * (T // PAGE), dtype=jnp.int32).reshape(N, T // PAGE)
        a, b = _compress_ab(z3.reshape(N * (T // PAGE), PAGE, n_main), pt, w["wab"], 3 * D_HALF // W)
        kc, vc = _cmp_proj(a, b, w["cmp_w"])
        o = _nsa_prompt(z3, zg3, w["nsa_gate_b"], kc, vc)
        win_state = kvw[:, -min(WINDOW, T):]
    else:
        win_buf, pool_cmp, pool_sel, page_table = past
        past_len = page_table.shape[1] * PAGE
        wb = win_buf.shape[1]
        y_c, vn = _gmlp(_pad_rows(z3[:, :, :2 * D_HALF], SAMPLE_PAD), w["gnorm_g"], w["ws"], w["bs_t"])
        y_c, vn = y_c[:, :T], vn[:, :T]
        assert (past_len + T) // CMP_STRIDE == past_len // CMP_STRIDE
        a, b = _compress_ab_rows(pool_cmp, page_table, w["wab_rows"])
        kc, vc = _cmp_proj(a, b, w["cmp_w"])
        kw_all = jnp.concatenate([win_buf.reshape(N, wb, W), kvw], axis=1)
        nw = -(-(wb + T) // LANE) * LANE
        o_cmp, o_win, sel = _nsa_sample_a(z3, kc, vc, _pad_rows(kw_all, nw), past_len, wb)
        o = _nsa_sample_b(z3, zg3, w["nsa_gate_b"], pool_sel, page_table, sel, o_cmp, o_win, past_len)
        win_state = kw_all[:, -wb:]

    kv5 = lambda t: t.reshape(N, t.shape[1], 2, NSA_KV, NSA_DH)
    return y_c, o, vn, kv5(kvc), kv5(kvs), kv5(win_state)


def _layer(xp, xs, xpn, xsn, w, ffn_stacks, layer, n_main, mixers_p, mixers_s):
    B, T, D = xp.shape
    Ns, Ts, _ = xs.shape
    xp2, xs2 = xp.reshape(B * T, D), xs.reshape(Ns * Ts, D)
    zp, zgp, zs, zgs = _in_proj(xpn, xsn, w["w_in"], w["w_in_layer"], n_main, w["w_gate"])
    a1p, a2p, *extra_p = mixers_p(zp.reshape(B, T, n_main), zgp.reshape(B, T, LANE))
    a1s, a2s, *extra_s = mixers_s(zs.reshape(Ns, Ts, n_main), zgs.reshape(Ns, Ts, LANE))
    flat = lambda t: t.reshape(-1, D_HALF)
    xp2, xpn, xs2, xsn = _out_proj(flat(a1p), flat(a2p), xp2, flat(a1s), flat(a2s), xs2, w["w_out_a"], w["w_out_b"],
                                   w["ng"][1:2], w["ng"][2:3])
    w1_stack, w3_stack, w2_stack = ffn_stacks
    hp, hs, w2_bf = _ffn_up(xpn, xsn, w1_stack, w3_stack, w2_stack, layer)
    xp2, xs2, xpn, xsn = _ffn_down(hp, hs, xp2, xs2, w["ng"][3:4], w["ng_next"], w2_bf)
    return xp2.reshape(B, T, D), xs2.reshape(Ns, Ts, D), xpn, xsn, extra_p, extra_s


def kernel(x_prompt, x_sample, state_pool, state_mlstm_c, state_mlstm_n, state_mlstm_m, state_win_kv, cache_cmp_kv, cache_sel_kv, page_table, norm_g, w_in_even, w_out_even, pool_w, pool_scale, mlstm_gate_b, mlstm_norm_g, w_in_odd, w_out_odd, gmlp_norm_g, gmlp_ws, gmlp_bs, nsa_cmp_pos, nsa_cmp_w, nsa_gate_b, ffn_w1, ffn_w3, ffn_w2):
    B = x_prompt.shape[0]
    depth = norm_g.shape[0]
    past_len = page_table.shape[1] * PAGE
    xp, xs = x_prompt, x_sample
    pool_p, pool_s, c_p, c_s, n_p, n_s, m_p, m_s = [], [], [], [], [], [], [], []
    gv_s, cmp_p, cmp_s, sel_p, sel_s, win_p, win_s = [], [], [], [], [], [], []
    ffn_stacks = (ffn_w1, ffn_w3, ffn_w2)
    D = x_prompt.shape[-1]
    xpn, xsn = _rms_cast(x_prompt.reshape(-1, D), x_sample.reshape(-1, D), norm_g[0][0:1])
    for l in range(depth):
        j = l // 2
        ng_next = norm_g[min(l + 1, depth - 1)][0:1]
        if l % 2 == 0:
            n_main = 4 * D_HALF
            w_out = w_out_even[j].astype(BF16)
            w_in_t = jnp.swapaxes(w_in_even, 1, 2)
            w = dict(ng=norm_g[l], ng_next=ng_next, w_in=w_in_t, w_in_layer=j, w_gate=_gate_rows(w_in_t[j], n_main),
                     w_out_a=w_out[:D_HALF], w_out_b=w_out[D_HALF:],
                     pool_w=pool_w[j].astype(BF16), pool_scale=pool_scale[j].reshape(1, D_HALF),
                     gate_b=_lane_pad(mlstm_gate_b[j].reshape(-1)), mnorm_g=mlstm_norm_g[j].reshape(1, D_HALF))
            zp = jnp.zeros((B, POOL_PAD, D_HALF), F32)
            zc = jnp.zeros((B, MLSTM_HEADS, MLSTM_DK, MLSTM_DV), F32)
            zn = jnp.zeros((B, MLSTM_HEADS, MLSTM_DK), F32)
            zm = jnp.zeros((B, MLSTM_HEADS), F32)
            xp, xs, xpn, xsn, (pb, c, n, m), (pbs, cs, ns_, ms) = _layer(
                xp, xs, xpn, xsn, w, ffn_stacks, l, n_main,
                lambda z3, zg3: _even_mixers(z3, zg3, 0, zp, zc, zn, zm, w),
                lambda z3, zg3: _even_mixers(z3, zg3, past_len, state_pool[j], state_mlstm_c[j], state_mlstm_n[j],
                                             state_mlstm_m[j], w))
            pool_p.append(pb); c_p.append(c); n_p.append(n); m_p.append(m)
            pool_s.append(pbs); c_s.append(cs); n_s.append(ns_); m_s.append(ms)
        else:
            n_main = 4 * D_HALF + D_HALF // 2
            w_out = w_out_odd[j].astype(BF16)
            cp = nsa_cmp_pos[j]
            wcol = jnp.repeat(cp, NSA_KV * NSA_DH, axis=1)
            wab = jnp.stack([wcol[:CMP_STRIDE], wcol[CMP_STRIDE:]])
            wab_rows = jnp.broadcast_to(jnp.repeat(cp, NSA_KV, axis=1).reshape(2, BLK_ROWS, 1), (2, BLK_ROWS, NSA_DH))
            w_in_t = jnp.swapaxes(w_in_odd, 1, 2)
            w = dict(ng=norm_g[l], ng_next=ng_next, w_in=w_in_t, w_in_layer=j, w_gate=_gate_rows(w_in_t[j], n_main),
                     w_out_a=w_out[:D_HALF], w_out_b=w_out[D_HALF:],
                     gnorm_g=gmlp_norm_g[j].reshape(1, D_HALF), ws=gmlp_ws[j],
                     bs_t=jnp.pad(gmlp_bs[j].T, ((0, 0), (0, LANE - GMLP_GROUPS))),
                     wab=wab, wab_rows=wab_rows, cmp_w=nsa_cmp_w[j], nsa_gate_b=_lane_pad(nsa_gate_b[j]))
            n_pool = cache_cmp_kv.shape[1]
            flat = lambda c: c.reshape(c.shape[0] * n_pool * PAGE_ROWS, NSA_DH)
            past = (state_win_kv[j], flat(cache_cmp_kv), flat(cache_sel_kv), page_table + j * n_pool)
            xp, xs, xpn, xsn, (_, kc, ksl, wv), (vn, kcs, ksls, wvs) = _layer(
                xp, xs, xpn, xsn, w, ffn_stacks, l, n_main,
                lambda z3, zg3: _odd_mixers(z3, zg3, w),
                lambda z3, zg3: _odd_mixers(z3, zg3, w, past=past))
            cmp_p.append(kc); sel_p.append(ksl); win_p.append(wv)
            gv_s.append(vn); cmp_s.append(kcs); sel_s.append(ksls); win_s.append(wvs)
    st = jnp.stack
    return (xp, xs, st(pool_p), st(pool_s), st(c_p), st(c_s), st(n_p), st(n_s), st(m_p), st(m_s),
            st(gv_s), st(cmp_p), st(cmp_s), st(sel_p), st(sel_s), st(win_p), st(win_s))
```

```python
import functools
import math

import numpy as np
import jax
import jax.numpy as jnp
from jax import lax
from jax.experimental import pallas as pl
from jax.experimental.pallas import tpu as pltpu

F32 = jnp.float32
BF16 = jnp.bfloat16
NEG = -1e30

D_HALF = 1024
POOL_WINDOWS = (2, 4, 8, 16)
POOL_DG = 256
POOL_PAD = 15
HALO = 16
MLSTM_HEADS = 4
MLSTM_DK = 128
MLSTM_DV = 256
GATE_CAP = 15.0
GMLP_CHUNK = 128
GMLP_GROUPS = 4
GMLP_DG = 256
NSA_HEADS = 8
NSA_DH = 128
NSA_KV = 2
NSA_G = 4
CMP_STRIDE = 16
SEL_BLOCK = 64
N_SEL = 16
WINDOW = 512
QBLK = 128
PAGE = 128
FORCE_SCORE = 1e9
LANE = 128
VMEM_LIMIT = 56 * 1024 * 1024

NT_DIMS = (((1,), (1,)), ((), ()))
TN_DIMS = (((0,), (0,)), ((), ()))


def _cparams(*sem):
    return pltpu.CompilerParams(dimension_semantics=sem, vmem_limit_bytes=VMEM_LIMIT)


def _tile(m, pref):
    if m <= pref:
        return m
    for t in range(pref, 7, -1):
        if m % t == 0 and t % 8 == 0:
            return t
    return m


def _dot(a, b):
    return jnp.dot(a, b, preferred_element_type=F32)


def _dot_nt(a, b):
    return lax.dot_general(a, b, NT_DIMS, preferred_element_type=F32)


def _rms(x, g, eps=1e-6):
    return x * lax.rsqrt(jnp.mean(x * x, axis=-1, keepdims=True) + eps) * g


def _rms_cast_body(x_ref, xs_ref, g_ref, o_ref, os_ref):
    o_ref[...] = _rms(x_ref[...], g_ref[...]).astype(BF16)

    @pl.when(pl.program_id(0) == 0)
    def _():
        os_ref[...] = _rms(xs_ref[...], g_ref[...]).astype(BF16)


def _rms_cast(x, xs, g, tm_pref=1024):
    M, K = x.shape
    Ms = xs.shape[0]
    tm = _tile(M, tm_pref)
    return pl.pallas_call(
        _rms_cast_body,
        grid=(M // tm,),
        in_specs=[pl.BlockSpec((tm, K), lambda i: (i, 0)), pl.BlockSpec((Ms, K), lambda i: (0, 0)),
                  pl.BlockSpec((1, K), lambda i: (0, 0))],
        out_specs=[pl.BlockSpec((tm, K), lambda i: (i, 0)), pl.BlockSpec((Ms, K), lambda i: (0, 0))],
        out_shape=[jax.ShapeDtypeStruct((M, K), BF16), jax.ShapeDtypeStruct((Ms, K), BF16)],
        compiler_params=_cparams("arbitrary"),
        name="rms_cast",
    )(x, xs, g)


def _in_proj_body(xn_ref, xsn_ref, w_ref, wg_ref, o_ref, og_ref, os_ref, ogs_ref, w_bf):
    j = pl.program_id(0)
    i = pl.program_id(1)

    @pl.when(i == 0)
    def _():
        w_bf[...] = w_ref[...].astype(BF16)
        os_ref[...] = _dot_nt(xsn_ref[...], w_bf[...])

    o_ref[...] = _dot_nt(xn_ref[...], w_bf[...])

    @pl.when(j == 0)
    def _():
        wg = wg_ref[...].astype(BF16)
        og_ref[...] = _dot_nt(xn_ref[...], wg)

        @pl.when(i == 0)
        def _():
            ogs_ref[...] = _dot_nt(xsn_ref[...], wg)


def _in_proj(xn, xsn, wt_stack, layer, n_main, wg_t, tm_pref=1024, tn_pref=1152):
    M, K = xn.shape
    Ms = xsn.shape[0]
    tm = _tile(M, tm_pref)
    tn = max(t for t in range(LANE, tn_pref + 1, LANE) if n_main % t == 0)
    ni = M // tm
    return pl.pallas_call(
        _in_proj_body,
        grid=(n_main // tn, ni),
        in_specs=[
            pl.BlockSpec((tm, K), lambda j, i: (i, 0)),
            pl.BlockSpec((Ms, K), lambda j, i: (0, 0)),
            pl.BlockSpec((None, tn, K), lambda j, i: (layer, j, 0)),
            pl.BlockSpec((LANE, K), lambda j, i: (0, 0)),
        ],
        out_specs=[
            pl.BlockSpec((tm, tn), lambda j, i: (i, j)),
            pl.BlockSpec((tm, LANE), lambda j, i: (jnp.where(j == 0, i, ni - 1), 0)),
            pl.BlockSpec((Ms, tn), lambda j, i: (0, j)),
            pl.BlockSpec((Ms, LANE), lambda j, i: (0, 0)),
        ],
        out_shape=[jax.ShapeDtypeStruct((M, n_main), F32), jax.ShapeDtypeStruct((M, LANE), F32),
                   jax.ShapeDtypeStruct((Ms, n_main), F32), jax.ShapeDtypeStruct((Ms, LANE), F32)],
        scratch_shapes=[pltpu.VMEM((tn, K), BF16)],
        compiler_params=_cparams("arbitrary", "arbitrary"),
        name="in_proj",
    )(xn, xsn, wt_stack, wg_t)


def _out_proj_body(a1_ref, a2_ref, res_ref, a1s_ref, a2s_ref, ress_ref, w1_ref, w2_ref, g_ref, gn_ref,
                   o_ref, on_ref, os_ref, osn_ref):
    def proj(a1, a2, res, o_r, on_r):
        y = _dot(a1[...].astype(BF16), w1_ref[...]) + _dot(a2[...].astype(BF16), w2_ref[...])
        x = res[...] + _rms(y, g_ref[...])
        o_r[...] = x
        on_r[...] = _rms(x, gn_ref[...]).astype(BF16)

    proj(a1_ref, a2_ref, res_ref, o_ref, on_ref)

    @pl.when(pl.program_id(0) == 0)
    def _():
        proj(a1s_ref, a2s_ref, ress_ref, os_ref, osn_ref)


def _out_proj(a1, a2, res, a1s, a2s, ress, w1, w2, g, g_next, tm_pref=512):
    M, K1 = a1.shape
    Ms = a1s.shape[0]
    K2 = a2.shape[1]
    D = w1.shape[1]
    tm = _tile(M, tm_pref)
    whole = lambda r, c: pl.BlockSpec((r, c), lambda i: (0, 0))
    rows = lambda c: pl.BlockSpec((tm, c), lambda i: (i, 0))
    return pl.pallas_call(
        _out_proj_body,
        grid=(M // tm,),
        in_specs=[
            rows(K1), rows(K2), rows(D),
            whole(Ms, K1), whole(Ms, K2), whole(Ms, D),
            whole(K1, D), whole(K2, D), whole(1, D), whole(1, D),
        ],
        out_specs=[rows(D), rows(D), whole(Ms, D), whole(Ms, D)],
        out_shape=[jax.ShapeDtypeStruct((M, D), F32), jax.ShapeDtypeStruct((M, D), BF16),
                   jax.ShapeDtypeStruct((Ms, D), F32), jax.ShapeDtypeStruct((Ms, D), BF16)],
        compiler_params=_cparams("arbitrary"),
        name="out_proj",
    )(a1, a2, res, a1s, a2s, ress, w1, w2, g, g_next)


def _ffn_up_body(xn_ref, xsn_ref, w1_ref, w3_ref, w2_ref, h_ref, hs_ref, w2_out, w1_bf, w3_bf):
    def swiglu(xn):
        h1 = _dot(xn, w1_bf[...])
        return (h1 * jax.nn.sigmoid(h1) * _dot(xn, w3_bf[...])).astype(BF16)

    @pl.when(pl.program_id(1) == 0)
    def _():
        w1_bf[...] = w1_ref[...].astype(BF16)
        w3_bf[...] = w3_ref[...].astype(BF16)
        w2_out[...] = w2_ref[...].astype(BF16)
        hs_ref[...] = swiglu(xsn_ref[...])

    h_ref[...] = swiglu(xn_ref[...])


def _ffn_up(xn, xsn, w1_stack, w3_stack, w2_stack, layer, tm_pref=1024, th_pref=512):
    M, D = xn.shape
    Ms = xsn.shape[0]
    H = w1_stack.shape[2]
    tm = _tile(M, tm_pref)
    th = _tile(H, th_pref)
    return pl.pallas_call(
        _ffn_up_body,
        grid=(H // th, M // tm),
        in_specs=[
            pl.BlockSpec((tm, D), lambda j, i: (i, 0)),
            pl.BlockSpec((Ms, D), lambda j, i: (0, 0)),
            pl.BlockSpec((None, D, th), lambda j, i: (layer, 0, j)),
            pl.BlockSpec((None, D, th), lambda j, i: (layer, 0, j)),
            pl.BlockSpec((None, th, D), lambda j, i: (layer, j, 0)),
        ],
        out_specs=[
            pl.BlockSpec((tm, th), lambda j, i: (i, j)),
            pl.BlockSpec((Ms, th), lambda j, i: (0, j)),
            pl.BlockSpec((th, D), lambda j, i: (j, 0)),
        ],
        out_shape=[jax.ShapeDtypeStruct((M, H), BF16), jax.ShapeDtypeStruct((Ms, H), BF16),
                   jax.ShapeDtypeStruct((H, D), BF16)],
        scratch_shapes=[pltpu.VMEM((D, th), BF16), pltpu.VMEM((D, th), BF16)],
        compiler_params=_cparams("arbitrary", "arbitrary"),
        name="ffn_up",
    )(xn, xsn, w1_stack, w3_stack, w2_stack)


def _ffn_down_body(h_ref, hs_ref, x_ref, xs_ref, g3_ref, gn_ref, w2_ref, o_ref, os_ref, on_ref, osn_ref):
    def rows(h_r, x_r, o_r, on_r):
        x = x_r[...] + _rms(_dot(h_r[...], w2_ref[...]), g3_ref[...])
        o_r[...] = x
        on_r[...] = _rms(x, gn_ref[...]).astype(BF16)

    rows(h_ref, x_ref, o_ref, on_ref)

    @pl.when(pl.program_id(0) == 0)
    def _():
        rows(hs_ref, xs_ref, os_ref, osn_ref)


def _ffn_down(h, hs, x, xs, g3, g_next, w2, tm_pref=256):
    M, D = x.shape
    Ms = xs.shape[0]
    H = h.shape[1]
    tm = _tile(M, tm_pref)
    whole = lambda r, c: pl.BlockSpec((r, c), lambda i: (0, 0))
    rows = lambda c: pl.BlockSpec((tm, c), lambda i: (i, 0))
    return pl.pallas_call(
        _ffn_down_body,
        grid=(M // tm,),
        in_specs=[rows(H), whole(Ms, H), rows(D), whole(Ms, D), whole(1, D), whole(1, D), whole(H, D)],
        out_specs=[rows(D), whole(Ms, D), rows(D), whole(Ms, D)],
        out_shape=[jax.ShapeDtypeStruct((M, D), F32), jax.ShapeDtypeStruct((Ms, D), F32),
                   jax.ShapeDtypeStruct((M, D), BF16), jax.ShapeDtypeStruct((Ms, D), BF16)],
        compiler_params=_cparams("arbitrary"),
        name="ffn_down",
    )(h, hs, x, xs, g3, g_next, w2)


def _pool_body(u_ref, prev_ref, st_ref, pw_ref, ps_ref, o_ref, ext_ref, *, tT, t0):
    t = pl.program_id(1)
    ext_ref[0:HALO, :] = jnp.where(t == 0, st_ref[...], prev_ref[...])
    ext_ref[HALO:HALO + tT, :] = u_ref[...]
    pos = t0 + t * tT + lax.broadcasted_iota(jnp.int32, (tT, 1), 0)
    for g, w in enumerate(POOL_WINDOWS):
        cs = slice(g * POOL_DG, (g + 1) * POOL_DG)
        x_new = ext_ref[HALO:HALO + tT, cs]
        tot = x_new
        for i in range(1, w):
            tot = tot + ext_ref[HALO - i:HALO - i + tT, cs]
        cnt = jnp.minimum(w, pos + 1).astype(F32)
        y = tot / cnt - x_new
        o_ref[:, cs] = (_dot(y.astype(BF16), pw_ref[g]) * ps_ref[:, cs]).astype(o_ref.dtype)


def _pool_mix(z3, st, pool_w, pool_scale, t0, tT_pref=512):
    N, T = z3.shape[:2]
    tT = _tile(T, tT_pref)
    nT = T // tT
    if nT > 1:
        assert tT % HALO == 0
        prev, prev_spec = z3, pl.BlockSpec((None, HALO, D_HALF), lambda n, t: (n, jnp.maximum(t * (tT // HALO) - 1, 0), 0))
    else:
        prev, prev_spec = st, pl.BlockSpec((None, HALO, D_HALF), lambda n, t: (n, 0, 0))
    return pl.pallas_call(
        functools.partial(_pool_body, tT=tT, t0=t0),
        grid=(N, nT),
        in_specs=[
            pl.BlockSpec((None, tT, D_HALF), lambda n, t: (n, t, 0)),
            prev_spec,
            pl.BlockSpec((None, HALO, D_HALF), lambda n, t: (n, 0, 0)),
            pl.BlockSpec((len(POOL_WINDOWS), POOL_DG, POOL_DG), lambda n, t: (0, 0, 0)),
            pl.BlockSpec((1, D_HALF), lambda n, t: (0, 0)),
        ],
        out_specs=pl.BlockSpec((None, tT, D_HALF), lambda n, t: (n, t, 0)),
        out_shape=jax.ShapeDtypeStruct((N, T, D_HALF), BF16 if tT % 16 == 0 else F32),
        scratch_shapes=[pltpu.VMEM((HALO + tT, D_HALF), F32)],
        compiler_params=_cparams("parallel", "arbitrary"),
        name="pool_mix",
    )(z3, prev, st, pool_w, pool_scale)


def _mlstm_body(q_ref, k_ref, v_ref, o_ref, zg_ref, gb_ref, mg_ref, c0_ref, n0_ref, m0_ref,
                y_ref, cN_ref, nN_ref, mN_ref, C_s, n_s, m_s, *, L, t_valid):
    c = pl.program_id(1)

    @pl.when(c == 0)
    def _():
        C_s[...] = c0_ref[...]
        n_s[...] = n0_ref[...]
        m_s[...] = m0_ref[...]

    a = GATE_CAP * jnp.tanh((zg_ref[...] + gb_ref[...]) / GATE_CAP)
    lane = lax.broadcasted_iota(jnp.int32, (L, LANE), 1)
    logsig = jnp.minimum(a, 0.0) - jnp.log1p(jnp.exp(-jnp.abs(a)))
    A = jnp.where(lane < MLSTM_HEADS, a, logsig)
    if t_valid < L:
        row = lax.broadcasted_iota(jnp.int32, (L, LANE), 0)
        A = jnp.where(row < t_valid, A, jnp.where(lane < MLSTM_HEADS, NEG, 0.0))
    r_i = lax.broadcasted_iota(jnp.int32, (L, L), 0)
    c_i = lax.broadcasted_iota(jnp.int32, (L, L), 1)
    causal = r_i >= c_i
    Bc = jnp.dot(causal.astype(F32), A, preferred_element_type=F32, precision=lax.Precision.HIGHEST)
    At = A.T
    Bt = Bc.T

    heads = range(MLSTM_HEADS)
    ks = lambda h: slice(h * MLSTM_DK, (h + 1) * MLSTM_DK)
    vs = lambda h: slice(h * MLSTM_DV, (h + 1) * MLSTM_DV)
    m0 = [m_s[h] for h in heads]
    C0 = [C_s[h] for h in heads]
    n0 = [n_s[h] for h in heads]
    qf = [q_ref[:, ks(h)] * (MLSTM_DK ** -0.5) for h in heads]
    qb = [q.astype(BF16) for q in qf]
    kf = [k_ref[:, ks(h)] for h in heads]
    vb = [v_ref[:, vs(h)].astype(BF16) for h in heads]
    qk = [_dot_nt(qb[h], kf[h].astype(BF16)) for h in heads]
    qc = [_dot(qb[h], C0[h].astype(BF16)) for h in heads]

    b_c = [Bc[:, MLSTM_HEADS + h:MLSTM_HEADS + h + 1] for h in heads]
    m_t, w_in, s = [], [], []
    for h in heads:
        b_r = Bt[MLSTM_HEADS + h:MLSTM_HEADS + h + 1, :]
        d = jnp.where(causal, b_c[h] - b_r + At[h:h + 1, :], NEG)
        inter = b_c[h] + m0[h]
        m_t.append(jnp.maximum(inter, jnp.max(d, axis=-1, keepdims=True)))
        w_in.append(jnp.exp(inter - m_t[h]))
        s.append(qk[h] * jnp.exp(d - m_t[h]))
    sv = [_dot(s[h].astype(BF16), vb[h]) for h in heads]

    kw, w_c, m_new = [], [], []
    for h in heads:
        b_end = b_c[h][L - 1:L, :]
        g_c = b_end - b_c[h] + A[:, h:h + 1]
        m_new.append(jnp.maximum(b_end + m0[h], jnp.max(g_c, axis=0, keepdims=True)))
        w_c.append(jnp.exp(b_end + m0[h] - m_new[h]))
        kw.append(kf[h] * jnp.exp(g_c - m_new[h]))
    kv = [lax.dot_general(kw[h].astype(BF16), vb[h], TN_DIMS, preferred_element_type=F32) for h in heads]

    for h in heads:
        num = w_in[h] * qc[h] + sv[h]
        den = w_in[h] * jnp.sum(qf[h] * n0[h], axis=-1, keepdims=True) + jnp.sum(s[h], axis=-1, keepdims=True)
        hh = num / jnp.maximum(jnp.abs(den), jnp.exp(-m_t[h]))
        y_ref[:, vs(h)] = (jax.nn.sigmoid(o_ref[:, vs(h)]) * _rms(hh, mg_ref[:, vs(h)])).astype(BF16)
    for h in heads:
        C_s[h] = w_c[h] * C0[h] + kv[h]
        n_s[h] = w_c[h] * n0[h] + jnp.sum(kw[h], axis=0, keepdims=True)
        m_s[h] = m_new[h]

    @pl.when(c == pl.num_programs(1) - 1)
    def _():
        cN_ref[...] = C_s[...]
        nN_ref[...] = n_s[...]
        mN_ref[...] = m_s[...]


def _mlstm(z3, zg3, gate_b, mnorm_g, C0, n0, m0, L, t_valid):
    N, Tp = z3.shape[:2]
    nc = Tp // L
    H = MLSTM_HEADS
    qk_w = H * MLSTM_DK
    v_w = H * MLSTM_DV
    st = lambda n, c: (n, 0, 0, 0)
    outs = pl.pallas_call(
        functools.partial(_mlstm_body, L=L, t_valid=t_valid),
        grid=(N, nc),
        in_specs=[
            pl.BlockSpec((None, L, qk_w), lambda n, c: (n, c, D_HALF // qk_w)),
            pl.BlockSpec((None, L, qk_w), lambda n, c: (n, c, D_HALF // qk_w + 1)),
            pl.BlockSpec((None, L, v_w), lambda n, c: (n, c, 2)),
            pl.BlockSpec((None, L, v_w), lambda n, c: (n, c, 3)),
            pl.BlockSpec((None, L, LANE), lambda n, c: (n, c, 0)),
            pl.BlockSpec((1, LANE), lambda n, c: (0, 0)),
            pl.BlockSpec((1, v_w), lambda n, c: (0, 0)),
            pl.BlockSpec((None, H, MLSTM_DK, MLSTM_DV), st),
            pl.BlockSpec((None, H, 1, MLSTM_DK), st),
            pl.BlockSpec((None, H, 1, 1), st),
        ],
        out_specs=[
            pl.BlockSpec((None, L, v_w), lambda n, c: (n, c, 0)),
            pl.BlockSpec((None, H, MLSTM_DK, MLSTM_DV), st),
            pl.BlockSpec((None, H, 1, MLSTM_DK), st),
            pl.BlockSpec((None, H, 1, 1), st),
        ],
        out_shape=[
            jax.ShapeDtypeStruct((N, Tp, v_w), BF16),
            jax.ShapeDtypeStruct((N, H, MLSTM_DK, MLSTM_DV), F32),
            jax.ShapeDtypeStruct((N, H, 1, MLSTM_DK), F32),
            jax.ShapeDtypeStruct((N, H, 1, 1), F32),
        ],
        scratch_shapes=[
            pltpu.VMEM((H, MLSTM_DK, MLSTM_DV), F32),
            pltpu.VMEM((H, 1, MLSTM_DK), F32),
            pltpu.VMEM((H, 1, 1), F32),
        ],
        compiler_params=_cparams("parallel", "arbitrary"),
        name="mlstm",
    )(z3, z3, z3, z3, zg3, gate_b, mnorm_g, C0, n0.reshape(N, H, 1, MLSTM_DK), m0.reshape(N, H, 1, 1))
    y, C, n, m = outs
    return y, C, n.reshape(N, H, MLSTM_DK), m.reshape(N, H)


def _gmlp_body(u_ref, v_ref, g_ref, ws_ref, bs_ref, y_ref, vn_ref, *, tT):
    v = v_ref[...]
    vc = v - jnp.mean(v, axis=-1, keepdims=True)
    vn = vc * lax.rsqrt(jnp.mean(vc * vc, axis=-1, keepdims=True) + 1e-5) * g_ref[...]
    vn_ref[...] = vn
    r_i = lax.broadcasted_iota(jnp.int32, (GMLP_CHUNK, GMLP_CHUNK), 0)
    c_i = lax.broadcasted_iota(jnp.int32, (GMLP_CHUNK, GMLP_CHUNK), 1)
    for g in range(GMLP_GROUPS):
        cs = slice(g * GMLP_DG, (g + 1) * GMLP_DG)
        wm = jnp.where(r_i >= c_i, ws_ref[g], 0.0).astype(BF16)
        bias = bs_ref[:, g:g + 1]
        for ch in range(tT // GMLP_CHUNK):
            rs = slice(ch * GMLP_CHUNK, (ch + 1) * GMLP_CHUNK)
            mix = _dot(wm, vn[rs, cs].astype(BF16)) + bias
            y_ref[rs, cs] = (u_ref[rs, cs] * mix).astype(BF16)


def _gmlp(z3, gnorm_g, ws, bs_t, tT_pref=512):
    N, Tp = z3.shape[:2]
    tT = _tile(Tp, tT_pref)
    assert tT % GMLP_CHUNK == 0
    return pl.pallas_call(
        functools.partial(_gmlp_body, tT=tT),
        grid=(N, Tp // tT),
        in_specs=[
            pl.BlockSpec((None, tT, D_HALF), lambda n, t: (n, t, 0)),
            pl.BlockSpec((None, tT, D_HALF), lambda n, t: (n, t, 1)),
            pl.BlockSpec((1, D_HALF), lambda n, t: (0, 0)),
            pl.BlockSpec((GMLP_GROUPS, GMLP_CHUNK, GMLP_CHUNK), lambda n, t: (0, 0, 0)),
            pl.BlockSpec((GMLP_CHUNK, LANE), lambda n, t: (0, 0)),
        ],
        out_specs=[
            pl.BlockSpec((None, tT, D_HALF), lambda n, t: (n, t, 0)),
            pl.BlockSpec((None, tT, D_HALF), lambda n, t: (n, t, 0)),
        ],
        out_shape=[jax.ShapeDtypeStruct((N, Tp, D_HALF), BF16), jax.ShapeDtypeStruct((N, Tp, D_HALF), F32)],
        compiler_params=_cparams("parallel", "parallel"),
        name="gmlp",
    )(z3, z3, gnorm_g, ws, bs_t)


PAGES_PER_STEP = 32
BLK_PER_PAGE = PAGE // CMP_STRIDE


def _compress_body(pt_ref, *refs):
    page_refs = refs[:PAGES_PER_STEP]
    wab_ref, a_ref, b_ref = refs[PAGES_PER_STEP:]
    wa = wab_ref[0]
    wb = wab_ref[1]
    for p, pr in enumerate(page_refs):
        x = pr[...].reshape(BLK_PER_PAGE, CMP_STRIDE, 4 * NSA_DH)
        rs = slice(p * BLK_PER_PAGE, (p + 1) * BLK_PER_PAGE)
        a_ref[rs, :] = jnp.sum(x * wa[None], axis=1)
        b_ref[rs, :] = jnp.sum(x * wb[None], axis=1)


def _compress_ab(pages, page_table, wab, col_block):
    N, n_pages = page_table.shape
    assert n_pages % PAGES_PER_STEP == 0
    W = 4 * NSA_DH
    rows = PAGES_PER_STEP * BLK_PER_PAGE
    page_specs = [
        pl.BlockSpec((None, PAGE, W), functools.partial(
            lambda n, s, pt, r: (pt[n, s * PAGES_PER_STEP + r], 0, col_block), r=r))
        for r in range(PAGES_PER_STEP)
    ]
    grid_spec = pltpu.PrefetchScalarGridSpec(
        num_scalar_prefetch=1,
        grid=(N, n_pages // PAGES_PER_STEP),
        in_specs=page_specs + [pl.BlockSpec((2, CMP_STRIDE, W), lambda n, s, pt: (0, 0, 0))],
        out_specs=[pl.BlockSpec((None, rows, W), lambda n, s, pt: (n, s, 0))] * 2,
    )
    ns = n_pages * BLK_PER_PAGE
    return pl.pallas_call(
        _compress_body,
        grid_spec=grid_spec,
        out_shape=[jax.ShapeDtypeStruct((N, ns, W), F32)] * 2,
        compiler_params=_cparams("parallel", "parallel"),
        name="nsa_compress",
    )(page_table, *([pages] * PAGES_PER_STEP), wab)


ROWS_PER_TOKEN = 2 * NSA_KV
PAGE_ROWS = PAGE * ROWS_PER_TOKEN
BLK_ROWS = CMP_STRIDE * ROWS_PER_TOKEN


def _compress_rows_body(pt_ref, *refs):
    page_refs = refs[:PAGES_PER_STEP]
    wab_ref, a_ref, b_ref, z_s = refs[PAGES_PER_STEP:]
    for p, pr in enumerate(page_refs):
        x = pr[...].reshape(BLK_PER_PAGE, BLK_ROWS, NSA_DH)
        for half, out_ref in enumerate((a_ref, b_ref)):
            y = (x * wab_ref[half][None]).reshape(BLK_PER_PAGE, BLK_ROWS // 8, 8, NSA_DH).sum(axis=1)
            y = y.reshape(BLK_PER_PAGE * 8, NSA_DH)
            slot = 2 * p + half
            z_s[slot] = y + pltpu.roll(y, BLK_PER_PAGE * 8 - ROWS_PER_TOKEN, 0)
            for c in range(ROWS_PER_TOKEN):
                out_ref[p * BLK_PER_PAGE:(p + 1) * BLK_PER_PAGE, c * NSA_DH:(c + 1) * NSA_DH] = (
                    z_s[slot, pl.ds(c, BLK_PER_PAGE, stride=8), :])


def _compress_ab_rows(rows, page_table, wab_rows):
    N, n_pages = page_table.shape
    assert n_pages % PAGES_PER_STEP == 0
    W = ROWS_PER_TOKEN * NSA_DH
    out_rows = PAGES_PER_STEP * BLK_PER_PAGE
    page_specs = [
        pl.BlockSpec((PAGE_ROWS, NSA_DH), functools.partial(
            lambda n, s, pt, r: (pt[n, s * PAGES_PER_STEP + r], 0), r=r))
        for r in range(PAGES_PER_STEP)
    ]
    grid_spec = pltpu.PrefetchScalarGridSpec(
        num_scalar_prefetch=1,
        grid=(N, n_pages // PAGES_PER_STEP),
        in_specs=page_specs + [pl.BlockSpec((2, BLK_ROWS, NSA_DH), lambda n, s, pt: (0, 0, 0))],
        out_specs=[pl.BlockSpec((None, out_rows, W), lambda n, s, pt: (n, s, 0))] * 2,
        scratch_shapes=[pltpu.VMEM((2 * PAGES_PER_STEP, BLK_PER_PAGE * 8, NSA_DH), F32)],
    )
    ns = n_pages * BLK_PER_PAGE
    return pl.pallas_call(
        _compress_rows_body,
        grid_spec=grid_spec,
        out_shape=[jax.ShapeDtypeStruct((N, ns, W), F32)] * 2,
        compiler_params=_cparams("parallel", "arbitrary"),
        name="nsa_compress_rows",
    )(page_table, *([rows] * PAGES_PER_STEP), wab_rows)


def _cmp_proj_body(a_ref, b_ref, w_ref, kc_ref, vc_ref, *, ns):
    b_next = pltpu.roll(b_ref[...], ns - 1, 0)
    row = lax.broadcasted_iota(jnp.int32, (ns, 1), 0)
    blk = jnp.where(row < ns - 1, a_ref[...] + b_next, 0.0).astype(BF16)
    w0 = w_ref[0].astype(BF16)
    w1 = w_ref[1].astype(BF16)
    for kv in range(NSA_KV):
        cs = slice(kv * NSA_DH, (kv + 1) * NSA_DH)
        kc_ref[:, cs] = _dot(blk[:, kv * NSA_DH:(kv + 1) * NSA_DH], w0)
        vc_ref[:, cs] = _dot(blk[:, (NSA_KV + kv) * NSA_DH:(NSA_KV + kv + 1) * NSA_DH], w1)


def _cmp_proj(a, b, w_cmp):
    N, ns, W = a.shape
    return pl.pallas_call(
        functools.partial(_cmp_proj_body, ns=ns),
        grid=(N,),
        in_specs=[
            pl.BlockSpec((None, ns, W), lambda n: (n, 0, 0)),
            pl.BlockSpec((None, ns, W), lambda n: (n, 0, 0)),
            pl.BlockSpec((2, NSA_DH, NSA_DH), lambda n: (0, 0, 0)),
        ],
        out_specs=[pl.BlockSpec((None, ns, NSA_KV * NSA_DH), lambda n: (n, 0, 0))] * 2,
        out_shape=[jax.ShapeDtypeStruct((N, ns, NSA_KV * NSA_DH), F32)] * 2,
        compiler_params=_cparams("parallel"),
        name="nsa_cmp_proj",
    )(a, b, w_cmp)


def _split3(x):
    hi = x.astype(BF16)
    r1 = x - hi.astype(F32)
    mid = r1.astype(BF16)
    lo = (r1 - mid.astype(F32)).astype(BF16)
    return hi, mid, lo


def _cmp_to_sel(p_sum, a_mat):
    hi, mid, lo = _split3(p_sum)
    return _dot(hi, a_mat) + _dot(mid, a_mat) + _dot(lo, a_mat)


def _topk_mask(score, n_valid, k):
    lane = lax.broadcasted_iota(jnp.int32, score.shape, 1)
    rank = jnp.zeros(score.shape, F32)
    for jp in range(n_valid):
        col = score[:, jp:jp + 1]
        beats = (col > score) | ((col == score) & (lane > jp))
        rank = rank + beats.astype(F32)
    return ((rank < k) & (lane < n_valid)).astype(F32)


def _sel_scores(imp, qblk, n_valid):
    lane = lax.broadcasted_iota(jnp.int32, imp.shape, 1)
    forced = (lane == 0) | (lane == qblk) | (lane == qblk - 1)
    score = jnp.where(forced, FORCE_SCORE, jnp.where(lane > qblk, -1.0, imp))
    return jnp.where(lane < n_valid, score, -2.0)


def _masked_softmax_rows(s, mask):
    s = jnp.where(mask, s, NEG)
    e = jnp.exp(s - jnp.max(s, axis=-1, keepdims=True))
    return jnp.where(mask, e / jnp.sum(e, axis=-1, keepdims=True), 0.0)


def _stack_heads(q, scale):
    return (jnp.concatenate([q[:, g * NSA_DH:(g + 1) * NSA_DH] for g in range(NSA_G)], axis=0) * scale).astype(BF16)


SEL_TK = 512
WIN_KEYS = WINDOW + QBLK


def _nsa_prompt_body(q_ref, zg_ref, gb_ref, kc_ref, vc_ref, ks_ref, vs_ref, kw_ref, vw_ref, amat_t_ref,
                     o_ref, ks_bf, kw_bf, vs_t, vw_t, kc_bf, vc_t, sc_ref, *, n_sel, nsr):
    qb = pl.program_id(1)
    T = ks_ref.shape[0]
    kvc = lambda kv: slice(kv * NSA_DH, (kv + 1) * NSA_DH)

    @pl.when(qb == 0)
    def _():
        ks_bf[...] = ks_ref[...].astype(BF16)
        kw_bf[...] = kw_ref[...].astype(BF16)
        kc_bf[...] = kc_ref[...].astype(BF16)
        vc_t[...] = vc_ref[...].T.astype(BF16)

        def transpose_values(i, c):
            r0 = pl.multiple_of(i * LANE, LANE)
            vs_t[i] = vs_ref[pl.ds(r0, LANE), :].T.astype(BF16)
            vw_t[i] = vw_ref[pl.ds(r0, LANE), :].T.astype(BF16)
            return c

        lax.fori_loop(0, T // LANE, transpose_values, 0)

    R = QBLK
    start = qb * R
    groups = range(NSA_KV)
    tile4 = lambda x: jnp.concatenate([x] * NSA_G, axis=1)
    q_t = [jnp.concatenate([(q_ref[:, (kv * NSA_G + g) * NSA_DH:(kv * NSA_G + g + 1) * NSA_DH] * (NSA_DH ** -0.5)).T
                            for g in range(NSA_G)], axis=1).astype(BF16) for kv in groups]
    pos = start + lax.broadcasted_iota(jnp.int32, (1, R), 1)

    ns = kc_bf.shape[0]
    cmp_end = (lax.broadcasted_iota(jnp.int32, (ns, R), 0) + 2) * CMP_STRIDE - 1
    bias_c = tile4(jnp.where(cmp_end <= pos, 0.0, NEG))
    any_c = tile4(jnp.where(pos >= 2 * CMP_STRIDE - 1, 1.0, 0.0))
    a_t = amat_t_ref[...]
    blk = lax.broadcasted_iota(jnp.int32, (nsr, R), 0)
    qblk = (start + lax.broadcasted_iota(jnp.int32, (nsr, R), 1)) // SEL_BLOCK
    forced = (blk == 0) | (blk == qblk) | (blk == qblk - 1)
    o_cmp, score = [], []
    for kv in groups:
        s_c = _dot(kc_bf[:, kvc(kv)], q_t[kv]) + bias_c
        e_c = jnp.exp(s_c - jnp.max(s_c, axis=0, keepdims=True))
        p_c = e_c * (any_c / jnp.sum(e_c, axis=0, keepdims=True))
        o_cmp.append(_dot(vc_t[kvc(kv), :], p_c.astype(BF16)))
        p_sum = p_c[:, 0:R] + p_c[:, R:2 * R] + p_c[:, 2 * R:3 * R] + p_c[:, 3 * R:4 * R]
        hi, mid, lo = _split3(p_sum)
        imp_t = _dot(a_t, hi) + _dot(a_t, mid) + _dot(a_t, lo)
        sc = jnp.where(forced, FORCE_SCORE, jnp.where(blk > qblk, -1.0, imp_t))
        score.append(jnp.where(blk < n_sel, sc, -2.0))
        sc_ref[kv] = score[kv]

    def rank_step(i, ranks):
        ranks = list(ranks)
        for jp in (2 * i, 2 * i + 1):
            for kv in groups:
                row = sc_ref[kv, pl.ds(jp, 1), :]
                beats = (row > score[kv]) | ((row == score[kv]) & (blk > jp))
                ranks[kv] = ranks[kv] + beats.astype(F32)
        return tuple(ranks)

    n_rank = jnp.minimum(start // SEL_BLOCK + QBLK // SEL_BLOCK, n_sel)
    ranks = lax.fori_loop(0, (n_rank + 1) // 2, rank_step, tuple(jnp.zeros((nsr, R), F32) for _ in groups))
    for kv in groups:
        sc_ref[kv] = jnp.where(ranks[kv] < N_SEL, 0.0, NEG)

    def values_product(v_ref, tile0, n_sub, kv, p):
        out = _dot(v_ref[tile0, kvc(kv), :], p[0:LANE])
        for i in range(1, n_sub):
            out = out + _dot(v_ref[tile0 + i, kvc(kv), :], p[i * LANE:(i + 1) * LANE])
        return out

    key_tk = lax.broadcasted_iota(jnp.int32, (SEL_TK, R), 0)

    def sel_tile(kt, carry, causal):
        k0 = pl.multiple_of(kt * SEL_TK, SEL_TK)
        scores = [_dot(ks_bf[pl.ds(k0, SEL_TK), kvc(kv)], q_t[kv]) for kv in groups]
        stats = []
        for kv in groups:
            m_prev, l_prev, _ = carry[kv]
            bias = jnp.concatenate(
                [jnp.broadcast_to(sc_ref[kv, pl.ds(kt * (SEL_TK // SEL_BLOCK) + b, 1), :], (SEL_BLOCK, R))
                 for b in range(SEL_TK // SEL_BLOCK)], axis=0)
            if causal:
                bias = jnp.where(k0 + key_tk <= pos, bias, NEG)
            s = scores[kv] + tile4(bias)
            m_new = jnp.maximum(m_prev, jnp.max(s, axis=0, keepdims=True))
            alpha = jnp.exp(m_prev - m_new)
            p = jnp.exp(s - m_new)
            stats.append((m_new, alpha, alpha * l_prev + jnp.sum(p, axis=0, keepdims=True), p.astype(BF16)))
        return tuple((m_new, l_new, alpha * carry[kv][2]
                      + values_product(vs_t, kt * (SEL_TK // LANE), SEL_TK // LANE, kv, p))
                     for kv, (m_new, alpha, l_new, p) in enumerate(stats))

    n_tiles = (start + R - 1) // SEL_TK + 1
    init = tuple((jnp.full((1, NSA_G * R), NEG, F32), jnp.zeros((1, NSA_G * R), F32),
                  jnp.zeros((NSA_DH, NSA_G * R), F32)) for _ in groups)
    carry = lax.fori_loop(0, n_tiles - 1, lambda kt, c: sel_tile(kt, c, False), init)
    o_sel = [acc / l for _, l, acc in sel_tile(n_tiles - 1, carry, True)]

    w0 = pl.multiple_of(jnp.maximum(start - WINDOW, 0), QBLK)
    key_w = w0 + lax.broadcasted_iota(jnp.int32, (WIN_KEYS, R), 0)
    bias_w = tile4(jnp.where((key_w <= pos) & (key_w > pos - WINDOW), 0.0, NEG))
    gates_t = jax.nn.sigmoid(zg_ref[...] + gb_ref[...]).T
    for kv in groups:
        s_w = _dot(kw_bf[pl.ds(w0, WIN_KEYS), kvc(kv)], q_t[kv]) + bias_w
        e_w = jnp.exp(s_w - jnp.max(s_w, axis=0, keepdims=True))
        o_win = (values_product(vw_t, w0 // LANE, WIN_KEYS // LANE, kv, e_w.astype(BF16))
                 / jnp.sum(e_w, axis=0, keepdims=True))
        gate = lambda branch: jnp.concatenate(
            [gates_t[branch * NSA_HEADS + kv * NSA_G + g:branch * NSA_HEADS + kv * NSA_G + g + 1, :]
             for g in range(NSA_G)], axis=1)
        out_t = gate(0) * o_cmp[kv] + gate(1) * o_sel[kv] + gate(2) * o_win
        for g in range(NSA_G):
            h = kv * NSA_G + g
            o_ref[:, h * NSA_DH:(h + 1) * NSA_DH] = out_t[:, g * R:(g + 1) * R].T.astype(BF16)


def _sel_map(ns, n_sel, nsb):
    i = np.arange(ns)[:, None]
    j = np.arange(nsb)[None, :]
    r = SEL_BLOCK // CMP_STRIDE
    return jnp.asarray(((i >= r * j - 1) & (i <= r * j + r - 1) & (j < n_sel)).astype(np.float32), BF16)


def _nsa_prompt(z3, zg3, gate_b, kc, vc):
    N, T = z3.shape[:2]
    assert T % SEL_TK == 0 and T >= WIN_KEYS
    ns = kc.shape[1]
    n_sel = T // SEL_BLOCK
    nsr = -(-n_sel // 8) * 8
    assert nsr <= LANE
    amat_t = _sel_map(ns, n_sel, nsr).T
    KVW = NSA_KV * NSA_DH
    qcol = 2 * D_HALF // (NSA_HEADS * NSA_DH)
    kvs_col = (3 * D_HALF + 2 * KVW) // KVW
    kvw_col = kvs_col + 2
    full = lambda off: pl.BlockSpec((None, T, KVW), lambda n, qb: (n, 0, off))
    return pl.pallas_call(
        functools.partial(_nsa_prompt_body, n_sel=n_sel, nsr=nsr),
        grid=(N, T // QBLK),
        in_specs=[
            pl.BlockSpec((None, QBLK, NSA_HEADS * NSA_DH), lambda n, qb: (n, qb, qcol)),
            pl.BlockSpec((None, QBLK, LANE), lambda n, qb: (n, qb, 0)),
            pl.BlockSpec((1, LANE), lambda n, qb: (0, 0)),
            pl.BlockSpec((None, ns, KVW), lambda n, qb: (n, 0, 0)),
            pl.BlockSpec((None, ns, KVW), lambda n, qb: (n, 0, 0)),
            full(kvs_col), full(kvs_col + 1), full(kvw_col), full(kvw_col + 1),
            pl.BlockSpec((nsr, ns), lambda n, qb: (0, 0)),
        ],
        out_specs=pl.BlockSpec((None, QBLK, NSA_HEADS * NSA_DH), lambda n, qb: (n, qb, 0)),
        out_shape=jax.ShapeDtypeStruct((N, T, NSA_HEADS * NSA_DH), BF16),
        scratch_shapes=[
            pltpu.VMEM((T, KVW), BF16), pltpu.VMEM((T, KVW), BF16),
            pltpu.VMEM((T // LANE, KVW, LANE), BF16), pltpu.VMEM((T // LANE, KVW, LANE), BF16),
            pltpu.VMEM((ns, KVW), BF16), pltpu.VMEM((KVW, ns), BF16),
            pltpu.VMEM((NSA_KV, nsr, QBLK), F32),
        ],
        compiler_params=_cparams("parallel", "arbitrary"),
        name="nsa_prompt",
    )(z3, zg3, gate_b, kc, vc, z3, z3, z3, z3, amat_t)


def _nsa_sample_a_body(q_ref, kc_ref, vc_ref, kw_ref, vw_ref, amat_ref, ocmp_ref, owin_ref, sel_ref,
                       *, Tq, past_len, n_sel, wb):
    qs = _stack_heads(q_ref[...], NSA_DH ** -0.5)
    pos = past_len + lax.broadcasted_iota(jnp.int32, (Tq, 1), 0)
    pos4 = jnp.concatenate([pos] * NSA_G, axis=0)

    ns = kc_ref.shape[0]
    cmp_i = lax.broadcasted_iota(jnp.int32, (NSA_G * Tq, ns), 1)
    m_c = ((cmp_i + 2) * CMP_STRIDE - 1 <= pos4) & (cmp_i < ns - 1)
    p_c = _masked_softmax_rows(_dot_nt(qs, kc_ref[...].astype(BF16)), m_c)
    ocmp_ref[...] = _dot(p_c.astype(BF16), vc_ref[...].astype(BF16))
    p_sum = p_c[0:Tq] + p_c[Tq:2 * Tq] + p_c[2 * Tq:3 * Tq] + p_c[3 * Tq:4 * Tq]
    imp = _cmp_to_sel(p_sum, amat_ref[...])
    sel_ref[...] = _topk_mask(_sel_scores(imp, pos // SEL_BLOCK, n_sel), n_sel, N_SEL)

    nw = kw_ref.shape[0]
    tok_w = past_len - wb + lax.broadcasted_iota(jnp.int32, (NSA_G * Tq, nw), 1)
    m_w = (tok_w >= 0) & (tok_w <= pos4) & (tok_w > pos4 - WINDOW)
    p_w = _masked_softmax_rows(_dot_nt(qs, kw_ref[...].astype(BF16)), m_w)
    owin_ref[...] = _dot(p_w.astype(BF16), vw_ref[...].astype(BF16))


def _nsa_sample_a(z3, kc, vc, kw_full, past_len, wb):
    N, Tq = z3.shape[:2]
    ns = kc.shape[1]
    nw = kw_full.shape[1]
    n_sel = -(-(past_len + Tq) // SEL_BLOCK)
    nsb = -(-n_sel // LANE) * LANE
    amat = _sel_map(ns, n_sel, nsb)
    qcol = 2 * D_HALF // (NSA_G * NSA_DH)
    R4 = NSA_G * Tq
    return pl.pallas_call(
        functools.partial(_nsa_sample_a_body, Tq=Tq, past_len=past_len, n_sel=n_sel, wb=wb),
        grid=(N, NSA_KV),
        in_specs=[
            pl.BlockSpec((None, Tq, NSA_G * NSA_DH), lambda n, kv: (n, 0, qcol + kv)),
            pl.BlockSpec((None, ns, NSA_DH), lambda n, kv: (n, 0, kv)),
            pl.BlockSpec((None, ns, NSA_DH), lambda n, kv: (n, 0, kv)),
            pl.BlockSpec((None, nw, NSA_DH), lambda n, kv: (n, 0, kv)),
            pl.BlockSpec((None, nw, NSA_DH), lambda n, kv: (n, 0, NSA_KV + kv)),
            pl.BlockSpec((ns, nsb), lambda n, kv: (0, 0)),
        ],
        out_specs=[
            pl.BlockSpec((None, None, R4, NSA_DH), lambda n, kv: (n, kv, 0, 0)),
            pl.BlockSpec((None, None, R4, NSA_DH), lambda n, kv: (n, kv, 0, 0)),
            pl.BlockSpec((None, None, Tq, nsb), lambda n, kv: (n, kv, 0, 0)),
        ],
        out_shape=[
            jax.ShapeDtypeStruct((N, NSA_KV, R4, NSA_DH), F32),
            jax.ShapeDtypeStruct((N, NSA_KV, R4, NSA_DH), F32),
            jax.ShapeDtypeStruct((N, NSA_KV, Tq, nsb), F32),
        ],
        compiler_params=_cparams("parallel", "parallel"),
        name="nsa_sample_cmp_win",
    )(z3, kc, vc, kw_full, kw_full, amat)


BLK_PER_STEP = PAGES_PER_STEP * PAGE // SEL_BLOCK


def _nsa_sample_b_body(pt_ref, *refs, Tq, past_len):
    page_refs = refs[:PAGES_PER_STEP]
    (q_ref, new_ref, sel_ref, exp_ref, ocmp_ref, owin_ref, zg_ref, gb_ref, o_ref, m_s, l_s, acc_s) = refs[PAGES_PER_STEP:]
    s_id = pl.program_id(1)
    R4 = NSA_G * Tq
    rep = lambda x: jnp.concatenate([x] * NSA_G, axis=0)
    pos = past_len + lax.broadcasted_iota(jnp.int32, (Tq, 1), 0)

    @pl.when(s_id == 0)
    def _():
        m_s[...] = jnp.full(m_s.shape, NEG, F32)
        l_s[...] = jnp.zeros(l_s.shape, F32)
        acc_s[...] = jnp.zeros(acc_s.shape, F32)

    def update(kv, s, mask, pv):
        s = jnp.where(mask, s, NEG)
        m_prev = m_s[kv]
        m_new = jnp.maximum(m_prev, jnp.max(s, axis=-1, keepdims=True))
        alpha = jnp.exp(m_prev - m_new)
        p = jnp.where(mask, jnp.exp(s - m_new), 0.0)
        l_s[kv] = alpha * l_s[kv] + jnp.sum(p, axis=-1, keepdims=True)
        acc_s[kv] = alpha * acc_s[kv] + pv(p.astype(BF16))
        m_s[kv] = m_new

    qs = [_stack_heads(q_ref[:, kv * NSA_G * NSA_DH:(kv + 1) * NSA_G * NSA_DH], NSA_DH ** -0.5) for kv in range(NSA_KV)]
    comp = lambda pr, c: pr[pl.ds(c, PAGE, stride=ROWS_PER_TOKEN), :].astype(BF16)
    groups = range(NSA_KV)
    scores = [jnp.concatenate([_dot_nt(qs[kv], comp(pr, kv)) for pr in page_refs], axis=1) for kv in groups]
    sel_tok = [_dot(sel_ref[kv].astype(BF16), exp_ref[...]) for kv in groups]
    stats = []
    for kv in groups:
        mask = rep(sel_tok[kv]) > 0.5
        s = jnp.where(mask, scores[kv], NEG)
        m_prev = m_s[kv]
        m_new = jnp.maximum(m_prev, jnp.max(s, axis=-1, keepdims=True))
        alpha = jnp.exp(m_prev - m_new)
        p = jnp.where(mask, jnp.exp(s - m_new), 0.0)
        stats.append((m_new, alpha, alpha * l_s[kv] + jnp.sum(p, axis=-1, keepdims=True), p.astype(BF16)))
    pvs = []
    for kv in groups:
        p = stats[kv][3]
        out = _dot(p[:, 0:PAGE], comp(page_refs[0], NSA_KV + kv))
        for i in range(1, PAGES_PER_STEP):
            out = out + _dot(p[:, i * PAGE:(i + 1) * PAGE], comp(page_refs[i], NSA_KV + kv))
        pvs.append(out)
    for kv in groups:
        m_new, alpha, l_new, _ = stats[kv]
        acc_s[kv] = alpha * acc_s[kv] + pvs[kv]
        l_s[kv] = l_new
        m_s[kv] = m_new

    @pl.when(s_id == pl.num_programs(1) - 1)
    def _():
        new = new_ref[...].astype(BF16)
        tok_n = past_len + lax.broadcasted_iota(jnp.int32, (R4, Tq), 1)
        gates = jax.nn.sigmoid(zg_ref[...] + gb_ref[...])
        for kv in range(NSA_KV):
            k_new = new[:, kv * NSA_DH:(kv + 1) * NSA_DH]
            v_new = new[:, (NSA_KV + kv) * NSA_DH:(NSA_KV + kv + 1) * NSA_DH]
            update(kv, _dot_nt(qs[kv], k_new), tok_n <= rep(pos), lambda p, v_new=v_new: _dot(p, v_new))
            o_sel = acc_s[kv] / l_s[kv]
            o_cmp = ocmp_ref[kv]
            o_win = owin_ref[kv]
            for g in range(NSA_G):
                rs = slice(g * Tq, (g + 1) * Tq)
                head = kv * NSA_G + g
                o_ref[:, head * NSA_DH:(head + 1) * NSA_DH] = (
                    gates[:, head:head + 1] * o_cmp[rs]
                    + gates[:, NSA_HEADS + head:NSA_HEADS + head + 1] * o_sel[rs]
                    + gates[:, 2 * NSA_HEADS + head:2 * NSA_HEADS + head + 1] * o_win[rs])


def _nsa_sample_b(z3, zg3, gate_b, pool_sel, page_table, sel, o_cmp, o_win, past_len):
    N, Tq = z3.shape[:2]
    n_pages = page_table.shape[1]
    assert n_pages % PAGES_PER_STEP == 0 and past_len % SEL_BLOCK == 0
    n_steps = n_pages // PAGES_PER_STEP
    sel_steps = sel[..., :past_len // SEL_BLOCK].reshape(N, NSA_KV, Tq, n_steps, BLK_PER_STEP).transpose(0, 3, 1, 2, 4)
    tok = np.arange(PAGES_PER_STEP * PAGE)[None, :]
    expand = jnp.asarray((tok // SEL_BLOCK == np.arange(BLK_PER_STEP)[:, None]).astype(np.float32), BF16)
    needed = (jnp.max(sel_steps, axis=(2, 3)) > 0).reshape(N, n_steps, PAGES_PER_STEP, PAGE // SEL_BLOCK).any(-1)
    last_needed = lax.cummax(jnp.where(needed, jnp.arange(n_steps, dtype=jnp.int32)[None, :, None], 0), axis=1)
    page_table = jnp.take_along_axis(page_table.reshape(N, n_steps, PAGES_PER_STEP), last_needed, axis=1).reshape(N, n_pages)
    W = 4 * NSA_DH
    R4 = NSA_G * Tq
    qcol = 2 * D_HALF // (NSA_HEADS * NSA_DH)
    kvs_col = (3 * D_HALF + W) // W
    page_specs = [
        pl.BlockSpec((PAGE_ROWS, NSA_DH), functools.partial(
            lambda n, s, pt, r: (pt[n, s * PAGES_PER_STEP + r], 0), r=r))
        for r in range(PAGES_PER_STEP)
    ]
    per_n4 =lambda shape: pl.BlockSpec((None,) + shape, lambda n, s, pt: (n, 0, 0, 0))
    grid_spec = pltpu.PrefetchScalarGridSpec(
        num_scalar_prefetch=1,
        grid=(N, n_pages // PAGES_PER_STEP),
        in_specs=page_specs + [
            pl.BlockSpec((None, Tq, NSA_HEADS * NSA_DH), lambda n, s, pt: (n, 0, qcol)),
            pl.BlockSpec((None, Tq, W), lambda n, s, pt: (n, 0, kvs_col)),
            pl.BlockSpec((None, None, NSA_KV, Tq, BLK_PER_STEP), lambda n, s, pt: (n, s, 0, 0, 0)),
            pl.BlockSpec((BLK_PER_STEP, PAGES_PER_STEP * PAGE), lambda n, s, pt: (0, 0)),
            per_n4((NSA_KV, R4, NSA_DH)),
            per_n4((NSA_KV, R4, NSA_DH)),
            pl.BlockSpec((None, Tq, LANE), lambda n, s, pt: (n, 0, 0)),
            pl.BlockSpec((1, LANE), lambda n, s, pt: (0, 0)),
        ],
        out_specs=pl.BlockSpec((None, Tq, NSA_HEADS * NSA_DH), lambda n, s, pt: (n, 0, 0)),
        scratch_shapes=[
            pltpu.VMEM((NSA_KV, R4, 1), F32),
            pltpu.VMEM((NSA_KV, R4, 1), F32),
            pltpu.VMEM((NSA_KV, R4, NSA_DH), F32),
        ],
    )
    return pl.pallas_call(
        functools.partial(_nsa_sample_b_body, Tq=Tq, past_len=past_len),
        grid_spec=grid_spec,
        out_shape=jax.ShapeDtypeStruct((N, Tq, NSA_HEADS * NSA_DH), F32),
        compiler_params=_cparams("parallel", "arbitrary"),
        name="nsa_sample_sel",
    )(page_table, *([pool_sel] * PAGES_PER_STEP), z3, z3, sel_steps, expand, o_cmp, o_win, zg3, gate_b)


MLSTM_CHUNK_PROMPT = 256
SAMPLE_PAD = 128


def _pad_rows(x, rows):
    return jnp.pad(x, ((0, 0), (0, rows - x.shape[1]), (0, 0)))


def _lane_pad(v):
    return jnp.pad(v.astype(F32), (0, LANE - v.shape[0])).reshape(1, LANE)


def _gate_rows(w_in_t, n_main):
    return jnp.pad(w_in_t[n_main:], ((0, LANE - (w_in_t.shape[0] - n_main)), (0, 0)))


def _even_mixers(z3, zg3, t0, pool_buf, C0, n0, m0, w):
    N, T = z3.shape[:2]
    st = jnp.pad(pool_buf, ((0, 0), (HALO - POOL_PAD, 0), (0, 0)))
    y_a = _pool_mix(z3, st, w["pool_w"], w["pool_scale"], t0)
    u_ext_tail = jnp.concatenate([pool_buf, z3[:, :, :D_HALF]], axis=1)[:, -POOL_PAD:] if T < POOL_PAD else z3[:, -POOL_PAD:, :D_HALF]

    if T % MLSTM_CHUNK_PROMPT == 0:
        L, zm, zgm = MLSTM_CHUNK_PROMPT, z3, zg3
    else:
        L, zm, zgm = SAMPLE_PAD, _pad_rows(z3, SAMPLE_PAD), _pad_rows(zg3, SAMPLE_PAD)
    y_b, C, n, m = _mlstm(zm, zgm, w["gate_b"], w["mnorm_g"], C0, n0, m0, L, min(T, L))
    return y_a, y_b[:, :T], u_ext_tail, C, n, m


def _odd_mixers(z3, zg3, w, past=None):
    N, T, n_main = z3.shape
    W = 4 * NSA_DH
    kvc = z3[:, :, 3 * D_HALF:3 * D_HALF + W]
    kvs = z3[:, :, 3 * D_HALF + W:3 * D_HALF + 2 * W]
    kvw = z3[:, :, 3 * D_HALF + 2 * W:3 * D_HALF + 3 * W]

    if past is None:
        y_c, _ = _gmlp(z3, w["gnorm_g"], w["ws"], w["bs_t"])
        vn = None
        pt = jnp.arange(N * (T // PAGE), dtype=jnp.int32).reshape(N, T // PAGE)
        a, b = _compress_ab(z3.reshape(N * (T // PAGE), PAGE, n_main), pt, w["wab"], 3 * D_HALF // W)
        kc, vc = _cmp_proj(a, b, w["cmp_w"])
        o = _nsa_prompt(z3, zg3, w["nsa_gate_b"], kc, vc)
        win_state = kvw[:, -min(WINDOW, T):]
    else:
        win_buf, pool_cmp, pool_sel, page_table = past
        past_len = page_table.shape[1] * PAGE
        wb = win_buf.shape[1]
        y_c, vn = _gmlp(_pad_rows(z3[:, :, :2 * D_HALF], SAMPLE_PAD), w["gnorm_g"], w["ws"], w["bs_t"])
        y_c, vn = y_c[:, :T], vn[:, :T]
        assert (past_len + T) // CMP_STRIDE == past_len // CMP_STRIDE
        a, b = _compress_ab_rows(pool_cmp, page_table, w["wab_rows"])
        kc, vc = _cmp_proj(a, b, w["cmp_w"])
        kw_all = jnp.concatenate([win_buf.reshape(N, wb, W), kvw], axis=1)
        nw = -(-(wb + T) // LANE) * LANE
        o_cmp, o_win, sel = _nsa_sample_a(z3, kc, vc, _pad_rows(kw_all, nw), past_len, wb)
        o = _nsa_sample_b(z3, zg3, w["nsa_gate_b"], pool_sel, page_table, sel, o_cmp, o_win, past_len)
        win_state = kw_all[:, -wb:]

    kv5 = lambda t: t.reshape(N, t.shape[1], 2, NSA_KV, NSA_DH)
    return y_c, o, vn, kv5(kvc), kv5(kvs), kv5(win_state)


def _layer(xp, xs, xpn, xsn, w, ffn_stacks, layer, n_main, mixers_p, mixers_s):
    B, T, D = xp.shape
    Ns, Ts, _ = xs.shape
    xp2, xs2 = xp.reshape(B * T, D), xs.reshape(Ns * Ts, D)
    zp, zgp, zs, zgs = _in_proj(xpn, xsn, w["w_in"], w["w_in_layer"], n_main, w["w_gate"])
    a1p, a2p, *extra_p = mixers_p(zp.reshape(B, T, n_main), zgp.reshape(B, T, LANE))
    a1s, a2s, *extra_s = mixers_s(zs.reshape(Ns, Ts, n_main), zgs.reshape(Ns, Ts, LANE))
    flat = lambda t: t.reshape(-1, D_HALF)
    xp2, xpn, xs2, xsn = _out_proj(flat(a1p), flat(a2p), xp2, flat(a1s), flat(a2s), xs2, w["w_out_a"], w["w_out_b"],
                                   w["ng"][1:2], w["ng"][2:3])
    w1_stack, w3_stack, w2_stack = ffn_stacks
    hp, hs, w2_bf = _ffn_up(xpn, xsn, w1_stack, w3_stack, w2_stack, layer)
    xp2, xs2, xpn, xsn = _ffn_down(hp, hs, xp2, xs2, w["ng"][3:4], w["ng_next"], w2_bf)
    return xp2.reshape(B, T, D), xs2.reshape(Ns, Ts, D), xpn, xsn, extra_p, extra_s


def kernel(x_prompt, x_sample, state_pool, state_mlstm_c, state_mlstm_n, state_mlstm_m, state_win_kv, cache_cmp_kv, cache_sel_kv, page_table, norm_g, w_in_even, w_out_even, pool_w, pool_scale, mlstm_gate_b, mlstm_norm_g, w_in_odd, w_out_odd, gmlp_norm_g, gmlp_ws, gmlp_bs, nsa_cmp_pos, nsa_cmp_w, nsa_gate_b, ffn_w1, ffn_w3, ffn_w2):
    B = x_prompt.shape[0]
    depth = norm_g.shape[0]
    past_len = page_table.shape[1] * PAGE
    xp, xs = x_prompt, x_sample
    pool_p, pool_s, c_p, c_s, n_p, n_s, m_p, m_s = [], [], [], [], [], [], [], []
    gv_s, cmp_p, cmp_s, sel_p, sel_s, win_p, win_s = [], [], [], [], [], [], []
    ffn_stacks = (ffn_w1, ffn_w3, ffn_w2)
    D = x_prompt.shape[-1]
    xpn, xsn = _rms_cast(x_prompt.reshape(-1, D), x_sample.reshape(-1, D), norm_g[0][0:1])
    for l in range(depth):
        j = l // 2
        ng_next = norm_g[min(l + 1, depth - 1)][0:1]
        if l % 2 == 0:
            n_main = 4 * D_HALF
            w_out = w_out_even[j].astype(BF16)
            w_in_t = jnp.swapaxes(w_in_even, 1, 2)
            w = dict(ng=norm_g[l], ng_next=ng_next, w_in=w_in_t, w_in_layer=j, w_gate=_gate_rows(w_in_t[j], n_main),
                     w_out_a=w_out[:D_HALF], w_out_b=w_out[D_HALF:],
                     pool_w=pool_w[j].astype(BF16), pool_scale=pool_scale[j].reshape(1, D_HALF),
                     gate_b=_lane_pad(mlstm_gate_b[j].reshape(-1)), mnorm_g=mlstm_norm_g[j].reshape(1, D_HALF))
            zp = jnp.zeros((B, POOL_PAD, D_HALF), F32)
            zc = jnp.zeros((B, MLSTM_HEADS, MLSTM_DK, MLSTM_DV), F32)
            zn = jnp.zeros((B, MLSTM_HEADS, MLSTM_DK), F32)
            zm = jnp.zeros((B, MLSTM_HEADS), F32)
            xp, xs, xpn, xsn, (pb, c, n, m), (pbs, cs, ns_, ms) = _layer(
                xp, xs, xpn, xsn, w, ffn_stacks, l, n_main,
                lambda z3, zg3: _even_mixers(z3, zg3, 0, zp, zc, zn, zm, w),
                lambda z3, zg3: _even_mixers(z3, zg3, past_len, state_pool[j], state_mlstm_c[j], state_mlstm_n[j],
                                             state_mlstm_m[j], w))
            pool_p.append(pb); c_p.append(c); n_p.append(n); m_p.append(m)
            pool_s.append(pbs); c_s.append(cs); n_s.append(ns_); m_s.append(ms)
        else:
            n_main = 4 * D_HALF + D_HALF // 2
            w_out = w_out_odd[j].astype(BF16)
            cp = nsa_cmp_pos[j]
            wcol = jnp.repeat(cp, NSA_KV * NSA_DH, axis=1)
            wab = jnp.stack([wcol[:CMP_STRIDE], wcol[CMP_STRIDE:]])
            wab_rows = jnp.broadcast_to(jnp.repeat(cp, NSA_KV, axis=1).reshape(2, BLK_ROWS, 1), (2, BLK_ROWS, NSA_DH))
            w_in_t = jnp.swapaxes(w_in_odd, 1, 2)
            w = dict(ng=norm_g[l], ng_next=ng_next, w_in=w_in_t, w_in_layer=j, w_gate=_gate_rows(w_in_t[j], n_main),
                     w_out_a=w_out[:D_HALF], w_out_b=w_out[D_HALF:],
                     gnorm_g=gmlp_norm_g[j].reshape(1, D_HALF), ws=gmlp_ws[j],
                     bs_t=jnp.pad(gmlp_bs[j].T, ((0, 0), (0, LANE - GMLP_GROUPS))),
                     wab=wab, wab_rows=wab_rows, cmp_w=nsa_cmp_w[j], nsa_gate_b=_lane_pad(nsa_gate_b[j]))
            n_pool = cache_cmp_kv.shape[1]
            flat = lambda c: c.reshape(c.shape[0] * n_pool * PAGE_ROWS, NSA_DH)
            past = (state_win_kv[j], flat(cache_cmp_kv), flat(cache_sel_kv), page_table + j * n_pool)
            xp, xs, xpn, xsn, (_, kc, ksl, wv), (vn, kcs, ksls, wvs) = _layer(
                xp, xs, xpn, xsn, w, ffn_stacks, l, n_main,
                lambda z3, zg3: _odd_mixers(z3, zg3, w),
                lambda z3, zg3: _odd_mixers(z3, zg3, w, past=past))
            cmp_p.append(kc); sel_p.append(ksl); win_p.append(wv)
            gv_s.append(vn); cmp_s.append(kcs); sel_s.append(ksls); win_s.append(wvs)
    st = jnp.stack
    return (xp, xs, st(pool_p), st(pool_s), st(c_p), st(c_s), st(n_p), st(n_s), st(m_p), st(m_s),
            st(gv_s), st(cmp_p), st(cmp_s), st(sel_p), st(sel_s), st(win_p), st(win_s))
```

```python
import functools
import math

import numpy as np
import jax
import jax.numpy as jnp
from jax import lax
from jax.experimental import pallas as pl
from jax.experimental.pallas import tpu as pltpu

F32 = jnp.float32
BF16 = jnp.bfloat16
NEG = -1e30

D_HALF = 1024
POOL_WINDOWS = (2, 4, 8, 16)
POOL_DG = 256
POOL_PAD = 15
HALO = 16
MLSTM_HEADS = 4
MLSTM_DK = 128
MLSTM_DV = 256
GATE_CAP = 15.0
GMLP_CHUNK = 128
GMLP_GROUPS = 4
GMLP_DG = 256
NSA_HEADS = 8
NSA_DH = 128
NSA_KV = 2
NSA_G = 4
CMP_STRIDE = 16
SEL_BLOCK = 64
N_SEL = 16
WINDOW = 512
QBLK = 128
PAGE = 128
FORCE_SCORE = 1e9
LANE = 128
VMEM_LIMIT = 56 * 1024 * 1024

NT_DIMS = (((1,), (1,)), ((), ()))
TN_DIMS = (((0,), (0,)), ((), ()))


def _cparams(*sem):
    return pltpu.CompilerParams(dimension_semantics=sem, vmem_limit_bytes=VMEM_LIMIT)


def _tile(m, pref):
    if m <= pref:
        return m
    for t in range(pref, 7, -1):
        if m % t == 0 and t % 8 == 0:
            return t
    return m


def _dot(a, b):
    return jnp.dot(a, b, preferred_element_type=F32)


def _dot_nt(a, b):
    return lax.dot_general(a, b, NT_DIMS, preferred_element_type=F32)


def _rms(x, g, eps=1e-6):
    return x * lax.rsqrt(jnp.mean(x * x, axis=-1, keepdims=True) + eps) * g


def _rms_cast_body(x_ref, xs_ref, g_ref, o_ref, os_ref):
    o_ref[...] = _rms(x_ref[...], g_ref[...]).astype(BF16)

    @pl.when(pl.program_id(0) == 0)
    def _():
        os_ref[...] = _rms(xs_ref[...], g_ref[...]).astype(BF16)


def _rms_cast(x, xs, g, tm_pref=1024):
    M, K = x.shape
    Ms = xs.shape[0]
    tm = _tile(M, tm_pref)
    return pl.pallas_call(
        _rms_cast_body,
        grid=(M // tm,),
        in_specs=[pl.BlockSpec((tm, K), lambda i: (i, 0)), pl.BlockSpec((Ms, K), lambda i: (0, 0)),
                  pl.BlockSpec((1, K), lambda i: (0, 0))],
        out_specs=[pl.BlockSpec((tm, K), lambda i: (i, 0)), pl.BlockSpec((Ms, K), lambda i: (0, 0))],
        out_shape=[jax.ShapeDtypeStruct((M, K), BF16), jax.ShapeDtypeStruct((Ms, K), BF16)],
        compiler_params=_cparams("arbitrary"),
        name="rms_cast",
    )(x, xs, g)


def _in_proj_body(xn_ref, xsn_ref, w_ref, wg_ref, o_ref, og_ref, os_ref, ogs_ref, w_bf):
    j = pl.program_id(0)
    i = pl.program_id(1)

    @pl.when(i == 0)
    def _():
        w_bf[...] = w_ref[...].astype(BF16)
        os_ref[...] = _dot_nt(xsn_ref[...], w_bf[...])

    o_ref[...] = _dot_nt(xn_ref[...], w_bf[...])

    @pl.when(j == 0)
    def _():
        wg = wg_ref[...].astype(BF16)
        og_ref[...] = _dot_nt(xn_ref[...], wg)

        @pl.when(i == 0)
        def _():
            ogs_ref[...] = _dot_nt(xsn_ref[...], wg)


def _in_proj(xn, xsn, wt_stack, layer, n_main, wg_t, tm_pref=1024, tn_pref=1152):
    M, K = xn.shape
    Ms = xsn.shape[0]
    tm = _tile(M, tm_pref)
    tn = max(t for t in range(LANE, tn_pref + 1, LANE) if n_main % t == 0)
    ni = M // tm
    return pl.pallas_call(
        _in_proj_body,
        grid=(n_main // tn, ni),
        in_specs=[
            pl.BlockSpec((tm, K), lambda j, i: (i, 0)),
            pl.BlockSpec((Ms, K), lambda j, i: (0, 0)),
            pl.BlockSpec((None, tn, K), lambda j, i: (layer, j, 0)),
            pl.BlockSpec((LANE, K), lambda j, i: (0, 0)),
        ],
        out_specs=[
            pl.BlockSpec((tm, tn), lambda j, i: (i, j)),
            pl.BlockSpec((tm, LANE), lambda j, i: (jnp.where(j == 0, i, ni - 1), 0)),
            pl.BlockSpec((Ms, tn), lambda j, i: (0, j)),
            pl.BlockSpec((Ms, LANE), lambda j, i: (0, 0)),
        ],
        out_shape=[jax.ShapeDtypeStruct((M, n_main), F32), jax.ShapeDtypeStruct((M, LANE), F32),
                   jax.ShapeDtypeStruct((Ms, n_main), F32), jax.ShapeDtypeStruct((Ms, LANE), F32)],
        scratch_shapes=[pltpu.VMEM((tn, K), BF16)],
        compiler_params=_cparams("arbitrary", "arbitrary"),
        name="in_proj",
    )(xn, xsn, wt_stack, wg_t)


def _out_proj_body(a1_ref, a2_ref, res_ref, a1s_ref, a2s_ref, ress_ref, w1_ref, w2_ref, g_ref, gn_ref,
                   o_ref, on_ref, os_ref, osn_ref):
    def proj(a1, a2, res, o_r, on_r):
        y = _dot(a1[...].astype(BF16), w1_ref[...]) + _dot(a2[...].astype(BF16), w2_ref[...])
        x = res[...] + _rms(y, g_ref[...])
        o_r[...] = x
        on_r[...] = _rms(x, gn_ref[...]).astype(BF16)

    proj(a1_ref, a2_ref, res_ref, o_ref, on_ref)

    @pl.when(pl.program_id(0) == 0)
    def _():
        proj(a1s_ref, a2s_ref, ress_ref, os_ref, osn_ref)


def _out_proj(a1, a2, res, a1s, a2s, ress, w1, w2, g, g_next, tm_pref=512):
    M, K1 = a1.shape
    Ms = a1s.shape[0]
    K2 = a2.shape[1]
    D = w1.shape[1]
    tm = _tile(M, tm_pref)
    whole = lambda r, c: pl.BlockSpec((r, c), lambda i: (0, 0))
    rows = lambda c: pl.BlockSpec((tm, c), lambda i: (i, 0))
    return pl.pallas_call(
        _out_proj_body,
        grid=(M // tm,),
        in_specs=[
            rows(K1), rows(K2), rows(D),
            whole(Ms, K1), whole(Ms, K2), whole(Ms, D),
            whole(K1, D), whole(K2, D), whole(1, D), whole(1, D),
        ],
        out_specs=[rows(D), rows(D), whole(Ms, D), whole(Ms, D)],
        out_shape=[jax.ShapeDtypeStruct((M, D), F32), jax.ShapeDtypeStruct((M, D), BF16),
                   jax.ShapeDtypeStruct((Ms, D), F32), jax.ShapeDtypeStruct((Ms, D), BF16)],
        compiler_params=_cparams("arbitrary"),
        name="out_proj",
    )(a1, a2, res, a1s, a2s, ress, w1, w2, g, g_next)


def _ffn_up_body(xn_ref, xsn_ref, w1_ref, w3_ref, w2_ref, h_ref, hs_ref, w2_out, w1_bf, w3_bf):
    def swiglu(xn):
        h1 = _dot(xn, w1_bf[...])
        return (h1 * jax.nn.sigmoid(h1) * _dot(xn, w3_bf[...])).astype(BF16)

    @pl.when(pl.program_id(1) == 0)
    def _():
        w1_bf[...] = w1_ref[...].astype(BF16)
        w3_bf[...] = w3_ref[...].astype(BF16)
        w2_out[...] = w2_ref[...].astype(BF16)
        hs_ref[...] = swiglu(xsn_ref[...])

    h_ref[...] = swiglu(xn_ref[...])


def _ffn_up(xn, xsn, w1_stack, w3_stack, w2_stack, layer, tm_pref=1024, th_pref=512):
    M, D = xn.shape
    Ms = xsn.shape[0]
    H = w1_stack.shape[2]
    tm = _tile(M, tm_pref)
    th = _tile(H, th_pref)
    return pl.pallas_call(
        _ffn_up_body,
        grid=(H // th, M // tm),
        in_specs=[
            pl.BlockSpec((tm, D), lambda j, i: (i, 0)),
            pl.BlockSpec((Ms, D), lambda j, i: (0, 0)),
            pl.BlockSpec((None, D, th), lambda j, i: (layer, 0, j)),
            pl.BlockSpec((None, D, th), lambda j, i: (layer, 0, j)),
            pl.BlockSpec((None, th, D), lambda j, i: (layer, j, 0)),
        ],
        out_specs=[
            pl.BlockSpec((tm, th), lambda j, i: (i, j)),
            pl.BlockSpec((Ms, th), lambda j, i: (0, j)),
            pl.BlockSpec((th, D), lambda j, i: (j, 0)),
        ],
        out_shape=[jax.ShapeDtypeStruct((M, H), BF16), jax.ShapeDtypeStruct((Ms, H), BF16),
                   jax.ShapeDtypeStruct((H, D), BF16)],
        scratch_shapes=[pltpu.VMEM((D, th), BF16), pltpu.VMEM((D, th), BF16)],
        compiler_params=_cparams("arbitrary", "arbitrary"),
        name="ffn_up",
    )(xn, xsn, w1_stack, w3_stack, w2_stack)


def _ffn_down_body(h_ref, hs_ref, x_ref, xs_ref, g3_ref, gn_ref, w2_ref, o_ref, os_ref, on_ref, osn_ref):
    def rows(h_r, x_r, o_r, on_r):
        x = x_r[...] + _rms(_dot(h_r[...], w2_ref[...]), g3_ref[...])
        o_r[...] = x
        on_r[...] = _rms(x, gn_ref[...]).astype(BF16)

    rows(h_ref, x_ref, o_ref, on_ref)

    @pl.when(pl.program_id(0) == 0)
    def _():
        rows(hs_ref, xs_ref, os_ref, osn_ref)


def _ffn_down(h, hs, x, xs, g3, g_next, w2, tm_pref=256):
    M, D = x.shape
    Ms = xs.shape[0]
    H = h.shape[1]
    tm = _tile(M, tm_pref)
    whole = lambda r, c: pl.BlockSpec((r, c), lambda i: (0, 0))
    rows = lambda c: pl.BlockSpec((tm, c), lambda i: (i, 0))
    return pl.pallas_call(
        _ffn_down_body,
        grid=(M // tm,),
        in_specs=[rows(H), whole(Ms, H), rows(D), whole(Ms, D), whole(1, D), whole(1, D), whole(H, D)],
        out_specs=[rows(D), whole(Ms, D), rows(D), whole(Ms, D)],
        out_shape=[jax.ShapeDtypeStruct((M, D), F32), jax.ShapeDtypeStruct((Ms, D), F32),
                   jax.ShapeDtypeStruct((M, D), BF16), jax.ShapeDtypeStruct((Ms, D), BF16)],
        compiler_params=_cparams("arbitrary"),
        name="ffn_down",
    )(h, hs, x, xs, g3, g_next, w2)


def _pool_body(u_ref, prev_ref, st_ref, pw_ref, ps_ref, o_ref, ext_ref, *, tT, t0):
    t = pl.program_id(1)
    ext_ref[0:HALO, :] = jnp.where(t == 0, st_ref[...], prev_ref[...])
    ext_ref[HALO:HALO + tT, :] = u_ref[...]
    pos = t0 + t * tT + lax.broadcasted_iota(jnp.int32, (tT, 1), 0)
    for g, w in enumerate(POOL_WINDOWS):
        cs = slice(g * POOL_DG, (g + 1) * POOL_DG)
        x_new = ext_ref[HALO:HALO + tT, cs]
        tot = x_new
        for i in range(1, w):
            tot = tot + ext_ref[HALO - i:HALO - i + tT, cs]
        cnt = jnp.minimum(w, pos + 1).astype(F32)
        y = tot / cnt - x_new
        o_ref[:, cs] = (_dot(y.astype(BF16), pw_ref[g]) * ps_ref[:, cs]).astype(o_ref.dtype)


def _pool_mix(z3, st, pool_w, pool_scale, t0, tT_pref=512):
    N, T = z3.shape[:2]
    tT = _tile(T, tT_pref)
    nT = T // tT
    if nT > 1:
        assert tT % HALO == 0
        prev, prev_spec = z3, pl.BlockSpec((None, HALO, D_HALF), lambda n, t: (n, jnp.maximum(t * (tT // HALO) - 1, 0), 0))
    else:
        prev, prev_spec = st, pl.BlockSpec((None, HALO, D_HALF), lambda n, t: (n, 0, 0))
    return pl.pallas_call(
        functools.partial(_pool_body, tT=tT, t0=t0),
        grid=(N, nT),
        in_specs=[
            pl.BlockSpec((None, tT, D_HALF), lambda n, t: (n, t, 0)),
            prev_spec,
            pl.BlockSpec((None, HALO, D_HALF), lambda n, t: (n, 0, 0)),
            pl.BlockSpec((len(POOL_WINDOWS), POOL_DG, POOL_DG), lambda n, t: (0, 0, 0)),
            pl.BlockSpec((1, D_HALF), lambda n, t: (0, 0)),
        ],
        out_specs=pl.BlockSpec((None, tT, D_HALF), lambda n, t: (n, t, 0)),
        out_shape=jax.ShapeDtypeStruct((N, T, D_HALF), BF16 if tT % 16 == 0 else F32),
        scratch_shapes=[pltpu.VMEM((HALO + tT, D_HALF), F32)],
        compiler_params=_cparams("parallel", "arbitrary"),
        name="pool_mix",
    )(z3, prev, st, pool_w, pool_scale)


def _mlstm_body(q_ref, k_ref, v_ref, o_ref, zg_ref, gb_ref, mg_ref, c0_ref, n0_ref, m0_ref,
                y_ref, cN_ref, nN_ref, mN_ref, C_s, n_s, m_s, *, L, t_valid):
    c = pl.program_id(1)

    @pl.when(c == 0)
    def _():
        C_s[...] = c0_ref[...]
        n_s[...] = n0_ref[...]
        m_s[...] = m0_ref[...]

    a = GATE_CAP * jnp.tanh((zg_ref[...] + gb_ref[...]) / GATE_CAP)
    lane = lax.broadcasted_iota(jnp.int32, (L, LANE), 1)
    logsig = jnp.minimum(a, 0.0) - jnp.log1p(jnp.exp(-jnp.abs(a)))
    A = jnp.where(lane < MLSTM_HEADS, a, logsig)
    if t_valid < L:
        row = lax.broadcasted_iota(jnp.int32, (L, LANE), 0)
        A = jnp.where(row < t_valid, A, jnp.where(lane < MLSTM_HEADS, NEG, 0.0))
    r_i = lax.broadcasted_iota(jnp.int32, (L, L), 0)
    c_i = lax.broadcasted_iota(jnp.int32, (L, L), 1)
    causal = r_i >= c_i
    Bc = jnp.dot(causal.astype(F32), A, preferred_element_type=F32, precision=lax.Precision.HIGHEST)
    At = A.T
    Bt = Bc.T

    heads = range(MLSTM_HEADS)
    ks = lambda h: slice(h * MLSTM_DK, (h + 1) * MLSTM_DK)
    vs = lambda h: slice(h * MLSTM_DV, (h + 1) * MLSTM_DV)
    m0 = [m_s[h] for h in heads]
    C0 = [C_s[h] for h in heads]
    n0 = [n_s[h] for h in heads]
    qf = [q_ref[:, ks(h)] * (MLSTM_DK ** -0.5) for h in heads]
    qb = [q.astype(BF16) for q in qf]
    kf = [k_ref[:, ks(h)] for h in heads]
    vb = [v_ref[:, vs(h)].astype(BF16) for h in heads]
    qk = [_dot_nt(qb[h], kf[h].astype(BF16)) for h in heads]
    qc = [_dot(qb[h], C0[h].astype(BF16)) for h in heads]

    b_c = [Bc[:, MLSTM_HEADS + h:MLSTM_HEADS + h + 1] for h in heads]
    m_t, w_in, s = [], [], []
    for h in heads:
        b_r = Bt[MLSTM_HEADS + h:MLSTM_HEADS + h + 1, :]
        d = jnp.where(causal, b_c[h] - b_r + At[h:h + 1, :], NEG)
        inter = b_c[h] + m0[h]
        m_t.append(jnp.maximum(inter, jnp.max(d, axis=-1, keepdims=True)))
        w_in.append(jnp.exp(inter - m_t[h]))
        s.append(qk[h] * jnp.exp(d - m_t[h]))
    sv = [_dot(s[h].astype(BF16), vb[h]) for h in heads]

    kw, w_c, m_new = [], [], []
    for h in heads:
        b_end = b_c[h][L - 1:L, :]
        g_c = b_end - b_c[h] + A[:, h:h + 1]
        m_new.append(jnp.maximum(b_end + m0[h], jnp.max(g_c, axis=0, keepdims=True)))
        w_c.append(jnp.exp(b_end + m0[h] - m_new[h]))
        kw.append(kf[h] * jnp.exp(g_c - m_new[h]))
    kv = [lax.dot_general(kw[h].astype(BF16), vb[h], TN_DIMS, preferred_element_type=F32) for h in heads]

    for h in heads:
        num = w_in[h] * qc[h] + sv[h]
        den = w_in[h] * jnp.sum(qf[h] * n0[h], axis=-1, keepdims=True) + jnp.sum(s[h], axis=-1, keepdims=True)
        hh = num / jnp.maximum(jnp.abs(den), jnp.exp(-m_t[h]))
        y_ref[:, vs(h)] = (jax.nn.sigmoid(o_ref[:, vs(h)]) * _rms(hh, mg_ref[:, vs(h)])).astype(BF16)
    for h in heads:
        C_s[h] = w_c[h] * C0[h] + kv[h]
        n_s[h] = w_c[h] * n0[h] + jnp.sum(kw[h], axis=0, keepdims=True)
        m_s[h] = m_new[h]

    @pl.when(c == pl.num_programs(1) - 1)
    def _():
        cN_ref[...] = C_s[...]
        nN_ref[...] = n_s[...]
        mN_ref[...] = m_s[...]


def _mlstm(z3, zg3, gate_b, mnorm_g, C0, n0, m0, L, t_valid):
    N, Tp = z3.shape[:2]
    nc = Tp // L
    H = MLSTM_HEADS
    qk_w = H * MLSTM_DK
    v_w = H * MLSTM_DV
    st = lambda n, c: (n, 0, 0, 0)
    outs = pl.pallas_call(
        functools.partial(_mlstm_body, L=L, t_valid=t_valid),
        grid=(N, nc),
        in_specs=[
            pl.BlockSpec((None, L, qk_w), lambda n, c: (n, c, D_HALF // qk_w)),
            pl.BlockSpec((None, L, qk_w), lambda n, c: (n, c, D_HALF // qk_w + 1)),
            pl.BlockSpec((None, L, v_w), lambda n, c: (n, c, 2)),
            pl.BlockSpec((None, L, v_w), lambda n, c: (n, c, 3)),
            pl.BlockSpec((None, L, LANE), lambda n, c: (n, c, 0)),
            pl.BlockSpec((1, LANE), lambda n, c: (0, 0)),
            pl.BlockSpec((1, v_w), lambda n, c: (0, 0)),
            pl.BlockSpec((None, H, MLSTM_DK, MLSTM_DV), st),
            pl.BlockSpec((None, H, 1, MLSTM_DK), st),
            pl.BlockSpec((None, H, 1, 1), st),
        ],
        out_specs=[
            pl.BlockSpec((None, L, v_w), lambda n, c: (n, c, 0)),
            pl.BlockSpec((None, H, MLSTM_DK, MLSTM_DV), st),
            pl.BlockSpec((None, H, 1, MLSTM_DK), st),
            pl.BlockSpec((None, H, 1, 1), st),
        ],
        out_shape=[
            jax.ShapeDtypeStruct((N, Tp, v_w), BF16),
            jax.ShapeDtypeStruct((N, H, MLSTM_DK, MLSTM_DV), F32),
            jax.ShapeDtypeStruct((N, H, 1, MLSTM_DK), F32),
            jax.ShapeDtypeStruct((N, H, 1, 1), F32),
        ],
        scratch_shapes=[
            pltpu.VMEM((H, MLSTM_DK, MLSTM_DV), F32),
            pltpu.VMEM((H, 1, MLSTM_DK), F32),
            pltpu.VMEM((H, 1, 1), F32),
        ],
        compiler_params=_cparams("parallel", "arbitrary"),
        name="mlstm",
    )(z3, z3, z3, z3, zg3, gate_b, mnorm_g, C0, n0.reshape(N, H, 1, MLSTM_DK), m0.reshape(N, H, 1, 1))
    y, C, n, m = outs
    return y, C, n.reshape(N, H, MLSTM_DK), m.reshape(N, H)


def _gmlp_body(u_ref, v_ref, g_ref, ws_ref, bs_ref, y_ref, *vn_refs, tT):
    v = v_ref[...]
    vc = v - jnp.mean(v, axis=-1, keepdims=True)
    vn = vc * lax.rsqrt(jnp.mean(vc * vc, axis=-1, keepdims=True) + 1e-5) * g_ref[...]
    for vn_ref in vn_refs:
        vn_ref[...] = vn
    r_i = lax.broadcasted_iota(jnp.int32, (GMLP_CHUNK, GMLP_CHUNK), 0)
    c_i = lax.broadcasted_iota(jnp.int32, (GMLP_CHUNK, GMLP_CHUNK), 1)
    for g in range(GMLP_GROUPS):
        cs = slice(g * GMLP_DG, (g + 1) * GMLP_DG)
        wm = jnp.where(r_i >= c_i, ws_ref[g], 0.0).astype(BF16)
        bias = bs_ref[:, g:g + 1]
        for ch in range(tT // GMLP_CHUNK):
            rs = slice(ch * GMLP_CHUNK, (ch + 1) * GMLP_CHUNK)
            mix = _dot(wm, vn[rs, cs].astype(BF16)) + bias
            y_ref[rs, cs] = (u_ref[rs, cs] * mix).astype(BF16)


def _gmlp(z3, gnorm_g, ws, bs_t, keep_vn, tT_pref=512):
    N, Tp = z3.shape[:2]
    tT = _tile(Tp, tT_pref)
    assert tT % GMLP_CHUNK == 0
    n_out = 2 if keep_vn else 1
    return pl.pallas_call(
        functools.partial(_gmlp_body, tT=tT),
        grid=(N, Tp // tT),
        in_specs=[
            pl.BlockSpec((None, tT, D_HALF), lambda n, t: (n, t, 0)),
            pl.BlockSpec((None, tT, D_HALF), lambda n, t: (n, t, 1)),
            pl.BlockSpec((1, D_HALF), lambda n, t: (0, 0)),
            pl.BlockSpec((GMLP_GROUPS, GMLP_CHUNK, GMLP_CHUNK), lambda n, t: (0, 0, 0)),
            pl.BlockSpec((GMLP_CHUNK, LANE), lambda n, t: (0, 0)),
        ],
        out_specs=[pl.BlockSpec((None, tT, D_HALF), lambda n, t: (n, t, 0))] * n_out,
        out_shape=[jax.ShapeDtypeStruct((N, Tp, D_HALF), BF16), jax.ShapeDtypeStruct((N, Tp, D_HALF), F32)][:n_out],
        compiler_params=_cparams("parallel", "parallel"),
        name="gmlp",
    )(z3, z3, gnorm_g, ws, bs_t)


PAGES_PER_STEP = 32
BLK_PER_PAGE = PAGE // CMP_STRIDE


def _compress_body(pt_ref, *refs):
    page_refs = refs[:PAGES_PER_STEP]
    wab_ref, a_ref, b_ref = refs[PAGES_PER_STEP:]
    wa = wab_ref[0]
    wb = wab_ref[1]
    for p, pr in enumerate(page_refs):
        x = pr[...].reshape(BLK_PER_PAGE, CMP_STRIDE, 4 * NSA_DH)
        rs = slice(p * BLK_PER_PAGE, (p + 1) * BLK_PER_PAGE)
        a_ref[rs, :] = jnp.sum(x * wa[None], axis=1)
        b_ref[rs, :] = jnp.sum(x * wb[None], axis=1)


def _compress_ab(pages, page_table, wab, col_block):
    N, n_pages = page_table.shape
    assert n_pages % PAGES_PER_STEP == 0
    W = 4 * NSA_DH
    rows = PAGES_PER_STEP * BLK_PER_PAGE
    page_specs = [
        pl.BlockSpec((None, PAGE, W), functools.partial(
            lambda n, s, pt, r: (pt[n, s * PAGES_PER_STEP + r], 0, col_block), r=r))
        for r in range(PAGES_PER_STEP)
    ]
    grid_spec = pltpu.PrefetchScalarGridSpec(
        num_scalar_prefetch=1,
        grid=(N, n_pages // PAGES_PER_STEP),
        in_specs=page_specs + [pl.BlockSpec((2, CMP_STRIDE, W), lambda n, s, pt: (0, 0, 0))],
        out_specs=[pl.BlockSpec((None, rows, W), lambda n, s, pt: (n, s, 0))] * 2,
    )
    ns = n_pages * BLK_PER_PAGE
    return pl.pallas_call(
        _compress_body,
        grid_spec=grid_spec,
        out_shape=[jax.ShapeDtypeStruct((N, ns, W), F32)] * 2,
        compiler_params=_cparams("parallel", "parallel"),
        name="nsa_compress",
    )(page_table, *([pages] * PAGES_PER_STEP), wab)


ROWS_PER_TOKEN = 2 * NSA_KV
PAGE_ROWS = PAGE * ROWS_PER_TOKEN
BLK_ROWS = CMP_STRIDE * ROWS_PER_TOKEN


def _compress_rows_body(pt_ref, *refs):
    page_refs = refs[:PAGES_PER_STEP]
    wab_ref, a_ref, b_ref, z_s = refs[PAGES_PER_STEP:]
    for p, pr in enumerate(page_refs):
        x = pr[...].reshape(BLK_PER_PAGE, BLK_ROWS, NSA_DH)
        for half, out_ref in enumerate((a_ref, b_ref)):
            y = (x * wab_ref[half][None]).reshape(BLK_PER_PAGE, BLK_ROWS // 8, 8, NSA_DH).sum(axis=1)
            y = y.reshape(BLK_PER_PAGE * 8, NSA_DH)
            slot = 2 * p + half
            z_s[slot] = y + pltpu.roll(y, BLK_PER_PAGE * 8 - ROWS_PER_TOKEN, 0)
            for c in range(ROWS_PER_TOKEN):
                out_ref[p * BLK_PER_PAGE:(p + 1) * BLK_PER_PAGE, c * NSA_DH:(c + 1) * NSA_DH] = (
                    z_s[slot, pl.ds(c, BLK_PER_PAGE, stride=8), :])


def _compress_ab_rows(rows, page_table, wab_rows):
    N, n_pages = page_table.shape
    assert n_pages % PAGES_PER_STEP == 0
    W = ROWS_PER_TOKEN * NSA_DH
    out_rows = PAGES_PER_STEP * BLK_PER_PAGE
    page_specs = [
        pl.BlockSpec((PAGE_ROWS, NSA_DH), functools.partial(
            lambda n, s, pt, r: (pt[n, s * PAGES_PER_STEP + r], 0), r=r))
        for r in range(PAGES_PER_STEP)
    ]
    grid_spec = pltpu.PrefetchScalarGridSpec(
        num_scalar_prefetch=1,
        grid=(N, n_pages // PAGES_PER_STEP),
        in_specs=page_specs + [pl.BlockSpec((2, BLK_ROWS, NSA_DH), lambda n, s, pt: (0, 0, 0))],
        out_specs=[pl.BlockSpec((None, out_rows, W), lambda n, s, pt: (n, s, 0))] * 2,
        scratch_shapes=[pltpu.VMEM((2 * PAGES_PER_STEP, BLK_PER_PAGE * 8, NSA_DH), F32)],
    )
    ns = n_pages * BLK_PER_PAGE
    return pl.pallas_call(
        _compress_rows_body,
        grid_spec=grid_spec,
        out_shape=[jax.ShapeDtypeStruct((N, ns, W), F32)] * 2,
        compiler_params=_cparams("parallel", "arbitrary"),
        name="nsa_compress_rows",
    )(page_table, *([rows] * PAGES_PER_STEP), wab_rows)


def _cmp_proj_body(a_ref, b_ref, w_ref, kc_ref, vc_ref, *, ns):
    b_next = pltpu.roll(b_ref[...], ns - 1, 0)
    row = lax.broadcasted_iota(jnp.int32, (ns, 1), 0)
    blk = jnp.where(row < ns - 1, a_ref[...] + b_next, 0.0).astype(BF16)
    w0 = w_ref[0].astype(BF16)
    w1 = w_ref[1].astype(BF16)
    for kv in range(NSA_KV):
        cs = slice(kv * NSA_DH, (kv + 1) * NSA_DH)
        kc_ref[:, cs] = _dot(blk[:, kv * NSA_DH:(kv + 1) * NSA_DH], w0)
        vc_ref[:, cs] = _dot(blk[:, (NSA_KV + kv) * NSA_DH:(NSA_KV + kv + 1) * NSA_DH], w1)


def _cmp_proj(a, b, w_cmp):
    N, ns, W = a.shape
    return pl.pallas_call(
        functools.partial(_cmp_proj_body, ns=ns),
        grid=(N,),
        in_specs=[
            pl.BlockSpec((None, ns, W), lambda n: (n, 0, 0)),
            pl.BlockSpec((None, ns, W), lambda n: (n, 0, 0)),
            pl.BlockSpec((2, NSA_DH, NSA_DH), lambda n: (0, 0, 0)),
        ],
        out_specs=[pl.BlockSpec((None, ns, NSA_KV * NSA_DH), lambda n: (n, 0, 0))] * 2,
        out_shape=[jax.ShapeDtypeStruct((N, ns, NSA_KV * NSA_DH), F32)] * 2,
        compiler_params=_cparams("parallel"),
        name="nsa_cmp_proj",
    )(a, b, w_cmp)


def _split3(x):
    hi = x.astype(BF16)
    r1 = x - hi.astype(F32)
    mid = r1.astype(BF16)
    lo = (r1 - mid.astype(F32)).astype(BF16)
    return hi, mid, lo


def _cmp_to_sel(p_sum, a_mat):
    hi, mid, lo = _split3(p_sum)
    return _dot(hi, a_mat) + _dot(mid, a_mat) + _dot(lo, a_mat)


def _topk_mask(score, n_valid, k):
    lane = lax.broadcasted_iota(jnp.int32, score.shape, 1)
    rank = jnp.zeros(score.shape, F32)
    for jp in range(n_valid):
        col = score[:, jp:jp + 1]
        beats = (col > score) | ((col == score) & (lane > jp))
        rank = rank + beats.astype(F32)
    return ((rank < k) & (lane < n_valid)).astype(F32)


def _sel_scores(imp, qblk, n_valid):
    lane = lax.broadcasted_iota(jnp.int32, imp.shape, 1)
    forced = (lane == 0) | (lane == qblk) | (lane == qblk - 1)
    score = jnp.where(forced, FORCE_SCORE, jnp.where(lane > qblk, -1.0, imp))
    return jnp.where(lane < n_valid, score, -2.0)


def _masked_softmax_rows(s, mask):
    s = jnp.where(mask, s, NEG)
    e = jnp.exp(s - jnp.max(s, axis=-1, keepdims=True))
    return jnp.where(mask, e / jnp.sum(e, axis=-1, keepdims=True), 0.0)


def _stack_heads(q, scale):
    return (jnp.concatenate([q[:, g * NSA_DH:(g + 1) * NSA_DH] for g in range(NSA_G)], axis=0) * scale).astype(BF16)


SEL_TK = 512
WIN_KEYS = WINDOW + QBLK


def _nsa_prompt_body(q_ref, zg_ref, gb_ref, kc_ref, vc_ref, ks_ref, vs_ref, kw_ref, vw_ref, amat_t_ref,
                     o_ref, ks_bf, kw_bf, vs_t, vw_t, kc_bf, vc_t, sc_ref, *, n_sel, nsr):
    qb = pl.program_id(1)
    T = ks_ref.shape[0]
    kvc = lambda kv: slice(kv * NSA_DH, (kv + 1) * NSA_DH)

    @pl.when(qb == 0)
    def _():
        ks_bf[...] = ks_ref[...].astype(BF16)
        kw_bf[...] = kw_ref[...].astype(BF16)
        kc_bf[...] = kc_ref[...].astype(BF16)
        vc_t[...] = vc_ref[...].T.astype(BF16)

        def transpose_values(i, c):
            r0 = pl.multiple_of(i * LANE, LANE)
            vs_t[i] = vs_ref[pl.ds(r0, LANE), :].T.astype(BF16)
            vw_t[i] = vw_ref[pl.ds(r0, LANE), :].T.astype(BF16)
            return c

        lax.fori_loop(0, T // LANE, transpose_values, 0)

    R = QBLK
    start = qb * R
    groups = range(NSA_KV)
    tile4 = lambda x: jnp.concatenate([x] * NSA_G, axis=1)
    q_t = [jnp.concatenate([(q_ref[:, (kv * NSA_G + g) * NSA_DH:(kv * NSA_G + g + 1) * NSA_DH] * (NSA_DH ** -0.5)).T
                            for g in range(NSA_G)], axis=1).astype(BF16) for kv in groups]
    pos = start + lax.broadcasted_iota(jnp.int32, (1, R), 1)

    ns = kc_bf.shape[0]
    cmp_end = (lax.broadcasted_iota(jnp.int32, (ns, R), 0) + 2) * CMP_STRIDE - 1
    bias_c = tile4(jnp.where(cmp_end <= pos, 0.0, NEG))
    any_c = tile4(jnp.where(pos >= 2 * CMP_STRIDE - 1, 1.0, 0.0))
    a_t = amat_t_ref[...]
    blk = lax.broadcasted_iota(jnp.int32, (nsr, R), 0)
    qblk = (start + lax.broadcasted_iota(jnp.int32, (nsr, R), 1)) // SEL_BLOCK
    forced = (blk == 0) | (blk == qblk) | (blk == qblk - 1)
    o_cmp, score = [], []
    for kv in groups:
        s_c = _dot(kc_bf[:, kvc(kv)], q_t[kv]) + bias_c
        e_c = jnp.exp(s_c - jnp.max(s_c, axis=0, keepdims=True))
        p_c = e_c * (any_c / jnp.sum(e_c, axis=0, keepdims=True))
        o_cmp.append(_dot(vc_t[kvc(kv), :], p_c.astype(BF16)))
        p_sum = p_c[:, 0:R] + p_c[:, R:2 * R] + p_c[:, 2 * R:3 * R] + p_c[:, 3 * R:4 * R]
        hi, mid, lo = _split3(p_sum)
        imp_t = _dot(a_t, hi) + _dot(a_t, mid) + _dot(a_t, lo)
        sc = jnp.where(forced, FORCE_SCORE, jnp.where(blk > qblk, -1.0, imp_t))
        score.append(jnp.where(blk < n_sel, sc, -2.0))
        sc_ref[kv] = score[kv]

    def rank_step(i, ranks):
        ranks = list(ranks)
        for jp in (2 * i, 2 * i + 1):
            for kv in groups:
                row = sc_ref[kv, pl.ds(jp, 1), :]
                beats = (row > score[kv]) | ((row == score[kv]) & (blk > jp))
                ranks[kv] = ranks[kv] + beats.astype(F32)
        return tuple(ranks)

    n_rank = jnp.minimum(start // SEL_BLOCK + QBLK // SEL_BLOCK, n_sel)
    ranks = lax.fori_loop(0, (n_rank + 1) // 2, rank_step, tuple(jnp.zeros((nsr, R), F32) for _ in groups))
    for kv in groups:
        sc_ref[kv] = jnp.where(ranks[kv] < N_SEL, 0.0, NEG)

    def values_product(v_ref, tile0, n_sub, kv, p):
        out = _dot(v_ref[tile0, kvc(kv), :], p[0:LANE])
        for i in range(1, n_sub):
            out = out + _dot(v_ref[tile0 + i, kvc(kv), :], p[i * LANE:(i + 1) * LANE])
        return out

    key_tk = lax.broadcasted_iota(jnp.int32, (SEL_TK, R), 0)

    def sel_tile(kt, carry, causal):
        k0 = pl.multiple_of(kt * SEL_TK, SEL_TK)
        scores = [_dot(ks_bf[pl.ds(k0, SEL_TK), kvc(kv)], q_t[kv]) for kv in groups]
        stats = []
        for kv in groups:
            m_prev, l_prev, _ = carry[kv]
            bias = jnp.concatenate(
                [jnp.broadcast_to(sc_ref[kv, pl.ds(kt * (SEL_TK // SEL_BLOCK) + b, 1), :], (SEL_BLOCK, R))
                 for b in range(SEL_TK // SEL_BLOCK)], axis=0)
            if causal:
                bias = jnp.where(k0 + key_tk <= pos, bias, NEG)
            s = scores[kv] + tile4(bias)
            m_new = jnp.maximum(m_prev, jnp.max(s, axis=0, keepdims=True))
            alpha = jnp.exp(m_prev - m_new)
            p = jnp.exp(s - m_new)
            stats.append((m_new, alpha, alpha * l_prev + jnp.sum(p, axis=0, keepdims=True), p.astype(BF16)))
        return tuple((m_new, l_new, alpha * carry[kv][2]
                      + values_product(vs_t, kt * (SEL_TK // LANE), SEL_TK // LANE, kv, p))
                     for kv, (m_new, alpha, l_new, p) in enumerate(stats))

    n_tiles = (start + R - 1) // SEL_TK + 1
    init = tuple((jnp.full((1, NSA_G * R), NEG, F32), jnp.zeros((1, NSA_G * R), F32),
                  jnp.zeros((NSA_DH, NSA_G * R), F32)) for _ in groups)
    carry = lax.fori_loop(0, n_tiles - 1, lambda kt, c: sel_tile(kt, c, False), init)
    o_sel = [acc / l for _, l, acc in sel_tile(n_tiles - 1, carry, True)]

    w0 = pl.multiple_of(jnp.maximum(start - WINDOW, 0), QBLK)
    key_w = w0 + lax.broadcasted_iota(jnp.int32, (WIN_KEYS, R), 0)
    bias_w = tile4(jnp.where((key_w <= pos) & (key_w > pos - WINDOW), 0.0, NEG))
    gates_t = jax.nn.sigmoid(zg_ref[...] + gb_ref[...]).T
    for kv in groups:
        s_w = _dot(kw_bf[pl.ds(w0, WIN_KEYS), kvc(kv)], q_t[kv]) + bias_w
        e_w = jnp.exp(s_w - jnp.max(s_w, axis=0, keepdims=True))
        o_win = (values_product(vw_t, w0 // LANE, WIN_KEYS // LANE, kv, e_w.astype(BF16))
                 / jnp.sum(e_w, axis=0, keepdims=True))
        gate = lambda branch: jnp.concatenate(
            [gates_t[branch * NSA_HEADS + kv * NSA_G + g:branch * NSA_HEADS + kv * NSA_G + g + 1, :]
             for g in range(NSA_G)], axis=1)
        out_t = gate(0) * o_cmp[kv] + gate(1) * o_sel[kv] + gate(2) * o_win
        for g in range(NSA_G):
            h = kv * NSA_G + g
            o_ref[:, h * NSA_DH:(h + 1) * NSA_DH] = out_t[:, g * R:(g + 1) * R].T.astype(BF16)


def _sel_map(ns, n_sel, nsb):
    i = np.arange(ns)[:, None]
    j = np.arange(nsb)[None, :]
    r = SEL_BLOCK // CMP_STRIDE
    return jnp.asarray(((i >= r * j - 1) & (i <= r * j + r - 1) & (j < n_sel)).astype(np.float32), BF16)


def _nsa_prompt(z3, zg3, gate_b, kc, vc):
    N, T = z3.shape[:2]
    assert T % SEL_TK == 0 and T >= WIN_KEYS
    ns = kc.shape[1]
    n_sel = T // SEL_BLOCK
    nsr = -(-n_sel // 8) * 8
    assert nsr <= LANE
    amat_t = _sel_map(ns, n_sel, nsr).T
    KVW = NSA_KV * NSA_DH
    qcol = 2 * D_HALF // (NSA_HEADS * NSA_DH)
    kvs_col = (3 * D_HALF + 2 * KVW) // KVW
    kvw_col = kvs_col + 2
    full = lambda off: pl.BlockSpec((None, T, KVW), lambda n, qb: (n, 0, off))
    return pl.pallas_call(
        functools.partial(_nsa_prompt_body, n_sel=n_sel, nsr=nsr),
        grid=(N, T // QBLK),
        in_specs=[
            pl.BlockSpec((None, QBLK, NSA_HEADS * NSA_DH), lambda n, qb: (n, qb, qcol)),
            pl.BlockSpec((None, QBLK, LANE), lambda n, qb: (n, qb, 0)),
            pl.BlockSpec((1, LANE), lambda n, qb: (0, 0)),
            pl.BlockSpec((None, ns, KVW), lambda n, qb: (n, 0, 0)),
            pl.BlockSpec((None, ns, KVW), lambda n, qb: (n, 0, 0)),
            full(kvs_col), full(kvs_col + 1), full(kvw_col), full(kvw_col + 1),
            pl.BlockSpec((nsr, ns), lambda n, qb: (0, 0)),
        ],
        out_specs=pl.BlockSpec((None, QBLK, NSA_HEADS * NSA_DH), lambda n, qb: (n, qb, 0)),
        out_shape=jax.ShapeDtypeStruct((N, T, NSA_HEADS * NSA_DH), BF16),
        scratch_shapes=[
            pltpu.VMEM((T, KVW), BF16), pltpu.VMEM((T, KVW), BF16),
            pltpu.VMEM((T // LANE, KVW, LANE), BF16), pltpu.VMEM((T // LANE, KVW, LANE), BF16),
            pltpu.VMEM((ns, KVW), BF16), pltpu.VMEM((KVW, ns), BF16),
            pltpu.VMEM((NSA_KV, nsr, QBLK), F32),
        ],
        compiler_params=_cparams("parallel", "arbitrary"),
        name="nsa_prompt",
    )(z3, zg3, gate_b, kc, vc, z3, z3, z3, z3, amat_t)


def _nsa_sample_a_body(q_ref, kc_ref, vc_ref, kw_ref, vw_ref, amat_ref, ocmp_ref, owin_ref, sel_ref,
                       *, Tq, past_len, n_sel, wb):
    qs = _stack_heads(q_ref[...], NSA_DH ** -0.5)
    pos = past_len + lax.broadcasted_iota(jnp.int32, (Tq, 1), 0)
    pos4 = jnp.concatenate([pos] * NSA_G, axis=0)

    ns = kc_ref.shape[0]
    cmp_i = lax.broadcasted_iota(jnp.int32, (NSA_G * Tq, ns), 1)
    m_c = ((cmp_i + 2) * CMP_STRIDE - 1 <= pos4) & (cmp_i < ns - 1)
    p_c = _masked_softmax_rows(_dot_nt(qs, kc_ref[...].astype(BF16)), m_c)
    ocmp_ref[...] = _dot(p_c.astype(BF16), vc_ref[...].astype(BF16))
    p_sum = p_c[0:Tq] + p_c[Tq:2 * Tq] + p_c[2 * Tq:3 * Tq] + p_c[3 * Tq:4 * Tq]
    imp = _cmp_to_sel(p_sum, amat_ref[...])
    sel_ref[...] = _topk_mask(_sel_scores(imp, pos // SEL_BLOCK, n_sel), n_sel, N_SEL)

    nw = kw_ref.shape[0]
    tok_w = past_len - wb + lax.broadcasted_iota(jnp.int32, (NSA_G * Tq, nw), 1)
    m_w = (tok_w >= 0) & (tok_w <= pos4) & (tok_w > pos4 - WINDOW)
    p_w = _masked_softmax_rows(_dot_nt(qs, kw_ref[...].astype(BF16)), m_w)
    owin_ref[...] = _dot(p_w.astype(BF16), vw_ref[...].astype(BF16))


def _nsa_sample_a(z3, kc, vc, kw_full, past_len, wb):
    N, Tq = z3.shape[:2]
    ns = kc.shape[1]
    nw = kw_full.shape[1]
    n_sel = -(-(past_len + Tq) // SEL_BLOCK)
    nsb = -(-n_sel // LANE) * LANE
    amat = _sel_map(ns, n_sel, nsb)
    qcol = 2 * D_HALF // (NSA_G * NSA_DH)
    R4 = NSA_G * Tq
    return pl.pallas_call(
        functools.partial(_nsa_sample_a_body, Tq=Tq, past_len=past_len, n_sel=n_sel, wb=wb),
        grid=(N, NSA_KV),
        in_specs=[
            pl.BlockSpec((None, Tq, NSA_G * NSA_DH), lambda n, kv: (n, 0, qcol + kv)),
            pl.BlockSpec((None, ns, NSA_DH), lambda n, kv: (n, 0, kv)),
            pl.BlockSpec((None, ns, NSA_DH), lambda n, kv: (n, 0, kv)),
            pl.BlockSpec((None, nw, NSA_DH), lambda n, kv: (n, 0, kv)),
            pl.BlockSpec((None, nw, NSA_DH), lambda n, kv: (n, 0, NSA_KV + kv)),
            pl.BlockSpec((ns, nsb), lambda n, kv: (0, 0)),
        ],
        out_specs=[
            pl.BlockSpec((None, None, R4, NSA_DH), lambda n, kv: (n, kv, 0, 0)),
            pl.BlockSpec((None, None, R4, NSA_DH), lambda n, kv: (n, kv, 0, 0)),
            pl.BlockSpec((None, None, Tq, nsb), lambda n, kv: (n, kv, 0, 0)),
        ],
        out_shape=[
            jax.ShapeDtypeStruct((N, NSA_KV, R4, NSA_DH), F32),
            jax.ShapeDtypeStruct((N, NSA_KV, R4, NSA_DH), F32),
            jax.ShapeDtypeStruct((N, NSA_KV, Tq, nsb), F32),
        ],
        compiler_params=_cparams("parallel", "parallel"),
        name="nsa_sample_cmp_win",
    )(z3, kc, vc, kw_full, kw_full, amat)


BLK_PER_STEP = PAGES_PER_STEP * PAGE // SEL_BLOCK


def _nsa_sample_b_body(pt_ref, *refs, Tq, past_len):
    page_refs = refs[:PAGES_PER_STEP]
    (q_ref, new_ref, sel_ref, exp_ref, ocmp_ref, owin_ref, zg_ref, gb_ref, o_ref, m_s, l_s, acc_s) = refs[PAGES_PER_STEP:]
    s_id = pl.program_id(1)
    R4 = NSA_G * Tq
    rep = lambda x: jnp.concatenate([x] * NSA_G, axis=0)
    pos = past_len + lax.broadcasted_iota(jnp.int32, (Tq, 1), 0)

    @pl.when(s_id == 0)
    def _():
        m_s[...] = jnp.full(m_s.shape, NEG, F32)
        l_s[...] = jnp.zeros(l_s.shape, F32)
        acc_s[...] = jnp.zeros(acc_s.shape, F32)

    def update(kv, s, mask, pv):
        s = jnp.where(mask, s, NEG)
        m_prev = m_s[kv]
        m_new = jnp.maximum(m_prev, jnp.max(s, axis=-1, keepdims=True))
        alpha = jnp.exp(m_prev - m_new)
        p = jnp.where(mask, jnp.exp(s - m_new), 0.0)
        l_s[kv] = alpha * l_s[kv] + jnp.sum(p, axis=-1, keepdims=True)
        acc_s[kv] = alpha * acc_s[kv] + pv(p.astype(BF16))
        m_s[kv] = m_new

    qs = [_stack_heads(q_ref[:, kv * NSA_G * NSA_DH:(kv + 1) * NSA_G * NSA_DH], NSA_DH ** -0.5) for kv in range(NSA_KV)]
    comp = lambda pr, c: pr[pl.ds(c, PAGE, stride=ROWS_PER_TOKEN), :].astype(BF16)
    groups = range(NSA_KV)
    scores = [jnp.concatenate([_dot_nt(qs[kv], comp(pr, kv)) for pr in page_refs], axis=1) for kv in groups]
    sel_tok = [_dot(sel_ref[kv].astype(BF16), exp_ref[...]) for kv in groups]
    stats = []
    for kv in groups:
        mask = rep(sel_tok[kv]) > 0.5
        s = jnp.where(mask, scores[kv], NEG)
        m_prev = m_s[kv]
        m_new = jnp.maximum(m_prev, jnp.max(s, axis=-1, keepdims=True))
        alpha = jnp.exp(m_prev - m_new)
        p = jnp.where(mask, jnp.exp(s - m_new), 0.0)
        stats.append((m_new, alpha, alpha * l_s[kv] + jnp.sum(p, axis=-1, keepdims=True), p.astype(BF16)))
    pvs = []
    for kv in groups:
        p = stats[kv][3]
        out = _dot(p[:, 0:PAGE], comp(page_refs[0], NSA_KV + kv))
        for i in range(1, PAGES_PER_STEP):
            out = out + _dot(p[:, i * PAGE:(i + 1) * PAGE], comp(page_refs[i], NSA_KV + kv))
        pvs.append(out)
    for kv in groups:
        m_new, alpha, l_new, _ = stats[kv]
        acc_s[kv] = alpha * acc_s[kv] + pvs[kv]
        l_s[kv] = l_new
        m_s[kv] = m_new

    @pl.when(s_id == pl.num_programs(1) - 1)
    def _():
        new = new_ref[...].astype(BF16)
        tok_n = past_len + lax.broadcasted_iota(jnp.int32, (R4, Tq), 1)
        gates = jax.nn.sigmoid(zg_ref[...] + gb_ref[...])
        for kv in range(NSA_KV):
            k_new = new[:, kv * NSA_DH:(kv + 1) * NSA_DH]
            v_new = new[:, (NSA_KV + kv) * NSA_DH:(NSA_KV + kv + 1) * NSA_DH]
            update(kv, _dot_nt(qs[kv], k_new), tok_n <= rep(pos), lambda p, v_new=v_new: _dot(p, v_new))
            o_sel = acc_s[kv] / l_s[kv]
            o_cmp = ocmp_ref[kv]
            o_win = owin_ref[kv]
            for g in range(NSA_G):
                rs = slice(g * Tq, (g + 1) * Tq)
                head = kv * NSA_G + g
                o_ref[:, head * NSA_DH:(head + 1) * NSA_DH] = (
                    gates[:, head:head + 1] * o_cmp[rs]
                    + gates[:, NSA_HEADS + head:NSA_HEADS + head + 1] * o_sel[rs]
                    + gates[:, 2 * NSA_HEADS + head:2 * NSA_HEADS + head + 1] * o_win[rs])


def _nsa_sample_b(z3, zg3, gate_b, pool_sel, page_table, sel, o_cmp, o_win, past_len):
    N, Tq = z3.shape[:2]
    n_pages = page_table.shape[1]
    assert n_pages % PAGES_PER_STEP == 0 and past_len % SEL_BLOCK == 0
    n_steps = n_pages // PAGES_PER_STEP
    sel_steps = sel[..., :past_len // SEL_BLOCK].reshape(N, NSA_KV, Tq, n_steps, BLK_PER_STEP).transpose(0, 3, 1, 2, 4)
    tok = np.arange(PAGES_PER_STEP * PAGE)[None, :]
    expand = jnp.asarray((tok // SEL_BLOCK == np.arange(BLK_PER_STEP)[:, None]).astype(np.float32), BF16)
    needed = (jnp.max(sel_steps, axis=(2, 3)) > 0).reshape(N, n_steps, PAGES_PER_STEP, PAGE // SEL_BLOCK).any(-1)
    last_needed = lax.cummax(jnp.where(needed, jnp.arange(n_steps, dtype=jnp.int32)[None, :, None], 0), axis=1)
    page_table = jnp.take_along_axis(page_table.reshape(N, n_steps, PAGES_PER_STEP), last_needed, axis=1).reshape(N, n_pages)
    W = 4 * NSA_DH
    R4 = NSA_G * Tq
    qcol = 2 * D_HALF // (NSA_HEADS * NSA_DH)
    kvs_col = (3 * D_HALF + W) // W
    page_specs = [
        pl.BlockSpec((PAGE_ROWS, NSA_DH), functools.partial(
            lambda n, s, pt, r: (pt[n, s * PAGES_PER_STEP + r], 0), r=r))
        for r in range(PAGES_PER_STEP)
    ]
    per_n4 =lambda shape: pl.BlockSpec((None,) + shape, lambda n, s, pt: (n, 0, 0, 0))
    grid_spec = pltpu.PrefetchScalarGridSpec(
        num_scalar_prefetch=1,
        grid=(N, n_pages // PAGES_PER_STEP),
        in_specs=page_specs + [
            pl.BlockSpec((None, Tq, NSA_HEADS * NSA_DH), lambda n, s, pt: (n, 0, qcol)),
            pl.BlockSpec((None, Tq, W), lambda n, s, pt: (n, 0, kvs_col)),
            pl.BlockSpec((None, None, NSA_KV, Tq, BLK_PER_STEP), lambda n, s, pt: (n, s, 0, 0, 0)),
            pl.BlockSpec((BLK_PER_STEP, PAGES_PER_STEP * PAGE), lambda n, s, pt: (0, 0)),
            per_n4((NSA_KV, R4, NSA_DH)),
            per_n4((NSA_KV, R4, NSA_DH)),
            pl.BlockSpec((None, Tq, LANE), lambda n, s, pt: (n, 0, 0)),
            pl.BlockSpec((1, LANE), lambda n, s, pt: (0, 0)),
        ],
        out_specs=pl.BlockSpec((None, Tq, NSA_HEADS * NSA_DH), lambda n, s, pt: (n, 0, 0)),
        scratch_shapes=[
            pltpu.VMEM((NSA_KV, R4, 1), F32),
            pltpu.VMEM((NSA_KV, R4, 1), F32),
            pltpu.VMEM((NSA_KV, R4, NSA_DH), F32),
        ],
    )
    return pl.pallas_call(
        functools.partial(_nsa_sample_b_body, Tq=Tq, past_len=past_len),
        grid_spec=grid_spec,
        out_shape=jax.ShapeDtypeStruct((N, Tq, NSA_HEADS * NSA_DH), F32),
        compiler_params=_cparams("parallel", "arbitrary"),
        name="nsa_sample_sel",
    )(page_table, *([pool_sel] * PAGES_PER_STEP), z3, z3, sel_steps, expand, o_cmp, o_win, zg3, gate_b)


MLSTM_CHUNK_PROMPT = 256
SAMPLE_PAD = 128


def _pad_rows(x, rows):
    return jnp.pad(x, ((0, 0), (0, rows - x.shape[1]), (0, 0)))


def _lane_pad(v):
    return jnp.pad(v.astype(F32), (0, LANE - v.shape[0])).reshape(1, LANE)


def _gate_rows(w_in_t, n_main):
    return jnp.pad(w_in_t[n_main:], ((0, LANE - (w_in_t.shape[0] - n_main)), (0, 0)))


def _even_mixers(z3, zg3, t0, pool_buf, C0, n0, m0, w):
    N, T = z3.shape[:2]
    st = jnp.pad(pool_buf, ((0, 0), (HALO - POOL_PAD, 0), (0, 0)))
    y_a = _pool_mix(z3, st, w["pool_w"], w["pool_scale"], t0)
    u_ext_tail = jnp.concatenate([pool_buf, z3[:, :, :D_HALF]], axis=1)[:, -POOL_PAD:] if T < POOL_PAD else z3[:, -POOL_PAD:, :D_HALF]

    if T % MLSTM_CHUNK_PROMPT == 0:
        L, zm, zgm = MLSTM_CHUNK_PROMPT, z3, zg3
    else:
        L, zm, zgm = SAMPLE_PAD, _pad_rows(z3, SAMPLE_PAD), _pad_rows(zg3, SAMPLE_PAD)
    y_b, C, n, m = _mlstm(zm, zgm, w["gate_b"], w["mnorm_g"], C0, n0, m0, L, min(T, L))
    return y_a, y_b[:, :T], u_ext_tail, C, n, m


def _kv_rows_body(x_ref, o_ref, *, tT):
    for c in range(ROWS_PER_TOKEN):
        o_ref[pl.ds(c, tT, stride=ROWS_PER_TOKEN), :] = x_ref[:, c * NSA_DH:(c + 1) * NSA_DH]


def _kv_rows(z3, col_block, first_row, n_rows, tT_pref=512):
    N = z3.shape[0]
    tT = _tile(n_rows, tT_pref)
    assert first_row % tT == 0
    rows = pl.pallas_call(
        functools.partial(_kv_rows_body, tT=tT),
        grid=(N, n_rows // tT),
        in_specs=[pl.BlockSpec((None, tT, ROWS_PER_TOKEN * NSA_DH), lambda n, t: (n, first_row // tT + t, col_block))],
        out_specs=pl.BlockSpec((None, ROWS_PER_TOKEN * tT, NSA_DH), lambda n, t: (n, t, 0)),
        out_shape=jax.ShapeDtypeStruct((N, ROWS_PER_TOKEN * n_rows, NSA_DH), F32),
        compiler_params=_cparams("parallel", "parallel"),
        name="kv_rows",
    )(z3)
    return rows.reshape(N, n_rows, 2, NSA_KV, NSA_DH)


def _odd_mixers(z3, zg3, w, past=None):
    N, T, n_main = z3.shape
    W = 4 * NSA_DH
    kv_col = 3 * D_HALF // W
    kv5 = lambda t: t.reshape(N, t.shape[1], 2, NSA_KV, NSA_DH)

    if past is None:
        y_c, = _gmlp(z3, w["gnorm_g"], w["ws"], w["bs_t"], keep_vn=False)
        pt = jnp.arange(N * (T // PAGE), dtype=jnp.int32).reshape(N, T // PAGE)
        a, b = _compress_ab(z3.reshape(N * (T // PAGE), PAGE, n_main), pt, w["wab"], kv_col)
        kc, vc = _cmp_proj(a, b, w["cmp_w"])
        o = _nsa_prompt(z3, zg3, w["nsa_gate_b"], kc, vc)
        wlen = min(WINDOW, T)
        return (y_c, o, None, _kv_rows(z3, kv_col, 0, T), _kv_rows(z3, kv_col + 1, 0, T),
                _kv_rows(z3, kv_col + 2, T - wlen, wlen))

    kvc = z3[:, :, 3 * D_HALF:3 * D_HALF + W]
    kvs = z3[:, :, 3 * D_HALF + W:3 * D_HALF + 2 * W]
    kvw = z3[:, :, 3 * D_HALF + 2 * W:3 * D_HALF + 3 * W]
    win_buf, pool_cmp, pool_sel, page_table = past
    past_len = page_table.shape[1] * PAGE
    wb = win_buf.shape[1]
    y_c, vn = _gmlp(_pad_rows(z3[:, :, :2 * D_HALF], SAMPLE_PAD), w["gnorm_g"], w["ws"], w["bs_t"], keep_vn=True)
    y_c, vn = y_c[:, :T], vn[:, :T]
    assert (past_len + T) // CMP_STRIDE == past_len // CMP_STRIDE
    a, b = _compress_ab_rows(pool_cmp, page_table, w["wab_rows"])
    kc, vc = _cmp_proj(a, b, w["cmp_w"])
    kw_all = jnp.concatenate([win_buf.reshape(N, wb, W), kvw], axis=1)
    nw = -(-(wb + T) // LANE) * LANE
    o_cmp, o_win, sel = _nsa_sample_a(z3, kc, vc, _pad_rows(kw_all, nw), past_len, wb)
    o = _nsa_sample_b(z3, zg3, w["nsa_gate_b"], pool_sel, page_table, sel, o_cmp, o_win, past_len)
    return y_c, o, vn, kv5(kvc), kv5(kvs), kv5(kw_all[:, -wb:])


def _layer(xp, xs, xpn, xsn, w, ffn_stacks, layer, n_main, mixers_p, mixers_s):
    B, T, D = xp.shape
    Ns, Ts, _ = xs.shape
    xp2, xs2 = xp.reshape(B * T, D), xs.reshape(Ns * Ts, D)
    zp, zgp, zs, zgs = _in_proj(xpn, xsn, w["w_in"], w["w_in_layer"], n_main, w["w_gate"])
    a1p, a2p, *extra_p = mixers_p(zp.reshape(B, T, n_main), zgp.reshape(B, T, LANE))
    a1s, a2s, *extra_s = mixers_s(zs.reshape(Ns, Ts, n_main), zgs.reshape(Ns, Ts, LANE))
    flat = lambda t: t.reshape(-1, D_HALF)
    xp2, xpn, xs2, xsn = _out_proj(flat(a1p), flat(a2p), xp2, flat(a1s), flat(a2s), xs2, w["w_out_a"], w["w_out_b"],
                                   w["ng"][1:2], w["ng"][2:3])
    w1_stack, w3_stack, w2_stack = ffn_stacks
    hp, hs, w2_bf = _ffn_up(xpn, xsn, w1_stack, w3_stack, w2_stack, layer)
    xp2, xs2, xpn, xsn = _ffn_down(hp, hs, xp2, xs2, w["ng"][3:4], w["ng_next"], w2_bf)
    return xp2.reshape(B, T, D), xs2.reshape(Ns, Ts, D), xpn, xsn, extra_p, extra_s


def kernel(x_prompt, x_sample, state_pool, state_mlstm_c, state_mlstm_n, state_mlstm_m, state_win_kv, cache_cmp_kv, cache_sel_kv, page_table, norm_g, w_in_even, w_out_even, pool_w, pool_scale, mlstm_gate_b, mlstm_norm_g, w_in_odd, w_out_odd, gmlp_norm_g, gmlp_ws, gmlp_bs, nsa_cmp_pos, nsa_cmp_w, nsa_gate_b, ffn_w1, ffn_w3, ffn_w2):
    B = x_prompt.shape[0]
    depth = norm_g.shape[0]
    past_len = page_table.shape[1] * PAGE
    xp, xs = x_prompt, x_sample
    pool_p, pool_s, c_p, c_s, n_p, n_s, m_p, m_s = [], [], [], [], [], [], [], []
    gv_s, cmp_p, cmp_s, sel_p, sel_s, win_p, win_s = [], [], [], [], [], [], []
    ffn_stacks = (ffn_w1, ffn_w3, ffn_w2)
    D = x_prompt.shape[-1]
    xpn, xsn = _rms_cast(x_prompt.reshape(-1, D), x_sample.reshape(-1, D), norm_g[0][0:1])
    for l in range(depth):
        j = l // 2
        ng_next = norm_g[min(l + 1, depth - 1)][0:1]
        if l % 2 == 0:
            n_main = 4 * D_HALF
            w_out = w_out_even[j].astype(BF16)
            w_in_t = jnp.swapaxes(w_in_even, 1, 2)
            w = dict(ng=norm_g[l], ng_next=ng_next, w_in=w_in_t, w_in_layer=j, w_gate=_gate_rows(w_in_t[j], n_main),
                     w_out_a=w_out[:D_HALF], w_out_b=w_out[D_HALF:],
                     pool_w=pool_w[j].astype(BF16), pool_scale=pool_scale[j].reshape(1, D_HALF),
                     gate_b=_lane_pad(mlstm_gate_b[j].reshape(-1)), mnorm_g=mlstm_norm_g[j].reshape(1, D_HALF))
            zp = jnp.zeros((B, POOL_PAD, D_HALF), F32)
            zc = jnp.zeros((B, MLSTM_HEADS, MLSTM_DK, MLSTM_DV), F32)
            zn = jnp.zeros((B, MLSTM_HEADS, MLSTM_DK), F32)
            zm = jnp.zeros((B, MLSTM_HEADS), F32)
            xp, xs, xpn, xsn, (pb, c, n, m), (pbs, cs, ns_, ms) = _layer(
                xp, xs, xpn, xsn, w, ffn_stacks, l, n_main,
                lambda z3, zg3: _even_mixers(z3, zg3, 0, zp, zc, zn, zm, w),
                lambda z3, zg3: _even_mixers(z3, zg3, past_len, state_pool[j], state_mlstm_c[j], state_mlstm_n[j],
                                             state_mlstm_m[j], w))
            pool_p.append(pb); c_p.append(c); n_p.append(n); m_p.append(m)
            pool_s.append(pbs); c_s.append(cs); n_s.append(ns_); m_s.append(ms)
        else:
            n_main = 4 * D_HALF + D_HALF // 2
            w_out = w_out_odd[j].astype(BF16)
            cp = nsa_cmp_pos[j]
            wcol = jnp.repeat(cp, NSA_KV * NSA_DH, axis=1)
            wab = jnp.stack([wcol[:CMP_STRIDE], wcol[CMP_STRIDE:]])
            wab_rows = jnp.broadcast_to(jnp.repeat(cp, NSA_KV, axis=1).reshape(2, BLK_ROWS, 1), (2, BLK_ROWS, NSA_DH))
            w_in_t = jnp.swapaxes(w_in_odd, 1, 2)
            w = dict(ng=norm_g[l], ng_next=ng_next, w_in=w_in_t, w_in_layer=j, w_gate=_gate_rows(w_in_t[j], n_main),
                     w_out_a=w_out[:D_HALF], w_out_b=w_out[D_HALF:],
                     gnorm_g=gmlp_norm_g[j].reshape(1, D_HALF), ws=gmlp_ws[j],
                     bs_t=jnp.pad(gmlp_bs[j].T, ((0, 0), (0, LANE - GMLP_GROUPS))),
                     wab=wab, wab_rows=wab_rows, cmp_w=nsa_cmp_w[j], nsa_gate_b=_lane_pad(nsa_gate_b[j]))
            n_pool = cache_cmp_kv.shape[1]
            flat = lambda c: c.reshape(c.shape[0] * n_pool * PAGE_ROWS, NSA_DH)
            past = (state_win_kv[j], flat(cache_cmp_kv), flat(cache_sel_kv), page_table + j * n_pool)
            xp, xs, xpn, xsn, (_, kc, ksl, wv), (vn, kcs, ksls, wvs) = _layer(
                xp, xs, xpn, xsn, w, ffn_stacks, l, n_main,
                lambda z3, zg3: _odd_mixers(z3, zg3, w),
                lambda z3, zg3: _odd_mixers(z3, zg3, w, past=past))
            cmp_p.append(kc); sel_p.append(ksl); win_p.append(wv)
            gv_s.append(vn); cmp_s.append(kcs); sel_s.append(ksls); win_s.append(wvs)
    st = jnp.stack
    return (xp, xs, st(pool_p), st(pool_s), st(c_p), st(c_s), st(n_p), st(n_s), st(m_p), st(m_s),
            st(gv_s), st(cmp_p), st(cmp_s), st(sel_p), st(sel_s), st(win_p), st(win_s))
```

```python
import functools

import numpy as np
import jax
import jax.numpy as jnp
from jax import lax
from jax.experimental import pallas as pl
from jax.experimental.pallas import tpu as pltpu

F32 = jnp.float32
BF16 = jnp.bfloat16
NEG = -1e30

D_HALF = 1024
POOL_WINDOWS = (2, 4, 8, 16)
POOL_DG = 256
POOL_PAD = 15
HALO = 16
MLSTM_HEADS = 4
MLSTM_DK = 128
MLSTM_DV = 256
GATE_CAP = 15.0
GMLP_CHUNK = 128
GMLP_GROUPS = 4
GMLP_DG = 256
NSA_HEADS = 8
NSA_DH = 128
NSA_KV = 2
NSA_G = 4
CMP_STRIDE = 16
SEL_BLOCK = 64
N_SEL = 16
WINDOW = 512
QBLK = 128
PAGE = 128
FORCE_SCORE = 1e9
LANE = 128
VMEM_LIMIT = 56 * 1024 * 1024

NT_DIMS = (((1,), (1,)), ((), ()))
TN_DIMS = (((0,), (0,)), ((), ()))


def _cparams(*sem):
    return pltpu.CompilerParams(dimension_semantics=sem, vmem_limit_bytes=VMEM_LIMIT)


def _tile(m, pref):
    if m <= pref:
        return m
    for t in range(pref, 7, -1):
        if m % t == 0 and t % 8 == 0:
            return t
    return m


def _dot(a, b):
    return jnp.dot(a, b, preferred_element_type=F32)


def _dot_nt(a, b):
    return lax.dot_general(a, b, NT_DIMS, preferred_element_type=F32)


def _rms(x, g, eps=1e-6):
    return x * lax.rsqrt(jnp.mean(x * x, axis=-1, keepdims=True) + eps) * g


def _rms_cast_body(x_ref, xs_ref, g_ref, o_ref, os_ref):
    o_ref[...] = _rms(x_ref[...], g_ref[...]).astype(BF16)

    @pl.when(pl.program_id(0) == 0)
    def _():
        os_ref[...] = _rms(xs_ref[...], g_ref[...]).astype(BF16)


def _rms_cast(x, xs, g, tm_pref=1024):
    M, K = x.shape
    Ms = xs.shape[0]
    tm = _tile(M, tm_pref)
    return pl.pallas_call(
        _rms_cast_body,
        grid=(M // tm,),
        in_specs=[pl.BlockSpec((tm, K), lambda i: (i, 0)), pl.BlockSpec((Ms, K), lambda i: (0, 0)),
                  pl.BlockSpec((1, K), lambda i: (0, 0))],
        out_specs=[pl.BlockSpec((tm, K), lambda i: (i, 0)), pl.BlockSpec((Ms, K), lambda i: (0, 0))],
        out_shape=[jax.ShapeDtypeStruct((M, K), BF16), jax.ShapeDtypeStruct((Ms, K), BF16)],
        compiler_params=_cparams("arbitrary"),
        name="rms_cast",
    )(x, xs, g)


def _in_proj_body(xn_ref, xsn_ref, w_ref, wg_ref, o_ref, og_ref, os_ref, ogs_ref, w_bf):
    j = pl.program_id(0)
    i = pl.program_id(1)

    @pl.when(i == 0)
    def _():
        w_bf[...] = w_ref[...].astype(BF16)
        os_ref[...] = _dot_nt(xsn_ref[...], w_bf[...])

    o_ref[...] = _dot_nt(xn_ref[...], w_bf[...])

    @pl.when(j == 0)
    def _():
        wg = wg_ref[...].astype(BF16)
        og_ref[...] = _dot_nt(xn_ref[...], wg)

        @pl.when(i == 0)
        def _():
            ogs_ref[...] = _dot_nt(xsn_ref[...], wg)


def _in_proj(xn, xsn, wt_stack, layer, n_main, wg_t, tm_pref=1024, tn_pref=1152):
    M, K = xn.shape
    Ms = xsn.shape[0]
    tm = _tile(M, tm_pref)
    tn = max(t for t in range(LANE, tn_pref + 1, LANE) if n_main % t == 0)
    ni = M // tm
    return pl.pallas_call(
        _in_proj_body,
        grid=(n_main // tn, ni),
        in_specs=[
            pl.BlockSpec((tm, K), lambda j, i: (i, 0)),
            pl.BlockSpec((Ms, K), lambda j, i: (0, 0)),
            pl.BlockSpec((None, tn, K), lambda j, i: (layer, j, 0)),
            pl.BlockSpec((LANE, K), lambda j, i: (0, 0)),
        ],
        out_specs=[
            pl.BlockSpec((tm, tn), lambda j, i: (i, j)),
            pl.BlockSpec((tm, LANE), lambda j, i: (jnp.where(j == 0, i, ni - 1), 0)),
            pl.BlockSpec((Ms, tn), lambda j, i: (0, j)),
            pl.BlockSpec((Ms, LANE), lambda j, i: (0, 0)),
        ],
        out_shape=[jax.ShapeDtypeStruct((M, n_main), F32), jax.ShapeDtypeStruct((M, LANE), F32),
                   jax.ShapeDtypeStruct((Ms, n_main), F32), jax.ShapeDtypeStruct((Ms, LANE), F32)],
        scratch_shapes=[pltpu.VMEM((tn, K), BF16)],
        compiler_params=_cparams("arbitrary", "arbitrary"),
        name="in_proj",
    )(xn, xsn, wt_stack, wg_t)


def _out_proj_body(a1_ref, a2_ref, res_ref, a1s_ref, a2s_ref, ress_ref, w1_ref, w2_ref, g_ref, gn_ref,
                   o_ref, on_ref, os_ref, osn_ref):
    def proj(a1, a2, res, o_r, on_r):
        y = _dot(a1[...].astype(BF16), w1_ref[...]) + _dot(a2[...].astype(BF16), w2_ref[...])
        x = res[...] + _rms(y, g_ref[...])
        o_r[...] = x
        on_r[...] = _rms(x, gn_ref[...]).astype(BF16)

    proj(a1_ref, a2_ref, res_ref, o_ref, on_ref)

    @pl.when(pl.program_id(0) == 0)
    def _():
        proj(a1s_ref, a2s_ref, ress_ref, os_ref, osn_ref)


def _out_proj(a1, a2, res, a1s, a2s, ress, w1, w2, g, g_next, tm_pref=512):
    M, K1 = a1.shape
    Ms = a1s.shape[0]
    K2 = a2.shape[1]
    D = w1.shape[1]
    tm = _tile(M, tm_pref)
    whole = lambda r, c: pl.BlockSpec((r, c), lambda i: (0, 0))
    rows = lambda c: pl.BlockSpec((tm, c), lambda i: (i, 0))
    return pl.pallas_call(
        _out_proj_body,
        grid=(M // tm,),
        in_specs=[
            rows(K1), rows(K2), rows(D),
            whole(Ms, K1), whole(Ms, K2), whole(Ms, D),
            whole(K1, D), whole(K2, D), whole(1, D), whole(1, D),
        ],
        out_specs=[rows(D), rows(D), whole(Ms, D), whole(Ms, D)],
        out_shape=[jax.ShapeDtypeStruct((M, D), F32), jax.ShapeDtypeStruct((M, D), BF16),
                   jax.ShapeDtypeStruct((Ms, D), F32), jax.ShapeDtypeStruct((Ms, D), BF16)],
        compiler_params=_cparams("arbitrary"),
        name="out_proj",
    )(a1, a2, res, a1s, a2s, ress, w1, w2, g, g_next)


def _ffn_up_body(xn_ref, xsn_ref, w1_ref, w3_ref, w2_ref, h_ref, hs_ref, w2_out, w1_bf, w3_bf):
    def swiglu(xn):
        h1 = _dot(xn, w1_bf[...])
        return (h1 * jax.nn.sigmoid(h1) * _dot(xn, w3_bf[...])).astype(BF16)

    @pl.when(pl.program_id(1) == 0)
    def _():
        w1_bf[...] = w1_ref[...].astype(BF16)
        w3_bf[...] = w3_ref[...].astype(BF16)
        w2_out[...] = w2_ref[...].astype(BF16)
        hs_ref[...] = swiglu(xsn_ref[...])

    h_ref[...] = swiglu(xn_ref[...])


def _ffn_up(xn, xsn, w1_stack, w3_stack, w2_stack, layer, tm_pref=1024, th_pref=512):
    M, D = xn.shape
    Ms = xsn.shape[0]
    H = w1_stack.shape[2]
    tm = _tile(M, tm_pref)
    th = _tile(H, th_pref)
    return pl.pallas_call(
        _ffn_up_body,
        grid=(H // th, M // tm),
        in_specs=[
            pl.BlockSpec((tm, D), lambda j, i: (i, 0)),
            pl.BlockSpec((Ms, D), lambda j, i: (0, 0)),
            pl.BlockSpec((None, D, th), lambda j, i: (layer, 0, j)),
            pl.BlockSpec((None, D, th), lambda j, i: (layer, 0, j)),
            pl.BlockSpec((None, th, D), lambda j, i: (layer, j, 0)),
        ],
        out_specs=[
            pl.BlockSpec((tm, th), lambda j, i: (i, j)),
            pl.BlockSpec((Ms, th), lambda j, i: (0, j)),
            pl.BlockSpec((th, D), lambda j, i: (j, 0)),
        ],
        out_shape=[jax.ShapeDtypeStruct((M, H), BF16), jax.ShapeDtypeStruct((Ms, H), BF16),
                   jax.ShapeDtypeStruct((H, D), BF16)],
        scratch_shapes=[pltpu.VMEM((D, th), BF16), pltpu.VMEM((D, th), BF16)],
        compiler_params=_cparams("arbitrary", "arbitrary"),
        name="ffn_up",
    )(xn, xsn, w1_stack, w3_stack, w2_stack)


def _ffn_down_body(h_ref, hs_ref, x_ref, xs_ref, g3_ref, gn_ref, w2_ref, o_ref, os_ref, on_ref, osn_ref):
    def rows(h_r, x_r, o_r, on_r):
        x = x_r[...] + _rms(_dot(h_r[...], w2_ref[...]), g3_ref[...])
        o_r[...] = x
        on_r[...] = _rms(x, gn_ref[...]).astype(BF16)

    rows(h_ref, x_ref, o_ref, on_ref)

    @pl.when(pl.program_id(0) == 0)
    def _():
        rows(hs_ref, xs_ref, os_ref, osn_ref)


def _ffn_down(h, hs, x, xs, g3, g_next, w2, tm_pref=256):
    M, D = x.shape
    Ms = xs.shape[0]
    H = h.shape[1]
    tm = _tile(M, tm_pref)
    whole = lambda r, c: pl.BlockSpec((r, c), lambda i: (0, 0))
    rows = lambda c: pl.BlockSpec((tm, c), lambda i: (i, 0))
    return pl.pallas_call(
        _ffn_down_body,
        grid=(M // tm,),
        in_specs=[rows(H), whole(Ms, H), rows(D), whole(Ms, D), whole(1, D), whole(1, D), whole(H, D)],
        out_specs=[rows(D), whole(Ms, D), rows(D), whole(Ms, D)],
        out_shape=[jax.ShapeDtypeStruct((M, D), F32), jax.ShapeDtypeStruct((Ms, D), F32),
                   jax.ShapeDtypeStruct((M, D), BF16), jax.ShapeDtypeStruct((Ms, D), BF16)],
        compiler_params=_cparams("arbitrary"),
        name="ffn_down",
    )(h, hs, x, xs, g3, g_next, w2)


def _pool_body(u_ref, prev_ref, st_ref, pw_ref, ps_ref, o_ref, ext_ref, *, tT, t0):
    t = pl.program_id(1)
    ext_ref[0:HALO, :] = jnp.where(t == 0, st_ref[...], prev_ref[...])
    ext_ref[HALO:HALO + tT, :] = u_ref[...]
    pos = t0 + t * tT + lax.broadcasted_iota(jnp.int32, (tT, 1), 0)
    for g, w in enumerate(POOL_WINDOWS):
        cs = slice(g * POOL_DG, (g + 1) * POOL_DG)
        x_new = ext_ref[HALO:HALO + tT, cs]
        tot = x_new
        for i in range(1, w):
            tot = tot + ext_ref[HALO - i:HALO - i + tT, cs]
        cnt = jnp.minimum(w, pos + 1).astype(F32)
        y = tot / cnt - x_new
        o_ref[:, cs] = (_dot(y.astype(BF16), pw_ref[g]) * ps_ref[:, cs]).astype(o_ref.dtype)


def _pool_mix(z3, st, pool_w, pool_scale, t0, tT_pref=512):
    N, T = z3.shape[:2]
    tT = _tile(T, tT_pref)
    nT = T // tT
    if nT > 1:
        assert tT % HALO == 0
        prev, prev_spec = z3, pl.BlockSpec((None, HALO, D_HALF), lambda n, t: (n, jnp.maximum(t * (tT // HALO) - 1, 0), 0))
    else:
        prev, prev_spec = st, pl.BlockSpec((None, HALO, D_HALF), lambda n, t: (n, 0, 0))
    return pl.pallas_call(
        functools.partial(_pool_body, tT=tT, t0=t0),
        grid=(N, nT),
        in_specs=[
            pl.BlockSpec((None, tT, D_HALF), lambda n, t: (n, t, 0)),
            prev_spec,
            pl.BlockSpec((None, HALO, D_HALF), lambda n, t: (n, 0, 0)),
            pl.BlockSpec((len(POOL_WINDOWS), POOL_DG, POOL_DG), lambda n, t: (0, 0, 0)),
            pl.BlockSpec((1, D_HALF), lambda n, t: (0, 0)),
        ],
        out_specs=pl.BlockSpec((None, tT, D_HALF), lambda n, t: (n, t, 0)),
        out_shape=jax.ShapeDtypeStruct((N, T, D_HALF), BF16 if tT % 16 == 0 else F32),
        scratch_shapes=[pltpu.VMEM((HALO + tT, D_HALF), F32)],
        compiler_params=_cparams("parallel", "arbitrary"),
        name="pool_mix",
    )(z3, prev, st, pool_w, pool_scale)


def _mlstm_body(q_ref, k_ref, v_ref, o_ref, zg_ref, gb_ref, mg_ref, c0_ref, n0_ref, m0_ref,
                y_ref, cN_ref, nN_ref, mN_ref, C_s, n_s, m_s, *, L, t_valid):
    c = pl.program_id(1)

    @pl.when(c == 0)
    def _():
        C_s[...] = c0_ref[...]
        n_s[...] = n0_ref[...]
        m_s[...] = m0_ref[...]

    a = GATE_CAP * jnp.tanh((zg_ref[...] + gb_ref[...]) / GATE_CAP)
    lane = lax.broadcasted_iota(jnp.int32, (L, LANE), 1)
    logsig = jnp.minimum(a, 0.0) - jnp.log1p(jnp.exp(-jnp.abs(a)))
    A = jnp.where(lane < MLSTM_HEADS, a, logsig)
    if t_valid < L:
        row = lax.broadcasted_iota(jnp.int32, (L, LANE), 0)
        A = jnp.where(row < t_valid, A, jnp.where(lane < MLSTM_HEADS, NEG, 0.0))
    r_i = lax.broadcasted_iota(jnp.int32, (L, L), 0)
    c_i = lax.broadcasted_iota(jnp.int32, (L, L), 1)
    causal = r_i >= c_i
    Bc = jnp.dot(causal.astype(F32), A, preferred_element_type=F32, precision=lax.Precision.HIGHEST)
    At = A.T
    Bt = Bc.T

    heads = range(MLSTM_HEADS)
    ks = lambda h: slice(h * MLSTM_DK, (h + 1) * MLSTM_DK)
    vs = lambda h: slice(h * MLSTM_DV, (h + 1) * MLSTM_DV)
    m0 = [m_s[h] for h in heads]
    C0 = [C_s[h] for h in heads]
    n0 = [n_s[h] for h in heads]
    qf = [q_ref[:, ks(h)] * (MLSTM_DK ** -0.5) for h in heads]
    qb = [q.astype(BF16) for q in qf]
    kf = [k_ref[:, ks(h)] for h in heads]
    vb = [v_ref[:, vs(h)].astype(BF16) for h in heads]
    qk = [_dot_nt(qb[h], kf[h].astype(BF16)) for h in heads]
    qc = [_dot(qb[h], C0[h].astype(BF16)) for h in heads]

    b_c = [Bc[:, MLSTM_HEADS + h:MLSTM_HEADS + h + 1] for h in heads]
    m_t, w_in, s = [], [], []
    for h in heads:
        b_r = Bt[MLSTM_HEADS + h:MLSTM_HEADS + h + 1, :]
        d = jnp.where(causal, b_c[h] - b_r + At[h:h + 1, :], NEG)
        inter = b_c[h] + m0[h]
        m_t.append(jnp.maximum(inter, jnp.max(d, axis=-1, keepdims=True)))
        w_in.append(jnp.exp(inter - m_t[h]))
        s.append(qk[h] * jnp.exp(d - m_t[h]))
    sv = [_dot(s[h].astype(BF16), vb[h]) for h in heads]

    kw, w_c, m_new = [], [], []
    for h in heads:
        b_end = b_c[h][L - 1:L, :]
        g_c = b_end - b_c[h] + A[:, h:h + 1]
        m_new.append(jnp.maximum(b_end + m0[h], jnp.max(g_c, axis=0, keepdims=True)))
        w_c.append(jnp.exp(b_end + m0[h] - m_new[h]))
        kw.append(kf[h] * jnp.exp(g_c - m_new[h]))
    kv = [lax.dot_general(kw[h].astype(BF16), vb[h], TN_DIMS, preferred_element_type=F32) for h in heads]

    for h in heads:
        num = w_in[h] * qc[h] + sv[h]
        den = w_in[h] * jnp.sum(qf[h] * n0[h], axis=-1, keepdims=True) + jnp.sum(s[h], axis=-1, keepdims=True)
        hh = num / jnp.maximum(jnp.abs(den), jnp.exp(-m_t[h]))
        y_ref[:, vs(h)] = (jax.nn.sigmoid(o_ref[:, vs(h)]) * _rms(hh, mg_ref[:, vs(h)])).astype(BF16)
    for h in heads:
        C_s[h] = w_c[h] * C0[h] + kv[h]
        n_s[h] = w_c[h] * n0[h] + jnp.sum(kw[h], axis=0, keepdims=True)
        m_s[h] = m_new[h]

    @pl.when(c == pl.num_programs(1) - 1)
    def _():
        cN_ref[...] = C_s[...]
        nN_ref[...] = n_s[...]
        mN_ref[...] = m_s[...]


def _mlstm(z3, zg3, gate_b, mnorm_g, C0, n0, m0, L, t_valid):
    N, Tp = z3.shape[:2]
    nc = Tp // L
    H = MLSTM_HEADS
    qk_w = H * MLSTM_DK
    v_w = H * MLSTM_DV
    st = lambda n, c: (n, 0, 0, 0)
    outs = pl.pallas_call(
        functools.partial(_mlstm_body, L=L, t_valid=t_valid),
        grid=(N, nc),
        in_specs=[
            pl.BlockSpec((None, L, qk_w), lambda n, c: (n, c, D_HALF // qk_w)),
            pl.BlockSpec((None, L, qk_w), lambda n, c: (n, c, D_HALF // qk_w + 1)),
            pl.BlockSpec((None, L, v_w), lambda n, c: (n, c, 2)),
            pl.BlockSpec((None, L, v_w), lambda n, c: (n, c, 3)),
            pl.BlockSpec((None, L, LANE), lambda n, c: (n, c, 0)),
            pl.BlockSpec((1, LANE), lambda n, c: (0, 0)),
            pl.BlockSpec((1, v_w), lambda n, c: (0, 0)),
            pl.BlockSpec((None, H, MLSTM_DK, MLSTM_DV), st),
            pl.BlockSpec((None, H, 1, MLSTM_DK), st),
            pl.BlockSpec((None, H, 1, 1), st),
        ],
        out_specs=[
            pl.BlockSpec((None, L, v_w), lambda n, c: (n, c, 0)),
            pl.BlockSpec((None, H, MLSTM_DK, MLSTM_DV), st),
            pl.BlockSpec((None, H, 1, MLSTM_DK), st),
            pl.BlockSpec((None, H, 1, 1), st),
        ],
        out_shape=[
            jax.ShapeDtypeStruct((N, Tp, v_w), BF16),
            jax.ShapeDtypeStruct((N, H, MLSTM_DK, MLSTM_DV), F32),
            jax.ShapeDtypeStruct((N, H, 1, MLSTM_DK), F32),
            jax.ShapeDtypeStruct((N, H, 1, 1), F32),
        ],
        scratch_shapes=[
            pltpu.VMEM((H, MLSTM_DK, MLSTM_DV), F32),
            pltpu.VMEM((H, 1, MLSTM_DK), F32),
            pltpu.VMEM((H, 1, 1), F32),
        ],
        compiler_params=_cparams("parallel", "arbitrary"),
        name="mlstm",
    )(z3, z3, z3, z3, zg3, gate_b, mnorm_g, C0, n0.reshape(N, H, 1, MLSTM_DK), m0.reshape(N, H, 1, 1))
    y, C, n, m = outs
    return y, C, n.reshape(N, H, MLSTM_DK), m.reshape(N, H)


def _gmlp_body(u_ref, v_ref, g_ref, ws_ref, bs_ref, y_ref, *vn_refs, tT):
    v = v_ref[...]
    vc = v - jnp.mean(v, axis=-1, keepdims=True)
    vn = vc * lax.rsqrt(jnp.mean(vc * vc, axis=-1, keepdims=True) + 1e-5) * g_ref[...]
    for vn_ref in vn_refs:
        vn_ref[...] = vn
    r_i = lax.broadcasted_iota(jnp.int32, (GMLP_CHUNK, GMLP_CHUNK), 0)
    c_i = lax.broadcasted_iota(jnp.int32, (GMLP_CHUNK, GMLP_CHUNK), 1)
    for g in range(GMLP_GROUPS):
        cs = slice(g * GMLP_DG, (g + 1) * GMLP_DG)
        wm = jnp.where(r_i >= c_i, ws_ref[g], 0.0).astype(BF16)
        bias = bs_ref[:, g:g + 1]
        for ch in range(tT // GMLP_CHUNK):
            rs = slice(ch * GMLP_CHUNK, (ch + 1) * GMLP_CHUNK)
            mix = _dot(wm, vn[rs, cs].astype(BF16)) + bias
            y_ref[rs, cs] = (u_ref[rs, cs] * mix).astype(BF16)


def _gmlp(z3, gnorm_g, ws, bs_t, keep_vn, tT_pref=512):
    N, Tp = z3.shape[:2]
    tT = _tile(Tp, tT_pref)
    assert tT % GMLP_CHUNK == 0
    n_out = 2 if keep_vn else 1
    return pl.pallas_call(
        functools.partial(_gmlp_body, tT=tT),
        grid=(N, Tp // tT),
        in_specs=[
            pl.BlockSpec((None, tT, D_HALF), lambda n, t: (n, t, 0)),
            pl.BlockSpec((None, tT, D_HALF), lambda n, t: (n, t, 1)),
            pl.BlockSpec((1, D_HALF), lambda n, t: (0, 0)),
            pl.BlockSpec((GMLP_GROUPS, GMLP_CHUNK, GMLP_CHUNK), lambda n, t: (0, 0, 0)),
            pl.BlockSpec((GMLP_CHUNK, LANE), lambda n, t: (0, 0)),
        ],
        out_specs=[pl.BlockSpec((None, tT, D_HALF), lambda n, t: (n, t, 0))] * n_out,
        out_shape=[jax.ShapeDtypeStruct((N, Tp, D_HALF), BF16), jax.ShapeDtypeStruct((N, Tp, D_HALF), F32)][:n_out],
        compiler_params=_cparams("parallel", "parallel"),
        name="gmlp",
    )(z3, z3, gnorm_g, ws, bs_t)


PAGES_PER_STEP = 32
BLK_PER_PAGE = PAGE // CMP_STRIDE


def _compress_body(pt_ref, *refs):
    page_refs = refs[:PAGES_PER_STEP]
    wab_ref, a_ref, b_ref = refs[PAGES_PER_STEP:]
    wa = wab_ref[0]
    wb = wab_ref[1]
    for p, pr in enumerate(page_refs):
        x = pr[...].reshape(BLK_PER_PAGE, CMP_STRIDE, 4 * NSA_DH)
        rs = slice(p * BLK_PER_PAGE, (p + 1) * BLK_PER_PAGE)
        a_ref[rs, :] = jnp.sum(x * wa[None], axis=1)
        b_ref[rs, :] = jnp.sum(x * wb[None], axis=1)


def _compress_ab(pages, page_table, wab, col_block):
    N, n_pages = page_table.shape
    assert n_pages % PAGES_PER_STEP == 0
    W = 4 * NSA_DH
    rows = PAGES_PER_STEP * BLK_PER_PAGE
    page_specs = [
        pl.BlockSpec((None, PAGE, W), functools.partial(
            lambda n, s, pt, r: (pt[n, s * PAGES_PER_STEP + r], 0, col_block), r=r))
        for r in range(PAGES_PER_STEP)
    ]
    grid_spec = pltpu.PrefetchScalarGridSpec(
        num_scalar_prefetch=1,
        grid=(N, n_pages // PAGES_PER_STEP),
        in_specs=page_specs + [pl.BlockSpec((2, CMP_STRIDE, W), lambda n, s, pt: (0, 0, 0))],
        out_specs=[pl.BlockSpec((None, rows, W), lambda n, s, pt: (n, s, 0))] * 2,
    )
    ns = n_pages * BLK_PER_PAGE
    return pl.pallas_call(
        _compress_body,
        grid_spec=grid_spec,
        out_shape=[jax.ShapeDtypeStruct((N, ns, W), F32)] * 2,
        compiler_params=_cparams("parallel", "parallel"),
        name="nsa_compress",
    )(page_table, *([pages] * PAGES_PER_STEP), wab)


ROWS_PER_TOKEN = 2 * NSA_KV
PAGE_ROWS = PAGE * ROWS_PER_TOKEN
BLK_ROWS = CMP_STRIDE * ROWS_PER_TOKEN


def _compress_rows_body(pt_ref, *refs):
    page_refs = refs[:PAGES_PER_STEP]
    wab_ref, a_ref, b_ref, z_s = refs[PAGES_PER_STEP:]
    for p, pr in enumerate(page_refs):
        x = pr[...].reshape(BLK_PER_PAGE, BLK_ROWS, NSA_DH)
        for half, out_ref in enumerate((a_ref, b_ref)):
            y = (x * wab_ref[half][None]).reshape(BLK_PER_PAGE, BLK_ROWS // 8, 8, NSA_DH).sum(axis=1)
            y = y.reshape(BLK_PER_PAGE * 8, NSA_DH)
            slot = 2 * p + half
            z_s[slot] = y + pltpu.roll(y, BLK_PER_PAGE * 8 - ROWS_PER_TOKEN, 0)
            for c in range(ROWS_PER_TOKEN):
                out_ref[p * BLK_PER_PAGE:(p + 1) * BLK_PER_PAGE, c * NSA_DH:(c + 1) * NSA_DH] = (
                    z_s[slot, pl.ds(c, BLK_PER_PAGE, stride=8), :])


def _compress_ab_rows(rows, page_table, wab_rows):
    N, n_pages = page_table.shape
    assert n_pages % PAGES_PER_STEP == 0
    W = ROWS_PER_TOKEN * NSA_DH
    out_rows = PAGES_PER_STEP * BLK_PER_PAGE
    page_specs = [
        pl.BlockSpec((PAGE_ROWS, NSA_DH), functools.partial(
            lambda n, s, pt, r: (pt[n, s * PAGES_PER_STEP + r], 0), r=r))
        for r in range(PAGES_PER_STEP)
    ]
    grid_spec = pltpu.PrefetchScalarGridSpec(
        num_scalar_prefetch=1,
        grid=(N, n_pages // PAGES_PER_STEP),
        in_specs=page_specs + [pl.BlockSpec((2, BLK_ROWS, NSA_DH), lambda n, s, pt: (0, 0, 0))],
        out_specs=[pl.BlockSpec((None, out_rows, W), lambda n, s, pt: (n, s, 0))] * 2,
        scratch_shapes=[pltpu.VMEM((2 * PAGES_PER_STEP, BLK_PER_PAGE * 8, NSA_DH), F32)],
    )
    ns = n_pages * BLK_PER_PAGE
    return pl.pallas_call(
        _compress_rows_body,
        grid_spec=grid_spec,
        out_shape=[jax.ShapeDtypeStruct((N, ns, W), F32)] * 2,
        compiler_params=_cparams("parallel", "arbitrary"),
        name="nsa_compress_rows",
    )(page_table, *([rows] * PAGES_PER_STEP), wab_rows)


def _cmp_proj_body(a_ref, b_ref, w_ref, kc_ref, vc_ref, *, ns):
    b_next = pltpu.roll(b_ref[...], ns - 1, 0)
    row = lax.broadcasted_iota(jnp.int32, (ns, 1), 0)
    blk = jnp.where(row < ns - 1, a_ref[...] + b_next, 0.0).astype(BF16)
    w0 = w_ref[0].astype(BF16)
    w1 = w_ref[1].astype(BF16)
    for kv in range(NSA_KV):
        cs = slice(kv * NSA_DH, (kv + 1) * NSA_DH)
        kc_ref[:, cs] = _dot(blk[:, kv * NSA_DH:(kv + 1) * NSA_DH], w0)
        vc_ref[:, cs] = _dot(blk[:, (NSA_KV + kv) * NSA_DH:(NSA_KV + kv + 1) * NSA_DH], w1)


def _cmp_proj(a, b, w_cmp):
    N, ns, W = a.shape
    return pl.pallas_call(
        functools.partial(_cmp_proj_body, ns=ns),
        grid=(N,),
        in_specs=[
            pl.BlockSpec((None, ns, W), lambda n: (n, 0, 0)),
            pl.BlockSpec((None, ns, W), lambda n: (n, 0, 0)),
            pl.BlockSpec((2, NSA_DH, NSA_DH), lambda n: (0, 0, 0)),
        ],
        out_specs=[pl.BlockSpec((None, ns, NSA_KV * NSA_DH), lambda n: (n, 0, 0))] * 2,
        out_shape=[jax.ShapeDtypeStruct((N, ns, NSA_KV * NSA_DH), F32)] * 2,
        compiler_params=_cparams("parallel"),
        name="nsa_cmp_proj",
    )(a, b, w_cmp)


def _split3(x):
    hi = x.astype(BF16)
    r1 = x - hi.astype(F32)
    mid = r1.astype(BF16)
    lo = (r1 - mid.astype(F32)).astype(BF16)
    return hi, mid, lo


def _cmp_to_sel(p_sum, a_mat):
    hi, mid, lo = _split3(p_sum)
    return _dot(hi, a_mat) + _dot(mid, a_mat) + _dot(lo, a_mat)


def _topk_mask(score, n_valid, k):
    lane = lax.broadcasted_iota(jnp.int32, score.shape, 1)
    rank = jnp.zeros(score.shape, F32)
    for jp in range(n_valid):
        col = score[:, jp:jp + 1]
        beats = (col > score) | ((col == score) & (lane > jp))
        rank = rank + beats.astype(F32)
    return ((rank < k) & (lane < n_valid)).astype(F32)


def _sel_scores(imp, qblk, n_valid):
    lane = lax.broadcasted_iota(jnp.int32, imp.shape, 1)
    forced = (lane == 0) | (lane == qblk) | (lane == qblk - 1)
    score = jnp.where(forced, FORCE_SCORE, jnp.where(lane > qblk, -1.0, imp))
    return jnp.where(lane < n_valid, score, -2.0)


def _masked_softmax_rows(s, mask):
    s = jnp.where(mask, s, NEG)
    e = jnp.exp(s - jnp.max(s, axis=-1, keepdims=True))
    return jnp.where(mask, e / jnp.sum(e, axis=-1, keepdims=True), 0.0)


def _stack_heads(q, scale):
    return (jnp.concatenate([q[:, g * NSA_DH:(g + 1) * NSA_DH] for g in range(NSA_G)], axis=0) * scale).astype(BF16)


SEL_TK = 512
WIN_KEYS = WINDOW + QBLK


def _nsa_prompt_body(q_ref, zg_ref, gb_ref, kc_ref, vc_ref, ks_ref, vs_ref, kw_ref, vw_ref, amat_t_ref,
                     o_ref, ks_bf, kw_bf, vs_t, vw_t, kc_bf, vc_t, sc_ref, *, n_sel, nsr):
    qb = pl.program_id(1)
    T = ks_ref.shape[0]
    kvc = lambda kv: slice(kv * NSA_DH, (kv + 1) * NSA_DH)

    @pl.when(qb == 0)
    def _():
        ks_bf[...] = ks_ref[...].astype(BF16)
        kw_bf[...] = kw_ref[...].astype(BF16)
        kc_bf[...] = kc_ref[...].astype(BF16)
        vc_t[...] = vc_ref[...].T.astype(BF16)

        def transpose_values(i, c):
            r0 = pl.multiple_of(i * LANE, LANE)
            vs_t[i] = vs_ref[pl.ds(r0, LANE), :].T.astype(BF16)
            vw_t[i] = vw_ref[pl.ds(r0, LANE), :].T.astype(BF16)
            return c

        lax.fori_loop(0, T // LANE, transpose_values, 0)

    R = QBLK
    start = qb * R
    groups = range(NSA_KV)
    tile4 = lambda x: jnp.concatenate([x] * NSA_G, axis=1)
    q_t = [jnp.concatenate([(q_ref[:, (kv * NSA_G + g) * NSA_DH:(kv * NSA_G + g + 1) * NSA_DH] * (NSA_DH ** -0.5)).T
                            for g in range(NSA_G)], axis=1).astype(BF16) for kv in groups]
    pos = start + lax.broadcasted_iota(jnp.int32, (1, R), 1)

    ns = kc_bf.shape[0]
    cmp_end = (lax.broadcasted_iota(jnp.int32, (ns, R), 0) + 2) * CMP_STRIDE - 1
    bias_c = tile4(jnp.where(cmp_end <= pos, 0.0, NEG))
    any_c = tile4(jnp.where(pos >= 2 * CMP_STRIDE - 1, 1.0, 0.0))
    a_t = amat_t_ref[...]
    blk = lax.broadcasted_iota(jnp.int32, (nsr, R), 0)
    qblk = (start + lax.broadcasted_iota(jnp.int32, (nsr, R), 1)) // SEL_BLOCK
    forced = (blk == 0) | (blk == qblk) | (blk == qblk - 1)
    o_cmp, score = [], []
    for kv in groups:
        s_c = _dot(kc_bf[:, kvc(kv)], q_t[kv]) + bias_c
        e_c = jnp.exp(s_c - jnp.max(s_c, axis=0, keepdims=True))
        p_c = e_c * (any_c / jnp.sum(e_c, axis=0, keepdims=True))
        o_cmp.append(_dot(vc_t[kvc(kv), :], p_c.astype(BF16)))
        p_sum = p_c[:, 0:R] + p_c[:, R:2 * R] + p_c[:, 2 * R:3 * R] + p_c[:, 3 * R:4 * R]
        hi, mid, lo = _split3(p_sum)
        imp_t = _dot(a_t, hi) + _dot(a_t, mid) + _dot(a_t, lo)
        sc = jnp.where(forced, FORCE_SCORE, jnp.where(blk > qblk, -1.0, imp_t))
        score.append(jnp.where(blk < n_sel, sc, -2.0))
        sc_ref[kv] = score[kv]

    def rank_step(i, ranks):
        ranks = list(ranks)
        for jp in (2 * i, 2 * i + 1):
            for kv in groups:
                row = sc_ref[kv, pl.ds(jp, 1), :]
                beats = (row > score[kv]) | ((row == score[kv]) & (blk > jp))
                ranks[kv] = ranks[kv] + beats.astype(F32)
        return tuple(ranks)

    n_rank = jnp.minimum(start // SEL_BLOCK + QBLK // SEL_BLOCK, n_sel)
    ranks = lax.fori_loop(0, (n_rank + 1) // 2, rank_step, tuple(jnp.zeros((nsr, R), F32) for _ in groups))
    for kv in groups:
        sc_ref[kv] = jnp.where(ranks[kv] < N_SEL, 0.0, NEG)

    def values_product(v_ref, tile0, n_sub, kv, p):
        out = _dot(v_ref[tile0, kvc(kv), :], p[0:LANE])
        for i in range(1, n_sub):
            out = out + _dot(v_ref[tile0 + i, kvc(kv), :], p[i * LANE:(i + 1) * LANE])
        return out

    key_tk = lax.broadcasted_iota(jnp.int32, (SEL_TK, R), 0)

    def sel_tile(kt, carry, causal):
        k0 = pl.multiple_of(kt * SEL_TK, SEL_TK)
        scores = [_dot(ks_bf[pl.ds(k0, SEL_TK), kvc(kv)], q_t[kv]) for kv in groups]
        stats = []
        for kv in groups:
            m_prev, l_prev, _ = carry[kv]
            bias = jnp.concatenate(
                [jnp.broadcast_to(sc_ref[kv, pl.ds(kt * (SEL_TK // SEL_BLOCK) + b, 1), :], (SEL_BLOCK, R))
                 for b in range(SEL_TK // SEL_BLOCK)], axis=0)
            if causal:
                bias = jnp.where(k0 + key_tk <= pos, bias, NEG)
            s = scores[kv] + tile4(bias)
            m_new = jnp.maximum(m_prev, jnp.max(s, axis=0, keepdims=True))
            alpha = jnp.exp(m_prev - m_new)
            p = jnp.exp(s - m_new)
            stats.append((m_new, alpha, alpha * l_prev + jnp.sum(p, axis=0, keepdims=True), p.astype(BF16)))
        return tuple((m_new, l_new, alpha * carry[kv][2]
                      + values_product(vs_t, kt * (SEL_TK // LANE), SEL_TK // LANE, kv, p))
                     for kv, (m_new, alpha, l_new, p) in enumerate(stats))

    n_tiles = (start + R - 1) // SEL_TK + 1
    init = tuple((jnp.full((1, NSA_G * R), NEG, F32), jnp.zeros((1, NSA_G * R), F32),
                  jnp.zeros((NSA_DH, NSA_G * R), F32)) for _ in groups)
    carry = lax.fori_loop(0, n_tiles - 1, lambda kt, c: sel_tile(kt, c, False), init)
    o_sel = [acc / l for _, l, acc in sel_tile(n_tiles - 1, carry, True)]

    w0 = pl.multiple_of(jnp.maximum(start - WINDOW, 0), QBLK)
    key_w = w0 + lax.broadcasted_iota(jnp.int32, (WIN_KEYS, R), 0)
    bias_w = tile4(jnp.where((key_w <= pos) & (key_w > pos - WINDOW), 0.0, NEG))
    gates_t = jax.nn.sigmoid(zg_ref[...] + gb_ref[...]).T
    for kv in groups:
        s_w = _dot(kw_bf[pl.ds(w0, WIN_KEYS), kvc(kv)], q_t[kv]) + bias_w
        e_w = jnp.exp(s_w - jnp.max(s_w, axis=0, keepdims=True))
        o_win = (values_product(vw_t, w0 // LANE, WIN_KEYS // LANE, kv, e_w.astype(BF16))
                 / jnp.sum(e_w, axis=0, keepdims=True))
        gate = lambda branch: jnp.concatenate(
            [gates_t[branch * NSA_HEADS + kv * NSA_G + g:branch * NSA_HEADS + kv * NSA_G + g + 1, :]
             for g in range(NSA_G)], axis=1)
        out_t = gate(0) * o_cmp[kv] + gate(1) * o_sel[kv] + gate(2) * o_win
        for g in range(NSA_G):
            h = kv * NSA_G + g
            o_ref[:, h * NSA_DH:(h + 1) * NSA_DH] = out_t[:, g * R:(g + 1) * R].T.astype(BF16)


def _sel_map(ns, n_sel, nsb):
    i = np.arange(ns)[:, None]
    j = np.arange(nsb)[None, :]
    r = SEL_BLOCK // CMP_STRIDE
    return jnp.asarray(((i >= r * j - 1) & (i <= r * j + r - 1) & (j < n_sel)).astype(np.float32), BF16)


def _nsa_prompt(z3, zg3, gate_b, kc, vc):
    N, T = z3.shape[:2]
    assert T % SEL_TK == 0 and T >= WIN_KEYS
    ns = kc.shape[1]
    n_sel = T // SEL_BLOCK
    nsr = -(-n_sel // 8) * 8
    assert nsr <= LANE
    amat_t = _sel_map(ns, n_sel, nsr).T
    KVW = NSA_KV * NSA_DH
    qcol = 2 * D_HALF // (NSA_HEADS * NSA_DH)
    kvs_col = (3 * D_HALF + 2 * KVW) // KVW
    kvw_col = kvs_col + 2
    full = lambda off: pl.BlockSpec((None, T, KVW), lambda n, qb: (n, 0, off))
    return pl.pallas_call(
        functools.partial(_nsa_prompt_body, n_sel=n_sel, nsr=nsr),
        grid=(N, T // QBLK),
        in_specs=[
            pl.BlockSpec((None, QBLK, NSA_HEADS * NSA_DH), lambda n, qb: (n, qb, qcol)),
            pl.BlockSpec((None, QBLK, LANE), lambda n, qb: (n, qb, 0)),
            pl.BlockSpec((1, LANE), lambda n, qb: (0, 0)),
            pl.BlockSpec((None, ns, KVW), lambda n, qb: (n, 0, 0)),
            pl.BlockSpec((None, ns, KVW), lambda n, qb: (n, 0, 0)),
            full(kvs_col), full(kvs_col + 1), full(kvw_col), full(kvw_col + 1),
            pl.BlockSpec((nsr, ns), lambda n, qb: (0, 0)),
        ],
        out_specs=pl.BlockSpec((None, QBLK, NSA_HEADS * NSA_DH), lambda n, qb: (n, qb, 0)),
        out_shape=jax.ShapeDtypeStruct((N, T, NSA_HEADS * NSA_DH), BF16),
        scratch_shapes=[
            pltpu.VMEM((T, KVW), BF16), pltpu.VMEM((T, KVW), BF16),
            pltpu.VMEM((T // LANE, KVW, LANE), BF16), pltpu.VMEM((T // LANE, KVW, LANE), BF16),
            pltpu.VMEM((ns, KVW), BF16), pltpu.VMEM((KVW, ns), BF16),
            pltpu.VMEM((NSA_KV, nsr, QBLK), F32),
        ],
        compiler_params=_cparams("parallel", "arbitrary"),
        name="nsa_prompt",
    )(z3, zg3, gate_b, kc, vc, z3, z3, z3, z3, amat_t)


def _nsa_sample_a_body(q_ref, kc_ref, vc_ref, kw_ref, vw_ref, amat_ref, ocmp_ref, owin_ref, sel_ref,
                       *, Tq, past_len, n_sel, wb):
    qs = _stack_heads(q_ref[...], NSA_DH ** -0.5)
    pos = past_len + lax.broadcasted_iota(jnp.int32, (Tq, 1), 0)
    pos4 = jnp.concatenate([pos] * NSA_G, axis=0)

    ns = kc_ref.shape[0]
    cmp_i = lax.broadcasted_iota(jnp.int32, (NSA_G * Tq, ns), 1)
    m_c = ((cmp_i + 2) * CMP_STRIDE - 1 <= pos4) & (cmp_i < ns - 1)
    p_c = _masked_softmax_rows(_dot_nt(qs, kc_ref[...].astype(BF16)), m_c)
    ocmp_ref[...] = _dot(p_c.astype(BF16), vc_ref[...].astype(BF16))
    p_sum = p_c[0:Tq] + p_c[Tq:2 * Tq] + p_c[2 * Tq:3 * Tq] + p_c[3 * Tq:4 * Tq]
    imp = _cmp_to_sel(p_sum, amat_ref[...])
    sel_ref[...] = _topk_mask(_sel_scores(imp, pos // SEL_BLOCK, n_sel), n_sel, N_SEL)

    nw = kw_ref.shape[0]
    tok_w = past_len - wb + lax.broadcasted_iota(jnp.int32, (NSA_G * Tq, nw), 1)
    m_w = (tok_w >= 0) & (tok_w <= pos4) & (tok_w > pos4 - WINDOW)
    p_w = _masked_softmax_rows(_dot_nt(qs, kw_ref[...].astype(BF16)), m_w)
    owin_ref[...] = _dot(p_w.astype(BF16), vw_ref[...].astype(BF16))


def _nsa_sample_a(z3, kc, vc, kw_full, past_len, wb):
    N, Tq = z3.shape[:2]
    ns = kc.shape[1]
    nw = kw_full.shape[1]
    n_sel = -(-(past_len + Tq) // SEL_BLOCK)
    nsb = -(-n_sel // LANE) * LANE
    amat = _sel_map(ns, n_sel, nsb)
    qcol = 2 * D_HALF // (NSA_G * NSA_DH)
    R4 = NSA_G * Tq
    return pl.pallas_call(
        functools.partial(_nsa_sample_a_body, Tq=Tq, past_len=past_len, n_sel=n_sel, wb=wb),
        grid=(N, NSA_KV),
        in_specs=[
            pl.BlockSpec((None, Tq, NSA_G * NSA_DH), lambda n, kv: (n, 0, qcol + kv)),
            pl.BlockSpec((None, ns, NSA_DH), lambda n, kv: (n, 0, kv)),
            pl.BlockSpec((None, ns, NSA_DH), lambda n, kv: (n, 0, kv)),
            pl.BlockSpec((None, nw, NSA_DH), lambda n, kv: (n, 0, kv)),
            pl.BlockSpec((None, nw, NSA_DH), lambda n, kv: (n, 0, NSA_KV + kv)),
            pl.BlockSpec((ns, nsb), lambda n, kv: (0, 0)),
        ],
        out_specs=[
            pl.BlockSpec((None, None, R4, NSA_DH), lambda n, kv: (n, kv, 0, 0)),
            pl.BlockSpec((None, None, R4, NSA_DH), lambda n, kv: (n, kv, 0, 0)),
            pl.BlockSpec((None, None, Tq, nsb), lambda n, kv: (n, kv, 0, 0)),
        ],
        out_shape=[
            jax.ShapeDtypeStruct((N, NSA_KV, R4, NSA_DH), F32),
            jax.ShapeDtypeStruct((N, NSA_KV, R4, NSA_DH), F32),
            jax.ShapeDtypeStruct((N, NSA_KV, Tq, nsb), F32),
        ],
        compiler_params=_cparams("parallel", "parallel"),
        name="nsa_sample_cmp_win",
    )(z3, kc, vc, kw_full, kw_full, amat)


BLK_PER_STEP = PAGES_PER_STEP * PAGE // SEL_BLOCK


def _nsa_sample_b_body(pt_ref, *refs, Tq, past_len):
    page_refs = refs[:PAGES_PER_STEP]
    (q_ref, new_ref, sel_ref, exp_ref, ocmp_ref, owin_ref, zg_ref, gb_ref, o_ref, m_s, l_s, acc_s) = refs[PAGES_PER_STEP:]
    s_id = pl.program_id(1)
    R4 = NSA_G * Tq
    rep = lambda x: jnp.concatenate([x] * NSA_G, axis=0)
    pos = past_len + lax.broadcasted_iota(jnp.int32, (Tq, 1), 0)

    @pl.when(s_id == 0)
    def _():
        m_s[...] = jnp.full(m_s.shape, NEG, F32)
        l_s[...] = jnp.zeros(l_s.shape, F32)
        acc_s[...] = jnp.zeros(acc_s.shape, F32)

    def update(kv, s, mask, pv):
        s = jnp.where(mask, s, NEG)
        m_prev = m_s[kv]
        m_new = jnp.maximum(m_prev, jnp.max(s, axis=-1, keepdims=True))
        alpha = jnp.exp(m_prev - m_new)
        p = jnp.where(mask, jnp.exp(s - m_new), 0.0)
        l_s[kv] = alpha * l_s[kv] + jnp.sum(p, axis=-1, keepdims=True)
        acc_s[kv] = alpha * acc_s[kv] + pv(p.astype(BF16))
        m_s[kv] = m_new

    qs = [_stack_heads(q_ref[:, kv * NSA_G * NSA_DH:(kv + 1) * NSA_G * NSA_DH], NSA_DH ** -0.5) for kv in range(NSA_KV)]
    comp = lambda pr, c: pr[pl.ds(c, PAGE, stride=ROWS_PER_TOKEN), :].astype(BF16)
    groups = range(NSA_KV)
    scores = [jnp.concatenate([_dot_nt(qs[kv], comp(pr, kv)) for pr in page_refs], axis=1) for kv in groups]
    sel_tok = [_dot(sel_ref[kv].astype(BF16), exp_ref[...]) for kv in groups]
    stats = []
    for kv in groups:
        mask = rep(sel_tok[kv]) > 0.5
        s = jnp.where(mask, scores[kv], NEG)
        m_prev = m_s[kv]
        m_new = jnp.maximum(m_prev, jnp.max(s, axis=-1, keepdims=True))
        alpha = jnp.exp(m_prev - m_new)
        p = jnp.where(mask, jnp.exp(s - m_new), 0.0)
        stats.append((m_new, alpha, alpha * l_s[kv] + jnp.sum(p, axis=-1, keepdims=True), p.astype(BF16)))
    pvs = []
    for kv in groups:
        p = stats[kv][3]
        out = _dot(p[:, 0:PAGE], comp(page_refs[0], NSA_KV + kv))
        for i in range(1, PAGES_PER_STEP):
            out = out + _dot(p[:, i * PAGE:(i + 1) * PAGE], comp(page_refs[i], NSA_KV + kv))
        pvs.append(out)
    for kv in groups:
        m_new, alpha, l_new, _ = stats[kv]
        acc_s[kv] = alpha * acc_s[kv] + pvs[kv]
        l_s[kv] = l_new
        m_s[kv] = m_new

    @pl.when(s_id == pl.num_programs(1) - 1)
    def _():
        new = new_ref[...].astype(BF16)
        tok_n = past_len + lax.broadcasted_iota(jnp.int32, (R4, Tq), 1)
        gates = jax.nn.sigmoid(zg_ref[...] + gb_ref[...])
        for kv in range(NSA_KV):
            k_new = new[:, kv * NSA_DH:(kv + 1) * NSA_DH]
            v_new = new[:, (NSA_KV + kv) * NSA_DH:(NSA_KV + kv + 1) * NSA_DH]
            update(kv, _dot_nt(qs[kv], k_new), tok_n <= rep(pos), lambda p, v_new=v_new: _dot(p, v_new))
            o_sel = acc_s[kv] / l_s[kv]
            o_cmp = ocmp_ref[kv]
            o_win = owin_ref[kv]
            for g in range(NSA_G):
                rs = slice(g * Tq, (g + 1) * Tq)
                head = kv * NSA_G + g
                o_ref[:, head * NSA_DH:(head + 1) * NSA_DH] = (
                    gates[:, head:head + 1] * o_cmp[rs]
                    + gates[:, NSA_HEADS + head:NSA_HEADS + head + 1] * o_sel[rs]
                    + gates[:, 2 * NSA_HEADS + head:2 * NSA_HEADS + head + 1] * o_win[rs])


def _nsa_sample_b(z3, zg3, gate_b, pool_sel, page_table, sel, o_cmp, o_win, past_len):
    N, Tq = z3.shape[:2]
    n_pages = page_table.shape[1]
    assert n_pages % PAGES_PER_STEP == 0 and past_len % SEL_BLOCK == 0
    n_steps = n_pages // PAGES_PER_STEP
    sel_steps = sel[..., :past_len // SEL_BLOCK].reshape(N, NSA_KV, Tq, n_steps, BLK_PER_STEP).transpose(0, 3, 1, 2, 4)
    tok = np.arange(PAGES_PER_STEP * PAGE)[None, :]
    expand = jnp.asarray((tok // SEL_BLOCK == np.arange(BLK_PER_STEP)[:, None]).astype(np.float32), BF16)
    needed = (jnp.max(sel_steps, axis=(2, 3)) > 0).reshape(N, n_steps, PAGES_PER_STEP, PAGE // SEL_BLOCK).any(-1)
    last_needed = lax.cummax(jnp.where(needed, jnp.arange(n_steps, dtype=jnp.int32)[None, :, None], 0), axis=1)
    page_table = jnp.take_along_axis(page_table.reshape(N, n_steps, PAGES_PER_STEP), last_needed, axis=1).reshape(N, n_pages)
    W = 4 * NSA_DH
    R4 = NSA_G * Tq
    qcol = 2 * D_HALF // (NSA_HEADS * NSA_DH)
    kvs_col = (3 * D_HALF + W) // W
    page_specs = [
        pl.BlockSpec((PAGE_ROWS, NSA_DH), functools.partial(
            lambda n, s, pt, r: (pt[n, s * PAGES_PER_STEP + r], 0), r=r))
        for r in range(PAGES_PER_STEP)
    ]
    per_n4 =lambda shape: pl.BlockSpec((None,) + shape, lambda n, s, pt: (n, 0, 0, 0))
    grid_spec = pltpu.PrefetchScalarGridSpec(
        num_scalar_prefetch=1,
        grid=(N, n_pages // PAGES_PER_STEP),
        in_specs=page_specs + [
            pl.BlockSpec((None, Tq, NSA_HEADS * NSA_DH), lambda n, s, pt: (n, 0, qcol)),
            pl.BlockSpec((None, Tq, W), lambda n, s, pt: (n, 0, kvs_col)),
            pl.BlockSpec((None, None, NSA_KV, Tq, BLK_PER_STEP), lambda n, s, pt: (n, s, 0, 0, 0)),
            pl.BlockSpec((BLK_PER_STEP, PAGES_PER_STEP * PAGE), lambda n, s, pt: (0, 0)),
            per_n4((NSA_KV, R4, NSA_DH)),
            per_n4((NSA_KV, R4, NSA_DH)),
            pl.BlockSpec((None, Tq, LANE), lambda n, s, pt: (n, 0, 0)),
            pl.BlockSpec((1, LANE), lambda n, s, pt: (0, 0)),
        ],
        out_specs=pl.BlockSpec((None, Tq, NSA_HEADS * NSA_DH), lambda n, s, pt: (n, 0, 0)),
        scratch_shapes=[
            pltpu.VMEM((NSA_KV, R4, 1), F32),
            pltpu.VMEM((NSA_KV, R4, 1), F32),
            pltpu.VMEM((NSA_KV, R4, NSA_DH), F32),
        ],
    )
    return pl.pallas_call(
        functools.partial(_nsa_sample_b_body, Tq=Tq, past_len=past_len),
        grid_spec=grid_spec,
        out_shape=jax.ShapeDtypeStruct((N, Tq, NSA_HEADS * NSA_DH), F32),
        compiler_params=_cparams("parallel", "arbitrary"),
        name="nsa_sample_sel",
    )(page_table, *([pool_sel] * PAGES_PER_STEP), z3, z3, sel_steps, expand, o_cmp, o_win, zg3, gate_b)


MLSTM_CHUNK_PROMPT = 256
SAMPLE_PAD = 128


def _pad_rows(x, rows):
    return jnp.pad(x, ((0, 0), (0, rows - x.shape[1]), (0, 0)))


def _lane_pad(v):
    return jnp.pad(v.astype(F32), (0, LANE - v.shape[0])).reshape(1, LANE)


def _gate_rows(w_in_t, n_main):
    return jnp.pad(w_in_t[n_main:], ((0, LANE - (w_in_t.shape[0] - n_main)), (0, 0)))


def _even_mixers(z3, zg3, t0, pool_buf, C0, n0, m0, w):
    N, T = z3.shape[:2]
    st = jnp.pad(pool_buf, ((0, 0), (HALO - POOL_PAD, 0), (0, 0)))
    y_a = _pool_mix(z3, st, w["pool_w"], w["pool_scale"], t0)
    u_ext_tail = jnp.concatenate([pool_buf, z3[:, :, :D_HALF]], axis=1)[:, -POOL_PAD:] if T < POOL_PAD else z3[:, -POOL_PAD:, :D_HALF]

    if T % MLSTM_CHUNK_PROMPT == 0:
        L, zm, zgm = MLSTM_CHUNK_PROMPT, z3, zg3
    else:
        L, zm, zgm = SAMPLE_PAD, _pad_rows(z3, SAMPLE_PAD), _pad_rows(zg3, SAMPLE_PAD)
    y_b, C, n, m = _mlstm(zm, zgm, w["gate_b"], w["mnorm_g"], C0, n0, m0, L, min(T, L))
    return y_a, y_b[:, :T], u_ext_tail, C, n, m


def _kv_rows_body(x_ref, o_ref, *, tT):
    for c in range(ROWS_PER_TOKEN):
        o_ref[pl.ds(c, tT, stride=ROWS_PER_TOKEN), :] = x_ref[:, c * NSA_DH:(c + 1) * NSA_DH]


def _kv_rows(z3, col_block, first_row, n_rows, tT_pref=512):
    N = z3.shape[0]
    tT = _tile(n_rows, tT_pref)
    assert first_row % tT == 0
    rows = pl.pallas_call(
        functools.partial(_kv_rows_body, tT=tT),
        grid=(N, n_rows // tT),
        in_specs=[pl.BlockSpec((None, tT, ROWS_PER_TOKEN * NSA_DH), lambda n, t: (n, first_row // tT + t, col_block))],
        out_specs=pl.BlockSpec((None, ROWS_PER_TOKEN * tT, NSA_DH), lambda n, t: (n, t, 0)),
        out_shape=jax.ShapeDtypeStruct((N, ROWS_PER_TOKEN * n_rows, NSA_DH), F32),
        compiler_params=_cparams("parallel", "parallel"),
        name="kv_rows",
    )(z3)
    return rows.reshape(N, n_rows, 2, NSA_KV, NSA_DH)


def _odd_mixers(z3, zg3, w, past=None):
    N, T, n_main = z3.shape
    W = 4 * NSA_DH
    kv_col = 3 * D_HALF // W
    kv5 = lambda t: t.reshape(N, t.shape[1], 2, NSA_KV, NSA_DH)

    if past is None:
        y_c, = _gmlp(z3, w["gnorm_g"], w["ws"], w["bs_t"], keep_vn=False)
        pt = jnp.arange(N * (T // PAGE), dtype=jnp.int32).reshape(N, T // PAGE)
        a, b = _compress_ab(z3.reshape(N * (T // PAGE), PAGE, n_main), pt, w["wab"], kv_col)
        kc, vc = _cmp_proj(a, b, w["cmp_w"])
        o = _nsa_prompt(z3, zg3, w["nsa_gate_b"], kc, vc)
        wlen = min(WINDOW, T)
        return (y_c, o, None, _kv_rows(z3, kv_col, 0, T), _kv_rows(z3, kv_col + 1, 0, T),
                _kv_rows(z3, kv_col + 2, T - wlen, wlen))

    kvc = z3[:, :, 3 * D_HALF:3 * D_HALF + W]
    kvs = z3[:, :, 3 * D_HALF + W:3 * D_HALF + 2 * W]
    kvw = z3[:, :, 3 * D_HALF + 2 * W:3 * D_HALF + 3 * W]
    win_buf, pool_cmp, pool_sel, page_table = past
    past_len = page_table.shape[1] * PAGE
    wb = win_buf.shape[1]
    y_c, vn = _gmlp(_pad_rows(z3[:, :, :2 * D_HALF], SAMPLE_PAD), w["gnorm_g"], w["ws"], w["bs_t"], keep_vn=True)
    y_c, vn = y_c[:, :T], vn[:, :T]
    assert (past_len + T) // CMP_STRIDE == past_len // CMP_STRIDE
    a, b = _compress_ab_rows(pool_cmp, page_table, w["wab_rows"])
    kc, vc = _cmp_proj(a, b, w["cmp_w"])
    kw_all = jnp.concatenate([win_buf.reshape(N, wb, W), kvw], axis=1)
    nw = -(-(wb + T) // LANE) * LANE
    o_cmp, o_win, sel = _nsa_sample_a(z3, kc, vc, _pad_rows(kw_all, nw), past_len, wb)
    o = _nsa_sample_b(z3, zg3, w["nsa_gate_b"], pool_sel, page_table, sel, o_cmp, o_win, past_len)
    win_state = jnp.concatenate([win_buf, kv5(kvw)], axis=1)[:, -wb:]
    return y_c, o, vn, kv5(kvc), kv5(kvs), win_state


def _layer(xp, xs, xpn, xsn, w, ffn_stacks, layer, n_main, mixers_p, mixers_s):
    B, T, D = xp.shape
    Ns, Ts, _ = xs.shape
    xp2, xs2 = xp.reshape(B * T, D), xs.reshape(Ns * Ts, D)
    zp, zgp, zs, zgs = _in_proj(xpn, xsn, w["w_in"], w["w_in_layer"], n_main, w["w_gate"])
    a1p, a2p, *extra_p = mixers_p(zp.reshape(B, T, n_main), zgp.reshape(B, T, LANE))
    a1s, a2s, *extra_s = mixers_s(zs.reshape(Ns, Ts, n_main), zgs.reshape(Ns, Ts, LANE))
    flat = lambda t: t.reshape(-1, D_HALF)
    xp2, xpn, xs2, xsn = _out_proj(flat(a1p), flat(a2p), xp2, flat(a1s), flat(a2s), xs2, w["w_out_a"], w["w_out_b"],
                                   w["ng"][1:2], w["ng"][2:3])
    w1_stack, w3_stack, w2_stack = ffn_stacks
    hp, hs, w2_bf = _ffn_up(xpn, xsn, w1_stack, w3_stack, w2_stack, layer)
    xp2, xs2, xpn, xsn = _ffn_down(hp, hs, xp2, xs2, w["ng"][3:4], w["ng_next"], w2_bf)
    return xp2.reshape(B, T, D), xs2.reshape(Ns, Ts, D), xpn, xsn, extra_p, extra_s


def kernel(x_prompt, x_sample, state_pool, state_mlstm_c, state_mlstm_n, state_mlstm_m, state_win_kv, cache_cmp_kv, cache_sel_kv, page_table, norm_g, w_in_even, w_out_even, pool_w, pool_scale, mlstm_gate_b, mlstm_norm_g, w_in_odd, w_out_odd, gmlp_norm_g, gmlp_ws, gmlp_bs, nsa_cmp_pos, nsa_cmp_w, nsa_gate_b, ffn_w1, ffn_w3, ffn_w2):
    B = x_prompt.shape[0]
    depth = norm_g.shape[0]
    past_len = page_table.shape[1] * PAGE
    xp, xs = x_prompt, x_sample
    pool_p, pool_s, c_p, c_s, n_p, n_s, m_p, m_s = [], [], [], [], [], [], [], []
    gv_s, cmp_p, cmp_s, sel_p, sel_s, win_p, win_s = [], [], [], [], [], [], []
    ffn_stacks = (ffn_w1, ffn_w3, ffn_w2)
    D = x_prompt.shape[-1]
    xpn, xsn = _rms_cast(x_prompt.reshape(-1, D), x_sample.reshape(-1, D), norm_g[0][0:1])
    for l in range(depth):
        j = l // 2
        ng_next = norm_g[min(l + 1, depth - 1)][0:1]
        if l % 2 == 0:
            n_main = 4 * D_HALF
            w_out = w_out_even[j].astype(BF16)
            w_in_t = jnp.swapaxes(w_in_even, 1, 2)
            w = dict(ng=norm_g[l], ng_next=ng_next, w_in=w_in_t, w_in_layer=j, w_gate=_gate_rows(w_in_t[j], n_main),
                     w_out_a=w_out[:D_HALF], w_out_b=w_out[D_HALF:],
                     pool_w=pool_w[j].astype(BF16), pool_scale=pool_scale[j].reshape(1, D_HALF),
                     gate_b=_lane_pad(mlstm_gate_b[j].reshape(-1)), mnorm_g=mlstm_norm_g[j].reshape(1, D_HALF))
            zp = jnp.zeros((B, POOL_PAD, D_HALF), F32)
            zc = jnp.zeros((B, MLSTM_HEADS, MLSTM_DK, MLSTM_DV), F32)
            zn = jnp.zeros((B, MLSTM_HEADS, MLSTM_DK), F32)
            zm = jnp.zeros((B, MLSTM_HEADS), F32)
            xp, xs, xpn, xsn, (pb, c, n, m), (pbs, cs, ns_, ms) = _layer(
                xp, xs, xpn, xsn, w, ffn_stacks, l, n_main,
                lambda z3, zg3: _even_mixers(z3, zg3, 0, zp, zc, zn, zm, w),
                lambda z3, zg3: _even_mixers(z3, zg3, past_len, state_pool[j], state_mlstm_c[j], state_mlstm_n[j],
                                             state_mlstm_m[j], w))
            pool_p.append(pb); c_p.append(c); n_p.append(n); m_p.append(m)
            pool_s.append(pbs); c_s.append(cs); n_s.append(ns_); m_s.append(ms)
        else:
            n_main = 4 * D_HALF + D_HALF // 2
            w_out = w_out_odd[j].astype(BF16)
            cp = nsa_cmp_pos[j]
            wcol = jnp.repeat(cp, NSA_KV * NSA_DH, axis=1)
            wab = jnp.stack([wcol[:CMP_STRIDE], wcol[CMP_STRIDE:]])
            wab_rows = jnp.broadcast_to(jnp.repeat(cp, NSA_KV, axis=1).reshape(2, BLK_ROWS, 1), (2, BLK_ROWS, NSA_DH))
            w_in_t = jnp.swapaxes(w_in_odd, 1, 2)
            w = dict(ng=norm_g[l], ng_next=ng_next, w_in=w_in_t, w_in_layer=j, w_gate=_gate_rows(w_in_t[j], n_main),
                     w_out_a=w_out[:D_HALF], w_out_b=w_out[D_HALF:],
                     gnorm_g=gmlp_norm_g[j].reshape(1, D_HALF), ws=gmlp_ws[j],
                     bs_t=jnp.pad(gmlp_bs[j].T, ((0, 0), (0, LANE - GMLP_GROUPS))),
                     wab=wab, wab_rows=wab_rows, cmp_w=nsa_cmp_w[j], nsa_gate_b=_lane_pad(nsa_gate_b[j]))
            n_pool = cache_cmp_kv.shape[1]
            flat = lambda c: c.reshape(c.shape[0] * n_pool * PAGE_ROWS, NSA_DH)
            past = (state_win_kv[j], flat(cache_cmp_kv), flat(cache_sel_kv), page_table + j * n_pool)
            xp, xs, xpn, xsn, (_, kc, ksl, wv), (vn, kcs, ksls, wvs) = _layer(
                xp, xs, xpn, xsn, w, ffn_stacks, l, n_main,
                lambda z3, zg3: _odd_mixers(z3, zg3, w),
                lambda z3, zg3: _odd_mixers(z3, zg3, w, past=past))
            cmp_p.append(kc); sel_p.append(ksl); win_p.append(wv)
            gv_s.append(vn); cmp_s.append(kcs); sel_s.append(ksls); win_s.append(wvs)
    st = jnp.stack
    return (xp, xs, st(pool_p), st(pool_s), st(c_p), st(c_s), st(n_p), st(n_s), st(m_p), st(m_s),
            st(gv_s), st(cmp_p), st(cmp_s), st(sel_p), st(sel_s), st(win_p), st(win_s))
```

```python
import functools

import numpy as np
import jax
import jax.numpy as jnp
from jax import lax
from jax.experimental import pallas as pl
from jax.experimental.pallas import tpu as pltpu

F32 = jnp.float32
BF16 = jnp.bfloat16
NEG = -1e30

D_HALF = 1024
POOL_WINDOWS = (2, 4, 8, 16)
POOL_DG = 256
POOL_PAD = 15
HALO = 16
MLSTM_HEADS = 4
MLSTM_DK = 128
MLSTM_DV = 256
GATE_CAP = 15.0
GMLP_CHUNK = 128
GMLP_GROUPS = 4
GMLP_DG = 256
NSA_HEADS = 8
NSA_DH = 128
NSA_KV = 2
NSA_G = 4
CMP_STRIDE = 16
SEL_BLOCK = 64
N_SEL = 16
WINDOW = 512
QBLK = 128
PAGE = 128
FORCE_SCORE = 1e9
LANE = 128
VMEM_LIMIT = 56 * 1024 * 1024

NT_DIMS = (((1,), (1,)), ((), ()))
TN_DIMS = (((0,), (0,)), ((), ()))


def _cparams(*sem):
    return pltpu.CompilerParams(dimension_semantics=sem, vmem_limit_bytes=VMEM_LIMIT)


def _tile(m, pref):
    if m <= pref:
        return m
    for t in range(pref, 7, -1):
        if m % t == 0 and t % 8 == 0:
            return t
    return m


def _dot(a, b):
    return jnp.dot(a, b, preferred_element_type=F32)


def _dot_nt(a, b):
    return lax.dot_general(a, b, NT_DIMS, preferred_element_type=F32)


def _rms(x, g, eps=1e-6):
    return x * lax.rsqrt(jnp.mean(x * x, axis=-1, keepdims=True) + eps) * g


def _rms_cast_body(x_ref, xs_ref, g_ref, o_ref, os_ref):
    o_ref[...] = _rms(x_ref[...], g_ref[...]).astype(BF16)

    @pl.when(pl.program_id(0) == 0)
    def _():
        os_ref[...] = _rms(xs_ref[...], g_ref[...]).astype(BF16)


def _rms_cast(x, xs, g, tm_pref=1024):
    M, K = x.shape
    Ms = xs.shape[0]
    tm = _tile(M, tm_pref)
    return pl.pallas_call(
        _rms_cast_body,
        grid=(M // tm,),
        in_specs=[pl.BlockSpec((tm, K), lambda i: (i, 0)), pl.BlockSpec((Ms, K), lambda i: (0, 0)),
                  pl.BlockSpec((1, K), lambda i: (0, 0))],
        out_specs=[pl.BlockSpec((tm, K), lambda i: (i, 0)), pl.BlockSpec((Ms, K), lambda i: (0, 0))],
        out_shape=[jax.ShapeDtypeStruct((M, K), BF16), jax.ShapeDtypeStruct((Ms, K), BF16)],
        compiler_params=_cparams("arbitrary"),
        name="rms_cast",
    )(x, xs, g)


def _in_proj_body(xn_ref, xsn_ref, w_ref, wg_ref, o_ref, og_ref, os_ref, ogs_ref, w_bf):
    j = pl.program_id(0)
    i = pl.program_id(1)

    @pl.when(i == 0)
    def _():
        w_bf[...] = w_ref[...].astype(BF16)
        os_ref[...] = _dot_nt(xsn_ref[...], w_bf[...])

    o_ref[...] = _dot_nt(xn_ref[...], w_bf[...])

    @pl.when(j == 0)
    def _():
        wg = wg_ref[...].astype(BF16)
        og_ref[...] = _dot_nt(xn_ref[...], wg)

        @pl.when(i == 0)
        def _():
            ogs_ref[...] = _dot_nt(xsn_ref[...], wg)


def _in_proj(xn, xsn, wt_stack, layer, n_main, wg_t, tm_pref=1024, tn_pref=1152):
    M, K = xn.shape
    Ms = xsn.shape[0]
    tm = _tile(M, tm_pref)
    tn = max(t for t in range(LANE, tn_pref + 1, LANE) if n_main % t == 0)
    ni = M // tm
    return pl.pallas_call(
        _in_proj_body,
        grid=(n_main // tn, ni),
        in_specs=[
            pl.BlockSpec((tm, K), lambda j, i: (i, 0)),
            pl.BlockSpec((Ms, K), lambda j, i: (0, 0)),
            pl.BlockSpec((None, tn, K), lambda j, i: (layer, j, 0)),
            pl.BlockSpec((LANE, K), lambda j, i: (0, 0)),
        ],
        out_specs=[
            pl.BlockSpec((tm, tn), lambda j, i: (i, j)),
            pl.BlockSpec((tm, LANE), lambda j, i: (jnp.where(j == 0, i, ni - 1), 0)),
            pl.BlockSpec((Ms, tn), lambda j, i: (0, j)),
            pl.BlockSpec((Ms, LANE), lambda j, i: (0, 0)),
        ],
        out_shape=[jax.ShapeDtypeStruct((M, n_main), F32), jax.ShapeDtypeStruct((M, LANE), F32),
                   jax.ShapeDtypeStruct((Ms, n_main), F32), jax.ShapeDtypeStruct((Ms, LANE), F32)],
        scratch_shapes=[pltpu.VMEM((tn, K), BF16)],
        compiler_params=_cparams("arbitrary", "arbitrary"),
        name="in_proj",
    )(xn, xsn, wt_stack, wg_t)


def _out_proj_body(a1_ref, a2_ref, res_ref, a1s_ref, a2s_ref, ress_ref, w1_ref, w2_ref, g_ref, gn_ref,
                   o_ref, on_ref, os_ref, osn_ref):
    def proj(a1, a2, res, o_r, on_r):
        y = _dot(a1[...].astype(BF16), w1_ref[...]) + _dot(a2[...].astype(BF16), w2_ref[...])
        x = res[...] + _rms(y, g_ref[...])
        o_r[...] = x
        on_r[...] = _rms(x, gn_ref[...]).astype(BF16)

    proj(a1_ref, a2_ref, res_ref, o_ref, on_ref)

    @pl.when(pl.program_id(0) == 0)
    def _():
        proj(a1s_ref, a2s_ref, ress_ref, os_ref, osn_ref)


def _out_proj(a1, a2, res, a1s, a2s, ress, w1, w2, g, g_next, tm_pref=512):
    M, K1 = a1.shape
    Ms = a1s.shape[0]
    K2 = a2.shape[1]
    D = w1.shape[1]
    tm = _tile(M, tm_pref)
    whole = lambda r, c: pl.BlockSpec((r, c), lambda i: (0, 0))
    rows = lambda c: pl.BlockSpec((tm, c), lambda i: (i, 0))
    return pl.pallas_call(
        _out_proj_body,
        grid=(M // tm,),
        in_specs=[
            rows(K1), rows(K2), rows(D),
            whole(Ms, K1), whole(Ms, K2), whole(Ms, D),
            whole(K1, D), whole(K2, D), whole(1, D), whole(1, D),
        ],
        out_specs=[rows(D), rows(D), whole(Ms, D), whole(Ms, D)],
        out_shape=[jax.ShapeDtypeStruct((M, D), F32), jax.ShapeDtypeStruct((M, D), BF16),
                   jax.ShapeDtypeStruct((Ms, D), F32), jax.ShapeDtypeStruct((Ms, D), BF16)],
        compiler_params=_cparams("arbitrary"),
        name="out_proj",
    )(a1, a2, res, a1s, a2s, ress, w1, w2, g, g_next)


def _ffn_up_body(xn_ref, xsn_ref, w1_ref, w3_ref, w2_ref, h_ref, hs_ref, w2_out, w1_bf, w3_bf):
    def swiglu(xn):
        h1 = _dot(xn, w1_bf[...])
        return (h1 * jax.nn.sigmoid(h1) * _dot(xn, w3_bf[...])).astype(BF16)

    @pl.when(pl.program_id(1) == 0)
    def _():
        w1_bf[...] = w1_ref[...].astype(BF16)
        w3_bf[...] = w3_ref[...].astype(BF16)
        w2_out[...] = w2_ref[...].astype(BF16)
        hs_ref[...] = swiglu(xsn_ref[...])

    h_ref[...] = swiglu(xn_ref[...])


def _ffn_up(xn, xsn, w1_stack, w3_stack, w2_stack, layer, tm_pref=1024, th_pref=512):
    M, D = xn.shape
    Ms = xsn.shape[0]
    H = w1_stack.shape[2]
    tm = _tile(M, tm_pref)
    th = _tile(H, th_pref)
    return pl.pallas_call(
        _ffn_up_body,
        grid=(H // th, M // tm),
        in_specs=[
            pl.BlockSpec((tm, D), lambda j, i: (i, 0)),
            pl.BlockSpec((Ms, D), lambda j, i: (0, 0)),
            pl.BlockSpec((None, D, th), lambda j, i: (layer, 0, j)),
            pl.BlockSpec((None, D, th), lambda j, i: (layer, 0, j)),
            pl.BlockSpec((None, th, D), lambda j, i: (layer, j, 0)),
        ],
        out_specs=[
            pl.BlockSpec((tm, th), lambda j, i: (i, j)),
            pl.BlockSpec((Ms, th), lambda j, i: (0, j)),
            pl.BlockSpec((th, D), lambda j, i: (j, 0)),
        ],
        out_shape=[jax.ShapeDtypeStruct((M, H), BF16), jax.ShapeDtypeStruct((Ms, H), BF16),
                   jax.ShapeDtypeStruct((H, D), BF16)],
        scratch_shapes=[pltpu.VMEM((D, th), BF16), pltpu.VMEM((D, th), BF16)],
        compiler_params=_cparams("arbitrary", "arbitrary"),
        name="ffn_up",
    )(xn, xsn, w1_stack, w3_stack, w2_stack)


def _ffn_down_body(h_ref, hs_ref, x_ref, xs_ref, g3_ref, gn_ref, w2_ref, o_ref, os_ref, on_ref, osn_ref):
    def rows(h_r, x_r, o_r, on_r):
        x = x_r[...] + _rms(_dot(h_r[...], w2_ref[...]), g3_ref[...])
        o_r[...] = x
        on_r[...] = _rms(x, gn_ref[...]).astype(BF16)

    rows(h_ref, x_ref, o_ref, on_ref)

    @pl.when(pl.program_id(0) == 0)
    def _():
        rows(hs_ref, xs_ref, os_ref, osn_ref)


def _ffn_down(h, hs, x, xs, g3, g_next, w2, tm_pref=256):
    M, D = x.shape
    Ms = xs.shape[0]
    H = h.shape[1]
    tm = _tile(M, tm_pref)
    whole = lambda r, c: pl.BlockSpec((r, c), lambda i: (0, 0))
    rows = lambda c: pl.BlockSpec((tm, c), lambda i: (i, 0))
    return pl.pallas_call(
        _ffn_down_body,
        grid=(M // tm,),
        in_specs=[rows(H), whole(Ms, H), rows(D), whole(Ms, D), whole(1, D), whole(1, D), whole(H, D)],
        out_specs=[rows(D), whole(Ms, D), rows(D), whole(Ms, D)],
        out_shape=[jax.ShapeDtypeStruct((M, D), F32), jax.ShapeDtypeStruct((Ms, D), F32),
                   jax.ShapeDtypeStruct((M, D), BF16), jax.ShapeDtypeStruct((Ms, D), BF16)],
        compiler_params=_cparams("arbitrary"),
        name="ffn_down",
    )(h, hs, x, xs, g3, g_next, w2)


def _pool_body(u_ref, prev_ref, st_ref, pw_ref, ps_ref, o_ref, ext_ref, *, tT, t0):
    t = pl.program_id(1)
    ext_ref[0:HALO, :] = jnp.where(t == 0, st_ref[...], prev_ref[...])
    ext_ref[HALO:HALO + tT, :] = u_ref[...]
    pos = t0 + t * tT + lax.broadcasted_iota(jnp.int32, (tT, 1), 0)
    for g, w in enumerate(POOL_WINDOWS):
        cs = slice(g * POOL_DG, (g + 1) * POOL_DG)
        x_new = ext_ref[HALO:HALO + tT, cs]
        tot = x_new
        for i in range(1, w):
            tot = tot + ext_ref[HALO - i:HALO - i + tT, cs]
        cnt = jnp.minimum(w, pos + 1).astype(F32)
        y = tot / cnt - x_new
        o_ref[:, cs] = (_dot(y.astype(BF16), pw_ref[g]) * ps_ref[:, cs]).astype(o_ref.dtype)


def _pool_mix(z3, st, pool_w, pool_scale, t0, tT_pref=512):
    N, T = z3.shape[:2]
    tT = _tile(T, tT_pref)
    nT = T // tT
    if nT > 1:
        assert tT % HALO == 0
        prev, prev_spec = z3, pl.BlockSpec((None, HALO, D_HALF), lambda n, t: (n, jnp.maximum(t * (tT // HALO) - 1, 0), 0))
    else:
        prev, prev_spec = st, pl.BlockSpec((None, HALO, D_HALF), lambda n, t: (n, 0, 0))
    return pl.pallas_call(
        functools.partial(_pool_body, tT=tT, t0=t0),
        grid=(N, nT),
        in_specs=[
            pl.BlockSpec((None, tT, D_HALF), lambda n, t: (n, t, 0)),
            prev_spec,
            pl.BlockSpec((None, HALO, D_HALF), lambda n, t: (n, 0, 0)),
            pl.BlockSpec((len(POOL_WINDOWS), POOL_DG, POOL_DG), lambda n, t: (0, 0, 0)),
            pl.BlockSpec((1, D_HALF), lambda n, t: (0, 0)),
        ],
        out_specs=pl.BlockSpec((None, tT, D_HALF), lambda n, t: (n, t, 0)),
        out_shape=jax.ShapeDtypeStruct((N, T, D_HALF), BF16 if tT % 16 == 0 else F32),
        scratch_shapes=[pltpu.VMEM((HALO + tT, D_HALF), F32)],
        compiler_params=_cparams("parallel", "arbitrary"),
        name="pool_mix",
    )(z3, prev, st, pool_w, pool_scale)


def _mlstm_body(q_ref, k_ref, v_ref, o_ref, zg_ref, gb_ref, mg_ref, c0_ref, n0_ref, m0_ref,
                y_ref, cN_ref, nN_ref, mN_ref, C_s, n_s, m_s, *, L, t_valid):
    c = pl.program_id(1)

    @pl.when(c == 0)
    def _():
        C_s[...] = c0_ref[...]
        n_s[...] = n0_ref[...]
        m_s[...] = m0_ref[...]

    a = GATE_CAP * jnp.tanh((zg_ref[...] + gb_ref[...]) / GATE_CAP)
    lane = lax.broadcasted_iota(jnp.int32, (L, LANE), 1)
    logsig = jnp.minimum(a, 0.0) - jnp.log1p(jnp.exp(-jnp.abs(a)))
    A = jnp.where(lane < MLSTM_HEADS, a, logsig)
    if t_valid < L:
        row = lax.broadcasted_iota(jnp.int32, (L, LANE), 0)
        A = jnp.where(row < t_valid, A, jnp.where(lane < MLSTM_HEADS, NEG, 0.0))
    r_i = lax.broadcasted_iota(jnp.int32, (L, L), 0)
    c_i = lax.broadcasted_iota(jnp.int32, (L, L), 1)
    causal = r_i >= c_i
    Bc = jnp.dot(causal.astype(F32), A, preferred_element_type=F32, precision=lax.Precision.HIGHEST)
    At = A.T
    Bt = Bc.T

    heads = range(MLSTM_HEADS)
    ks = lambda h: slice(h * MLSTM_DK, (h + 1) * MLSTM_DK)
    vs = lambda h: slice(h * MLSTM_DV, (h + 1) * MLSTM_DV)
    m0 = [m_s[h] for h in heads]
    C0 = [C_s[h] for h in heads]
    n0 = [n_s[h] for h in heads]
    qf = [q_ref[:, ks(h)] * (MLSTM_DK ** -0.5) for h in heads]
    qb = [q.astype(BF16) for q in qf]
    kf = [k_ref[:, ks(h)] for h in heads]
    vb = [v_ref[:, vs(h)].astype(BF16) for h in heads]
    qk = [_dot_nt(qb[h], kf[h].astype(BF16)) for h in heads]
    qc = [_dot(qb[h], C0[h].astype(BF16)) for h in heads]

    b_c = [Bc[:, MLSTM_HEADS + h:MLSTM_HEADS + h + 1] for h in heads]
    m_t, w_in, s = [], [], []
    for h in heads:
        b_r = Bt[MLSTM_HEADS + h:MLSTM_HEADS + h + 1, :]
        d = jnp.where(causal, b_c[h] - b_r + At[h:h + 1, :], NEG)
        inter = b_c[h] + m0[h]
        m_t.append(jnp.maximum(inter, jnp.max(d, axis=-1, keepdims=True)))
        w_in.append(jnp.exp(inter - m_t[h]))
        s.append(qk[h] * jnp.exp(d - m_t[h]))
    sv = [_dot(s[h].astype(BF16), vb[h]) for h in heads]

    kw, w_c, m_new = [], [], []
    for h in heads:
        b_end = b_c[h][L - 1:L, :]
        g_c = b_end - b_c[h] + A[:, h:h + 1]
        m_new.append(jnp.maximum(b_end + m0[h], jnp.max(g_c, axis=0, keepdims=True)))
        w_c.append(jnp.exp(b_end + m0[h] - m_new[h]))
        kw.append(kf[h] * jnp.exp(g_c - m_new[h]))
    kv = [lax.dot_general(kw[h].astype(BF16), vb[h], TN_DIMS, preferred_element_type=F32) for h in heads]

    for h in heads:
        num = w_in[h] * qc[h] + sv[h]
        den = w_in[h] * jnp.sum(qf[h] * n0[h], axis=-1, keepdims=True) + jnp.sum(s[h], axis=-1, keepdims=True)
        hh = num / jnp.maximum(jnp.abs(den), jnp.exp(-m_t[h]))
        y_ref[:, vs(h)] = (jax.nn.sigmoid(o_ref[:, vs(h)]) * _rms(hh, mg_ref[:, vs(h)])).astype(BF16)
    for h in heads:
        C_s[h] = w_c[h] * C0[h] + kv[h]
        n_s[h] = w_c[h] * n0[h] + jnp.sum(kw[h], axis=0, keepdims=True)
        m_s[h] = m_new[h]

    @pl.when(c == pl.num_programs(1) - 1)
    def _():
        cN_ref[...] = C_s[...]
        nN_ref[...] = n_s[...]
        mN_ref[...] = m_s[...]


def _mlstm(z3, zg3, gate_b, mnorm_g, C0, n0, m0, L, t_valid):
    N, Tp = z3.shape[:2]
    nc = Tp // L
    H = MLSTM_HEADS
    qk_w = H * MLSTM_DK
    v_w = H * MLSTM_DV
    st = lambda n, c: (n, 0, 0, 0)
    outs = pl.pallas_call(
        functools.partial(_mlstm_body, L=L, t_valid=t_valid),
        grid=(N, nc),
        in_specs=[
            pl.BlockSpec((None, L, qk_w), lambda n, c: (n, c, D_HALF // qk_w)),
            pl.BlockSpec((None, L, qk_w), lambda n, c: (n, c, D_HALF // qk_w + 1)),
            pl.BlockSpec((None, L, v_w), lambda n, c: (n, c, 2)),
            pl.BlockSpec((None, L, v_w), lambda n, c: (n, c, 3)),
            pl.BlockSpec((None, L, LANE), lambda n, c: (n, c, 0)),
            pl.BlockSpec((1, LANE), lambda n, c: (0, 0)),
            pl.BlockSpec((1, v_w), lambda n, c: (0, 0)),
            pl.BlockSpec((None, H, MLSTM_DK, MLSTM_DV), st),
            pl.BlockSpec((None, H, 1, MLSTM_DK), st),
            pl.BlockSpec((None, H, 1, 1), st),
        ],
        out_specs=[
            pl.BlockSpec((None, L, v_w), lambda n, c: (n, c, 0)),
            pl.BlockSpec((None, H, MLSTM_DK, MLSTM_DV), st),
            pl.BlockSpec((None, H, 1, MLSTM_DK), st),
            pl.BlockSpec((None, H, 1, 1), st),
        ],
        out_shape=[
            jax.ShapeDtypeStruct((N, Tp, v_w), BF16),
            jax.ShapeDtypeStruct((N, H, MLSTM_DK, MLSTM_DV), F32),
            jax.ShapeDtypeStruct((N, H, 1, MLSTM_DK), F32),
            jax.ShapeDtypeStruct((N, H, 1, 1), F32),
        ],
        scratch_shapes=[
            pltpu.VMEM((H, MLSTM_DK, MLSTM_DV), F32),
            pltpu.VMEM((H, 1, MLSTM_DK), F32),
            pltpu.VMEM((H, 1, 1), F32),
        ],
        compiler_params=_cparams("parallel", "arbitrary"),
        name="mlstm",
    )(z3, z3, z3, z3, zg3, gate_b, mnorm_g, C0, n0.reshape(N, H, 1, MLSTM_DK), m0.reshape(N, H, 1, 1))
    y, C, n, m = outs
    return y, C, n.reshape(N, H, MLSTM_DK), m.reshape(N, H)


def _gmlp_body(u_ref, v_ref, g_ref, ws_ref, bs_ref, y_ref, *vn_refs, tT):
    v = v_ref[...]
    vc = v - jnp.mean(v, axis=-1, keepdims=True)
    vn = vc * lax.rsqrt(jnp.mean(vc * vc, axis=-1, keepdims=True) + 1e-5) * g_ref[...]
    for vn_ref in vn_refs:
        vn_ref[...] = vn
    r_i = lax.broadcasted_iota(jnp.int32, (GMLP_CHUNK, GMLP_CHUNK), 0)
    c_i = lax.broadcasted_iota(jnp.int32, (GMLP_CHUNK, GMLP_CHUNK), 1)
    for g in range(GMLP_GROUPS):
        cs = slice(g * GMLP_DG, (g + 1) * GMLP_DG)
        wm = jnp.where(r_i >= c_i, ws_ref[g], 0.0).astype(BF16)
        bias = bs_ref[:, g:g + 1]
        for ch in range(tT // GMLP_CHUNK):
            rs = slice(ch * GMLP_CHUNK, (ch + 1) * GMLP_CHUNK)
            mix = _dot(wm, vn[rs, cs].astype(BF16)) + bias
            y_ref[rs, cs] = (u_ref[rs, cs] * mix).astype(BF16)


def _gmlp(z3, gnorm_g, ws, bs_t, keep_vn, tT_pref=512):
    N, Tp = z3.shape[:2]
    tT = _tile(Tp, tT_pref)
    assert tT % GMLP_CHUNK == 0
    n_out = 2 if keep_vn else 1
    return pl.pallas_call(
        functools.partial(_gmlp_body, tT=tT),
        grid=(N, Tp // tT),
        in_specs=[
            pl.BlockSpec((None, tT, D_HALF), lambda n, t: (n, t, 0)),
            pl.BlockSpec((None, tT, D_HALF), lambda n, t: (n, t, 1)),
            pl.BlockSpec((1, D_HALF), lambda n, t: (0, 0)),
            pl.BlockSpec((GMLP_GROUPS, GMLP_CHUNK, GMLP_CHUNK), lambda n, t: (0, 0, 0)),
            pl.BlockSpec((GMLP_CHUNK, LANE), lambda n, t: (0, 0)),
        ],
        out_specs=[pl.BlockSpec((None, tT, D_HALF), lambda n, t: (n, t, 0))] * n_out,
        out_shape=[jax.ShapeDtypeStruct((N, Tp, D_HALF), BF16), jax.ShapeDtypeStruct((N, Tp, D_HALF), F32)][:n_out],
        compiler_params=_cparams("parallel", "parallel"),
        name="gmlp",
    )(z3, z3, gnorm_g, ws, bs_t)


PAGES_PER_STEP = 32
BLK_PER_PAGE = PAGE // CMP_STRIDE


def _compress_body(pt_ref, *refs):
    page_refs = refs[:PAGES_PER_STEP]
    wab_ref, a_ref, b_ref = refs[PAGES_PER_STEP:]
    wa = wab_ref[0]
    wb = wab_ref[1]
    for p, pr in enumerate(page_refs):
        x = pr[...].reshape(BLK_PER_PAGE, CMP_STRIDE, 4 * NSA_DH)
        rs = slice(p * BLK_PER_PAGE, (p + 1) * BLK_PER_PAGE)
        a_ref[rs, :] = jnp.sum(x * wa[None], axis=1)
        b_ref[rs, :] = jnp.sum(x * wb[None], axis=1)


def _compress_ab(pages, page_table, wab, col_block):
    N, n_pages = page_table.shape
    assert n_pages % PAGES_PER_STEP == 0
    W = 4 * NSA_DH
    rows = PAGES_PER_STEP * BLK_PER_PAGE
    page_specs = [
        pl.BlockSpec((None, PAGE, W), functools.partial(
            lambda n, s, pt, r: (pt[n, s * PAGES_PER_STEP + r], 0, col_block), r=r))
        for r in range(PAGES_PER_STEP)
    ]
    grid_spec = pltpu.PrefetchScalarGridSpec(
        num_scalar_prefetch=1,
        grid=(N, n_pages // PAGES_PER_STEP),
        in_specs=page_specs + [pl.BlockSpec((2, CMP_STRIDE, W), lambda n, s, pt: (0, 0, 0))],
        out_specs=[pl.BlockSpec((None, rows, W), lambda n, s, pt: (n, s, 0))] * 2,
    )
    ns = n_pages * BLK_PER_PAGE
    return pl.pallas_call(
        _compress_body,
        grid_spec=grid_spec,
        out_shape=[jax.ShapeDtypeStruct((N, ns, W), F32)] * 2,
        compiler_params=_cparams("parallel", "parallel"),
        name="nsa_compress",
    )(page_table, *([pages] * PAGES_PER_STEP), wab)


ROWS_PER_TOKEN = 2 * NSA_KV
PAGE_ROWS = PAGE * ROWS_PER_TOKEN
BLK_ROWS = CMP_STRIDE * ROWS_PER_TOKEN


def _compress_rows_body(pt_ref, *refs):
    page_refs = refs[:PAGES_PER_STEP]
    wab_ref, a_ref, b_ref, z_s = refs[PAGES_PER_STEP:]
    for p, pr in enumerate(page_refs):
        x = pr[...].reshape(BLK_PER_PAGE, BLK_ROWS, NSA_DH)
        for half, out_ref in enumerate((a_ref, b_ref)):
            y = (x * wab_ref[half][None]).reshape(BLK_PER_PAGE, BLK_ROWS // 8, 8, NSA_DH).sum(axis=1)
            y = y.reshape(BLK_PER_PAGE * 8, NSA_DH)
            slot = 2 * p + half
            z_s[slot] = y + pltpu.roll(y, BLK_PER_PAGE * 8 - ROWS_PER_TOKEN, 0)
            for c in range(ROWS_PER_TOKEN):
                out_ref[p * BLK_PER_PAGE:(p + 1) * BLK_PER_PAGE, c * NSA_DH:(c + 1) * NSA_DH] = (
                    z_s[slot, pl.ds(c, BLK_PER_PAGE, stride=8), :])


def _compress_ab_rows(rows, page_table, wab_rows):
    N, n_pages = page_table.shape
    assert n_pages % PAGES_PER_STEP == 0
    W = ROWS_PER_TOKEN * NSA_DH
    out_rows = PAGES_PER_STEP * BLK_PER_PAGE
    page_specs = [
        pl.BlockSpec((PAGE_ROWS, NSA_DH), functools.partial(
            lambda n, s, pt, r: (pt[n, s * PAGES_PER_STEP + r], 0), r=r))
        for r in range(PAGES_PER_STEP)
    ]
    grid_spec = pltpu.PrefetchScalarGridSpec(
        num_scalar_prefetch=1,
        grid=(N, n_pages // PAGES_PER_STEP),
        in_specs=page_specs + [pl.BlockSpec((2, BLK_ROWS, NSA_DH), lambda n, s, pt: (0, 0, 0))],
        out_specs=[pl.BlockSpec((None, out_rows, W), lambda n, s, pt: (n, s, 0))] * 2,
        scratch_shapes=[pltpu.VMEM((2 * PAGES_PER_STEP, BLK_PER_PAGE * 8, NSA_DH), F32)],
    )
    ns = n_pages * BLK_PER_PAGE
    return pl.pallas_call(
        _compress_rows_body,
        grid_spec=grid_spec,
        out_shape=[jax.ShapeDtypeStruct((N, ns, W), F32)] * 2,
        compiler_params=_cparams("parallel", "arbitrary"),
        name="nsa_compress_rows",
    )(page_table, *([rows] * PAGES_PER_STEP), wab_rows)


def _cmp_proj_body(a_ref, b_ref, w_ref, kc_ref, vc_ref, *, ns):
    b_next = pltpu.roll(b_ref[...], ns - 1, 0)
    row = lax.broadcasted_iota(jnp.int32, (ns, 1), 0)
    blk = jnp.where(row < ns - 1, a_ref[...] + b_next, 0.0).astype(BF16)
    w0 = w_ref[0].astype(BF16)
    w1 = w_ref[1].astype(BF16)
    for kv in range(NSA_KV):
        cs = slice(kv * NSA_DH, (kv + 1) * NSA_DH)
        kc_ref[:, cs] = _dot(blk[:, kv * NSA_DH:(kv + 1) * NSA_DH], w0)
        vc_ref[:, cs] = _dot(blk[:, (NSA_KV + kv) * NSA_DH:(NSA_KV + kv + 1) * NSA_DH], w1)


def _cmp_proj(a, b, w_cmp):
    N, ns, W = a.shape
    return pl.pallas_call(
        functools.partial(_cmp_proj_body, ns=ns),
        grid=(N,),
        in_specs=[
            pl.BlockSpec((None, ns, W), lambda n: (n, 0, 0)),
            pl.BlockSpec((None, ns, W), lambda n: (n, 0, 0)),
            pl.BlockSpec((2, NSA_DH, NSA_DH), lambda n: (0, 0, 0)),
        ],
        out_specs=[pl.BlockSpec((None, ns, NSA_KV * NSA_DH), lambda n: (n, 0, 0))] * 2,
        out_shape=[jax.ShapeDtypeStruct((N, ns, NSA_KV * NSA_DH), F32)] * 2,
        compiler_params=_cparams("parallel"),
        name="nsa_cmp_proj",
    )(a, b, w_cmp)


def _split3(x):
    hi = x.astype(BF16)
    r1 = x - hi.astype(F32)
    mid = r1.astype(BF16)
    lo = (r1 - mid.astype(F32)).astype(BF16)
    return hi, mid, lo


def _cmp_to_sel(p_sum, a_mat):
    hi, mid, lo = _split3(p_sum)
    return _dot(hi, a_mat) + _dot(mid, a_mat) + _dot(lo, a_mat)


def _topk_mask(score, n_valid, k):
    lane = lax.broadcasted_iota(jnp.int32, score.shape, 1)
    rank = jnp.zeros(score.shape, F32)
    for jp in range(n_valid):
        col = score[:, jp:jp + 1]
        beats = (col > score) | ((col == score) & (lane > jp))
        rank = rank + beats.astype(F32)
    return ((rank < k) & (lane < n_valid)).astype(F32)


def _sel_scores(imp, qblk, n_valid):
    lane = lax.broadcasted_iota(jnp.int32, imp.shape, 1)
    forced = (lane == 0) | (lane == qblk) | (lane == qblk - 1)
    score = jnp.where(forced, FORCE_SCORE, jnp.where(lane > qblk, -1.0, imp))
    return jnp.where(lane < n_valid, score, -2.0)


def _masked_softmax_rows(s, mask):
    s = jnp.where(mask, s, NEG)
    e = jnp.exp(s - jnp.max(s, axis=-1, keepdims=True))
    return jnp.where(mask, e / jnp.sum(e, axis=-1, keepdims=True), 0.0)


def _stack_heads(q, scale):
    return (jnp.concatenate([q[:, g * NSA_DH:(g + 1) * NSA_DH] for g in range(NSA_G)], axis=0) * scale).astype(BF16)


SEL_TK = 512
WIN_KEYS = WINDOW + QBLK


def _nsa_prompt_body(q_ref, zg_ref, gb_ref, kc_ref, vc_ref, ks_ref, vs_ref, kw_ref, vw_ref, amat_t_ref,
                     o_ref, ks_bf, kw_bf, vs_t, vw_t, kc_bf, vc_t, sc_ref, *, n_sel, nsr):
    qb = pl.program_id(1)
    T = ks_ref.shape[0]
    kvc = lambda kv: slice(kv * NSA_DH, (kv + 1) * NSA_DH)

    @pl.when(qb == 0)
    def _():
        ks_bf[...] = ks_ref[...].astype(BF16)
        kw_bf[...] = kw_ref[...].astype(BF16)
        kc_bf[...] = kc_ref[...].astype(BF16)
        vc_t[...] = vc_ref[...].T.astype(BF16)

        def transpose_values(i, c):
            r0 = pl.multiple_of(i * LANE, LANE)
            vs_t[i] = vs_ref[pl.ds(r0, LANE), :].T.astype(BF16)
            vw_t[i] = vw_ref[pl.ds(r0, LANE), :].T.astype(BF16)
            return c

        lax.fori_loop(0, T // LANE, transpose_values, 0)

    R = QBLK
    start = qb * R
    groups = range(NSA_KV)
    tile4 = lambda x: jnp.concatenate([x] * NSA_G, axis=1)
    q_t = [jnp.concatenate([(q_ref[:, (kv * NSA_G + g) * NSA_DH:(kv * NSA_G + g + 1) * NSA_DH] * (NSA_DH ** -0.5)).T
                            for g in range(NSA_G)], axis=1).astype(BF16) for kv in groups]
    pos = start + lax.broadcasted_iota(jnp.int32, (1, R), 1)

    ns = kc_bf.shape[0]
    cmp_end = (lax.broadcasted_iota(jnp.int32, (ns, R), 0) + 2) * CMP_STRIDE - 1
    bias_c = tile4(jnp.where(cmp_end <= pos, 0.0, NEG))
    any_c = tile4(jnp.where(pos >= 2 * CMP_STRIDE - 1, 1.0, 0.0))
    a_t = amat_t_ref[...]
    blk = lax.broadcasted_iota(jnp.int32, (nsr, R), 0)
    qblk = (start + lax.broadcasted_iota(jnp.int32, (nsr, R), 1)) // SEL_BLOCK
    forced = (blk == 0) | (blk == qblk) | (blk == qblk - 1)
    o_cmp, score = [], []
    for kv in groups:
        s_c = _dot(kc_bf[:, kvc(kv)], q_t[kv]) + bias_c
        e_c = jnp.exp(s_c - jnp.max(s_c, axis=0, keepdims=True))
        p_c = e_c * (any_c / jnp.sum(e_c, axis=0, keepdims=True))
        o_cmp.append(_dot(vc_t[kvc(kv), :], p_c.astype(BF16)))
        p_sum = p_c[:, 0:R] + p_c[:, R:2 * R] + p_c[:, 2 * R:3 * R] + p_c[:, 3 * R:4 * R]
        hi, mid, lo = _split3(p_sum)
        imp_t = _dot(a_t, hi) + _dot(a_t, mid) + _dot(a_t, lo)
        sc = jnp.where(forced, FORCE_SCORE, jnp.where(blk > qblk, -1.0, imp_t))
        score.append(jnp.where(blk < n_sel, sc, -2.0))
        sc_ref[kv] = score[kv]

    def rank_step(i, ranks):
        ranks = list(ranks)
        for jp in (2 * i, 2 * i + 1):
            for kv in groups:
                row = sc_ref[kv, pl.ds(jp, 1), :]
                beats = (row > score[kv]) | ((row == score[kv]) & (blk > jp))
                ranks[kv] = ranks[kv] + beats.astype(F32)
        return tuple(ranks)

    n_rank = jnp.minimum(start // SEL_BLOCK + QBLK // SEL_BLOCK, n_sel)
    ranks = lax.fori_loop(0, (n_rank + 1) // 2, rank_step, tuple(jnp.zeros((nsr, R), F32) for _ in groups))
    for kv in groups:
        sc_ref[kv] = jnp.where(ranks[kv] < N_SEL, 0.0, NEG)

    def values_product(v_ref, tile0, n_sub, kv, p):
        out = _dot(v_ref[tile0, kvc(kv), :], p[0:LANE])
        for i in range(1, n_sub):
            out = out + _dot(v_ref[tile0 + i, kvc(kv), :], p[i * LANE:(i + 1) * LANE])
        return out

    key_tk = lax.broadcasted_iota(jnp.int32, (SEL_TK, R), 0)

    def sel_tile(kt, carry, causal):
        k0 = pl.multiple_of(kt * SEL_TK, SEL_TK)
        scores = [_dot(ks_bf[pl.ds(k0, SEL_TK), kvc(kv)], q_t[kv]) for kv in groups]
        stats = []
        for kv in groups:
            m_prev, l_prev, _ = carry[kv]
            bias = jnp.concatenate(
                [jnp.broadcast_to(sc_ref[kv, pl.ds(kt * (SEL_TK // SEL_BLOCK) + b, 1), :], (SEL_BLOCK, R))
                 for b in range(SEL_TK // SEL_BLOCK)], axis=0)
            if causal:
                bias = jnp.where(k0 + key_tk <= pos, bias, NEG)
            s = scores[kv] + tile4(bias)
            m_new = jnp.maximum(m_prev, jnp.max(s, axis=0, keepdims=True))
            alpha = jnp.exp(m_prev - m_new)
            p = jnp.exp(s - m_new)
            stats.append((m_new, alpha, alpha * l_prev + jnp.sum(p, axis=0, keepdims=True), p.astype(BF16)))
        return tuple((m_new, l_new, alpha * carry[kv][2]
                      + values_product(vs_t, kt * (SEL_TK // LANE), SEL_TK // LANE, kv, p))
                     for kv, (m_new, alpha, l_new, p) in enumerate(stats))

    n_tiles = (start + R - 1) // SEL_TK + 1
    init = tuple((jnp.full((1, NSA_G * R), NEG, F32), jnp.zeros((1, NSA_G * R), F32),
                  jnp.zeros((NSA_DH, NSA_G * R), F32)) for _ in groups)
    carry = lax.fori_loop(0, n_tiles - 1, lambda kt, c: sel_tile(kt, c, False), init)
    o_sel = [acc / l for _, l, acc in sel_tile(n_tiles - 1, carry, True)]

    w0 = pl.multiple_of(jnp.maximum(start - WINDOW, 0), QBLK)
    key_w = w0 + lax.broadcasted_iota(jnp.int32, (WIN_KEYS, R), 0)
    bias_w = tile4(jnp.where((key_w <= pos) & (key_w > pos - WINDOW), 0.0, NEG))
    gates_t = jax.nn.sigmoid(zg_ref[...] + gb_ref[...]).T
    for kv in groups:
        s_w = _dot(kw_bf[pl.ds(w0, WIN_KEYS), kvc(kv)], q_t[kv]) + bias_w
        e_w = jnp.exp(s_w - jnp.max(s_w, axis=0, keepdims=True))
        o_win = (values_product(vw_t, w0 // LANE, WIN_KEYS // LANE, kv, e_w.astype(BF16))
                 / jnp.sum(e_w, axis=0, keepdims=True))
        gate = lambda branch: jnp.concatenate(
            [gates_t[branch * NSA_HEADS + kv * NSA_G + g:branch * NSA_HEADS + kv * NSA_G + g + 1, :]
             for g in range(NSA_G)], axis=1)
        out_t = gate(0) * o_cmp[kv] + gate(1) * o_sel[kv] + gate(2) * o_win
        for g in range(NSA_G):
            h = kv * NSA_G + g
            o_ref[:, h * NSA_DH:(h + 1) * NSA_DH] = out_t[:, g * R:(g + 1) * R].T.astype(BF16)


def _sel_map(ns, n_sel, nsb):
    i = np.arange(ns)[:, None]
    j = np.arange(nsb)[None, :]
    r = SEL_BLOCK // CMP_STRIDE
    return jnp.asarray(((i >= r * j - 1) & (i <= r * j + r - 1) & (j < n_sel)).astype(np.float32), BF16)


def _nsa_prompt(z3, zg3, gate_b, kc, vc):
    N, T = z3.shape[:2]
    assert T % SEL_TK == 0 and T >= WIN_KEYS
    ns = kc.shape[1]
    n_sel = T // SEL_BLOCK
    nsr = -(-n_sel // 8) * 8
    assert nsr <= LANE
    amat_t = _sel_map(ns, n_sel, nsr).T
    KVW = NSA_KV * NSA_DH
    qcol = 2 * D_HALF // (NSA_HEADS * NSA_DH)
    kvs_col = (3 * D_HALF + 2 * KVW) // KVW
    kvw_col = kvs_col + 2
    full = lambda off: pl.BlockSpec((None, T, KVW), lambda n, qb: (n, 0, off))
    return pl.pallas_call(
        functools.partial(_nsa_prompt_body, n_sel=n_sel, nsr=nsr),
        grid=(N, T // QBLK),
        in_specs=[
            pl.BlockSpec((None, QBLK, NSA_HEADS * NSA_DH), lambda n, qb: (n, qb, qcol)),
            pl.BlockSpec((None, QBLK, LANE), lambda n, qb: (n, qb, 0)),
            pl.BlockSpec((1, LANE), lambda n, qb: (0, 0)),
            pl.BlockSpec((None, ns, KVW), lambda n, qb: (n, 0, 0)),
            pl.BlockSpec((None, ns, KVW), lambda n, qb: (n, 0, 0)),
            full(kvs_col), full(kvs_col + 1), full(kvw_col), full(kvw_col + 1),
            pl.BlockSpec((nsr, ns), lambda n, qb: (0, 0)),
        ],
        out_specs=pl.BlockSpec((None, QBLK, NSA_HEADS * NSA_DH), lambda n, qb: (n, qb, 0)),
        out_shape=jax.ShapeDtypeStruct((N, T, NSA_HEADS * NSA_DH), BF16),
        scratch_shapes=[
            pltpu.VMEM((T, KVW), BF16), pltpu.VMEM((T, KVW), BF16),
            pltpu.VMEM((T // LANE, KVW, LANE), BF16), pltpu.VMEM((T // LANE, KVW, LANE), BF16),
            pltpu.VMEM((ns, KVW), BF16), pltpu.VMEM((KVW, ns), BF16),
            pltpu.VMEM((NSA_KV, nsr, QBLK), F32),
        ],
        compiler_params=_cparams("parallel", "arbitrary"),
        name="nsa_prompt",
    )(z3, zg3, gate_b, kc, vc, z3, z3, z3, z3, amat_t)


def _nsa_sample_a_body(q_ref, kc_ref, vc_ref, kw_ref, vw_ref, amat_ref, ocmp_ref, owin_ref, sel_ref,
                       *, Tq, past_len, n_sel, wb):
    qs = _stack_heads(q_ref[...], NSA_DH ** -0.5)
    pos = past_len + lax.broadcasted_iota(jnp.int32, (Tq, 1), 0)
    pos4 = jnp.concatenate([pos] * NSA_G, axis=0)

    ns = kc_ref.shape[0]
    cmp_i = lax.broadcasted_iota(jnp.int32, (NSA_G * Tq, ns), 1)
    m_c = ((cmp_i + 2) * CMP_STRIDE - 1 <= pos4) & (cmp_i < ns - 1)
    p_c = _masked_softmax_rows(_dot_nt(qs, kc_ref[...].astype(BF16)), m_c)
    ocmp_ref[...] = _dot(p_c.astype(BF16), vc_ref[...].astype(BF16))
    p_sum = p_c[0:Tq] + p_c[Tq:2 * Tq] + p_c[2 * Tq:3 * Tq] + p_c[3 * Tq:4 * Tq]
    imp = _cmp_to_sel(p_sum, amat_ref[...])
    sel_ref[...] = _topk_mask(_sel_scores(imp, pos // SEL_BLOCK, n_sel), n_sel, N_SEL)

    nw = kw_ref.shape[0]
    tok_w = past_len - wb + lax.broadcasted_iota(jnp.int32, (NSA_G * Tq, nw), 1)
    m_w = (tok_w >= 0) & (tok_w <= pos4) & (tok_w > pos4 - WINDOW)
    p_w = _masked_softmax_rows(_dot_nt(qs, kw_ref[...].astype(BF16)), m_w)
    owin_ref[...] = _dot(p_w.astype(BF16), vw_ref[...].astype(BF16))


def _nsa_sample_a(z3, kc, vc, kw_full, past_len, wb):
    N, Tq = z3.shape[:2]
    ns = kc.shape[1]
    nw = kw_full.shape[1]
    n_sel = -(-(past_len + Tq) // SEL_BLOCK)
    nsb = -(-n_sel // LANE) * LANE
    amat = _sel_map(ns, n_sel, nsb)
    qcol = 2 * D_HALF // (NSA_G * NSA_DH)
    R4 = NSA_G * Tq
    return pl.pallas_call(
        functools.partial(_nsa_sample_a_body, Tq=Tq, past_len=past_len, n_sel=n_sel, wb=wb),
        grid=(N, NSA_KV),
        in_specs=[
            pl.BlockSpec((None, Tq, NSA_G * NSA_DH), lambda n, kv: (n, 0, qcol + kv)),
            pl.BlockSpec((None, ns, NSA_DH), lambda n, kv: (n, 0, kv)),
            pl.BlockSpec((None, ns, NSA_DH), lambda n, kv: (n, 0, kv)),
            pl.BlockSpec((None, nw, NSA_DH), lambda n, kv: (n, 0, kv)),
            pl.BlockSpec((None, nw, NSA_DH), lambda n, kv: (n, 0, NSA_KV + kv)),
            pl.BlockSpec((ns, nsb), lambda n, kv: (0, 0)),
        ],
        out_specs=[
            pl.BlockSpec((None, None, R4, NSA_DH), lambda n, kv: (n, kv, 0, 0)),
            pl.BlockSpec((None, None, R4, NSA_DH), lambda n, kv: (n, kv, 0, 0)),
            pl.BlockSpec((None, None, Tq, nsb), lambda n, kv: (n, kv, 0, 0)),
        ],
        out_shape=[
            jax.ShapeDtypeStruct((N, NSA_KV, R4, NSA_DH), F32),
            jax.ShapeDtypeStruct((N, NSA_KV, R4, NSA_DH), F32),
            jax.ShapeDtypeStruct((N, NSA_KV, Tq, nsb), F32),
        ],
        compiler_params=_cparams("parallel", "parallel"),
        name="nsa_sample_cmp_win",
    )(z3, kc, vc, kw_full, kw_full, amat)


BLK_PER_STEP = PAGES_PER_STEP * PAGE // SEL_BLOCK


def _nsa_sample_b_body(pt_ref, *refs, Tq, past_len):
    page_refs = refs[:PAGES_PER_STEP]
    (q_ref, new_ref, sel_ref, exp_ref, ocmp_ref, owin_ref, zg_ref, gb_ref, o_ref, m_s, l_s, acc_s) = refs[PAGES_PER_STEP:]
    s_id = pl.program_id(1)
    R4 = NSA_G * Tq
    rep = lambda x: jnp.concatenate([x] * NSA_G, axis=0)
    pos = past_len + lax.broadcasted_iota(jnp.int32, (Tq, 1), 0)

    @pl.when(s_id == 0)
    def _():
        m_s[...] = jnp.full(m_s.shape, NEG, F32)
        l_s[...] = jnp.zeros(l_s.shape, F32)
        acc_s[...] = jnp.zeros(acc_s.shape, F32)

    def update(kv, s, mask, pv):
        s = jnp.where(mask, s, NEG)
        m_prev = m_s[kv]
        m_new = jnp.maximum(m_prev, jnp.max(s, axis=-1, keepdims=True))
        alpha = jnp.exp(m_prev - m_new)
        p = jnp.where(mask, jnp.exp(s - m_new), 0.0)
        l_s[kv] = alpha * l_s[kv] + jnp.sum(p, axis=-1, keepdims=True)
        acc_s[kv] = alpha * acc_s[kv] + pv(p.astype(BF16))
        m_s[kv] = m_new

    qs = [_stack_heads(q_ref[:, kv * NSA_G * NSA_DH:(kv + 1) * NSA_G * NSA_DH], NSA_DH ** -0.5) for kv in range(NSA_KV)]
    comp = lambda pr, c: pr[pl.ds(c, PAGE, stride=ROWS_PER_TOKEN), :].astype(BF16)
    groups = range(NSA_KV)
    scores = [jnp.concatenate([_dot_nt(qs[kv], comp(pr, kv)) for pr in page_refs], axis=1) for kv in groups]
    sel_tok = [_dot(sel_ref[kv].astype(BF16), exp_ref[...]) for kv in groups]
    stats = []
    for kv in groups:
        mask = rep(sel_tok[kv]) > 0.5
        s = jnp.where(mask, scores[kv], NEG)
        m_prev = m_s[kv]
        m_new = jnp.maximum(m_prev, jnp.max(s, axis=-1, keepdims=True))
        alpha = jnp.exp(m_prev - m_new)
        p = jnp.where(mask, jnp.exp(s - m_new), 0.0)
        stats.append((m_new, alpha, alpha * l_s[kv] + jnp.sum(p, axis=-1, keepdims=True), p.astype(BF16)))
    pvs = []
    for kv in groups:
        p = stats[kv][3]
        out = _dot(p[:, 0:PAGE], comp(page_refs[0], NSA_KV + kv))
        for i in range(1, PAGES_PER_STEP):
            out = out + _dot(p[:, i * PAGE:(i + 1) * PAGE], comp(page_refs[i], NSA_KV + kv))
        pvs.append(out)
    for kv in groups:
        m_new, alpha, l_new, _ = stats[kv]
        acc_s[kv] = alpha * acc_s[kv] + pvs[kv]
        l_s[kv] = l_new
        m_s[kv] = m_new

    @pl.when(s_id == pl.num_programs(1) - 1)
    def _():
        new = new_ref[...].astype(BF16)
        tok_n = past_len + lax.broadcasted_iota(jnp.int32, (R4, Tq), 1)
        gates = jax.nn.sigmoid(zg_ref[...] + gb_ref[...])
        for kv in range(NSA_KV):
            k_new = new[:, kv * NSA_DH:(kv + 1) * NSA_DH]
            v_new = new[:, (NSA_KV + kv) * NSA_DH:(NSA_KV + kv + 1) * NSA_DH]
            update(kv, _dot_nt(qs[kv], k_new), tok_n <= rep(pos), lambda p, v_new=v_new: _dot(p, v_new))
            o_sel = acc_s[kv] / l_s[kv]
            o_cmp = ocmp_ref[kv]
            o_win = owin_ref[kv]
            for g in range(NSA_G):
                rs = slice(g * Tq, (g + 1) * Tq)
                head = kv * NSA_G + g
                o_ref[:, head * NSA_DH:(head + 1) * NSA_DH] = (
                    gates[:, head:head + 1] * o_cmp[rs]
                    + gates[:, NSA_HEADS + head:NSA_HEADS + head + 1] * o_sel[rs]
                    + gates[:, 2 * NSA_HEADS + head:2 * NSA_HEADS + head + 1] * o_win[rs])


def _nsa_sample_b(z3, zg3, gate_b, pool_sel, page_table, sel, o_cmp, o_win, past_len):
    N, Tq = z3.shape[:2]
    n_pages = page_table.shape[1]
    assert n_pages % PAGES_PER_STEP == 0 and past_len % SEL_BLOCK == 0
    n_steps = n_pages // PAGES_PER_STEP
    sel_steps = sel[..., :past_len // SEL_BLOCK].reshape(N, NSA_KV, Tq, n_steps, BLK_PER_STEP).transpose(0, 3, 1, 2, 4)
    tok = np.arange(PAGES_PER_STEP * PAGE)[None, :]
    expand = jnp.asarray((tok // SEL_BLOCK == np.arange(BLK_PER_STEP)[:, None]).astype(np.float32), BF16)
    needed = (jnp.max(sel_steps, axis=(2, 3)) > 0).reshape(N, n_steps, PAGES_PER_STEP, PAGE // SEL_BLOCK).any(-1)
    last_needed = lax.cummax(jnp.where(needed, jnp.arange(n_steps, dtype=jnp.int32)[None, :, None], 0), axis=1)
    page_table = jnp.take_along_axis(page_table.reshape(N, n_steps, PAGES_PER_STEP), last_needed, axis=1).reshape(N, n_pages)
    W = 4 * NSA_DH
    R4 = NSA_G * Tq
    qcol = 2 * D_HALF // (NSA_HEADS * NSA_DH)
    kvs_col = (3 * D_HALF + W) // W
    page_specs = [
        pl.BlockSpec((PAGE_ROWS, NSA_DH), functools.partial(
            lambda n, s, pt, r: (pt[n, s * PAGES_PER_STEP + r], 0), r=r))
        for r in range(PAGES_PER_STEP)
    ]
    per_n4 =lambda shape: pl.BlockSpec((None,) + shape, lambda n, s, pt: (n, 0, 0, 0))
    grid_spec = pltpu.PrefetchScalarGridSpec(
        num_scalar_prefetch=1,
        grid=(N, n_pages // PAGES_PER_STEP),
        in_specs=page_specs + [
            pl.BlockSpec((None, Tq, NSA_HEADS * NSA_DH), lambda n, s, pt: (n, 0, qcol)),
            pl.BlockSpec((None, Tq, W), lambda n, s, pt: (n, 0, kvs_col)),
            pl.BlockSpec((None, None, NSA_KV, Tq, BLK_PER_STEP), lambda n, s, pt: (n, s, 0, 0, 0)),
            pl.BlockSpec((BLK_PER_STEP, PAGES_PER_STEP * PAGE), lambda n, s, pt: (0, 0)),
            per_n4((NSA_KV, R4, NSA_DH)),
            per_n4((NSA_KV, R4, NSA_DH)),
            pl.BlockSpec((None, Tq, LANE), lambda n, s, pt: (n, 0, 0)),
            pl.BlockSpec((1, LANE), lambda n, s, pt: (0, 0)),
        ],
        out_specs=pl.BlockSpec((None, Tq, NSA_HEADS * NSA_DH), lambda n, s, pt: (n, 0, 0)),
        scratch_shapes=[
            pltpu.VMEM((NSA_KV, R4, 1), F32),
            pltpu.VMEM((NSA_KV, R4, 1), F32),
            pltpu.VMEM((NSA_KV, R4, NSA_DH), F32),
        ],
    )
    return pl.pallas_call(
        functools.partial(_nsa_sample_b_body, Tq=Tq, past_len=past_len),
        grid_spec=grid_spec,
        out_shape=jax.ShapeDtypeStruct((N, Tq, NSA_HEADS * NSA_DH), F32),
        compiler_params=_cparams("parallel", "arbitrary"),
        name="nsa_sample_sel",
    )(page_table, *([pool_sel] * PAGES_PER_STEP), z3, z3, sel_steps, expand, o_cmp, o_win, zg3, gate_b)


MLSTM_CHUNK_PROMPT = 256
MLSTM_CHUNK_SAMPLE = 32
SAMPLE_PAD = 128


def _pad_rows(x, rows):
    return jnp.pad(x, ((0, 0), (0, rows - x.shape[1]), (0, 0)))


def _lane_pad(v):
    return jnp.pad(v.astype(F32), (0, LANE - v.shape[0])).reshape(1, LANE)


def _gate_rows(w_in_t, n_main):
    return jnp.pad(w_in_t[n_main:], ((0, LANE - (w_in_t.shape[0] - n_main)), (0, 0)))


def _even_mixers(z3, zg3, t0, pool_buf, C0, n0, m0, w):
    N, T = z3.shape[:2]
    st = jnp.pad(pool_buf, ((0, 0), (HALO - POOL_PAD, 0), (0, 0)))
    y_a = _pool_mix(z3, st, w["pool_w"], w["pool_scale"], t0)
    u_ext_tail = jnp.concatenate([pool_buf, z3[:, :, :D_HALF]], axis=1)[:, -POOL_PAD:] if T < POOL_PAD else z3[:, -POOL_PAD:, :D_HALF]

    if T % MLSTM_CHUNK_PROMPT == 0:
        L, zm, zgm = MLSTM_CHUNK_PROMPT, z3, zg3
    else:
        assert T <= MLSTM_CHUNK_SAMPLE
        L, zm, zgm = MLSTM_CHUNK_SAMPLE, _pad_rows(z3, MLSTM_CHUNK_SAMPLE), _pad_rows(zg3, MLSTM_CHUNK_SAMPLE)
    y_b, C, n, m = _mlstm(zm, zgm, w["gate_b"], w["mnorm_g"], C0, n0, m0, L, min(T, L))
    return y_a, y_b[:, :T], u_ext_tail, C, n, m


def _kv_rows_body(x_ref, o_ref, *, tT):
    for c in range(ROWS_PER_TOKEN):
        o_ref[pl.ds(c, tT, stride=ROWS_PER_TOKEN), :] = x_ref[:, c * NSA_DH:(c + 1) * NSA_DH]


def _kv_rows(z3, col_block, first_row, n_rows, tT_pref=512):
    N = z3.shape[0]
    tT = _tile(n_rows, tT_pref)
    assert first_row % tT == 0
    rows = pl.pallas_call(
        functools.partial(_kv_rows_body, tT=tT),
        grid=(N, n_rows // tT),
        in_specs=[pl.BlockSpec((None, tT, ROWS_PER_TOKEN * NSA_DH), lambda n, t: (n, first_row // tT + t, col_block))],
        out_specs=pl.BlockSpec((None, ROWS_PER_TOKEN * tT, NSA_DH), lambda n, t: (n, t, 0)),
        out_shape=jax.ShapeDtypeStruct((N, ROWS_PER_TOKEN * n_rows, NSA_DH), F32),
        compiler_params=_cparams("parallel", "parallel"),
        name="kv_rows",
    )(z3)
    return rows.reshape(N, n_rows, 2, NSA_KV, NSA_DH)


def _odd_mixers(z3, zg3, w, past=None):
    N, T, n_main = z3.shape
    W = 4 * NSA_DH
    kv_col = 3 * D_HALF // W
    kv5 = lambda t: t.reshape(N, t.shape[1], 2, NSA_KV, NSA_DH)

    if past is None:
        y_c, = _gmlp(z3, w["gnorm_g"], w["ws"], w["bs_t"], keep_vn=False)
        pt = jnp.arange(N * (T // PAGE), dtype=jnp.int32).reshape(N, T // PAGE)
        a, b = _compress_ab(z3.reshape(N * (T // PAGE), PAGE, n_main), pt, w["wab"], kv_col)
        kc, vc = _cmp_proj(a, b, w["cmp_w"])
        o = _nsa_prompt(z3, zg3, w["nsa_gate_b"], kc, vc)
        wlen = min(WINDOW, T)
        return (y_c, o, None, _kv_rows(z3, kv_col, 0, T), _kv_rows(z3, kv_col + 1, 0, T),
                _kv_rows(z3, kv_col + 2, T - wlen, wlen))

    kvc = z3[:, :, 3 * D_HALF:3 * D_HALF + W]
    kvs = z3[:, :, 3 * D_HALF + W:3 * D_HALF + 2 * W]
    kvw = z3[:, :, 3 * D_HALF + 2 * W:3 * D_HALF + 3 * W]
    win_buf, pool_cmp, pool_sel, page_table = past
    past_len = page_table.shape[1] * PAGE
    wb = win_buf.shape[1]
    y_c, vn = _gmlp(_pad_rows(z3[:, :, :2 * D_HALF], SAMPLE_PAD), w["gnorm_g"], w["ws"], w["bs_t"], keep_vn=True)
    y_c, vn = y_c[:, :T], vn[:, :T]
    assert (past_len + T) // CMP_STRIDE == past_len // CMP_STRIDE
    a, b = _compress_ab_rows(pool_cmp, page_table, w["wab_rows"])
    kc, vc = _cmp_proj(a, b, w["cmp_w"])
    kw_all = jnp.concatenate([win_buf.reshape(N, wb, W), kvw], axis=1)
    nw = -(-(wb + T) // LANE) * LANE
    o_cmp, o_win, sel = _nsa_sample_a(z3, kc, vc, _pad_rows(kw_all, nw), past_len, wb)
    o = _nsa_sample_b(z3, zg3, w["nsa_gate_b"], pool_sel, page_table, sel, o_cmp, o_win, past_len)
    win_state = jnp.concatenate([win_buf, kv5(kvw)], axis=1)[:, -wb:]
    return y_c, o, vn, kv5(kvc), kv5(kvs), win_state


def _layer(xp, xs, xpn, xsn, w, ffn_stacks, layer, n_main, mixers_p, mixers_s):
    B, T, D = xp.shape
    Ns, Ts, _ = xs.shape
    xp2, xs2 = xp.reshape(B * T, D), xs.reshape(Ns * Ts, D)
    zp, zgp, zs, zgs = _in_proj(xpn, xsn, w["w_in"], w["w_in_layer"], n_main, w["w_gate"])
    a1p, a2p, *extra_p = mixers_p(zp.reshape(B, T, n_main), zgp.reshape(B, T, LANE))
    a1s, a2s, *extra_s = mixers_s(zs.reshape(Ns, Ts, n_main), zgs.reshape(Ns, Ts, LANE))
    flat = lambda t: t.reshape(-1, D_HALF)
    xp2, xpn, xs2, xsn = _out_proj(flat(a1p), flat(a2p), xp2, flat(a1s), flat(a2s), xs2, w["w_out_a"], w["w_out_b"],
                                   w["ng"][1:2], w["ng"][2:3])
    w1_stack, w3_stack, w2_stack = ffn_stacks
    hp, hs, w2_bf = _ffn_up(xpn, xsn, w1_stack, w3_stack, w2_stack, layer)
    xp2, xs2, xpn, xsn = _ffn_down(hp, hs, xp2, xs2, w["ng"][3:4], w["ng_next"], w2_bf)
    return xp2.reshape(B, T, D), xs2.reshape(Ns, Ts, D), xpn, xsn, extra_p, extra_s


def kernel(x_prompt, x_sample, state_pool, state_mlstm_c, state_mlstm_n, state_mlstm_m, state_win_kv, cache_cmp_kv, cache_sel_kv, page_table, norm_g, w_in_even, w_out_even, pool_w, pool_scale, mlstm_gate_b, mlstm_norm_g, w_in_odd, w_out_odd, gmlp_norm_g, gmlp_ws, gmlp_bs, nsa_cmp_pos, nsa_cmp_w, nsa_gate_b, ffn_w1, ffn_w3, ffn_w2):
    B = x_prompt.shape[0]
    depth = norm_g.shape[0]
    past_len = page_table.shape[1] * PAGE
    xp, xs = x_prompt, x_sample
    pool_p, pool_s, c_p, c_s, n_p, n_s, m_p, m_s = [], [], [], [], [], [], [], []
    gv_s, cmp_p, cmp_s, sel_p, sel_s, win_p, win_s = [], [], [], [], [], [], []
    ffn_stacks = (ffn_w1, ffn_w3, ffn_w2)
    D = x_prompt.shape[-1]
    xpn, xsn = _rms_cast(x_prompt.reshape(-1, D), x_sample.reshape(-1, D), norm_g[0][0:1])
    for l in range(depth):
        j = l // 2
        ng_next = norm_g[min(l + 1, depth - 1)][0:1]
        if l % 2 == 0:
            n_main = 4 * D_HALF
            w_out = w_out_even[j].astype(BF16)
            w_in_t = jnp.swapaxes(w_in_even, 1, 2)
            w = dict(ng=norm_g[l], ng_next=ng_next, w_in=w_in_t, w_in_layer=j, w_gate=_gate_rows(w_in_t[j], n_main),
                     w_out_a=w_out[:D_HALF], w_out_b=w_out[D_HALF:],
                     pool_w=pool_w[j].astype(BF16), pool_scale=pool_scale[j].reshape(1, D_HALF),
                     gate_b=_lane_pad(mlstm_gate_b[j].reshape(-1)), mnorm_g=mlstm_norm_g[j].reshape(1, D_HALF))
            zp = jnp.zeros((B, POOL_PAD, D_HALF), F32)
            zc = jnp.zeros((B, MLSTM_HEADS, MLSTM_DK, MLSTM_DV), F32)
            zn = jnp.zeros((B, MLSTM_HEADS, MLSTM_DK), F32)
            zm = jnp.zeros((B, MLSTM_HEADS), F32)
            xp, xs, xpn, xsn, (pb, c, n, m), (pbs, cs, ns_, ms) = _layer(
                xp, xs, xpn, xsn, w, ffn_stacks, l, n_main,
                lambda z3, zg3: _even_mixers(z3, zg3, 0, zp, zc, zn, zm, w),
                lambda z3, zg3: _even_mixers(z3, zg3, past_len, state_pool[j], state_mlstm_c[j], state_mlstm_n[j],
                                             state_mlstm_m[j], w))
            pool_p.append(pb); c_p.append(c); n_p.append(n); m_p.append(m)
            pool_s.append(pbs); c_s.append(cs); n_s.append(ns_); m_s.append(ms)
        else:
            n_main = 4 * D_HALF + D_HALF // 2
            w_out = w_out_odd[j].astype(BF16)
            cp = nsa_cmp_pos[j]
            wcol = jnp.repeat(cp, NSA_KV * NSA_DH, axis=1)
            wab = jnp.stack([wcol[:CMP_STRIDE], wcol[CMP_STRIDE:]])
            wab_rows = jnp.broadcast_to(jnp.repeat(cp, NSA_KV, axis=1).reshape(2, BLK_ROWS, 1), (2, BLK_ROWS, NSA_DH))
            w_in_t = jnp.swapaxes(w_in_odd, 1, 2)
            w = dict(ng=norm_g[l], ng_next=ng_next, w_in=w_in_t, w_in_layer=j, w_gate=_gate_rows(w_in_t[j], n_main),
                     w_out_a=w_out[:D_HALF], w_out_b=w_out[D_HALF:],
                     gnorm_g=gmlp_norm_g[j].reshape(1, D_HALF), ws=gmlp_ws[j],
                     bs_t=jnp.pad(gmlp_bs[j].T, ((0, 0), (0, LANE - GMLP_GROUPS))),
                     wab=wab, wab_rows=wab_rows, cmp_w=nsa_cmp_w[j], nsa_gate_b=_lane_pad(nsa_gate_b[j]))
            n_pool = cache_cmp_kv.shape[1]
            flat = lambda c: c.reshape(c.shape[0] * n_pool * PAGE_ROWS, NSA_DH)
            past = (state_win_kv[j], flat(cache_cmp_kv), flat(cache_sel_kv), page_table + j * n_pool)
            xp, xs, xpn, xsn, (_, kc, ksl, wv), (vn, kcs, ksls, wvs) = _layer(
                xp, xs, xpn, xsn, w, ffn_stacks, l, n_main,
                lambda z3, zg3: _odd_mixers(z3, zg3, w),
                lambda z3, zg3: _odd_mixers(z3, zg3, w, past=past))
            cmp_p.append(kc); sel_p.append(ksl); win_p.append(wv)
            gv_s.append(vn); cmp_s.append(kcs); sel_s.append(ksls); win_s.append(wvs)
    st = jnp.stack
    return (xp, xs, st(pool_p), st(pool_s), st(c_p), st(c_s), st(n_p), st(n_s), st(m_p), st(m_s),
            st(gv_s), st(cmp_p), st(cmp_s), st(sel_p), st(sel_s), st(win_p), st(win_s))
```
